```python
import math
import jax
import jax.numpy as jnp
from jax import lax
import numpy as np


D_MODEL = 1024
BATCH = 1
SEQ = 16384
DEPTH = 2
DEC_BATCH = 16
DEC_SEQ = 16
PAST_LEN = 4096

CHUNK = 64
EPS = 1e-6
N_AB = (DEPTH + 1) // 2
N_C = DEPTH // 2

POOL_WINDOWS = (2, 4, 8, 16)
POOL_GROUPS = 4
POOL_DIM = D_MODEL // 2
POOL_GDIM = POOL_DIM // POOL_GROUPS
POOL_HIST = max(POOL_WINDOWS) - 1

MLA_HEADS = 8
QK_NOPE = 64
QK_ROPE = 32
V_HEAD = 64
Q_LORA = 384
KV_LORA = 256
MLA_OUT = MLA_HEADS * V_HEAD
ROPE_THETA = 10000.0
Q_BLOCK = 128
SM_SCALE = (QK_NOPE + QK_ROPE) ** -0.5
IN_AB = POOL_DIM + Q_LORA + KV_LORA + QK_ROPE
NEG = -1e30

D_INNER = 2 * D_MODEL
SSM_HEADDIM = 64
SSM_HEADS = D_INNER // SSM_HEADDIM
SSM_GROUPS = 4
SSM_HPG = SSM_HEADS // SSM_GROUPS
D_STATE = 128
D_CONV = 4
CONV_DIM = D_INNER + 2 * SSM_GROUPS * D_STATE
IN_C = D_INNER + CONV_DIM + SSM_HEADS

D_FF = -(-8 * D_MODEL // (3 * 256)) * 256

kernel_name = 'hybrid_pool_mla_ssd_stream_step'


def rmsnorm(x, g):
    xf = x.astype(jnp.float32)
    y = xf * lax.rsqrt(jnp.mean(xf * xf, axis=-1, keepdims=True) + EPS)
    return (y * g.astype(jnp.float32)).astype(x.dtype)


def swiglu(h, w_gate, w_up, w_down):
    return (jax.nn.silu(h @ w_gate) * (h @ w_up)) @ w_down


def rope_tables(pos):
    half = QK_ROPE // 2
    inv = ROPE_THETA ** (-jnp.arange(half, dtype=jnp.float32) / half)
    ang = pos.astype(jnp.float32)[:, None] * inv[None, :]
    return jnp.cos(ang), jnp.sin(ang)


def apply_rope(x, cos, sin):
    half = QK_ROPE // 2
    extra = x.ndim - 3
    cos = cos.reshape(cos.shape[:1] + (1,) * extra + cos.shape[1:])
    sin = sin.reshape(sin.shape[:1] + (1,) * extra + sin.shape[1:])
    xf = x.astype(jnp.float32)
    x1, x2 = xf[..., :half], xf[..., half:]
    return jnp.concatenate([x1 * cos - x2 * sin, x2 * cos + x1 * sin], axis=-1).astype(x.dtype)


def pool_mix(full, pos0, w_pool, pool_scale):
    b, tot, _ = full.shape
    s = tot - POOL_HIST
    ff = full.astype(jnp.float32)
    cs = jnp.cumsum(ff, axis=1)
    cs = jnp.concatenate([jnp.zeros_like(cs[:, :1]), cs], axis=1)
    pos = pos0 + jnp.arange(s)
    outs = []
    for g, w in enumerate(POOL_WINDOWS):
        sl = slice(g * POOL_GDIM, (g + 1) * POOL_GDIM)
        win_sum = cs[:, POOL_HIST + 1:POOL_HIST + 1 + s, sl] - cs[:, POOL_HIST + 1 - w:POOL_HIST + 1 - w + s, sl]
        cnt = jnp.minimum(pos + 1, w).astype(jnp.float32)[None, :, None]
        outs.append(win_sum / cnt - ff[:, POOL_HIST:, sl])
    pooled = jnp.stack(outs, axis=2)
    mixed = jnp.einsum('bsgc,gcd->bsgd', pooled, w_pool.astype(jnp.float32)).reshape(b, s, POOL_DIM)
    return (mixed * pool_scale.astype(jnp.float32)).astype(full.dtype)


def mla_attend(q_nope, q_pe, ckv, kpe, q_pos, k_pos, w_uk, w_uv):
    b, sq = q_nope.shape[:2]
    k_nope = jnp.einsum('bkc,chd->bkhd', ckv, w_uk)
    v = jnp.einsum('bkc,chd->bkhd', ckv, w_uv)
    k_chunk = k_pos // CHUNK
    qb = Q_BLOCK if sq % Q_BLOCK == 0 else sq
    nb = sq // qb

    def block(args):
        qn, qp, qpos = args
        s = (jnp.einsum('bqhd,bkhd->bhqk', qn, k_nope)
             + jnp.einsum('bqhr,bkr->bhqk', qp, kpe)).astype(jnp.float32) * SM_SCALE
        mask = k_chunk[None, :] <= (qpos // CHUNK)[:, None]
        s = jnp.where(mask[None, None], s, NEG)
        p = jax.nn.softmax(s, axis=-1).astype(v.dtype)
        return jnp.einsum('bhqk,bkhd->bqhd', p, v)

    qn_b = jnp.moveaxis(q_nope.reshape(b, nb, qb, MLA_HEADS, QK_NOPE), 1, 0)
    qp_b = jnp.moveaxis(q_pe.reshape(b, nb, qb, MLA_HEADS, QK_ROPE), 1, 0)
    pos_b = q_pos.reshape(nb, qb)
    out = lax.map(block, (qn_b, qp_b, pos_b))
    return jnp.moveaxis(out, 0, 1).reshape(b, sq, MLA_OUT)


def mixer_ab(h, pos0, pool_hist, ckv_past, kpe_past, w_in, w_pool, pool_scale, q_norm, w_uq,
             kv_norm, w_uk, w_uv, w_out):
    b, s, _ = h.shape
    proj = h @ w_in
    o1 = POOL_DIM
    o2 = o1 + Q_LORA
    o3 = o2 + KV_LORA
    u_pool, q_lat, kv_lat, k_pe = proj[..., :o1], proj[..., o1:o2], proj[..., o2:o3], proj[..., o3:]
    pool_full = jnp.concatenate([pool_hist.astype(h.dtype), u_pool], axis=1)
    pool_out = pool_mix(pool_full, pos0, w_pool, pool_scale)
    new_pool = pool_full[:, -POOL_HIST:]
    pos = pos0 + jnp.arange(s)
    cos, sin = rope_tables(pos)
    q = (rmsnorm(q_lat, q_norm) @ w_uq).reshape(b, s, MLA_HEADS, QK_NOPE + QK_ROPE)
    q_nope = q[..., :QK_NOPE]
    q_pe = apply_rope(q[..., QK_NOPE:], cos, sin)
    ckv = rmsnorm(kv_lat, kv_norm)
    kpe = apply_rope(k_pe, cos, sin)
    ckv_all = jnp.concatenate([ckv_past.astype(h.dtype), ckv], axis=1)
    kpe_all = jnp.concatenate([kpe_past.astype(h.dtype), kpe], axis=1)
    k_pos = pos0 - ckv_past.shape[1] + jnp.arange(ckv_all.shape[1])
    attn = mla_attend(q_nope, q_pe, ckv_all, kpe_all, pos, k_pos, w_uk, w_uv)
    y = jnp.concatenate([pool_out, attn], axis=-1) @ w_out
    return y, new_pool, ckv, kpe


def ssd_scan(x, dt, a, bm, cm, h0):
    b, s = x.shape[:2]
    ln = CHUNK if s % CHUNK == 0 else s
    nc = s // ln

    def chunks(t):
        return jnp.moveaxis(t.reshape((b, nc, ln) + t.shape[2:]), 1, 0)

    xs = chunks(x.astype(jnp.float32).reshape(b, s, SSM_GROUPS, SSM_HPG, SSM_HEADDIM))
    dts = chunks(dt.reshape(b, s, SSM_GROUPS, SSM_HPG))
    bs = chunks(bm.astype(jnp.float32))
    cs = chunks(cm.astype(jnp.float32))
    a_g = a.reshape(SSM_GROUPS, SSM_HPG)
    causal = jnp.tril(jnp.ones((ln, ln), dtype=bool))

    def step(h, inp):
        xc, dtc, bc, cc = inp
        acum = jnp.cumsum(dtc * a_g, axis=1)
        seg = acum[:, :, None] - acum[:, None, :]
        decay = jnp.exp(jnp.where(causal[None, :, :, None, None], seg, -jnp.inf))
        scores = jnp.einsum('bign,bjgn->bijg', cc, bc)[..., None] * decay * dtc[:, None]
        y = jnp.einsum('bijgh,bjghp->bighp', scores, xc)
        y = y + jnp.einsum('bign,bghpn->bighp', cc, h) * jnp.exp(acum)[..., None]
        last = acum[:, -1]
        wj = jnp.exp(last[:, None] - acum) * dtc
        h = h * jnp.exp(last)[..., None, None] + jnp.einsum('bjgh,bjgn,bjghp->bghpn', wj, bc, xc)
        return h, y

    hinit = h0.astype(jnp.float32).reshape(b, SSM_GROUPS, SSM_HPG, SSM_HEADDIM, D_STATE)
    h_fin, ys = lax.scan(step, hinit, (xs, dts, bs, cs))
    y = jnp.moveaxis(ys, 0, 1).reshape(b, s, SSM_HEADS, SSM_HEADDIM)
    return y, h_fin.reshape(b, SSM_HEADS, SSM_HEADDIM, D_STATE)


def mixer_c(h, conv_hist, ssm_hist, w_in, conv_w, conv_b, dt_bias, a_log, d_skip, ssm_norm, w_out):
    b, s, _ = h.shape
    proj = h @ w_in
    z = proj[..., :D_INNER]
    xbc = proj[..., D_INNER:D_INNER + CONV_DIM]
    dt_raw = proj[..., D_INNER + CONV_DIM:]
    xbc_full = jnp.concatenate([conv_hist.astype(h.dtype), xbc], axis=1)
    new_conv = xbc_full[:, -(D_CONV - 1):]
    conv = lax.conv_general_dilated(xbc_full, conv_w[:, None, :].astype(h.dtype), window_strides=(1,),
                                    padding='VALID', dimension_numbers=('NWC', 'WIO', 'NWC'),
                                    feature_group_count=CONV_DIM)
    xbc = jax.nn.silu(conv + conv_b)
    gn = SSM_GROUPS * D_STATE
    xs = xbc[..., :D_INNER].reshape(b, s, SSM_HEADS, SSM_HEADDIM)
    bm = xbc[..., D_INNER:D_INNER + gn].reshape(b, s, SSM_GROUPS, D_STATE)
    cm = xbc[..., D_INNER + gn:].reshape(b, s, SSM_GROUPS, D_STATE)
    dt = jax.nn.softplus(dt_raw.astype(jnp.float32) + dt_bias.astype(jnp.float32))
    a = -jnp.exp(a_log.astype(jnp.float32))
    y, new_h = ssd_scan(xs, dt, a, bm, cm, ssm_hist)
    y = y + xs.astype(jnp.float32) * d_skip.astype(jnp.float32)[:, None]
    y = y.reshape(b, s, D_INNER) * jax.nn.silu(z.astype(jnp.float32))
    yg = y.reshape(b, s, SSM_GROUPS, D_INNER // SSM_GROUPS)
    yg = yg * lax.rsqrt(jnp.mean(yg * yg, axis=-1, keepdims=True) + EPS)
    y = (yg.reshape(b, s, D_INNER) * ssm_norm.astype(jnp.float32)).astype(h.dtype)
    return y @ w_out, new_conv, new_h


def trunk(x, pos0, ckv_past, kpe_past, pool_hist, conv_hist, ssm_hist, norm_mix, norm_ffn, norm_final,
          w_in_ab, w_pool, pool_scale, q_norm, w_uq, kv_norm, w_uk, w_uv, w_out_ab, w_in_c, conv_w,
          conv_b, dt_bias, a_log, d_skip, ssm_norm, w_out_c, w_gate, w_up, w_down):
    ckv_new, kpe_new, pool_new, conv_new, ssm_new = [], [], [], [], []
    for layer in range(DEPTH):
        i = layer // 2
        h = rmsnorm(x, norm_mix[layer])
        if layer % 2 == 0:
            y, p, c, k = mixer_ab(h, pos0, pool_hist[i], ckv_past[i], kpe_past[i], w_in_ab[i], w_pool[i],
                                  pool_scale[i], q_norm[i], w_uq[i], kv_norm[i], w_uk[i], w_uv[i],
                                  w_out_ab[i])
            pool_new.append(p)
            ckv_new.append(c)
            kpe_new.append(k)
        else:
            y, cv, st = mixer_c(h, conv_hist[i], ssm_hist[i], w_in_c[i], conv_w[i], conv_b[i],
                                dt_bias[i], a_log[i], d_skip[i], ssm_norm[i], w_out_c[i])
            conv_new.append(cv)
            ssm_new.append(st)
        x = x + y
        x = x + swiglu(rmsnorm(x, norm_ffn[layer]), w_gate[layer], w_up[layer], w_down[layer])
    return (rmsnorm(x, norm_final), jnp.stack(ckv_new), jnp.stack(kpe_new), jnp.stack(pool_new),
            jnp.stack(conv_new), jnp.stack(ssm_new))


def setup_inputs(seed: int = 0) -> dict:
    key = jax.random.key(seed)
    ks = jax.random.split(key, 32)

    def nrm(i, shape, scale=1.0):
        return scale * jax.random.normal(ks[i], shape, jnp.float32)

    dt0 = jnp.exp(jax.random.uniform(ks[22], (N_C, SSM_HEADS), jnp.float32,
                                     minval=math.log(1e-3), maxval=math.log(1e-1)))
    return {
        'x_prompt': nrm(0, (BATCH, SEQ, D_MODEL)),
        'x_sample': nrm(1, (DEC_BATCH, DEC_SEQ, D_MODEL)),
        'cache_ckv': nrm(2, (N_AB, DEC_BATCH, PAST_LEN, KV_LORA)),
        'cache_kpe': nrm(3, (N_AB, DEC_BATCH, PAST_LEN, QK_ROPE)),
        'state_pool': nrm(4, (N_AB, DEC_BATCH, POOL_HIST, POOL_DIM)),
        'state_conv': nrm(5, (N_C, DEC_BATCH, D_CONV - 1, CONV_DIM)),
        'state_ssm': nrm(6, (N_C, DEC_BATCH, SSM_HEADS, SSM_HEADDIM, D_STATE), 0.3),
        'norm_mix': 1.0 + nrm(7, (DEPTH, D_MODEL), 0.01),
        'norm_ffn': 1.0 + nrm(8, (DEPTH, D_MODEL), 0.01),
        'norm_final': 1.0 + nrm(9, (D_MODEL,), 0.01),
        'w_in_ab': nrm(10, (N_AB, D_MODEL, IN_AB), D_MODEL ** -0.5),
        'w_pool': nrm(11, (N_AB, POOL_GROUPS, POOL_GDIM, POOL_GDIM), POOL_GDIM ** -0.5),
        'pool_scale': 1.0 + nrm(12, (N_AB, POOL_DIM), 0.1),
        'q_norm': 1.0 + nrm(13, (N_AB, Q_LORA), 0.01),
        'w_uq': nrm(14, (N_AB, Q_LORA, MLA_HEADS * (QK_NOPE + QK_ROPE)), Q_LORA ** -0.5),
        'kv_norm': 1.0 + nrm(15, (N_AB, KV_LORA), 0.01),
        'w_uk': nrm(16, (N_AB, KV_LORA, MLA_HEADS, QK_NOPE), KV_LORA ** -0.5),
        'w_uv': nrm(17, (N_AB, KV_LORA, MLA_HEADS, V_HEAD), KV_LORA ** -0.5),
        'w_out_ab': nrm(18, (N_AB, POOL_DIM + MLA_OUT, D_MODEL), (POOL_DIM + MLA_OUT) ** -0.5),
        'w_in_c': nrm(19, (N_C, D_MODEL, IN_C), D_MODEL ** -0.5),
        'conv_w': nrm(20, (N_C, D_CONV, CONV_DIM), D_CONV ** -0.5),
        'conv_b': nrm(21, (N_C, CONV_DIM), 0.02),
        'dt_bias': dt0 + jnp.log(-jnp.expm1(-dt0)),
        'a_log': jnp.log(jax.random.uniform(ks[23], (N_C, SSM_HEADS), jnp.float32, minval=1.0, maxval=16.0)),
        'd_skip': 1.0 + nrm(24, (N_C, SSM_HEADS), 0.1),
        'ssm_norm': 1.0 + nrm(25, (N_C, D_INNER), 0.01),
        'w_out_c': nrm(26, (N_C, D_INNER, D_MODEL), D_INNER ** -0.5),
        'w_gate': nrm(27, (DEPTH, D_MODEL, D_FF), D_MODEL ** -0.5),
        'w_up': nrm(28, (DEPTH, D_MODEL, D_FF), D_MODEL ** -0.5),
        'w_down': nrm(29, (DEPTH, D_FF, D_MODEL), D_FF ** -0.5),
    }


def reference(x_prompt, x_sample, cache_ckv, cache_kpe, state_pool, state_conv, state_ssm,
              norm_mix, norm_ffn, norm_final, w_in_ab, w_pool, pool_scale, q_norm, w_uq, kv_norm,
              w_uk, w_uv, w_out_ab, w_in_c, conv_w, conv_b, dt_bias, a_log, d_skip, ssm_norm,
              w_out_c, w_gate, w_up, w_down):
    weights = (norm_mix, norm_ffn, norm_final, w_in_ab, w_pool, pool_scale, q_norm, w_uq, kv_norm,
               w_uk, w_uv, w_out_ab, w_in_c, conv_w, conv_b, dt_bias, a_log, d_skip, ssm_norm,
               w_out_c, w_gate, w_up, w_down)
    b = x_prompt.shape[0]
    dtp = x_prompt.dtype
    (y_prompt, ckv_p, kpe_p, pool_p, conv_p, ssm_p) = trunk(
        x_prompt, 0,
        jnp.zeros((N_AB, b, 0, KV_LORA), dtp), jnp.zeros((N_AB, b, 0, QK_ROPE), dtp),
        jnp.zeros((N_AB, b, POOL_HIST, POOL_DIM), dtp), jnp.zeros((N_C, b, D_CONV - 1, CONV_DIM), dtp),
        jnp.zeros((N_C, b, SSM_HEADS, SSM_HEADDIM, D_STATE), jnp.float32), *weights)
    (y_sample, ckv_s, kpe_s, pool_s, conv_s, ssm_s) = trunk(
        x_sample, cache_ckv.shape[2], cache_ckv, cache_kpe, state_pool, state_conv, state_ssm, *weights)
    return (y_prompt, y_sample, ckv_p, kpe_p, pool_p, conv_p, ssm_p, ckv_s, kpe_s, pool_s, conv_s, ssm_s)
```

```python
import functools
import math

import numpy as np
import jax
import jax.numpy as jnp
from jax import lax
from jax.experimental import pallas as pl
from jax.experimental.pallas import tpu as pltpu

F32 = jnp.float32
BF16 = jnp.bfloat16

D_MODEL = 1024
CHUNK = 64
EPS = 1e-6
NEG = -1e30

POOL_WINDOWS = (2, 4, 8, 16)
POOL_GROUPS = 4
POOL_DIM = D_MODEL // 2
POOL_GDIM = POOL_DIM // POOL_GROUPS
POOL_HIST = max(POOL_WINDOWS) - 1

MLA_HEADS = 8
QK_NOPE = 64
QK_ROPE = 32
ROPE_HALF = QK_ROPE // 2
V_HEAD = 64
Q_LORA = 384
KV_LORA = 256
MLA_OUT = MLA_HEADS * V_HEAD
ROPE_THETA = 10000.0
SM_SCALE = (QK_NOPE + QK_ROPE) ** -0.5
IN_AB = POOL_DIM + Q_LORA + KV_LORA + QK_ROPE

D_INNER = 2 * D_MODEL
SSM_HEADDIM = 64
SSM_HEADS = D_INNER // SSM_HEADDIM
SSM_GROUPS = 4
SSM_HPG = SSM_HEADS // SSM_GROUPS
D_STATE = 128
D_CONV = 4
CONV_DIM = D_INNER + 2 * SSM_GROUPS * D_STATE
GROUP_COLS = D_INNER // SSM_GROUPS

D_FF = -(-8 * D_MODEL // (3 * 256)) * 256

LANES = 128
SUBLANES = 8
HEAD_PAD = LANES
POOL_CARRY = 16
CONV_CARRY = SUBLANES
IN_AB_PAD = 10 * LANES
VMEM_LIMIT = 56 * 1024 * 1024

OFF_Q = POOL_DIM
OFF_KV = OFF_Q + Q_LORA
OFF_KPE = OFF_KV + KV_LORA


def _const_spec(shape):
    nd = len(shape)
    return pl.BlockSpec(shape, lambda *_: (0,) * nd, pipeline_mode=pl.Buffered(1))


def _rms(x, g):
    return x * lax.rsqrt(jnp.mean(x * x, axis=-1, keepdims=True) + EPS) * g


def _dot(a, b):
    return jnp.dot(a, b, preferred_element_type=F32)


def _dot_nt(a, b):
    return lax.dot_general(a, b, (((1,), (1,)), ((), ())), preferred_element_type=F32)


def _dot_tn(a, b):
    return lax.dot_general(a, b, (((0,), (0,)), ((), ())), preferred_element_type=F32)


def _split3(x):
    hi = x.astype(BF16)
    r1 = x - hi.astype(F32)
    mid = r1.astype(BF16)
    lo = (r1 - mid.astype(F32)).astype(BF16)
    return hi, mid, lo


def _dot_sel_r(x, sel):
    hi, mid, lo = _split3(x)
    return _dot(hi, sel) + _dot(mid, sel) + _dot(lo, sel)


def _dot_sel_l(sel, x):
    hi, mid, lo = _split3(x)
    return _dot(sel, hi) + _dot(sel, mid) + _dot(sel, lo)


def _ab_in_kernel(*refs, tm, pos0, n_t, emit_kv):
    (x_ref, g_ref, win_ref, qn_ref, wuq_ref, kvn_ref, wpool_ref, pscale_ref, tab_ref, hist_ref) = refs[:10]
    refs = refs[10:]
    if emit_kv:
        wuk_ref, wuv_ref, place_ref = refs[:3]
        refs = refs[3:]
    q_ref, ckv_ref, kpe_ref, pool_ref, plast_ref = refs[:5]
    refs = refs[5:]
    if emit_kv:
        k_ref, v_ref = refs[:2]
        refs = refs[2:]
    (cat_scr,) = refs

    t = pl.program_id(1)
    x = x_ref[...]
    hb = _rms(x, g_ref[...]).astype(BF16)
    proj = _dot(hb, win_ref[...])
    u_pool = proj[:, :POOL_DIM]
    q_lat = proj[:, OFF_Q:OFF_KV]
    kv_lat = proj[:, OFF_KV:OFF_KPE]
    kblk = proj[:, OFF_KPE:OFF_KPE + LANES]

    @pl.when(t == 0)
    def _():
        cat_scr[0:POOL_CARRY, :] = hist_ref[...]

    cat_scr[POOL_CARRY:POOL_CARRY + tm, :] = u_pool
    pos = pos0 + t * tm + lax.broadcasted_iota(jnp.int32, (tm, 1), 0)
    for g, w in enumerate(POOL_WINDOWS):
        cols = slice(g * POOL_GDIM, (g + 1) * POOL_GDIM)
        u_g = u_pool[:, cols]
        win = u_g
        for d in range(1, w):
            win = win + cat_scr[pl.ds(POOL_CARRY - d, tm), cols]
        cnt = jnp.minimum(pos + 1, w).astype(F32)
        pooled = win / cnt - u_g
        mixed = _dot(pooled.astype(BF16), wpool_ref[g]) * pscale_ref[:, cols]
        pool_ref[:, cols] = mixed.astype(pool_ref.dtype)
    cat_scr[0:POOL_CARRY, :] = cat_scr[tm:tm + POOL_CARRY, :]

    @pl.when(t == n_t - 1)
    def _():
        plast_ref[...] = u_pool[tm - POOL_CARRY:, :]

    tq_c = tab_ref[:, 0 * LANES:1 * LANES]
    tq_a = tab_ref[:, 1 * LANES:2 * LANES]
    tq_b = tab_ref[:, 2 * LANES:3 * LANES]
    tk_c = tab_ref[:, 3 * LANES:4 * LANES]
    tk_s = tab_ref[:, 4 * LANES:5 * LANES]

    qn = _rms(q_lat, qn_ref[...]).astype(BF16)
    q = _dot(qn, wuq_ref[...])
    for h in range(MLA_HEADS):
        qh = q[:, h * HEAD_PAD:(h + 1) * HEAD_PAD]
        qr = (qh * tq_c + pltpu.roll(qh, ROPE_HALF, 1) * tq_a
              + pltpu.roll(qh, HEAD_PAD - ROPE_HALF, 1) * tq_b)
        q_ref[h] = qr.astype(q_ref.dtype)

    ckv = _rms(kv_lat, kvn_ref[...])
    ckv_ref[...] = ckv
    kpe_blk = kblk * tk_c + pltpu.roll(kblk, LANES - QK_ROPE, 1) * tk_s
    kpe_ref[...] = kpe_blk[:, :QK_ROPE]

    if emit_kv:
        ckv_b = ckv.astype(BF16)
        k_all = _dot(ckv_b, wuk_ref[...]) + _dot(kpe_blk.astype(BF16), place_ref[...])
        for h in range(MLA_HEADS):
            k_ref[h] = k_all[:, h * HEAD_PAD:(h + 1) * HEAD_PAD].astype(k_ref.dtype)
        v_ref[...] = _dot(ckv_b, wuv_ref[...]).astype(v_ref.dtype)


def _rope_tables(pos0, s):
    pos = (pos0 + jnp.arange(s)).astype(F32)
    inv = ROPE_THETA ** (-jnp.arange(ROPE_HALF, dtype=F32) / ROPE_HALF)
    ang = pos[:, None] * inv[None, :]
    cos, sin = jnp.cos(ang), jnp.sin(ang)
    z = lambda n: jnp.zeros((s, n), F32)
    one = jnp.ones((s, QK_NOPE), F32)
    tail = HEAD_PAD - QK_NOPE - QK_ROPE
    tq_c = SM_SCALE * jnp.concatenate([one, cos, cos, z(tail)], axis=1)
    tq_a = SM_SCALE * jnp.concatenate([z(QK_NOPE + ROPE_HALF), sin, z(tail)], axis=1)
    tq_b = SM_SCALE * jnp.concatenate([z(QK_NOPE), -sin, z(ROPE_HALF + tail)], axis=1)
    tk_c = jnp.concatenate([cos, cos, z(LANES - QK_ROPE)], axis=1)
    tk_s = jnp.concatenate([-sin, sin, z(LANES - QK_ROPE)], axis=1)
    return jnp.concatenate([tq_c, tq_a, tq_b, tk_c, tk_s], axis=1)


def _prep_ab_weights(w_in, w_uq, w_uk, w_uv):
    kpe_w = w_in[:, OFF_KPE:]
    swapped = jnp.concatenate([kpe_w[:, ROPE_HALF:], kpe_w[:, :ROPE_HALF]], axis=1)
    pad = jnp.zeros((D_MODEL, IN_AB_PAD - IN_AB - QK_ROPE), w_in.dtype)
    win_p = jnp.concatenate([w_in, swapped, pad], axis=1).astype(BF16)

    tail = HEAD_PAD - QK_NOPE - QK_ROPE
    wq = w_uq.reshape(Q_LORA, MLA_HEADS, QK_NOPE + QK_ROPE)
    wq = jnp.pad(wq, ((0, 0), (0, 0), (0, tail))).reshape(Q_LORA, MLA_HEADS * HEAD_PAD).astype(BF16)
    wk = jnp.pad(w_uk, ((0, 0), (0, 0), (0, HEAD_PAD - QK_NOPE))).reshape(KV_LORA, MLA_HEADS * HEAD_PAD)
    wk = wk.astype(BF16)
    wv = w_uv.reshape(KV_LORA, MLA_OUT).astype(BF16)
    place = np.zeros((LANES, MLA_HEADS * HEAD_PAD), np.float32)
    for h in range(MLA_HEADS):
        for i in range(QK_ROPE):
            place[i, h * HEAD_PAD + QK_NOPE + i] = 1.0
    return win_p, wq, wk, wv, jnp.asarray(place, BF16)


def _ab_in(x, pos0, pool_hist, g_mix, win_p, q_norm, wq, kv_norm, w_pool, pool_scale, wk, wv, place,
           *, tm, emit_kv):
    b, s, _ = x.shape
    n_t = s // tm
    tab = _rope_tables(pos0, s)
    hist = jnp.pad(pool_hist.astype(F32), ((0, 0), (POOL_CARRY - POOL_HIST, 0), (0, 0)))
    row = lambda shape: pl.BlockSpec((None,) + shape, lambda bi, ti: (bi, ti, 0))

    in_specs = [
        row((tm, D_MODEL)),
        _const_spec((1, D_MODEL)),
        _const_spec((D_MODEL, IN_AB_PAD)),
        _const_spec((1, Q_LORA)),
        _const_spec((Q_LORA, MLA_HEADS * HEAD_PAD)),
        _const_spec((1, KV_LORA)),
        _const_spec((POOL_GROUPS, POOL_GDIM, POOL_GDIM)),
        _const_spec((1, POOL_DIM)),
        pl.BlockSpec((tm, 5 * LANES), lambda bi, ti: (ti, 0)),
        pl.BlockSpec((None, POOL_CARRY, POOL_DIM), lambda bi, ti: (bi, 0, 0)),
    ]
    args = [x, g_mix.reshape(1, -1), win_p, q_norm.reshape(1, -1), wq, kv_norm.reshape(1, -1),
            w_pool.astype(BF16), pool_scale.reshape(1, -1), tab, hist]
    if emit_kv:
        in_specs += [_const_spec((KV_LORA, MLA_HEADS * HEAD_PAD)), _const_spec((KV_LORA, MLA_OUT)),
                     _const_spec((LANES, MLA_HEADS * HEAD_PAD))]
        args += [wk, wv, place]

    head_spec = pl.BlockSpec((None, MLA_HEADS, tm, HEAD_PAD), lambda bi, ti: (bi, 0, ti, 0))
    out_shape = [
        jax.ShapeDtypeStruct((b, MLA_HEADS, s, HEAD_PAD), BF16),
        jax.ShapeDtypeStruct((b, s, KV_LORA), F32),
        jax.ShapeDtypeStruct((b, s, QK_ROPE), F32),
        jax.ShapeDtypeStruct((b, s, POOL_DIM), BF16),
        jax.ShapeDtypeStruct((b, POOL_CARRY, POOL_DIM), F32),
    ]
    out_specs = [
        head_spec,
        row((tm, KV_LORA)),
        row((tm, QK_ROPE)),
        row((tm, POOL_DIM)),
        pl.BlockSpec((None, POOL_CARRY, POOL_DIM), lambda bi, ti: (bi, 0, 0)),
    ]
    if emit_kv:
        out_shape += [jax.ShapeDtypeStruct((b, MLA_HEADS, s, HEAD_PAD), BF16),
                      jax.ShapeDtypeStruct((b, s, MLA_OUT), BF16)]
        out_specs += [head_spec, row((tm, MLA_OUT))]

    return pl.pallas_call(
        functools.partial(_ab_in_kernel, tm=tm, pos0=pos0, n_t=n_t, emit_kv=emit_kv),
        out_shape=out_shape,
        grid=(b, n_t),
        in_specs=in_specs,
        out_specs=out_specs,
        scratch_shapes=[pltpu.VMEM((POOL_CARRY + tm, POOL_DIM), F32)],
        compiler_params=pltpu.CompilerParams(dimension_semantics=("arbitrary", "arbitrary"),
                                             vmem_limit_bytes=VMEM_LIMIT),
        name="ab_in",
    )(*args)


def _flash_kernel(q_ref, k_ref, v_ref, o_ref, m_scr, l_scr, acc_scr, *, tq):
    t = pl.program_id(2)
    lane = lax.broadcasted_iota(jnp.int32, (tq, LANES), 1)

    def step(q, k, v, mask):
        s = _dot_nt(q, k)
        if mask is not None:
            s = jnp.where(mask, s, NEG)
        m_prev = m_scr[...]
        m_new = jnp.maximum(m_prev, jnp.max(s, axis=-1, keepdims=True))
        p = jnp.exp(s - m_new)
        alpha = jnp.exp(m_prev - m_new)
        l_scr[...] = alpha * l_scr[...] + jnp.sum(p, axis=-1, keepdims=True)
        acc_scr[...] = alpha * acc_scr[...] + _dot(p.astype(BF16), v)
        m_scr[...] = m_new

    outs = []
    for hh in range(2):
        q = q_ref[hh]
        m_scr[...] = jnp.full(m_scr.shape, NEG, F32)
        l_scr[...] = jnp.zeros(l_scr.shape, F32)
        acc_scr[...] = jnp.zeros(acc_scr.shape, F32)

        def body(j, carry, hh=hh, q=q):
            off = pl.multiple_of(j * tq, tq)
            step(q, k_ref[hh, pl.ds(off, tq), :], v_ref[pl.ds(off, tq), :], None)
            return carry

        lax.fori_loop(0, t, body, 0)
        off = pl.multiple_of(t * tq, tq)
        qc = lax.broadcasted_iota(jnp.int32, (tq, tq), 0) // CHUNK
        kc = lax.broadcasted_iota(jnp.int32, (tq, tq), 1) // CHUNK
        step(q, k_ref[hh, pl.ds(off, tq), :], v_ref[pl.ds(off, tq), :], kc <= qc)
        outs.append(acc_scr[...] / l_scr[...])
    o_ref[...] = jnp.where(lane < V_HEAD, outs[0], outs[1]).astype(o_ref.dtype)


def _flash_attention(q, k, v, *, tq):
    b, h, s, _ = q.shape
    return pl.pallas_call(
        functools.partial(_flash_kernel, tq=tq),
        out_shape=jax.ShapeDtypeStruct((b, s, MLA_OUT), BF16),
        grid=(b, h // 2, s // tq),
        in_specs=[
            pl.BlockSpec((None, 2, tq, HEAD_PAD), lambda bi, hp, ti: (bi, hp, ti, 0)),
            pl.BlockSpec((None, 2, s, HEAD_PAD), lambda bi, hp, ti: (bi, hp, 0, 0)),
            pl.BlockSpec((None, s, 2 * V_HEAD), lambda bi, hp, ti: (bi, 0, hp)),
        ],
        out_specs=pl.BlockSpec((None, tq, 2 * V_HEAD), lambda bi, hp, ti: (bi, ti, hp)),
        scratch_shapes=[pltpu.VMEM((tq, 1), F32), pltpu.VMEM((tq, 1), F32),
                        pltpu.VMEM((tq, 2 * V_HEAD), F32)],
        compiler_params=pltpu.CompilerParams(dimension_semantics=("arbitrary", "arbitrary", "arbitrary"),
                                             vmem_limit_bytes=VMEM_LIMIT),
        name="flash_attention",
    )(q, k, v)


def _latent_attn_kernel(q_ref, wabs_ref, cckv_ref, ckpe_ref, nckv_ref, nkpe_ref, wuv_ref, o_ref,
                        *, sq, past, pos0):
    rows = MLA_HEADS * sq
    qcat = jnp.concatenate([_dot(q_ref[h], wabs_ref[h]) for h in range(MLA_HEADS)], axis=0)
    qa = qcat[:, :KV_LORA].astype(BF16)
    qp = qcat[:, KV_LORA:].astype(BF16)
    ck = cckv_ref[...].astype(BF16)
    kp = ckpe_ref[...].astype(BF16)
    nk = nckv_ref[...].astype(BF16)
    np_ = nkpe_ref[...].astype(BF16)
    s_past = _dot_nt(qa, ck) + _dot_nt(qp, kp)
    s_new = _dot_nt(qa, nk) + _dot_nt(qp, np_)

    q_chunk = (pos0 + lax.broadcasted_iota(jnp.int32, (rows, 1), 0) % sq) // CHUNK
    kc_past = (pos0 - past + lax.broadcasted_iota(jnp.int32, (1, past), 1)) // CHUNK
    kc_new = (pos0 + lax.broadcasted_iota(jnp.int32, (1, sq), 1)) // CHUNK
    s_past = jnp.where(kc_past <= q_chunk, s_past, NEG)
    s_new = jnp.where(kc_new <= q_chunk, s_new, NEG)

    m = jnp.maximum(jnp.max(s_past, axis=-1, keepdims=True), jnp.max(s_new, axis=-1, keepdims=True))
    p_past = jnp.exp(s_past - m)
    p_new = jnp.exp(s_new - m)
    l = jnp.sum(p_past, axis=-1, keepdims=True) + jnp.sum(p_new, axis=-1, keepdims=True)
    o_lat = (_dot(p_past.astype(BF16), ck) + _dot(p_new.astype(BF16), nk)) / l
    o_lat = o_lat.astype(BF16)
    out = _dot(o_lat[0:sq], wuv_ref[0])
    for h in range(1, MLA_HEADS):
        out = out + _dot(o_lat[h * sq:(h + 1) * sq], wuv_ref[h])
    o_ref[...] = out.astype(o_ref.dtype)


def _prep_latent_weights(w_uk, w_uv):
    wabs = np.zeros((MLA_HEADS, HEAD_PAD, KV_LORA + QK_ROPE), np.float32)
    for i in range(QK_ROPE):
        wabs[:, QK_NOPE + i, KV_LORA + i] = 1.0
    wabs = jnp.asarray(wabs).at[:, :QK_NOPE, :KV_LORA].set(jnp.transpose(w_uk, (1, 2, 0)))
    wuv_place = jnp.zeros((MLA_HEADS, KV_LORA, MLA_OUT), F32)
    for h in range(MLA_HEADS):
        wuv_place = wuv_place.at[h, :, h * V_HEAD:(h + 1) * V_HEAD].set(w_uv[:, h, :])
    return wabs.astype(BF16), wuv_place.astype(BF16)


def _latent_attention(q, wabs, cache_ckv, cache_kpe, ckv_new, kpe_new, wuv_place, *, pos0):
    b, h, sq, _ = q.shape
    past = cache_ckv.shape[1]
    seq = lambda n, d: pl.BlockSpec((None, n, d), lambda bi: (bi, 0, 0))
    return pl.pallas_call(
        functools.partial(_latent_attn_kernel, sq=sq, past=past, pos0=pos0),
        out_shape=jax.ShapeDtypeStruct((b, sq, MLA_OUT), BF16),
        grid=(b,),
        in_specs=[
            pl.BlockSpec((None, h, sq, HEAD_PAD), lambda bi: (bi, 0, 0, 0)),
            _const_spec((h, HEAD_PAD, KV_LORA + QK_ROPE)),
            seq(past, KV_LORA), seq(past, QK_ROPE), seq(sq, KV_LORA), seq(sq, QK_ROPE),
            _const_spec((h, KV_LORA, MLA_OUT)),
        ],
        out_specs=seq(sq, MLA_OUT),
        compiler_params=pltpu.CompilerParams(dimension_semantics=("arbitrary",),
                                             vmem_limit_bytes=VMEM_LIMIT),
        name="latent_attention",
    )(q, wabs, cache_ckv, cache_kpe, ckv_new, kpe_new, wuv_place)


def _out_ffn_kernel(*refs, n_act, final_norm):
    x_ref = refs[0]
    act_refs = refs[1:1 + n_act]
    wout_refs = refs[1 + n_act:1 + 2 * n_act]
    gffn_ref, wg_ref, wu_ref, wd_ref = refs[1 + 2 * n_act:5 + 2 * n_act]
    rest = refs[5 + 2 * n_act:]
    if final_norm:
        gfin_ref, o_ref = rest
    else:
        (o_ref,) = rest

    y = _dot(act_refs[0][...], wout_refs[0][...])
    for a_ref, w_ref in zip(act_refs[1:], wout_refs[1:]):
        y = y + _dot(a_ref[...], w_ref[...])
    x1 = x_ref[...] + y
    h2 = _rms(x1, gffn_ref[...]).astype(BF16)
    gate = _dot(h2, wg_ref[...])
    up = _dot(h2, wu_ref[...])
    act = (gate * jax.nn.sigmoid(gate) * up).astype(BF16)
    x2 = x1 + _dot(act, wd_ref[...])
    if final_norm:
        x2 = _rms(x2, gfin_ref[...])
    o_ref[...] = x2


def _out_ffn(x, acts, wouts, g_ffn, w_gate, w_up, w_down, g_final, *, tm):
    t, _ = x.shape
    n_act = len(acts)
    row = lambda d: pl.BlockSpec((tm, d), lambda i: (i, 0))
    in_specs = ([row(D_MODEL)] + [row(a.shape[1]) for a in acts] + [_const_spec(w.shape) for w in wouts]
                + [_const_spec((1, D_MODEL)), _const_spec((D_MODEL, D_FF)), _const_spec((D_MODEL, D_FF)),
                   _const_spec((D_FF, D_MODEL))])
    args = [x, *acts, *wouts, g_ffn.reshape(1, -1), w_gate.astype(BF16), w_up.astype(BF16),
            w_down.astype(BF16)]
    if g_final is not None:
        in_specs.append(_const_spec((1, D_MODEL)))
        args.append(g_final.reshape(1, -1))
    return pl.pallas_call(
        functools.partial(_out_ffn_kernel, n_act=n_act, final_norm=g_final is not None),
        out_shape=jax.ShapeDtypeStruct((t, D_MODEL), F32),
        grid=(t // tm,),
        in_specs=in_specs,
        out_specs=row(D_MODEL),
        compiler_params=pltpu.CompilerParams(dimension_semantics=("arbitrary",),
                                             vmem_limit_bytes=VMEM_LIMIT),
        name="out_ffn",
    )(*args)


def _softplus(x):
    return jnp.maximum(x, 0.0) + jnp.log1p(jnp.exp(-jnp.abs(x)))


def _ssd_in_kernel(x_ref, g_ref, wz_ref, wxbc_ref, wdt_ref, wdtt_ref, cw_ref, cb_ref, dtb_ref, dtbt_ref,
                   hist_ref, z_ref, xs_ref, b_ref, c_ref, dt_ref, dtt_ref, clast_ref, cat_scr, *, tm, n_t):
    t = pl.program_id(1)
    hb = _rms(x_ref[...], g_ref[...]).astype(BF16)
    z_ref[...] = _dot(hb, wz_ref[...]).astype(z_ref.dtype)
    xbc = _dot(hb, wxbc_ref[...])
    dt_ref[...] = _softplus(_dot(hb, wdt_ref[...])[:, :SSM_HEADS] + dtb_ref[...])
    dtt_ref[...] = _softplus(_dot_nt(wdtt_ref[...], hb) + dtbt_ref[...])

    @pl.when(t == 0)
    def _():
        cat_scr[0:CONV_CARRY, :] = hist_ref[...]

    cat_scr[CONV_CARRY:CONV_CARRY + tm, :] = xbc
    conv = cb_ref[...] + xbc * cw_ref[D_CONV - 1:D_CONV, :]
    for k in range(D_CONV - 1):
        back = D_CONV - 1 - k
        conv = conv + cat_scr[pl.ds(CONV_CARRY - back, tm), :] * cw_ref[k:k + 1, :]
    act = conv * jax.nn.sigmoid(conv)
    gn = SSM_GROUPS * D_STATE
    xs_ref[...] = act[:, :D_INNER].astype(xs_ref.dtype)
    b_ref[...] = act[:, D_INNER:D_INNER + gn].astype(b_ref.dtype)
    c_ref[...] = act[:, D_INNER + gn:].astype(c_ref.dtype)
    cat_scr[0:CONV_CARRY, :] = cat_scr[tm:tm + CONV_CARRY, :]

    @pl.when(t == n_t - 1)
    def _():
        clast_ref[...] = xbc[tm - CONV_CARRY:, :]


def _ssd_in(x, conv_hist, g_mix, w_in, conv_w, conv_b, dt_bias, *, tm):
    b, s, _ = x.shape
    n_t = s // tm
    gn = SSM_GROUPS * D_STATE
    wz = w_in[:, :D_INNER].astype(BF16)
    wxbc = w_in[:, D_INNER:D_INNER + CONV_DIM].astype(BF16)
    wdt = w_in[:, D_INNER + CONV_DIM:]
    wdt_p = jnp.pad(wdt, ((0, 0), (0, LANES - SSM_HEADS))).astype(BF16)
    wdt_t = wdt.T.astype(BF16)
    hist = jnp.pad(conv_hist.astype(F32), ((0, 0), (CONV_CARRY - (D_CONV - 1), 0), (0, 0)))
    row = lambda d: pl.BlockSpec((None, tm, d), lambda bi, ti: (bi, ti, 0))
    return pl.pallas_call(
        functools.partial(_ssd_in_kernel, tm=tm, n_t=n_t),
        out_shape=[
            jax.ShapeDtypeStruct((b, s, D_INNER), BF16),
            jax.ShapeDtypeStruct((b, s, D_INNER), BF16),
            jax.ShapeDtypeStruct((b, s, gn), BF16),
            jax.ShapeDtypeStruct((b, s, gn), BF16),
            jax.ShapeDtypeStruct((b, s, SSM_HEADS), F32),
            jax.ShapeDtypeStruct((b, SSM_HEADS, s), F32),
            jax.ShapeDtypeStruct((b, CONV_CARRY, CONV_DIM), F32),
        ],
        grid=(b, n_t),
        in_specs=[
            row(D_MODEL),
            _const_spec((1, D_MODEL)),
            _const_spec((D_MODEL, D_INNER)),
            _const_spec((D_MODEL, CONV_DIM)),
            _const_spec((D_MODEL, LANES)),
            _const_spec((SSM_HEADS, D_MODEL)),
            _const_spec((D_CONV, CONV_DIM)),
            _const_spec((1, CONV_DIM)),
            _const_spec((1, SSM_HEADS)),
            _const_spec((SSM_HEADS, 1)),
            pl.BlockSpec((None, CONV_CARRY, CONV_DIM), lambda bi, ti: (bi, 0, 0)),
        ],
        out_specs=[
            row(D_INNER), row(D_INNER), row(gn), row(gn), row(SSM_HEADS),
            pl.BlockSpec((None, SSM_HEADS, tm), lambda bi, ti: (bi, 0, ti)),
            pl.BlockSpec((None, CONV_CARRY, CONV_DIM), lambda bi, ti: (bi, 0, 0)),
        ],
        scratch_shapes=[pltpu.VMEM((CONV_CARRY + tm, CONV_DIM), F32)],
        compiler_params=pltpu.CompilerParams(dimension_semantics=("arbitrary", "arbitrary"),
                                             vmem_limit_bytes=VMEM_LIMIT),
        name="ssd_in",
    )(x, g_mix.reshape(1, -1), wz, wxbc, wdt_p, wdt_t, conv_w, conv_b.reshape(1, -1),
      dt_bias.reshape(1, -1), dt_bias.reshape(-1, 1), hist)


def _ssd_scan_kernel(xs_ref, z_ref, b_ref, c_ref, dt_ref, dtt_ref, alog_ref, alogt_ref, dexp_ref, norm_ref,
                     expand_ref, h0_ref, y_ref, hfin_ref, h_scr, *, ln, n_c):
    c = pl.program_id(1)

    @pl.when(c == 0)
    def _():
        h_scr[...] = h0_ref[...]

    row_i = lax.broadcasted_iota(jnp.int32, (ln, ln), 0)
    col_i = lax.broadcasted_iota(jnp.int32, (ln, ln), 1)
    causal = row_i >= col_i
    tril = jnp.where(causal, 1.0, 0.0).astype(BF16)
    triu = jnp.where(row_i <= col_i, 1.0, 0.0).astype(BF16)
    lane = lax.broadcasted_iota(jnp.int32, (ln, LANES), 1)

    dt = dt_ref[...]
    dtt = dtt_ref[...]
    acum = _dot_sel_l(tril, dt * -jnp.exp(alog_ref[...]))
    acum_t = _dot_sel_r(dtt * -jnp.exp(alogt_ref[...]), triu)
    last = acum[ln - 1:ln, :]
    expand = expand_ref[...]
    dt_e = _dot_sel_r(dt, expand)
    ea_e = _dot_sel_r(jnp.exp(acum), expand)
    wj_e = _dot_sel_r(jnp.exp(last - acum) * dt, expand)

    x = xs_ref[...].astype(F32)
    xdt = (x * dt_e).astype(BF16)
    xw = (x * wj_e).astype(BF16)

    for g in range(SSM_GROUPS):
        gcols = slice(g * GROUP_COLS, (g + 1) * GROUP_COLS)
        bg = b_ref[:, g * D_STATE:(g + 1) * D_STATE]
        cg = c_ref[:, g * D_STATE:(g + 1) * D_STATE]
        cb = _dot_nt(cg, bg)
        y_state = _dot(cg, h_scr[g].astype(BF16))
        gated = []
        for p in range(SSM_HPG // 2):
            h0 = g * SSM_HPG + 2 * p
            cols = slice(h0 * SSM_HEADDIM, (h0 + 2) * SSM_HEADDIM)
            scores = []
            for h in (h0, h0 + 1):
                seg = acum[:, h:h + 1] - acum_t[h:h + 1, :]
                scores.append((cb * jnp.where(causal, jnp.exp(seg), 0.0)).astype(BF16))
            xp = xdt[:, cols]
            zero = jnp.zeros_like(xp)
            rhs = jnp.concatenate([jnp.where(lane < SSM_HEADDIM, xp, zero),
                                   jnp.where(lane >= SSM_HEADDIM, xp, zero)], axis=0)
            y = _dot(jnp.concatenate(scores, axis=1), rhs)
            y = y + y_state[:, 2 * p * SSM_HEADDIM:(2 * p + 2) * SSM_HEADDIM] * ea_e[:, cols]
            y = y + x[:, cols] * dexp_ref[:, cols]
            zc = z_ref[:, cols].astype(F32)
            gated.append(y * (zc * jax.nn.sigmoid(zc)))
        yg = jnp.concatenate(gated, axis=1)
        yg = yg * lax.rsqrt(jnp.mean(yg * yg, axis=-1, keepdims=True) + EPS)
        y_ref[:, gcols] = (yg * norm_ref[:, gcols]).astype(y_ref.dtype)
        h_scr[g] = h_scr[g] * ea_e[ln - 1:ln, gcols] + _dot_tn(bg, xw[:, gcols])

    @pl.when(c == n_c - 1)
    def _():
        hfin_ref[...] = h_scr[...]


def _ssd_scan(xs, z, bm, cm, dt, dtt, a_log, d_skip, ssm_norm, h0, *, ln):
    b, s, _ = xs.shape
    n_c = s // ln
    gn = SSM_GROUPS * D_STATE
    expand = jnp.asarray(np.kron(np.eye(SSM_HEADS, dtype=np.float32), np.ones((1, SSM_HEADDIM), np.float32)),
                         BF16)
    d_exp = jnp.repeat(d_skip.astype(F32), SSM_HEADDIM).reshape(1, D_INNER)
    h0_t = jnp.transpose(h0.astype(F32).reshape(b, SSM_GROUPS, GROUP_COLS, D_STATE), (0, 1, 3, 2))
    row = lambda d: pl.BlockSpec((None, ln, d), lambda bi, ci: (bi, ci, 0))
    state_spec = pl.BlockSpec((None, SSM_GROUPS, D_STATE, GROUP_COLS), lambda bi, ci: (bi, 0, 0, 0))
    y, h_t = pl.pallas_call(
        functools.partial(_ssd_scan_kernel, ln=ln, n_c=n_c),
        out_shape=[jax.ShapeDtypeStruct((b, s, D_INNER), BF16),
                   jax.ShapeDtypeStruct((b, SSM_GROUPS, D_STATE, GROUP_COLS), F32)],
        grid=(b, n_c),
        in_specs=[
            row(D_INNER), row(D_INNER), row(gn), row(gn), row(SSM_HEADS),
            pl.BlockSpec((None, SSM_HEADS, ln), lambda bi, ci: (bi, 0, ci)),
            _const_spec((1, SSM_HEADS)), _const_spec((SSM_HEADS, 1)),
            _const_spec((1, D_INNER)), _const_spec((1, D_INNER)),
            _const_spec((SSM_HEADS, D_INNER)),
            state_spec,
        ],
        out_specs=[row(D_INNER), state_spec],
        scratch_shapes=[pltpu.VMEM((SSM_GROUPS, D_STATE, GROUP_COLS), F32)],
        compiler_params=pltpu.CompilerParams(dimension_semantics=("arbitrary", "arbitrary"),
                                             vmem_limit_bytes=VMEM_LIMIT),
        name="ssd_scan",
    )(xs, z, bm, cm, dt, dtt, a_log.reshape(1, -1), a_log.reshape(-1, 1), d_exp, ssm_norm.reshape(1, -1),
      expand, h0_t)
    h_fin = jnp.transpose(h_t, (0, 1, 3, 2)).reshape(b, SSM_HEADS, SSM_HEADDIM, D_STATE)
    return y, h_fin


def _trunk(x, pos0, cache, pool_hist, conv_hist, ssm_hist, w, *, tm_proj, tm_ffn, tq, ln):
    b, s, _ = x.shape
    prompt = cache is None
    win_p, wq, wk, wv, place = _prep_ab_weights(w["w_in_ab"][0], w["w_uq"][0], w["w_uk"][0], w["w_uv"][0])
    outs = _ab_in(x, pos0, pool_hist, w["norm_mix"][0], win_p, w["q_norm"][0], wq, w["kv_norm"][0],
                  w["w_pool"][0], w["pool_scale"][0], wk, wv, place, tm=tm_proj, emit_kv=prompt)
    q, ckv, kpe, pool_out, pool_last = outs[:5]
    if prompt:
        k, v = outs[5:]
        attn = _flash_attention(q, k, v, tq=tq)
    else:
        wabs, wuv_place = _prep_latent_weights(w["w_uk"][0], w["w_uv"][0])
        attn = _latent_attention(q, wabs, cache[0], cache[1], ckv, kpe, wuv_place, pos0=pos0)
    w_out = w["w_out_ab"][0].astype(BF16)
    x1 = _out_ffn(x.reshape(b * s, D_MODEL),
                  [pool_out.reshape(b * s, POOL_DIM), attn.reshape(b * s, MLA_OUT)],
                  [w_out[:POOL_DIM], w_out[POOL_DIM:]],
                  w["norm_ffn"][0], w["w_gate"][0], w["w_up"][0], w["w_down"][0], None, tm=tm_ffn)

    z, xs, bm, cm, dt, dtt, conv_last = _ssd_in(x1.reshape(b, s, D_MODEL), conv_hist, w["norm_mix"][1],
                                                w["w_in_c"][0], w["conv_w"][0], w["conv_b"][0],
                                                w["dt_bias"][0], tm=tm_proj)
    y, h_fin = _ssd_scan(xs, z, bm, cm, dt, dtt, w["a_log"][0], w["d_skip"][0], w["ssm_norm"][0], ssm_hist,
                         ln=ln)
    x2 = _out_ffn(x1, [y.reshape(b * s, D_INNER)], [w["w_out_c"][0].astype(BF16)],
                  w["norm_ffn"][1], w["w_gate"][1], w["w_up"][1], w["w_down"][1], w["norm_final"], tm=tm_ffn)
    return (x2.reshape(b, s, D_MODEL), ckv[None], kpe[None], pool_last[None, :, POOL_CARRY - POOL_HIST:],
            conv_last[None, :, CONV_CARRY - (D_CONV - 1):], h_fin[None])


def kernel(x_prompt, x_sample, cache_ckv, cache_kpe, state_pool, state_conv, state_ssm, norm_mix, norm_ffn, norm_final, w_in_ab, w_pool, pool_scale, q_norm, w_uq, kv_norm, w_uk, w_uv, w_out_ab, w_in_c, conv_w, conv_b, dt_bias, a_log, d_skip, ssm_norm, w_out_c, w_gate, w_up, w_down):
    assert norm_mix.shape[0] == 2 and w_in_ab.shape[0] == 1 and w_in_c.shape[0] == 1
    w = dict(norm_mix=norm_mix, norm_ffn=norm_ffn, norm_final=norm_final, w_in_ab=w_in_ab, w_pool=w_pool,
             pool_scale=pool_scale, q_norm=q_norm, w_uq=w_uq, kv_norm=kv_norm, w_uk=w_uk, w_uv=w_uv,
             w_out_ab=w_out_ab, w_in_c=w_in_c, conv_w=conv_w, conv_b=conv_b, dt_bias=dt_bias, a_log=a_log,
             d_skip=d_skip, ssm_norm=ssm_norm, w_out_c=w_out_c, w_gate=w_gate, w_up=w_up, w_down=w_down)
    bp, sp, _ = x_prompt.shape
    bs, ss, _ = x_sample.shape
    past = cache_ckv.shape[2]

    prompt = _trunk(x_prompt, 0, None,
                    jnp.zeros((bp, POOL_HIST, POOL_DIM), F32), jnp.zeros((bp, D_CONV - 1, CONV_DIM), F32),
                    jnp.zeros((bp, SSM_HEADS, SSM_HEADDIM, D_STATE), F32), w,
                    tm_proj=512, tm_ffn=512, tq=512, ln=128)
    sample = _trunk(x_sample, past, (cache_ckv[0], cache_kpe[0]), state_pool[0], state_conv[0], state_ssm[0], w,
                    tm_proj=ss, tm_ffn=bs * ss, tq=None, ln=ss)
    return (prompt[0], sample[0]) + prompt[1:] + sample[1:]
```

```python
import functools
import math

import numpy as np
import jax
import jax.numpy as jnp
from jax import lax
from jax.experimental import pallas as pl
from jax.experimental.pallas import tpu as pltpu

F32 = jnp.float32
BF16 = jnp.bfloat16

D_MODEL = 1024
CHUNK = 64
EPS = 1e-6
NEG = -1e30

POOL_WINDOWS = (2, 4, 8, 16)
POOL_GROUPS = 4
POOL_DIM = D_MODEL // 2
POOL_GDIM = POOL_DIM // POOL_GROUPS
POOL_HIST = max(POOL_WINDOWS) - 1

MLA_HEADS = 8
QK_NOPE = 64
QK_ROPE = 32
ROPE_HALF = QK_ROPE // 2
V_HEAD = 64
Q_LORA = 384
KV_LORA = 256
MLA_OUT = MLA_HEADS * V_HEAD
ROPE_THETA = 10000.0
SM_SCALE = (QK_NOPE + QK_ROPE) ** -0.5
Q_SCALE = SM_SCALE * math.log2(math.e)
IN_AB = POOL_DIM + Q_LORA + KV_LORA + QK_ROPE

D_INNER = 2 * D_MODEL
SSM_HEADDIM = 64
SSM_HEADS = D_INNER // SSM_HEADDIM
SSM_GROUPS = 4
SSM_HPG = SSM_HEADS // SSM_GROUPS
D_STATE = 128
D_CONV = 4
CONV_DIM = D_INNER + 2 * SSM_GROUPS * D_STATE
GROUP_COLS = D_INNER // SSM_GROUPS

D_FF = -(-8 * D_MODEL // (3 * 256)) * 256

LANES = 128
SUBLANES = 8
HEAD_PAD = LANES
ROWSUM_LANE_EVEN = V_HEAD
ROWSUM_LANE_ODD = 0
POOL_CARRY = 16
CONV_CARRY = SUBLANES
IN_AB_PAD = 10 * LANES
VMEM_LIMIT = 56 * 1024 * 1024

OFF_Q = POOL_DIM
OFF_KV = OFF_Q + Q_LORA
OFF_KPE = OFF_KV + KV_LORA


def _const_spec(shape):
    nd = len(shape)
    return pl.BlockSpec(shape, lambda *_: (0,) * nd, pipeline_mode=pl.Buffered(1))


def _rms(x, g):
    return x * lax.rsqrt(jnp.mean(x * x, axis=-1, keepdims=True) + EPS) * g


def _dot(a, b):
    return jnp.dot(a, b, preferred_element_type=F32)


def _dot_nt(a, b):
    return lax.dot_general(a, b, (((1,), (1,)), ((), ())), preferred_element_type=F32)


def _dot_tn(a, b):
    return lax.dot_general(a, b, (((0,), (0,)), ((), ())), preferred_element_type=F32)


def _split3(x):
    hi = x.astype(BF16)
    r1 = x - hi.astype(F32)
    mid = r1.astype(BF16)
    lo = (r1 - mid.astype(F32)).astype(BF16)
    return hi, mid, lo


def _dot_sel_r(x, sel):
    hi, mid, lo = _split3(x)
    return _dot(hi, sel) + _dot(mid, sel) + _dot(lo, sel)


def _dot_sel_l(sel, x):
    hi, mid, lo = _split3(x)
    return _dot(sel, hi) + _dot(sel, mid) + _dot(sel, lo)


def _ab_in_kernel(*refs, tm, pos0, n_t, emit_kv):
    (x_ref, g_ref, win_ref, qn_ref, wuq_ref, kvn_ref, wpool_ref, pscale_ref, tab_ref, hist_ref) = refs[:10]
    refs = refs[10:]
    if emit_kv:
        wuk_ref, wuv_ref, place_ref = refs[:3]
        refs = refs[3:]
    q_ref, ckv_ref, kpe_ref, pool_ref, plast_ref = refs[:5]
    refs = refs[5:]
    if emit_kv:
        k_ref, v_ref = refs[:2]
        refs = refs[2:]
    (cat_scr,) = refs

    t = pl.program_id(1)
    x = x_ref[...]
    hb = _rms(x, g_ref[...]).astype(BF16)
    proj = _dot(hb, win_ref[...])
    u_pool = proj[:, :POOL_DIM]
    q_lat = proj[:, OFF_Q:OFF_KV]
    kv_lat = proj[:, OFF_KV:OFF_KPE]
    kblk = proj[:, OFF_KPE:OFF_KPE + LANES]

    @pl.when(t == 0)
    def _():
        cat_scr[0:POOL_CARRY, :] = hist_ref[...]

    cat_scr[POOL_CARRY:POOL_CARRY + tm, :] = u_pool
    pos = pos0 + t * tm + lax.broadcasted_iota(jnp.int32, (tm, 1), 0)
    for g, w in enumerate(POOL_WINDOWS):
        cols = slice(g * POOL_GDIM, (g + 1) * POOL_GDIM)
        u_g = u_pool[:, cols]
        win = u_g
        for d in range(1, w):
            win = win + cat_scr[pl.ds(POOL_CARRY - d, tm), cols]
        cnt = jnp.minimum(pos + 1, w).astype(F32)
        pooled = win / cnt - u_g
        mixed = _dot(pooled.astype(BF16), wpool_ref[g]) * pscale_ref[:, cols]
        pool_ref[:, cols] = mixed.astype(pool_ref.dtype)
    cat_scr[0:POOL_CARRY, :] = cat_scr[tm:tm + POOL_CARRY, :]

    @pl.when(t == n_t - 1)
    def _():
        plast_ref[...] = u_pool[tm - POOL_CARRY:, :]

    tq_c = tab_ref[:, 0 * LANES:1 * LANES]
    tq_a = tab_ref[:, 1 * LANES:2 * LANES]
    tq_b = tab_ref[:, 2 * LANES:3 * LANES]
    tk_c = tab_ref[:, 3 * LANES:4 * LANES]
    tk_s = tab_ref[:, 4 * LANES:5 * LANES]

    qn = _rms(q_lat, qn_ref[...]).astype(BF16)
    q = _dot(qn, wuq_ref[...])
    for h in range(MLA_HEADS):
        qh = q[:, h * HEAD_PAD:(h + 1) * HEAD_PAD]
        qr = (qh * tq_c + pltpu.roll(qh, ROPE_HALF, 1) * tq_a
              + pltpu.roll(qh, HEAD_PAD - ROPE_HALF, 1) * tq_b)
        q_ref[h] = qr.astype(q_ref.dtype)

    ckv = _rms(kv_lat, kvn_ref[...])
    ckv_ref[...] = ckv
    kpe_blk = kblk * tk_c + pltpu.roll(kblk, LANES - QK_ROPE, 1) * tk_s
    kpe_ref[...] = kpe_blk[:, :QK_ROPE]

    if emit_kv:
        ckv_b = ckv.astype(BF16)
        k_all = _dot(ckv_b, wuk_ref[...]) + _dot(kpe_blk.astype(BF16), place_ref[...])
        for h in range(MLA_HEADS):
            k_ref[h] = k_all[:, h * HEAD_PAD:(h + 1) * HEAD_PAD].astype(k_ref.dtype)
        v_all = _dot(ckv_b, wuv_ref[...])
        lane = lax.broadcasted_iota(jnp.int32, (tm, LANES), 1)
        for hp in range(MLA_HEADS // 2):
            vp = v_all[:, hp * LANES:(hp + 1) * LANES]
            even = jnp.where(lane < V_HEAD, vp, jnp.where(lane == ROWSUM_LANE_EVEN, 1.0, 0.0))
            odd = jnp.where(lane >= V_HEAD, vp, jnp.where(lane == ROWSUM_LANE_ODD, 1.0, 0.0))
            v_ref[2 * hp] = even.astype(v_ref.dtype)
            v_ref[2 * hp + 1] = odd.astype(v_ref.dtype)


def _rope_tables(pos0, s):
    pos = (pos0 + jnp.arange(s)).astype(F32)
    inv = ROPE_THETA ** (-jnp.arange(ROPE_HALF, dtype=F32) / ROPE_HALF)
    ang = pos[:, None] * inv[None, :]
    cos, sin = jnp.cos(ang), jnp.sin(ang)
    z = lambda n: jnp.zeros((s, n), F32)
    one = jnp.ones((s, QK_NOPE), F32)
    tail = HEAD_PAD - QK_NOPE - QK_ROPE
    tq_c = Q_SCALE * jnp.concatenate([one, cos, cos, z(tail)], axis=1)
    tq_a = Q_SCALE * jnp.concatenate([z(QK_NOPE + ROPE_HALF), sin, z(tail)], axis=1)
    tq_b = Q_SCALE * jnp.concatenate([z(QK_NOPE), -sin, z(ROPE_HALF + tail)], axis=1)
    tk_c = jnp.concatenate([cos, cos, z(LANES - QK_ROPE)], axis=1)
    tk_s = jnp.concatenate([-sin, sin, z(LANES - QK_ROPE)], axis=1)
    return jnp.concatenate([tq_c, tq_a, tq_b, tk_c, tk_s], axis=1)


def _prep_ab_weights(w_in, w_uq, w_uk, w_uv):
    kpe_w = w_in[:, OFF_KPE:]
    swapped = jnp.concatenate([kpe_w[:, ROPE_HALF:], kpe_w[:, :ROPE_HALF]], axis=1)
    pad = jnp.zeros((D_MODEL, IN_AB_PAD - IN_AB - QK_ROPE), w_in.dtype)
    win_p = jnp.concatenate([w_in, swapped, pad], axis=1).astype(BF16)

    tail = HEAD_PAD - QK_NOPE - QK_ROPE
    wq = w_uq.reshape(Q_LORA, MLA_HEADS, QK_NOPE + QK_ROPE)
    wq = jnp.pad(wq, ((0, 0), (0, 0), (0, tail))).reshape(Q_LORA, MLA_HEADS * HEAD_PAD).astype(BF16)
    wk = jnp.pad(w_uk, ((0, 0), (0, 0), (0, HEAD_PAD - QK_NOPE))).reshape(KV_LORA, MLA_HEADS * HEAD_PAD)
    wk = wk.astype(BF16)
    wv = w_uv.reshape(KV_LORA, MLA_OUT).astype(BF16)
    place = np.zeros((LANES, MLA_HEADS * HEAD_PAD), np.float32)
    for h in range(MLA_HEADS):
        for i in range(QK_ROPE):
            place[i, h * HEAD_PAD + QK_NOPE + i] = 1.0
    return win_p, wq, wk, wv, jnp.asarray(place, BF16)


def _ab_in(x, pos0, pool_hist, g_mix, win_p, q_norm, wq, kv_norm, w_pool, pool_scale, wk, wv, place,
           *, tm, emit_kv):
    b, s, _ = x.shape
    n_t = s // tm
    tab = _rope_tables(pos0, s)
    hist = jnp.pad(pool_hist.astype(F32), ((0, 0), (POOL_CARRY - POOL_HIST, 0), (0, 0)))
    row = lambda shape: pl.BlockSpec((None,) + shape, lambda bi, ti: (bi, ti, 0))

    in_specs = [
        row((tm, D_MODEL)),
        _const_spec((1, D_MODEL)),
        _const_spec((D_MODEL, IN_AB_PAD)),
        _const_spec((1, Q_LORA)),
        _const_spec((Q_LORA, MLA_HEADS * HEAD_PAD)),
        _const_spec((1, KV_LORA)),
        _const_spec((POOL_GROUPS, POOL_GDIM, POOL_GDIM)),
        _const_spec((1, POOL_DIM)),
        pl.BlockSpec((tm, 5 * LANES), lambda bi, ti: (ti, 0)),
        pl.BlockSpec((None, POOL_CARRY, POOL_DIM), lambda bi, ti: (bi, 0, 0)),
    ]
    args = [x, g_mix.reshape(1, -1), win_p, q_norm.reshape(1, -1), wq, kv_norm.reshape(1, -1),
            w_pool.astype(BF16), pool_scale.reshape(1, -1), tab, hist]
    if emit_kv:
        in_specs += [_const_spec((KV_LORA, MLA_HEADS * HEAD_PAD)), _const_spec((KV_LORA, MLA_OUT)),
                     _const_spec((LANES, MLA_HEADS * HEAD_PAD))]
        args += [wk, wv, place]

    head_spec = pl.BlockSpec((None, MLA_HEADS, tm, HEAD_PAD), lambda bi, ti: (bi, 0, ti, 0))
    out_shape = [
        jax.ShapeDtypeStruct((b, MLA_HEADS, s, HEAD_PAD), BF16),
        jax.ShapeDtypeStruct((b, s, KV_LORA), F32),
        jax.ShapeDtypeStruct((b, s, QK_ROPE), F32),
        jax.ShapeDtypeStruct((b, s, POOL_DIM), BF16),
        jax.ShapeDtypeStruct((b, POOL_CARRY, POOL_DIM), F32),
    ]
    out_specs = [
        head_spec,
        row((tm, KV_LORA)),
        row((tm, QK_ROPE)),
        row((tm, POOL_DIM)),
        pl.BlockSpec((None, POOL_CARRY, POOL_DIM), lambda bi, ti: (bi, 0, 0)),
    ]
    if emit_kv:
        out_shape += [jax.ShapeDtypeStruct((b, MLA_HEADS, s, HEAD_PAD), BF16),
                      jax.ShapeDtypeStruct((b, MLA_HEADS, s, LANES), BF16)]
        out_specs += [head_spec, head_spec]

    return pl.pallas_call(
        functools.partial(_ab_in_kernel, tm=tm, pos0=pos0, n_t=n_t, emit_kv=emit_kv),
        out_shape=out_shape,
        grid=(b, n_t),
        in_specs=in_specs,
        out_specs=out_specs,
        scratch_shapes=[pltpu.VMEM((POOL_CARRY + tm, POOL_DIM), F32)],
        compiler_params=pltpu.CompilerParams(dimension_semantics=("arbitrary", "arbitrary"),
                                             vmem_limit_bytes=VMEM_LIMIT),
        name="ab_in",
    )(*args)


def _flash_kernel(q_ref, k_ref, v_ref, o_ref, m_scr, acc_scr, *, tq, tk):
    t = pl.program_id(2)
    heads = (0, 1)

    def step(hh, q, k, v, mask):
        s = _dot_nt(q, k)
        if mask is not None:
            s = jnp.where(mask, s, NEG)
        m_prev = m_scr[hh]
        m_new = jnp.maximum(m_prev, jnp.max(s, axis=-1, keepdims=True))
        alpha = jnp.exp2(m_prev - m_new)
        p = jnp.concatenate([jnp.exp2(s[:, c * LANES:(c + 1) * LANES] - m_new) for c in range(tk // LANES)],
                            axis=1).astype(BF16)
        acc_scr[hh] = alpha * acc_scr[hh] + _dot(p, v)
        m_scr[hh] = m_new

    m_scr[...] = jnp.full(m_scr.shape, NEG, F32)
    acc_scr[...] = jnp.zeros(acc_scr.shape, F32)
    q = [q_ref[hh] for hh in heads]

    def body(j, carry):
        off = pl.multiple_of(j * tk, tk)
        for hh in heads:
            step(hh, q[hh], k_ref[hh, pl.ds(off, tk), :], v_ref[hh, pl.ds(off, tk), :], None)
        return carry

    n_full = (t * tq) // tk
    lax.fori_loop(0, n_full, body, 0)
    for d in range(tq // tk):
        off = pl.multiple_of(t * tq + d * tk, tk)
        qc = lax.broadcasted_iota(jnp.int32, (tq, tk), 0) // CHUNK
        kc = (d * tk + lax.broadcasted_iota(jnp.int32, (tq, tk), 1)) // CHUNK
        for hh in heads:
            step(hh, q[hh], k_ref[hh, pl.ds(off, tk), :], v_ref[hh, pl.ds(off, tk), :], kc <= qc)

    acc0, acc1 = acc_scr[0], acc_scr[1]
    out0 = acc0 / acc0[:, ROWSUM_LANE_EVEN:ROWSUM_LANE_EVEN + 1]
    out1 = acc1 / acc1[:, ROWSUM_LANE_ODD:ROWSUM_LANE_ODD + 1]
    lane = lax.broadcasted_iota(jnp.int32, (tq, LANES), 1)
    o_ref[...] = jnp.where(lane < V_HEAD, out0, out1).astype(o_ref.dtype)


def _flash_attention(q, k, v, *, tq, tk):
    b, h, s, _ = q.shape
    assert tq % tk == 0 and tk % CHUNK == 0 and s % tq == 0
    pair = lambda n: pl.BlockSpec((None, 2, n, LANES), lambda bi, hp, ti: (bi, hp, 0, 0))
    return pl.pallas_call(
        functools.partial(_flash_kernel, tq=tq, tk=tk),
        out_shape=jax.ShapeDtypeStruct((b, s, MLA_OUT), BF16),
        grid=(b, h // 2, s // tq),
        in_specs=[
            pl.BlockSpec((None, 2, tq, HEAD_PAD), lambda bi, hp, ti: (bi, hp, ti, 0)),
            pair(s),
            pair(s),
        ],
        out_specs=pl.BlockSpec((None, tq, 2 * V_HEAD), lambda bi, hp, ti: (bi, ti, hp)),
        scratch_shapes=[pltpu.VMEM((2, tq, LANES), F32), pltpu.VMEM((2, tq, LANES), F32)],
        compiler_params=pltpu.CompilerParams(dimension_semantics=("arbitrary", "arbitrary", "arbitrary"),
                                             vmem_limit_bytes=VMEM_LIMIT),
        name="flash_attention",
    )(q, k, v)


def _latent_attn_kernel(q_ref, wabs_ref, cckv_ref, ckpe_ref, nckv_ref, nkpe_ref, wuv_ref, o_ref,
                        *, sq, past, pos0):
    rows = MLA_HEADS * sq
    qcat = jnp.concatenate([_dot(q_ref[h], wabs_ref[h]) for h in range(MLA_HEADS)], axis=0)
    qa = qcat[:, :KV_LORA].astype(BF16)
    qp = qcat[:, KV_LORA:].astype(BF16)
    ck = cckv_ref[...].astype(BF16)
    kp = ckpe_ref[...].astype(BF16)
    nk = nckv_ref[...].astype(BF16)
    np_ = nkpe_ref[...].astype(BF16)
    s_past = _dot_nt(qa, ck) + _dot_nt(qp, kp)
    s_new = _dot_nt(qa, nk) + _dot_nt(qp, np_)

    q_chunk = (pos0 + lax.broadcasted_iota(jnp.int32, (rows, 1), 0) % sq) // CHUNK
    kc_past = (pos0 - past + lax.broadcasted_iota(jnp.int32, (1, past), 1)) // CHUNK
    kc_new = (pos0 + lax.broadcasted_iota(jnp.int32, (1, sq), 1)) // CHUNK
    s_past = jnp.where(kc_past <= q_chunk, s_past, NEG)
    s_new = jnp.where(kc_new <= q_chunk, s_new, NEG)

    m = jnp.maximum(jnp.max(s_past, axis=-1, keepdims=True), jnp.max(s_new, axis=-1, keepdims=True))
    p_past = jnp.exp2(s_past - m)
    p_new = jnp.exp2(s_new - m)
    l = jnp.sum(p_past, axis=-1, keepdims=True) + jnp.sum(p_new, axis=-1, keepdims=True)
    o_lat = (_dot(p_past.astype(BF16), ck) + _dot(p_new.astype(BF16), nk)) / l
    o_lat = o_lat.astype(BF16)
    out = _dot(o_lat[0:sq], wuv_ref[0])
    for h in range(1, MLA_HEADS):
        out = out + _dot(o_lat[h * sq:(h + 1) * sq], wuv_ref[h])
    o_ref[...] = out.astype(o_ref.dtype)


def _prep_latent_weights(w_uk, w_uv):
    wabs = np.zeros((MLA_HEADS, HEAD_PAD, KV_LORA + QK_ROPE), np.float32)
    for i in range(QK_ROPE):
        wabs[:, QK_NOPE + i, KV_LORA + i] = 1.0
    wabs = jnp.asarray(wabs).at[:, :QK_NOPE, :KV_LORA].set(jnp.transpose(w_uk, (1, 2, 0)))
    wuv_place = jnp.zeros((MLA_HEADS, KV_LORA, MLA_OUT), F32)
    for h in range(MLA_HEADS):
        wuv_place = wuv_place.at[h, :, h * V_HEAD:(h + 1) * V_HEAD].set(w_uv[:, h, :])
    return wabs.astype(BF16), wuv_place.astype(BF16)


def _latent_attention(q, wabs, cache_ckv, cache_kpe, ckv_new, kpe_new, wuv_place, *, pos0):
    b, h, sq, _ = q.shape
    past = cache_ckv.shape[1]
    seq = lambda n, d: pl.BlockSpec((None, n, d), lambda bi: (bi, 0, 0))
    return pl.pallas_call(
        functools.partial(_latent_attn_kernel, sq=sq, past=past, pos0=pos0),
        out_shape=jax.ShapeDtypeStruct((b, sq, MLA_OUT), BF16),
        grid=(b,),
        in_specs=[
            pl.BlockSpec((None, h, sq, HEAD_PAD), lambda bi: (bi, 0, 0, 0)),
            _const_spec((h, HEAD_PAD, KV_LORA + QK_ROPE)),
            seq(past, KV_LORA), seq(past, QK_ROPE), seq(sq, KV_LORA), seq(sq, QK_ROPE),
            _const_spec((h, KV_LORA, MLA_OUT)),
        ],
        out_specs=seq(sq, MLA_OUT),
        compiler_params=pltpu.CompilerParams(dimension_semantics=("arbitrary",),
                                             vmem_limit_bytes=VMEM_LIMIT),
        name="latent_attention",
    )(q, wabs, cache_ckv, cache_kpe, ckv_new, kpe_new, wuv_place)


def _out_ffn_kernel(*refs, n_act, final_norm):
    x_ref = refs[0]
    act_refs = refs[1:1 + n_act]
    wout_refs = refs[1 + n_act:1 + 2 * n_act]
    gffn_ref, wg_ref, wu_ref, wd_ref = refs[1 + 2 * n_act:5 + 2 * n_act]
    rest = refs[5 + 2 * n_act:]
    if final_norm:
        gfin_ref, o_ref = rest
    else:
        (o_ref,) = rest

    y = _dot(act_refs[0][...], wout_refs[0][...])
    for a_ref, w_ref in zip(act_refs[1:], wout_refs[1:]):
        y = y + _dot(a_ref[...], w_ref[...])
    x1 = x_ref[...] + y
    h2 = _rms(x1, gffn_ref[...]).astype(BF16)
    gate = _dot(h2, wg_ref[...])
    up = _dot(h2, wu_ref[...])
    act = (gate * jax.nn.sigmoid(gate) * up).astype(BF16)
    x2 = x1 + _dot(act, wd_ref[...])
    if final_norm:
        x2 = _rms(x2, gfin_ref[...])
    o_ref[...] = x2


def _out_ffn(x, acts, wouts, g_ffn, w_gate, w_up, w_down, g_final, *, tm):
    t, _ = x.shape
    n_act = len(acts)
    row = lambda d: pl.BlockSpec((tm, d), lambda i: (i, 0))
    in_specs = ([row(D_MODEL)] + [row(a.shape[1]) for a in acts] + [_const_spec(w.shape) for w in wouts]
                + [_const_spec((1, D_MODEL)), _const_spec((D_MODEL, D_FF)), _const_spec((D_MODEL, D_FF)),
                   _const_spec((D_FF, D_MODEL))])
    args = [x, *acts, *wouts, g_ffn.reshape(1, -1), w_gate.astype(BF16), w_up.astype(BF16),
            w_down.astype(BF16)]
    if g_final is not None:
        in_specs.append(_const_spec((1, D_MODEL)))
        args.append(g_final.reshape(1, -1))
    return pl.pallas_call(
        functools.partial(_out_ffn_kernel, n_act=n_act, final_norm=g_final is not None),
        out_shape=jax.ShapeDtypeStruct((t, D_MODEL), F32),
        grid=(t // tm,),
        in_specs=in_specs,
        out_specs=row(D_MODEL),
        compiler_params=pltpu.CompilerParams(dimension_semantics=("arbitrary",),
                                             vmem_limit_bytes=VMEM_LIMIT),
        name="out_ffn",
    )(*args)


def _softplus(x):
    return jnp.maximum(x, 0.0) + jnp.log1p(jnp.exp(-jnp.abs(x)))


def _ssd_in_kernel(x_ref, g_ref, wz_ref, wxbc_ref, wdt_ref, wdtt_ref, cw_ref, cb_ref, dtb_ref, dtbt_ref,
                   hist_ref, z_ref, xs_ref, b_ref, c_ref, dt_ref, dtt_ref, clast_ref, cat_scr, *, tm, n_t):
    t = pl.program_id(1)
    hb = _rms(x_ref[...], g_ref[...]).astype(BF16)
    z_ref[...] = _dot(hb, wz_ref[...]).astype(z_ref.dtype)
    xbc = _dot(hb, wxbc_ref[...])
    dt_ref[...] = _softplus(_dot(hb, wdt_ref[...])[:, :SSM_HEADS] + dtb_ref[...])
    dtt_ref[...] = _softplus(_dot_nt(wdtt_ref[...], hb) + dtbt_ref[...])

    @pl.when(t == 0)
    def _():
        cat_scr[0:CONV_CARRY, :] = hist_ref[...]

    cat_scr[CONV_CARRY:CONV_CARRY + tm, :] = xbc
    conv = cb_ref[...] + xbc * cw_ref[D_CONV - 1:D_CONV, :]
    for k in range(D_CONV - 1):
        back = D_CONV - 1 - k
        conv = conv + cat_scr[pl.ds(CONV_CARRY - back, tm), :] * cw_ref[k:k + 1, :]
    act = conv * jax.nn.sigmoid(conv)
    gn = SSM_GROUPS * D_STATE
    xs_ref[...] = act[:, :D_INNER].astype(xs_ref.dtype)
    b_ref[...] = act[:, D_INNER:D_INNER + gn].astype(b_ref.dtype)
    c_ref[...] = act[:, D_INNER + gn:].astype(c_ref.dtype)
    cat_scr[0:CONV_CARRY, :] = cat_scr[tm:tm + CONV_CARRY, :]

    @pl.when(t == n_t - 1)
    def _():
        clast_ref[...] = xbc[tm - CONV_CARRY:, :]


def _ssd_in(x, conv_hist, g_mix, w_in, conv_w, conv_b, dt_bias, *, tm):
    b, s, _ = x.shape
    n_t = s // tm
    gn = SSM_GROUPS * D_STATE
    wz = w_in[:, :D_INNER].astype(BF16)
    wxbc = w_in[:, D_INNER:D_INNER + CONV_DIM].astype(BF16)
    wdt = w_in[:, D_INNER + CONV_DIM:]
    wdt_p = jnp.pad(wdt, ((0, 0), (0, LANES - SSM_HEADS))).astype(BF16)
    wdt_t = wdt.T.astype(BF16)
    hist = jnp.pad(conv_hist.astype(F32), ((0, 0), (CONV_CARRY - (D_CONV - 1), 0), (0, 0)))
    row = lambda d: pl.BlockSpec((None, tm, d), lambda bi, ti: (bi, ti, 0))
    return pl.pallas_call(
        functools.partial(_ssd_in_kernel, tm=tm, n_t=n_t),
        out_shape=[
            jax.ShapeDtypeStruct((b, s, D_INNER), BF16),
            jax.ShapeDtypeStruct((b, s, D_INNER), BF16),
            jax.ShapeDtypeStruct((b, s, gn), BF16),
            jax.ShapeDtypeStruct((b, s, gn), BF16),
            jax.ShapeDtypeStruct((b, s, SSM_HEADS), F32),
            jax.ShapeDtypeStruct((b, SSM_HEADS, s), F32),
            jax.ShapeDtypeStruct((b, CONV_CARRY, CONV_DIM), F32),
        ],
        grid=(b, n_t),
        in_specs=[
            row(D_MODEL),
            _const_spec((1, D_MODEL)),
            _const_spec((D_MODEL, D_INNER)),
            _const_spec((D_MODEL, CONV_DIM)),
            _const_spec((D_MODEL, LANES)),
            _const_spec((SSM_HEADS, D_MODEL)),
            _const_spec((D_CONV, CONV_DIM)),
            _const_spec((1, CONV_DIM)),
            _const_spec((1, SSM_HEADS)),
            _const_spec((SSM_HEADS, 1)),
            pl.BlockSpec((None, CONV_CARRY, CONV_DIM), lambda bi, ti: (bi, 0, 0)),
        ],
        out_specs=[
            row(D_INNER), row(D_INNER), row(gn), row(gn), row(SSM_HEADS),
            pl.BlockSpec((None, SSM_HEADS, tm), lambda bi, ti: (bi, 0, ti)),
            pl.BlockSpec((None, CONV_CARRY, CONV_DIM), lambda bi, ti: (bi, 0, 0)),
        ],
        scratch_shapes=[pltpu.VMEM((CONV_CARRY + tm, CONV_DIM), F32)],
        compiler_params=pltpu.CompilerParams(dimension_semantics=("arbitrary", "arbitrary"),
                                             vmem_limit_bytes=VMEM_LIMIT),
        name="ssd_in",
    )(x, g_mix.reshape(1, -1), wz, wxbc, wdt_p, wdt_t, conv_w, conv_b.reshape(1, -1),
      dt_bias.reshape(1, -1), dt_bias.reshape(-1, 1), hist)


def _ssd_scan_kernel(xs_ref, z_ref, b_ref, c_ref, dt_ref, dtt_ref, alog_ref, alogt_ref, dexp_ref, norm_ref,
                     expand_ref, h0_ref, y_ref, hfin_ref, h_scr, *, ln, n_c):
    c = pl.program_id(1)

    @pl.when(c == 0)
    def _():
        h_scr[...] = h0_ref[...]

    row_i = lax.broadcasted_iota(jnp.int32, (ln, ln), 0)
    col_i = lax.broadcasted_iota(jnp.int32, (ln, ln), 1)
    causal = row_i >= col_i
    tril = jnp.where(causal, 1.0, 0.0).astype(BF16)
    triu = jnp.where(row_i <= col_i, 1.0, 0.0).astype(BF16)
    lane = lax.broadcasted_iota(jnp.int32, (ln, LANES), 1)

    dt = dt_ref[...]
    dtt = dtt_ref[...]
    acum = _dot_sel_l(tril, dt * -jnp.exp(alog_ref[...]))
    acum_t = _dot_sel_r(dtt * -jnp.exp(alogt_ref[...]), triu)
    last = acum[ln - 1:ln, :]
    expand = expand_ref[...]
    dt_e = _dot_sel_r(dt, expand)
    ea_e = _dot_sel_r(jnp.exp(acum), expand)
    wj_e = _dot_sel_r(jnp.exp(last - acum) * dt, expand)

    x = xs_ref[...].astype(F32)
    xdt = (x * dt_e).astype(BF16)
    xw = (x * wj_e).astype(BF16)

    for g in range(SSM_GROUPS):
        gcols = slice(g * GROUP_COLS, (g + 1) * GROUP_COLS)
        bg = b_ref[:, g * D_STATE:(g + 1) * D_STATE]
        cg = c_ref[:, g * D_STATE:(g + 1) * D_STATE]
        cb = _dot_nt(cg, bg)
        y_state = _dot(cg, h_scr[g].astype(BF16))
        gated = []
        for p in range(SSM_HPG // 2):
            h0 = g * SSM_HPG + 2 * p
            cols = slice(h0 * SSM_HEADDIM, (h0 + 2) * SSM_HEADDIM)
            scores = []
            for h in (h0, h0 + 1):
                seg = acum[:, h:h + 1] - acum_t[h:h + 1, :]
                scores.append((cb * jnp.where(causal, jnp.exp(seg), 0.0)).astype(BF16))
            xp = xdt[:, cols]
            zero = jnp.zeros_like(xp)
            rhs = jnp.concatenate([jnp.where(lane < SSM_HEADDIM, xp, zero),
                                   jnp.where(lane >= SSM_HEADDIM, xp, zero)], axis=0)
            y = _dot(jnp.concatenate(scores, axis=1), rhs)
            y = y + y_state[:, 2 * p * SSM_HEADDIM:(2 * p + 2) * SSM_HEADDIM] * ea_e[:, cols]
            y = y + x[:, cols] * dexp_ref[:, cols]
            zc = z_ref[:, cols].astype(F32)
            gated.append(y * (zc * jax.nn.sigmoid(zc)))
        yg = jnp.concatenate(gated, axis=1)
        yg = yg * lax.rsqrt(jnp.mean(yg * yg, axis=-1, keepdims=True) + EPS)
        y_ref[:, gcols] = (yg * norm_ref[:, gcols]).astype(y_ref.dtype)
        h_scr[g] = h_scr[g] * ea_e[ln - 1:ln, gcols] + _dot_tn(bg, xw[:, gcols])

    @pl.when(c == n_c - 1)
    def _():
        hfin_ref[...] = h_scr[...]


def _ssd_scan(xs, z, bm, cm, dt, dtt, a_log, d_skip, ssm_norm, h0, *, ln):
    b, s, _ = xs.shape
    n_c = s // ln
    gn = SSM_GROUPS * D_STATE
    expand = jnp.asarray(np.kron(np.eye(SSM_HEADS, dtype=np.float32), np.ones((1, SSM_HEADDIM), np.float32)),
                         BF16)
    d_exp = jnp.repeat(d_skip.astype(F32), SSM_HEADDIM).reshape(1, D_INNER)
    h0_t = jnp.transpose(h0.astype(F32).reshape(b, SSM_GROUPS, GROUP_COLS, D_STATE), (0, 1, 3, 2))
    row = lambda d: pl.BlockSpec((None, ln, d), lambda bi, ci: (bi, ci, 0))
    state_spec = pl.BlockSpec((None, SSM_GROUPS, D_STATE, GROUP_COLS), lambda bi, ci: (bi, 0, 0, 0))
    y, h_t = pl.pallas_call(
        functools.partial(_ssd_scan_kernel, ln=ln, n_c=n_c),
        out_shape=[jax.ShapeDtypeStruct((b, s, D_INNER), BF16),
                   jax.ShapeDtypeStruct((b, SSM_GROUPS, D_STATE, GROUP_COLS), F32)],
        grid=(b, n_c),
        in_specs=[
            row(D_INNER), row(D_INNER), row(gn), row(gn), row(SSM_HEADS),
            pl.BlockSpec((None, SSM_HEADS, ln), lambda bi, ci: (bi, 0, ci)),
            _const_spec((1, SSM_HEADS)), _const_spec((SSM_HEADS, 1)),
            _const_spec((1, D_INNER)), _const_spec((1, D_INNER)),
            _const_spec((SSM_HEADS, D_INNER)),
            state_spec,
        ],
        out_specs=[row(D_INNER), state_spec],
        scratch_shapes=[pltpu.VMEM((SSM_GROUPS, D_STATE, GROUP_COLS), F32)],
        compiler_params=pltpu.CompilerParams(dimension_semantics=("arbitrary", "arbitrary"),
                                             vmem_limit_bytes=VMEM_LIMIT),
        name="ssd_scan",
    )(xs, z, bm, cm, dt, dtt, a_log.reshape(1, -1), a_log.reshape(-1, 1), d_exp, ssm_norm.reshape(1, -1),
      expand, h0_t)
    h_fin = jnp.transpose(h_t, (0, 1, 3, 2)).reshape(b, SSM_HEADS, SSM_HEADDIM, D_STATE)
    return y, h_fin


def _trunk(x, pos0, cache, pool_hist, conv_hist, ssm_hist, w, *, tm_proj, tm_ffn, tq, tk, ln):
    b, s, _ = x.shape
    prompt = cache is None
    win_p, wq, wk, wv, place = _prep_ab_weights(w["w_in_ab"][0], w["w_uq"][0], w["w_uk"][0], w["w_uv"][0])
    outs = _ab_in(x, pos0, pool_hist, w["norm_mix"][0], win_p, w["q_norm"][0], wq, w["kv_norm"][0],
                  w["w_pool"][0], w["pool_scale"][0], wk, wv, place, tm=tm_proj, emit_kv=prompt)
    q, ckv, kpe, pool_out, pool_last = outs[:5]
    if prompt:
        k, v = outs[5:]
        attn = _flash_attention(q, k, v, tq=tq, tk=tk)
    else:
        wabs, wuv_place = _prep_latent_weights(w["w_uk"][0], w["w_uv"][0])
        attn = _latent_attention(q, wabs, cache[0], cache[1], ckv, kpe, wuv_place, pos0=pos0)
    w_out = w["w_out_ab"][0].astype(BF16)
    x1 = _out_ffn(x.reshape(b * s, D_MODEL),
                  [pool_out.reshape(b * s, POOL_DIM), attn.reshape(b * s, MLA_OUT)],
                  [w_out[:POOL_DIM], w_out[POOL_DIM:]],
                  w["norm_ffn"][0], w["w_gate"][0], w["w_up"][0], w["w_down"][0], None, tm=tm_ffn)

    z, xs, bm, cm, dt, dtt, conv_last = _ssd_in(x1.reshape(b, s, D_MODEL), conv_hist, w["norm_mix"][1],
                                                w["w_in_c"][0], w["conv_w"][0], w["conv_b"][0],
                                                w["dt_bias"][0], tm=tm_proj)
    y, h_fin = _ssd_scan(xs, z, bm, cm, dt, dtt, w["a_log"][0], w["d_skip"][0], w["ssm_norm"][0], ssm_hist,
                         ln=ln)
    x2 = _out_ffn(x1, [y.reshape(b * s, D_INNER)], [w["w_out_c"][0].astype(BF16)],
                  w["norm_ffn"][1], w["w_gate"][1], w["w_up"][1], w["w_down"][1], w["norm_final"], tm=tm_ffn)
    return (x2.reshape(b, s, D_MODEL), ckv[None], kpe[None], pool_last[None, :, POOL_CARRY - POOL_HIST:],
            conv_last[None, :, CONV_CARRY - (D_CONV - 1):], h_fin[None])


def kernel(x_prompt, x_sample, cache_ckv, cache_kpe, state_pool, state_conv, state_ssm, norm_mix, norm_ffn, norm_final, w_in_ab, w_pool, pool_scale, q_norm, w_uq, kv_norm, w_uk, w_uv, w_out_ab, w_in_c, conv_w, conv_b, dt_bias, a_log, d_skip, ssm_norm, w_out_c, w_gate, w_up, w_down):
    assert norm_mix.shape[0] == 2 and w_in_ab.shape[0] == 1 and w_in_c.shape[0] == 1
    w = dict(norm_mix=norm_mix, norm_ffn=norm_ffn, norm_final=norm_final, w_in_ab=w_in_ab, w_pool=w_pool,
             pool_scale=pool_scale, q_norm=q_norm, w_uq=w_uq, kv_norm=kv_norm, w_uk=w_uk, w_uv=w_uv,
             w_out_ab=w_out_ab, w_in_c=w_in_c, conv_w=conv_w, conv_b=conv_b, dt_bias=dt_bias, a_log=a_log,
             d_skip=d_skip, ssm_norm=ssm_norm, w_out_c=w_out_c, w_gate=w_gate, w_up=w_up, w_down=w_down)
    bp, sp, _ = x_prompt.shape
    bs, ss, _ = x_sample.shape
    past = cache_ckv.shape[2]

    prompt = _trunk(x_prompt, 0, None,
                    jnp.zeros((bp, POOL_HIST, POOL_DIM), F32), jnp.zeros((bp, D_CONV - 1, CONV_DIM), F32),
                    jnp.zeros((bp, SSM_HEADS, SSM_HEADDIM, D_STATE), F32), w,
                    tm_proj=512, tm_ffn=512, tq=512, tk=512, ln=128)
    sample = _trunk(x_sample, past, (cache_ckv[0], cache_kpe[0]), state_pool[0], state_conv[0], state_ssm[0], w,
                    tm_proj=ss, tm_ffn=bs * ss, tq=None, tk=None, ln=ss)
    return (prompt[0], sample[0]) + prompt[1:] + sample[1:]
```

```python
import functools
import math

import numpy as np
import jax
import jax.numpy as jnp
from jax import lax
from jax.experimental import pallas as pl
from jax.experimental.pallas import tpu as pltpu

F32 = jnp.float32
BF16 = jnp.bfloat16

D_MODEL = 1024
CHUNK = 64
EPS = 1e-6
NEG = -1e30

POOL_WINDOWS = (2, 4, 8, 16)
POOL_GROUPS = 4
POOL_DIM = D_MODEL // 2
POOL_GDIM = POOL_DIM // POOL_GROUPS
POOL_HIST = max(POOL_WINDOWS) - 1

MLA_HEADS = 8
QK_NOPE = 64
QK_ROPE = 32
ROPE_HALF = QK_ROPE // 2
V_HEAD = 64
Q_LORA = 384
KV_LORA = 256
MLA_OUT = MLA_HEADS * V_HEAD
ROPE_THETA = 10000.0
SM_SCALE = (QK_NOPE + QK_ROPE) ** -0.5
Q_SCALE = SM_SCALE * math.log2(math.e)
IN_AB = POOL_DIM + Q_LORA + KV_LORA + QK_ROPE

D_INNER = 2 * D_MODEL
SSM_HEADDIM = 64
SSM_HEADS = D_INNER // SSM_HEADDIM
SSM_GROUPS = 4
SSM_HPG = SSM_HEADS // SSM_GROUPS
D_STATE = 128
D_CONV = 4
CONV_DIM = D_INNER + 2 * SSM_GROUPS * D_STATE
GROUP_COLS = D_INNER // SSM_GROUPS

D_FF = -(-8 * D_MODEL // (3 * 256)) * 256

LANES = 128
SUBLANES = 8
HEAD_PAD = LANES
ROWSUM_LANE_EVEN = V_HEAD
ROWSUM_LANE_ODD = 0
POOL_CARRY = 16
CONV_CARRY = SUBLANES
IN_AB_PAD = 10 * LANES
VMEM_LIMIT = 56 * 1024 * 1024

OFF_Q = POOL_DIM
OFF_KV = OFF_Q + Q_LORA
OFF_KPE = OFF_KV + KV_LORA


def _const_spec(shape):
    nd = len(shape)
    return pl.BlockSpec(shape, lambda *_: (0,) * nd, pipeline_mode=pl.Buffered(1))


def _rms(x, g):
    return x * lax.rsqrt(jnp.mean(x * x, axis=-1, keepdims=True) + EPS) * g


def _dot(a, b):
    return jnp.dot(a, b, preferred_element_type=F32)


def _dot_nt(a, b):
    return lax.dot_general(a, b, (((1,), (1,)), ((), ())), preferred_element_type=F32)


def _dot_tn(a, b):
    return lax.dot_general(a, b, (((0,), (0,)), ((), ())), preferred_element_type=F32)


def _split3(x):
    hi = x.astype(BF16)
    r1 = x - hi.astype(F32)
    mid = r1.astype(BF16)
    lo = (r1 - mid.astype(F32)).astype(BF16)
    return hi, mid, lo


def _dot_sel_r(x, sel):
    hi, mid, lo = _split3(x)
    return _dot(hi, sel) + _dot(mid, sel) + _dot(lo, sel)


def _dot_sel_l(sel, x):
    hi, mid, lo = _split3(x)
    return _dot(sel, hi) + _dot(sel, mid) + _dot(sel, lo)


def _ab_in_kernel(*refs, tm, pos0, n_t, emit_kv):
    (x_ref, g_ref, win_ref, qn_ref, wuq_ref, kvn_ref, wpool_ref, pscale_ref, tab_ref, hist_ref) = refs[:10]
    refs = refs[10:]
    if emit_kv:
        wuk_ref, wuv_ref, place_ref = refs[:3]
        refs = refs[3:]
    q_ref, ckv_ref, kpe_ref, pool_ref, plast_ref = refs[:5]
    refs = refs[5:]
    if emit_kv:
        k_ref, v_ref = refs[:2]
        refs = refs[2:]
    (cat_scr,) = refs

    t = pl.program_id(1)
    x = x_ref[...]
    hb = _rms(x, g_ref[...]).astype(BF16)
    proj = _dot(hb, win_ref[...])
    u_pool = proj[:, :POOL_DIM]
    q_lat = proj[:, OFF_Q:OFF_KV]
    kv_lat = proj[:, OFF_KV:OFF_KPE]
    kblk = proj[:, OFF_KPE:OFF_KPE + LANES]

    @pl.when(t == 0)
    def _():
        cat_scr[0:POOL_CARRY, :] = hist_ref[...]

    cat_scr[POOL_CARRY:POOL_CARRY + tm, :] = u_pool
    pos = pos0 + t * tm + lax.broadcasted_iota(jnp.int32, (tm, 1), 0)
    for g, w in enumerate(POOL_WINDOWS):
        cols = slice(g * POOL_GDIM, (g + 1) * POOL_GDIM)
        u_g = u_pool[:, cols]
        win = u_g
        for d in range(1, w):
            win = win + cat_scr[pl.ds(POOL_CARRY - d, tm), cols]
        cnt = jnp.minimum(pos + 1, w).astype(F32)
        pooled = win / cnt - u_g
        mixed = _dot(pooled.astype(BF16), wpool_ref[g]) * pscale_ref[:, cols]
        pool_ref[:, cols] = mixed.astype(pool_ref.dtype)
    cat_scr[0:POOL_CARRY, :] = cat_scr[tm:tm + POOL_CARRY, :]

    @pl.when(t == n_t - 1)
    def _():
        plast_ref[...] = u_pool[tm - POOL_CARRY:, :]

    tq_c = tab_ref[:, 0 * LANES:1 * LANES]
    tq_a = tab_ref[:, 1 * LANES:2 * LANES]
    tq_b = tab_ref[:, 2 * LANES:3 * LANES]
    tk_c = tab_ref[:, 3 * LANES:4 * LANES]
    tk_s = tab_ref[:, 4 * LANES:5 * LANES]

    qn = _rms(q_lat, qn_ref[...]).astype(BF16)
    q = _dot(qn, wuq_ref[...])
    for h in range(MLA_HEADS):
        qh = q[:, h * HEAD_PAD:(h + 1) * HEAD_PAD]
        qr = (qh * tq_c + pltpu.roll(qh, ROPE_HALF, 1) * tq_a
              + pltpu.roll(qh, HEAD_PAD - ROPE_HALF, 1) * tq_b)
        q_ref[h] = qr.astype(q_ref.dtype)

    ckv = _rms(kv_lat, kvn_ref[...])
    ckv_ref[...] = ckv
    kpe_blk = kblk * tk_c + pltpu.roll(kblk, LANES - QK_ROPE, 1) * tk_s
    kpe_ref[...] = kpe_blk[:, :QK_ROPE]

    if emit_kv:
        ckv_b = ckv.astype(BF16)
        k_all = _dot(ckv_b, wuk_ref[...]) + _dot(kpe_blk.astype(BF16), place_ref[...])
        for h in range(MLA_HEADS):
            k_ref[h] = k_all[:, h * HEAD_PAD:(h + 1) * HEAD_PAD].astype(k_ref.dtype)
        v_all = _dot(ckv_b, wuv_ref[...])
        lane = lax.broadcasted_iota(jnp.int32, (tm, LANES), 1)
        for hp in range(MLA_HEADS // 2):
            vp = v_all[:, hp * LANES:(hp + 1) * LANES]
            even = jnp.where(lane < V_HEAD, vp, jnp.where(lane == ROWSUM_LANE_EVEN, 1.0, 0.0))
            odd = jnp.where(lane >= V_HEAD, vp, jnp.where(lane == ROWSUM_LANE_ODD, 1.0, 0.0))
            v_ref[2 * hp] = even.astype(v_ref.dtype)
            v_ref[2 * hp + 1] = odd.astype(v_ref.dtype)


def _rope_tables(pos0, s):
    pos = (pos0 + jnp.arange(s)).astype(F32)
    inv = ROPE_THETA ** (-jnp.arange(ROPE_HALF, dtype=F32) / ROPE_HALF)
    ang = pos[:, None] * inv[None, :]
    cos, sin = jnp.cos(ang), jnp.sin(ang)
    z = lambda n: jnp.zeros((s, n), F32)
    one = jnp.ones((s, QK_NOPE), F32)
    tail = HEAD_PAD - QK_NOPE - QK_ROPE
    tq_c = Q_SCALE * jnp.concatenate([one, cos, cos, z(tail)], axis=1)
    tq_a = Q_SCALE * jnp.concatenate([z(QK_NOPE + ROPE_HALF), sin, z(tail)], axis=1)
    tq_b = Q_SCALE * jnp.concatenate([z(QK_NOPE), -sin, z(ROPE_HALF + tail)], axis=1)
    tk_c = jnp.concatenate([cos, cos, z(LANES - QK_ROPE)], axis=1)
    tk_s = jnp.concatenate([-sin, sin, z(LANES - QK_ROPE)], axis=1)
    return jnp.concatenate([tq_c, tq_a, tq_b, tk_c, tk_s], axis=1)


def _prep_ab_weights(w_in, w_uq, w_uk, w_uv):
    kpe_w = w_in[:, OFF_KPE:]
    swapped = jnp.concatenate([kpe_w[:, ROPE_HALF:], kpe_w[:, :ROPE_HALF]], axis=1)
    pad = jnp.zeros((D_MODEL, IN_AB_PAD - IN_AB - QK_ROPE), w_in.dtype)
    win_p = jnp.concatenate([w_in, swapped, pad], axis=1).astype(BF16)

    tail = HEAD_PAD - QK_NOPE - QK_ROPE
    wq = w_uq.reshape(Q_LORA, MLA_HEADS, QK_NOPE + QK_ROPE)
    wq = jnp.pad(wq, ((0, 0), (0, 0), (0, tail))).reshape(Q_LORA, MLA_HEADS * HEAD_PAD).astype(BF16)
    wk = jnp.pad(w_uk, ((0, 0), (0, 0), (0, HEAD_PAD - QK_NOPE))).reshape(KV_LORA, MLA_HEADS * HEAD_PAD)
    wk = wk.astype(BF16)
    wv = w_uv.reshape(KV_LORA, MLA_OUT).astype(BF16)
    place = np.zeros((LANES, MLA_HEADS * HEAD_PAD), np.float32)
    for h in range(MLA_HEADS):
        for i in range(QK_ROPE):
            place[i, h * HEAD_PAD + QK_NOPE + i] = 1.0
    return win_p, wq, wk, wv, jnp.asarray(place, BF16)


def _ab_in(x, pos0, pool_hist, g_mix, win_p, q_norm, wq, kv_norm, w_pool, pool_scale, wk, wv, place,
           *, tm, emit_kv):
    b, s, _ = x.shape
    n_t = s // tm
    tab = _rope_tables(pos0, s)
    hist = jnp.pad(pool_hist.astype(F32), ((0, 0), (POOL_CARRY - POOL_HIST, 0), (0, 0)))
    row = lambda shape: pl.BlockSpec((None,) + shape, lambda bi, ti: (bi, ti, 0))

    in_specs = [
        row((tm, D_MODEL)),
        _const_spec((1, D_MODEL)),
        _const_spec((D_MODEL, IN_AB_PAD)),
        _const_spec((1, Q_LORA)),
        _const_spec((Q_LORA, MLA_HEADS * HEAD_PAD)),
        _const_spec((1, KV_LORA)),
        _const_spec((POOL_GROUPS, POOL_GDIM, POOL_GDIM)),
        _const_spec((1, POOL_DIM)),
        pl.BlockSpec((tm, 5 * LANES), lambda bi, ti: (ti, 0)),
        pl.BlockSpec((None, POOL_CARRY, POOL_DIM), lambda bi, ti: (bi, 0, 0)),
    ]
    args = [x, g_mix.reshape(1, -1), win_p, q_norm.reshape(1, -1), wq, kv_norm.reshape(1, -1),
            w_pool.astype(BF16), pool_scale.reshape(1, -1), tab, hist]
    if emit_kv:
        in_specs += [_const_spec((KV_LORA, MLA_HEADS * HEAD_PAD)), _const_spec((KV_LORA, MLA_OUT)),
                     _const_spec((LANES, MLA_HEADS * HEAD_PAD))]
        args += [wk, wv, place]

    head_spec = pl.BlockSpec((None, MLA_HEADS, tm, HEAD_PAD), lambda bi, ti: (bi, 0, ti, 0))
    out_shape = [
        jax.ShapeDtypeStruct((b, MLA_HEADS, s, HEAD_PAD), BF16),
        jax.ShapeDtypeStruct((b, s, KV_LORA), F32),
        jax.ShapeDtypeStruct((b, s, QK_ROPE), F32),
        jax.ShapeDtypeStruct((b, s, POOL_DIM), BF16),
        jax.ShapeDtypeStruct((b, POOL_CARRY, POOL_DIM), F32),
    ]
    out_specs = [
        head_spec,
        row((tm, KV_LORA)),
        row((tm, QK_ROPE)),
        row((tm, POOL_DIM)),
        pl.BlockSpec((None, POOL_CARRY, POOL_DIM), lambda bi, ti: (bi, 0, 0)),
    ]
    if emit_kv:
        out_shape += [jax.ShapeDtypeStruct((b, MLA_HEADS, s, HEAD_PAD), BF16),
                      jax.ShapeDtypeStruct((b, MLA_HEADS, s, LANES), BF16)]
        out_specs += [head_spec, head_spec]

    return pl.pallas_call(
        functools.partial(_ab_in_kernel, tm=tm, pos0=pos0, n_t=n_t, emit_kv=emit_kv),
        out_shape=out_shape,
        grid=(b, n_t),
        in_specs=in_specs,
        out_specs=out_specs,
        scratch_shapes=[pltpu.VMEM((POOL_CARRY + tm, POOL_DIM), F32)],
        compiler_params=pltpu.CompilerParams(dimension_semantics=("arbitrary", "arbitrary"),
                                             vmem_limit_bytes=VMEM_LIMIT),
        name="ab_in",
    )(*args)


def _flash_kernel(q_ref, k_ref, v_ref, o_ref, m_scr, acc_scr, *, tq):
    t = pl.program_id(2)
    heads = (0, 1)

    def group(blocks):
        s = {}
        for n, (blk, masked) in enumerate(blocks):
            off = pl.multiple_of(blk * tq, tq)
            for hh in heads:
                sh = _dot_nt(q_ref[hh], k_ref[hh, pl.ds(off, tq), :])
                if masked:
                    qc = lax.broadcasted_iota(jnp.int32, (tq, tq), 0) // CHUNK
                    kc = lax.broadcasted_iota(jnp.int32, (tq, tq), 1) // CHUNK
                    sh = jnp.where(kc <= qc, sh, NEG)
                s[n, hh] = sh
        for n, (blk, _) in enumerate(blocks):
            off = pl.multiple_of(blk * tq, tq)
            for hh in heads:
                sh = s[n, hh]
                m_prev = m_scr[hh]
                m_new = jnp.maximum(m_prev, jnp.max(sh, axis=-1, keepdims=True))
                alpha = jnp.exp2(m_prev - m_new)
                p = jnp.concatenate([jnp.exp2(sh[:, c * LANES:(c + 1) * LANES] - m_new)
                                     for c in range(tq // LANES)], axis=1).astype(BF16)
                acc_scr[hh] = alpha * acc_scr[hh] + _dot(p, v_ref[hh, pl.ds(off, tq), :])
                m_scr[hh] = m_new

    m_scr[...] = jnp.full(m_scr.shape, NEG, F32)
    acc_scr[...] = jnp.zeros(acc_scr.shape, F32)

    def body(i, carry):
        group([(2 * i, False), (2 * i + 1, False)])
        return carry

    lax.fori_loop(0, t // 2, body, 0)

    @pl.when(t % 2 == 1)
    def _():
        group([(t - 1, False), (t, True)])

    @pl.when(t % 2 == 0)
    def _():
        group([(t, True)])

    acc0, acc1 = acc_scr[0], acc_scr[1]
    out0 = acc0 / acc0[:, ROWSUM_LANE_EVEN:ROWSUM_LANE_EVEN + 1]
    out1 = acc1 / acc1[:, ROWSUM_LANE_ODD:ROWSUM_LANE_ODD + 1]
    lane = lax.broadcasted_iota(jnp.int32, (tq, LANES), 1)
    o_ref[...] = jnp.where(lane < V_HEAD, out0, out1).astype(o_ref.dtype)


def _flash_attention(q, k, v, *, tq):
    b, h, s, _ = q.shape
    assert tq % CHUNK == 0 and s % tq == 0
    pair = lambda n: pl.BlockSpec((None, 2, n, LANES), lambda bi, hp, ti: (bi, hp, 0, 0))
    return pl.pallas_call(
        functools.partial(_flash_kernel, tq=tq),
        out_shape=jax.ShapeDtypeStruct((b, s, MLA_OUT), BF16),
        grid=(b, h // 2, s // tq),
        in_specs=[
            pl.BlockSpec((None, 2, tq, HEAD_PAD), lambda bi, hp, ti: (bi, hp, ti, 0)),
            pair(s),
            pair(s),
        ],
        out_specs=pl.BlockSpec((None, tq, 2 * V_HEAD), lambda bi, hp, ti: (bi, ti, hp)),
        scratch_shapes=[pltpu.VMEM((2, tq, LANES), F32), pltpu.VMEM((2, tq, LANES), F32)],
        compiler_params=pltpu.CompilerParams(dimension_semantics=("arbitrary", "arbitrary", "arbitrary"),
                                             vmem_limit_bytes=VMEM_LIMIT),
        name="flash_attention",
    )(q, k, v)


def _latent_attn_kernel(q_ref, wabs_ref, cckv_ref, ckpe_ref, nckv_ref, nkpe_ref, wuv_ref, o_ref,
                        *, sq, past, pos0):
    rows = MLA_HEADS * sq
    qcat = jnp.concatenate([_dot(q_ref[h], wabs_ref[h]) for h in range(MLA_HEADS)], axis=0)
    qa = qcat[:, :KV_LORA].astype(BF16)
    qp = qcat[:, KV_LORA:].astype(BF16)
    ck = cckv_ref[...].astype(BF16)
    kp = ckpe_ref[...].astype(BF16)
    nk = nckv_ref[...].astype(BF16)
    np_ = nkpe_ref[...].astype(BF16)
    s_past = _dot_nt(qa, ck) + _dot_nt(qp, kp)
    s_new = _dot_nt(qa, nk) + _dot_nt(qp, np_)

    q_chunk = (pos0 + lax.broadcasted_iota(jnp.int32, (rows, 1), 0) % sq) // CHUNK
    kc_past = (pos0 - past + lax.broadcasted_iota(jnp.int32, (1, past), 1)) // CHUNK
    kc_new = (pos0 + lax.broadcasted_iota(jnp.int32, (1, sq), 1)) // CHUNK
    s_past = jnp.where(kc_past <= q_chunk, s_past, NEG)
    s_new = jnp.where(kc_new <= q_chunk, s_new, NEG)

    m = jnp.maximum(jnp.max(s_past, axis=-1, keepdims=True), jnp.max(s_new, axis=-1, keepdims=True))
    p_past = jnp.exp2(s_past - m)
    p_new = jnp.exp2(s_new - m)
    l = jnp.sum(p_past, axis=-1, keepdims=True) + jnp.sum(p_new, axis=-1, keepdims=True)
    o_lat = (_dot(p_past.astype(BF16), ck) + _dot(p_new.astype(BF16), nk)) / l
    o_lat = o_lat.astype(BF16)
    out = _dot(o_lat[0:sq], wuv_ref[0])
    for h in range(1, MLA_HEADS):
        out = out + _dot(o_lat[h * sq:(h + 1) * sq], wuv_ref[h])
    o_ref[...] = out.astype(o_ref.dtype)


def _prep_latent_weights(w_uk, w_uv):
    wabs = np.zeros((MLA_HEADS, HEAD_PAD, KV_LORA + QK_ROPE), np.float32)
    for i in range(QK_ROPE):
        wabs[:, QK_NOPE + i, KV_LORA + i] = 1.0
    wabs = jnp.asarray(wabs).at[:, :QK_NOPE, :KV_LORA].set(jnp.transpose(w_uk, (1, 2, 0)))
    wuv_place = jnp.zeros((MLA_HEADS, KV_LORA, MLA_OUT), F32)
    for h in range(MLA_HEADS):
        wuv_place = wuv_place.at[h, :, h * V_HEAD:(h + 1) * V_HEAD].set(w_uv[:, h, :])
    return wabs.astype(BF16), wuv_place.astype(BF16)


def _latent_attention(q, wabs, cache_ckv, cache_kpe, ckv_new, kpe_new, wuv_place, *, pos0):
    b, h, sq, _ = q.shape
    past = cache_ckv.shape[1]
    seq = lambda n, d: pl.BlockSpec((None, n, d), lambda bi: (bi, 0, 0))
    return pl.pallas_call(
        functools.partial(_latent_attn_kernel, sq=sq, past=past, pos0=pos0),
        out_shape=jax.ShapeDtypeStruct((b, sq, MLA_OUT), BF16),
        grid=(b,),
        in_specs=[
            pl.BlockSpec((None, h, sq, HEAD_PAD), lambda bi: (bi, 0, 0, 0)),
            _const_spec((h, HEAD_PAD, KV_LORA + QK_ROPE)),
            seq(past, KV_LORA), seq(past, QK_ROPE), seq(sq, KV_LORA), seq(sq, QK_ROPE),
            _const_spec((h, KV_LORA, MLA_OUT)),
        ],
        out_specs=seq(sq, MLA_OUT),
        compiler_params=pltpu.CompilerParams(dimension_semantics=("arbitrary",),
                                             vmem_limit_bytes=VMEM_LIMIT),
        name="latent_attention",
    )(q, wabs, cache_ckv, cache_kpe, ckv_new, kpe_new, wuv_place)


def _out_ffn_kernel(*refs, n_act, final_norm):
    x_ref = refs[0]
    act_refs = refs[1:1 + n_act]
    wout_refs = refs[1 + n_act:1 + 2 * n_act]
    gffn_ref, wg_ref, wu_ref, wd_ref = refs[1 + 2 * n_act:5 + 2 * n_act]
    rest = refs[5 + 2 * n_act:]
    if final_norm:
        gfin_ref, o_ref = rest
    else:
        (o_ref,) = rest

    y = _dot(act_refs[0][...], wout_refs[0][...])
    for a_ref, w_ref in zip(act_refs[1:], wout_refs[1:]):
        y = y + _dot(a_ref[...], w_ref[...])
    x1 = x_ref[...] + y
    h2 = _rms(x1, gffn_ref[...]).astype(BF16)
    gate = _dot(h2, wg_ref[...])
    up = _dot(h2, wu_ref[...])
    act = (gate * jax.nn.sigmoid(gate) * up).astype(BF16)
    x2 = x1 + _dot(act, wd_ref[...])
    if final_norm:
        x2 = _rms(x2, gfin_ref[...])
    o_ref[...] = x2


def _out_ffn(x, acts, wouts, g_ffn, w_gate, w_up, w_down, g_final, *, tm):
    t, _ = x.shape
    n_act = len(acts)
    row = lambda d: pl.BlockSpec((tm, d), lambda i: (i, 0))
    in_specs = ([row(D_MODEL)] + [row(a.shape[1]) for a in acts] + [_const_spec(w.shape) for w in wouts]
                + [_const_spec((1, D_MODEL)), _const_spec((D_MODEL, D_FF)), _const_spec((D_MODEL, D_FF)),
                   _const_spec((D_FF, D_MODEL))])
    args = [x, *acts, *wouts, g_ffn.reshape(1, -1), w_gate.astype(BF16), w_up.astype(BF16),
            w_down.astype(BF16)]
    if g_final is not None:
        in_specs.append(_const_spec((1, D_MODEL)))
        args.append(g_final.reshape(1, -1))
    return pl.pallas_call(
        functools.partial(_out_ffn_kernel, n_act=n_act, final_norm=g_final is not None),
        out_shape=jax.ShapeDtypeStruct((t, D_MODEL), F32),
        grid=(t // tm,),
        in_specs=in_specs,
        out_specs=row(D_MODEL),
        compiler_params=pltpu.CompilerParams(dimension_semantics=("arbitrary",),
                                             vmem_limit_bytes=VMEM_LIMIT),
        name="out_ffn",
    )(*args)


def _softplus(x):
    return jnp.maximum(x, 0.0) + jnp.log1p(jnp.exp(-jnp.abs(x)))


def _ssd_in_kernel(x_ref, g_ref, wz_ref, wxbc_ref, wdt_ref, wdtt_ref, cw_ref, cb_ref, dtb_ref, dtbt_ref,
                   hist_ref, z_ref, xs_ref, b_ref, c_ref, dt_ref, dtt_ref, clast_ref, cat_scr, *, tm, n_t):
    t = pl.program_id(1)
    hb = _rms(x_ref[...], g_ref[...]).astype(BF16)
    z_ref[...] = _dot(hb, wz_ref[...]).astype(z_ref.dtype)
    xbc = _dot(hb, wxbc_ref[...])
    dt_ref[...] = _softplus(_dot(hb, wdt_ref[...])[:, :SSM_HEADS] + dtb_ref[...])
    dtt_ref[...] = _softplus(_dot_nt(wdtt_ref[...], hb) + dtbt_ref[...])

    @pl.when(t == 0)
    def _():
        cat_scr[0:CONV_CARRY, :] = hist_ref[...]

    cat_scr[CONV_CARRY:CONV_CARRY + tm, :] = xbc
    conv = cb_ref[...] + xbc * cw_ref[D_CONV - 1:D_CONV, :]
    for k in range(D_CONV - 1):
        back = D_CONV - 1 - k
        conv = conv + cat_scr[pl.ds(CONV_CARRY - back, tm), :] * cw_ref[k:k + 1, :]
    act = conv * jax.nn.sigmoid(conv)
    gn = SSM_GROUPS * D_STATE
    xs_ref[...] = act[:, :D_INNER].astype(xs_ref.dtype)
    b_ref[...] = act[:, D_INNER:D_INNER + gn].astype(b_ref.dtype)
    c_ref[...] = act[:, D_INNER + gn:].astype(c_ref.dtype)
    cat_scr[0:CONV_CARRY, :] = cat_scr[tm:tm + CONV_CARRY, :]

    @pl.when(t == n_t - 1)
    def _():
        clast_ref[...] = xbc[tm - CONV_CARRY:, :]


def _ssd_in(x, conv_hist, g_mix, w_in, conv_w, conv_b, dt_bias, *, tm):
    b, s, _ = x.shape
    n_t = s // tm
    gn = SSM_GROUPS * D_STATE
    wz = w_in[:, :D_INNER].astype(BF16)
    wxbc = w_in[:, D_INNER:D_INNER + CONV_DIM].astype(BF16)
    wdt = w_in[:, D_INNER + CONV_DIM:]
    wdt_p = jnp.pad(wdt, ((0, 0), (0, LANES - SSM_HEADS))).astype(BF16)
    wdt_t = wdt.T.astype(BF16)
    hist = jnp.pad(conv_hist.astype(F32), ((0, 0), (CONV_CARRY - (D_CONV - 1), 0), (0, 0)))
    row = lambda d: pl.BlockSpec((None, tm, d), lambda bi, ti: (bi, ti, 0))
    return pl.pallas_call(
        functools.partial(_ssd_in_kernel, tm=tm, n_t=n_t),
        out_shape=[
            jax.ShapeDtypeStruct((b, s, D_INNER), BF16),
            jax.ShapeDtypeStruct((b, s, D_INNER), BF16),
            jax.ShapeDtypeStruct((b, s, gn), BF16),
            jax.ShapeDtypeStruct((b, s, gn), BF16),
            jax.ShapeDtypeStruct((b, s, SSM_HEADS), F32),
            jax.ShapeDtypeStruct((b, SSM_HEADS, s), F32),
            jax.ShapeDtypeStruct((b, CONV_CARRY, CONV_DIM), F32),
        ],
        grid=(b, n_t),
        in_specs=[
            row(D_MODEL),
            _const_spec((1, D_MODEL)),
            _const_spec((D_MODEL, D_INNER)),
            _const_spec((D_MODEL, CONV_DIM)),
            _const_spec((D_MODEL, LANES)),
            _const_spec((SSM_HEADS, D_MODEL)),
            _const_spec((D_CONV, CONV_DIM)),
            _const_spec((1, CONV_DIM)),
            _const_spec((1, SSM_HEADS)),
            _const_spec((SSM_HEADS, 1)),
            pl.BlockSpec((None, CONV_CARRY, CONV_DIM), lambda bi, ti: (bi, 0, 0)),
        ],
        out_specs=[
            row(D_INNER), row(D_INNER), row(gn), row(gn), row(SSM_HEADS),
            pl.BlockSpec((None, SSM_HEADS, tm), lambda bi, ti: (bi, 0, ti)),
            pl.BlockSpec((None, CONV_CARRY, CONV_DIM), lambda bi, ti: (bi, 0, 0)),
        ],
        scratch_shapes=[pltpu.VMEM((CONV_CARRY + tm, CONV_DIM), F32)],
        compiler_params=pltpu.CompilerParams(dimension_semantics=("arbitrary", "arbitrary"),
                                             vmem_limit_bytes=VMEM_LIMIT),
        name="ssd_in",
    )(x, g_mix.reshape(1, -1), wz, wxbc, wdt_p, wdt_t, conv_w, conv_b.reshape(1, -1),
      dt_bias.reshape(1, -1), dt_bias.reshape(-1, 1), hist)


def _ssd_scan_kernel(xs_ref, z_ref, b_ref, c_ref, dt_ref, dtt_ref, alog_ref, alogt_ref, dexp_ref, norm_ref,
                     expand_ref, h0_ref, y_ref, hfin_ref, h_scr, *, ln, n_c):
    c = pl.program_id(1)

    @pl.when(c == 0)
    def _():
        h_scr[...] = h0_ref[...]

    row_i = lax.broadcasted_iota(jnp.int32, (ln, ln), 0)
    col_i = lax.broadcasted_iota(jnp.int32, (ln, ln), 1)
    causal = row_i >= col_i
    tril = jnp.where(causal, 1.0, 0.0).astype(BF16)
    triu = jnp.where(row_i <= col_i, 1.0, 0.0).astype(BF16)
    lane = lax.broadcasted_iota(jnp.int32, (ln, LANES), 1)

    dt = dt_ref[...]
    dtt = dtt_ref[...]
    acum = _dot_sel_l(tril, dt * -jnp.exp(alog_ref[...]))
    acum_t = _dot_sel_r(dtt * -jnp.exp(alogt_ref[...]), triu)
    last = acum[ln - 1:ln, :]
    expand = expand_ref[...]
    dt_e = _dot_sel_r(dt, expand)
    ea_e = _dot_sel_r(jnp.exp(acum), expand)
    wj_e = _dot_sel_r(jnp.exp(last - acum) * dt, expand)

    x = xs_ref[...].astype(F32)
    xdt = (x * dt_e).astype(BF16)
    xw = (x * wj_e).astype(BF16)

    for g in range(SSM_GROUPS):
        gcols = slice(g * GROUP_COLS, (g + 1) * GROUP_COLS)
        bg = b_ref[:, g * D_STATE:(g + 1) * D_STATE]
        cg = c_ref[:, g * D_STATE:(g + 1) * D_STATE]
        cb = _dot_nt(cg, bg)
        y_state = _dot(cg, h_scr[g].astype(BF16))
        gated = []
        for p in range(SSM_HPG // 2):
            h0 = g * SSM_HPG + 2 * p
            cols = slice(h0 * SSM_HEADDIM, (h0 + 2) * SSM_HEADDIM)
            scores = []
            for h in (h0, h0 + 1):
                seg = acum[:, h:h + 1] - acum_t[h:h + 1, :]
                scores.append((cb * jnp.where(causal, jnp.exp(seg), 0.0)).astype(BF16))
            xp = xdt[:, cols]
            zero = jnp.zeros_like(xp)
            rhs = jnp.concatenate([jnp.where(lane < SSM_HEADDIM, xp, zero),
                                   jnp.where(lane >= SSM_HEADDIM, xp, zero)], axis=0)
            y = _dot(jnp.concatenate(scores, axis=1), rhs)
            y = y + y_state[:, 2 * p * SSM_HEADDIM:(2 * p + 2) * SSM_HEADDIM] * ea_e[:, cols]
            y = y + x[:, cols] * dexp_ref[:, cols]
            zc = z_ref[:, cols].astype(F32)
            gated.append(y * (zc * jax.nn.sigmoid(zc)))
        yg = jnp.concatenate(gated, axis=1)
        yg = yg * lax.rsqrt(jnp.mean(yg * yg, axis=-1, keepdims=True) + EPS)
        y_ref[:, gcols] = (yg * norm_ref[:, gcols]).astype(y_ref.dtype)
        h_scr[g] = h_scr[g] * ea_e[ln - 1:ln, gcols] + _dot_tn(bg, xw[:, gcols])

    @pl.when(c == n_c - 1)
    def _():
        hfin_ref[...] = h_scr[...]


def _ssd_scan(xs, z, bm, cm, dt, dtt, a_log, d_skip, ssm_norm, h0, *, ln):
    b, s, _ = xs.shape
    n_c = s // ln
    gn = SSM_GROUPS * D_STATE
    expand = jnp.asarray(np.kron(np.eye(SSM_HEADS, dtype=np.float32), np.ones((1, SSM_HEADDIM), np.float32)),
                         BF16)
    d_exp = jnp.repeat(d_skip.astype(F32), SSM_HEADDIM).reshape(1, D_INNER)
    h0_t = jnp.transpose(h0.astype(F32).reshape(b, SSM_GROUPS, GROUP_COLS, D_STATE), (0, 1, 3, 2))
    row = lambda d: pl.BlockSpec((None, ln, d), lambda bi, ci: (bi, ci, 0))
    state_spec = pl.BlockSpec((None, SSM_GROUPS, D_STATE, GROUP_COLS), lambda bi, ci: (bi, 0, 0, 0))
    y, h_t = pl.pallas_call(
        functools.partial(_ssd_scan_kernel, ln=ln, n_c=n_c),
        out_shape=[jax.ShapeDtypeStruct((b, s, D_INNER), BF16),
                   jax.ShapeDtypeStruct((b, SSM_GROUPS, D_STATE, GROUP_COLS), F32)],
        grid=(b, n_c),
        in_specs=[
            row(D_INNER), row(D_INNER), row(gn), row(gn), row(SSM_HEADS),
            pl.BlockSpec((None, SSM_HEADS, ln), lambda bi, ci: (bi, 0, ci)),
            _const_spec((1, SSM_HEADS)), _const_spec((SSM_HEADS, 1)),
            _const_spec((1, D_INNER)), _const_spec((1, D_INNER)),
            _const_spec((SSM_HEADS, D_INNER)),
            state_spec,
        ],
        out_specs=[row(D_INNER), state_spec],
        scratch_shapes=[pltpu.VMEM((SSM_GROUPS, D_STATE, GROUP_COLS), F32)],
        compiler_params=pltpu.CompilerParams(dimension_semantics=("arbitrary", "arbitrary"),
                                             vmem_limit_bytes=VMEM_LIMIT),
        name="ssd_scan",
    )(xs, z, bm, cm, dt, dtt, a_log.reshape(1, -1), a_log.reshape(-1, 1), d_exp, ssm_norm.reshape(1, -1),
      expand, h0_t)
    h_fin = jnp.transpose(h_t, (0, 1, 3, 2)).reshape(b, SSM_HEADS, SSM_HEADDIM, D_STATE)
    return y, h_fin


def _trunk(x, pos0, cache, pool_hist, conv_hist, ssm_hist, w, *, tm_proj, tm_ffn, tq, ln):
    b, s, _ = x.shape
    prompt = cache is None
    win_p, wq, wk, wv, place = _prep_ab_weights(w["w_in_ab"][0], w["w_uq"][0], w["w_uk"][0], w["w_uv"][0])
    outs = _ab_in(x, pos0, pool_hist, w["norm_mix"][0], win_p, w["q_norm"][0], wq, w["kv_norm"][0],
                  w["w_pool"][0], w["pool_scale"][0], wk, wv, place, tm=tm_proj, emit_kv=prompt)
    q, ckv, kpe, pool_out, pool_last = outs[:5]
    if prompt:
        k, v = outs[5:]
        attn = _flash_attention(q, k, v, tq=tq)
    else:
        wabs, wuv_place = _prep_latent_weights(w["w_uk"][0], w["w_uv"][0])
        attn = _latent_attention(q, wabs, cache[0], cache[1], ckv, kpe, wuv_place, pos0=pos0)
    w_out = w["w_out_ab"][0].astype(BF16)
    x1 = _out_ffn(x.reshape(b * s, D_MODEL),
                  [pool_out.reshape(b * s, POOL_DIM), attn.reshape(b * s, MLA_OUT)],
                  [w_out[:POOL_DIM], w_out[POOL_DIM:]],
                  w["norm_ffn"][0], w["w_gate"][0], w["w_up"][0], w["w_down"][0], None, tm=tm_ffn)

    z, xs, bm, cm, dt, dtt, conv_last = _ssd_in(x1.reshape(b, s, D_MODEL), conv_hist, w["norm_mix"][1],
                                                w["w_in_c"][0], w["conv_w"][0], w["conv_b"][0],
                                                w["dt_bias"][0], tm=tm_proj)
    y, h_fin = _ssd_scan(xs, z, bm, cm, dt, dtt, w["a_log"][0], w["d_skip"][0], w["ssm_norm"][0], ssm_hist,
                         ln=ln)
    x2 = _out_ffn(x1, [y.reshape(b * s, D_INNER)], [w["w_out_c"][0].astype(BF16)],
                  w["norm_ffn"][1], w["w_gate"][1], w["w_up"][1], w["w_down"][1], w["norm_final"], tm=tm_ffn)
    return (x2.reshape(b, s, D_MODEL), ckv[None], kpe[None], pool_last[None, :, POOL_CARRY - POOL_HIST:],
            conv_last[None, :, CONV_CARRY - (D_CONV - 1):], h_fin[None])


def kernel(x_prompt, x_sample, cache_ckv, cache_kpe, state_pool, state_conv, state_ssm, norm_mix, norm_ffn, norm_final, w_in_ab, w_pool, pool_scale, q_norm, w_uq, kv_norm, w_uk, w_uv, w_out_ab, w_in_c, conv_w, conv_b, dt_bias, a_log, d_skip, ssm_norm, w_out_c, w_gate, w_up, w_down):
    assert norm_mix.shape[0] == 2 and w_in_ab.shape[0] == 1 and w_in_c.shape[0] == 1
    w = dict(norm_mix=norm_mix, norm_ffn=norm_ffn, norm_final=norm_final, w_in_ab=w_in_ab, w_pool=w_pool,
             pool_scale=pool_scale, q_norm=q_norm, w_uq=w_uq, kv_norm=kv_norm, w_uk=w_uk, w_uv=w_uv,
             w_out_ab=w_out_ab, w_in_c=w_in_c, conv_w=conv_w, conv_b=conv_b, dt_bias=dt_bias, a_log=a_log,
             d_skip=d_skip, ssm_norm=ssm_norm, w_out_c=w_out_c, w_gate=w_gate, w_up=w_up, w_down=w_down)
    bp, sp, _ = x_prompt.shape
    bs, ss, _ = x_sample.shape
    past = cache_ckv.shape[2]

    prompt = _trunk(x_prompt, 0, None,
                    jnp.zeros((bp, POOL_HIST, POOL_DIM), F32), jnp.zeros((bp, D_CONV - 1, CONV_DIM), F32),
                    jnp.zeros((bp, SSM_HEADS, SSM_HEADDIM, D_STATE), F32), w,
                    tm_proj=512, tm_ffn=512, tq=512, ln=128)
    sample = _trunk(x_sample, past, (cache_ckv[0], cache_kpe[0]), state_pool[0], state_conv[0], state_ssm[0], w,
                    tm_proj=ss, tm_ffn=bs * ss, tq=None, ln=ss)
    return (prompt[0], sample[0]) + prompt[1:] + sample[1:]
```

```python
import functools
import math

import numpy as np
import jax
import jax.numpy as jnp
from jax import lax
from jax.experimental import pallas as pl
from jax.experimental.pallas import tpu as pltpu

F32 = jnp.float32
BF16 = jnp.bfloat16

D_MODEL = 1024
CHUNK = 64
EPS = 1e-6
NEG = -1e30

POOL_WINDOWS = (2, 4, 8, 16)
POOL_GROUPS = 4
POOL_DIM = D_MODEL // 2
POOL_GDIM = POOL_DIM // POOL_GROUPS
POOL_HIST = max(POOL_WINDOWS) - 1

MLA_HEADS = 8
QK_NOPE = 64
QK_ROPE = 32
ROPE_HALF = QK_ROPE // 2
V_HEAD = 64
Q_LORA = 384
KV_LORA = 256
MLA_OUT = MLA_HEADS * V_HEAD
ROPE_THETA = 10000.0
SM_SCALE = (QK_NOPE + QK_ROPE) ** -0.5
LOG2E = math.log2(math.e)
Q_SCALE = SM_SCALE * LOG2E
IN_AB = POOL_DIM + Q_LORA + KV_LORA + QK_ROPE

D_INNER = 2 * D_MODEL
SSM_HEADDIM = 64
SSM_HEADS = D_INNER // SSM_HEADDIM
SSM_GROUPS = 4
SSM_HPG = SSM_HEADS // SSM_GROUPS
D_STATE = 128
D_CONV = 4
CONV_DIM = D_INNER + 2 * SSM_GROUPS * D_STATE
GROUP_COLS = D_INNER // SSM_GROUPS

D_FF = -(-8 * D_MODEL // (3 * 256)) * 256

LANES = 128
SUBLANES = 8
HEAD_PAD = LANES
ROWSUM_LANE_EVEN = V_HEAD
ROWSUM_LANE_ODD = 0
POOL_CARRY = 4 * SUBLANES
POOL_KEEP = 2 * SUBLANES
CONV_CARRY = 2 * SUBLANES
CONV_CHUNK = 512
IN_AB_PAD = 10 * LANES
VMEM_LIMIT = 56 * 1024 * 1024

OFF_Q = POOL_DIM
OFF_KV = OFF_Q + Q_LORA
OFF_KPE = OFF_KV + KV_LORA


def _const_spec(shape):
    nd = len(shape)
    return pl.BlockSpec(shape, lambda *_: (0,) * nd, pipeline_mode=pl.Buffered(1))


def _rms(x, g):
    return x * lax.rsqrt(jnp.mean(x * x, axis=-1, keepdims=True) + EPS) * g


def _dot(a, b):
    return jnp.dot(a, b, preferred_element_type=F32)


def _dot_nt(a, b):
    return lax.dot_general(a, b, (((1,), (1,)), ((), ())), preferred_element_type=F32)


def _dot_tn(a, b):
    return lax.dot_general(a, b, (((0,), (0,)), ((), ())), preferred_element_type=F32)


def _split3(x):
    hi = x.astype(BF16)
    r1 = x - hi.astype(F32)
    mid = r1.astype(BF16)
    lo = (r1 - mid.astype(F32)).astype(BF16)
    return hi, mid, lo


def _dot_sel_r(x, sel):
    hi, mid, lo = _split3(x)
    return _dot(hi, sel) + _dot(mid, sel) + _dot(lo, sel)


def _dot_sel_l(sel, x):
    hi, mid, lo = _split3(x)
    return _dot(sel, hi) + _dot(sel, mid) + _dot(sel, lo)


def _ab_in_kernel(*refs, tm, pos0, n_t, emit_kv):
    (x_ref, g_ref, win_ref, qn_ref, wuq_ref, kvn_ref, wpool_ref, pscale_ref, tab_ref, hist_ref) = refs[:10]
    refs = refs[10:]
    if emit_kv:
        wuk_ref, wuv_ref, place_ref = refs[:3]
        refs = refs[3:]
    q_ref, ckv_ref, kpe_ref, pool_ref, plast_ref = refs[:5]
    refs = refs[5:]
    if emit_kv:
        k_ref, v_ref = refs[:2]
        refs = refs[2:]
    (cat_scr,) = refs

    t = pl.program_id(1)
    x = x_ref[...]
    hb = _rms(x, g_ref[...]).astype(BF16)
    proj = _dot(hb, win_ref[...])
    u_pool = proj[:, :POOL_DIM]
    q_lat = proj[:, OFF_Q:OFF_KV]
    kv_lat = proj[:, OFF_KV:OFF_KPE]
    kblk = proj[:, OFF_KPE:OFF_KPE + LANES]

    @pl.when(t == 0)
    def _():
        cat_scr[...] = hist_ref[...]

    ext2d = jnp.concatenate([cat_scr[...], u_pool], axis=0)
    ext = ext2d.reshape((POOL_CARRY + tm) // SUBLANES, SUBLANES, POOL_DIM)
    lead = POOL_CARRY // SUBLANES
    pos = pos0 + t * tm + lax.broadcasted_iota(jnp.int32, (tm, 1), 0)
    for g, w in enumerate(POOL_WINDOWS):
        cols = slice(g * POOL_GDIM, (g + 1) * POOL_GDIM)
        win, span, dropped = ext[:, :, cols], 1, 0
        while span < w:
            prev = win[:-1] if span == SUBLANES else _shift_rows(win, span)
            win, span, dropped = win[1:] + prev, 2 * span, dropped + 1
        win = win[lead - dropped:].reshape(tm, POOL_GDIM)
        u_g = u_pool[:, cols]
        cnt = jnp.minimum(pos + 1, w).astype(F32)
        pooled = win / cnt - u_g
        mixed = _dot(pooled.astype(BF16), wpool_ref[g]) * pscale_ref[:, cols]
        pool_ref[:, cols] = mixed.astype(pool_ref.dtype)
    cat_scr[...] = ext2d[tm:, :]
    plast_ref[...] = ext2d[POOL_CARRY + tm - POOL_KEEP:, :]

    tab = tab_ref[...]
    up = pltpu.roll(tab, ROPE_HALF, 1)
    down = pltpu.roll(tab, LANES - ROPE_HALF, 1)
    ln_i = lax.broadcasted_iota(jnp.int32, (tm, LANES), 1)
    x1_q, x2_q = QK_NOPE, QK_NOPE + ROPE_HALF
    tq_c = jnp.where(ln_i < x1_q, 1.0, jnp.where(ln_i < x2_q, tab, jnp.where(ln_i < x2_q + ROPE_HALF, up, 0.0)))
    tq_a = jnp.where((ln_i >= x2_q) & (ln_i < x2_q + ROPE_HALF), tab, 0.0)
    tq_b = jnp.where((ln_i >= x1_q) & (ln_i < x2_q), -down, 0.0)
    tk_c = jnp.where(ln_i < ROPE_HALF, tab, jnp.where(ln_i < QK_ROPE, up, 0.0))
    tk_s = jnp.where(ln_i < ROPE_HALF, -down, jnp.where(ln_i < QK_ROPE, tab, 0.0))

    qn = _rms(q_lat, qn_ref[...]).astype(BF16)
    q = _dot(qn, wuq_ref[...])
    for h in range(MLA_HEADS):
        qh = q[:, h * HEAD_PAD:(h + 1) * HEAD_PAD]
        qr = (qh * tq_c + pltpu.roll(qh, ROPE_HALF, 1) * tq_a
              + pltpu.roll(qh, HEAD_PAD - ROPE_HALF, 1) * tq_b)
        q_ref[h] = qr.astype(q_ref.dtype)

    ckv = _rms(kv_lat, kvn_ref[...])
    ckv_ref[...] = ckv
    kpe_blk = kblk * tk_c + pltpu.roll(kblk, LANES - QK_ROPE, 1) * tk_s
    kpe_ref[...] = kpe_blk[:, :QK_ROPE]

    if emit_kv:
        ckv_b = ckv.astype(BF16)
        k_all = _dot(ckv_b, wuk_ref[...]) + _dot(kpe_blk.astype(BF16), place_ref[...])
        for h in range(MLA_HEADS):
            k_ref[h] = k_all[:, h * HEAD_PAD:(h + 1) * HEAD_PAD].astype(k_ref.dtype)
        v_all = _dot(ckv_b, wuv_ref[...])
        lane = lax.broadcasted_iota(jnp.int32, (tm, LANES), 1)
        for hp in range(MLA_HEADS // 2):
            vp = v_all[:, hp * LANES:(hp + 1) * LANES]
            even = jnp.where(lane < V_HEAD, vp, jnp.where(lane == ROWSUM_LANE_EVEN, 1.0, 0.0))
            odd = jnp.where(lane >= V_HEAD, vp, jnp.where(lane == ROWSUM_LANE_ODD, 1.0, 0.0))
            v_ref[2 * hp] = even.astype(v_ref.dtype)
            v_ref[2 * hp + 1] = odd.astype(v_ref.dtype)


def _rope_table(pos0, s):
    pos = (pos0 + jnp.arange(s)).astype(F32)
    inv = ROPE_THETA ** (-jnp.arange(ROPE_HALF, dtype=F32) / ROPE_HALF)
    ang = pos[:, None] * inv[None, :]
    return jnp.tile(jnp.concatenate([jnp.cos(ang), jnp.sin(ang)], axis=1), (1, LANES // QK_ROPE))


def _prep_ab_weights(w_in, w_uq, w_uk, w_uv):
    kpe_w = w_in[:, OFF_KPE:]
    swapped = jnp.concatenate([kpe_w[:, ROPE_HALF:], kpe_w[:, :ROPE_HALF]], axis=1)
    pad = jnp.zeros((D_MODEL, IN_AB_PAD - IN_AB - QK_ROPE), w_in.dtype)
    win_p = jnp.concatenate([w_in, swapped, pad], axis=1).astype(BF16)

    tail = HEAD_PAD - QK_NOPE - QK_ROPE
    wq = w_uq.reshape(Q_LORA, MLA_HEADS, QK_NOPE + QK_ROPE)
    wq = jnp.pad(wq, ((0, 0), (0, 0), (0, tail))).reshape(Q_LORA, MLA_HEADS * HEAD_PAD).astype(BF16)
    wk = jnp.pad(w_uk, ((0, 0), (0, 0), (0, HEAD_PAD - QK_NOPE))).reshape(KV_LORA, MLA_HEADS * HEAD_PAD)
    wk = wk.astype(BF16)
    wv = w_uv.reshape(KV_LORA, MLA_OUT).astype(BF16)
    place = np.zeros((LANES, MLA_HEADS * HEAD_PAD), np.float32)
    for h in range(MLA_HEADS):
        for i in range(QK_ROPE):
            place[i, h * HEAD_PAD + QK_NOPE + i] = 1.0
    return win_p, wq, wk, wv, jnp.asarray(place, BF16)


def _ab_in(x, pos0, pool_hist, g_mix, win_p, q_norm, wq, kv_norm, w_pool, pool_scale, wk, wv, place,
           *, tm, emit_kv):
    b, s, _ = x.shape
    n_t = s // tm
    tab = _rope_table(pos0, s)
    hist = jnp.pad(pool_hist.astype(F32), ((0, 0), (POOL_CARRY - POOL_HIST, 0), (0, 0)))
    row = lambda shape: pl.BlockSpec((None,) + shape, lambda bi, ti: (bi, ti, 0))

    in_specs = [
        row((tm, D_MODEL)),
        _const_spec((1, D_MODEL)),
        _const_spec((D_MODEL, IN_AB_PAD)),
        _const_spec((1, Q_LORA)),
        _const_spec((Q_LORA, MLA_HEADS * HEAD_PAD)),
        _const_spec((1, KV_LORA)),
        _const_spec((POOL_GROUPS, POOL_GDIM, POOL_GDIM)),
        _const_spec((1, POOL_DIM)),
        pl.BlockSpec((tm, LANES), lambda bi, ti: (ti, 0)),
        pl.BlockSpec((None, POOL_CARRY, POOL_DIM), lambda bi, ti: (bi, 0, 0)),
    ]
    args = [x, g_mix.reshape(1, -1), win_p, (q_norm * Q_SCALE).reshape(1, -1), wq, kv_norm.reshape(1, -1),
            w_pool.astype(BF16), pool_scale.reshape(1, -1), tab, hist]
    if emit_kv:
        in_specs += [_const_spec((KV_LORA, MLA_HEADS * HEAD_PAD)), _const_spec((KV_LORA, MLA_OUT)),
                     _const_spec((LANES, MLA_HEADS * HEAD_PAD))]
        args += [wk, wv, place]

    head_spec = pl.BlockSpec((None, MLA_HEADS, tm, HEAD_PAD), lambda bi, ti: (bi, 0, ti, 0))
    out_shape = [
        jax.ShapeDtypeStruct((b, MLA_HEADS, s, HEAD_PAD), BF16),
        jax.ShapeDtypeStruct((b, s, KV_LORA), F32),
        jax.ShapeDtypeStruct((b, s, QK_ROPE), F32),
        jax.ShapeDtypeStruct((b, s, POOL_DIM), BF16),
        jax.ShapeDtypeStruct((b, POOL_KEEP, POOL_DIM), F32),
    ]
    out_specs = [
        head_spec,
        row((tm, KV_LORA)),
        row((tm, QK_ROPE)),
        row((tm, POOL_DIM)),
        pl.BlockSpec((None, POOL_KEEP, POOL_DIM), lambda bi, ti: (bi, 0, 0)),
    ]
    if emit_kv:
        out_shape += [jax.ShapeDtypeStruct((b, MLA_HEADS, s, HEAD_PAD), BF16),
                      jax.ShapeDtypeStruct((b, MLA_HEADS, s, LANES), BF16)]
        out_specs += [head_spec, head_spec]

    return pl.pallas_call(
        functools.partial(_ab_in_kernel, tm=tm, pos0=pos0, n_t=n_t, emit_kv=emit_kv),
        out_shape=out_shape,
        grid=(b, n_t),
        in_specs=in_specs,
        out_specs=out_specs,
        scratch_shapes=[pltpu.VMEM((POOL_CARRY, POOL_DIM), F32)],
        compiler_params=pltpu.CompilerParams(dimension_semantics=("arbitrary", "arbitrary"),
                                             vmem_limit_bytes=VMEM_LIMIT),
        name="ab_in",
    )(*args)


def _flash_kernel(q_ref, k_ref, v_ref, o_ref, m_scr, acc_scr, *, tq):
    t = pl.program_id(2)
    heads = (0, 1)

    def group(blocks):
        s = {}
        for n, (blk, masked) in enumerate(blocks):
            off = pl.multiple_of(blk * tq, tq)
            for hh in heads:
                sh = _dot_nt(q_ref[hh], k_ref[hh, pl.ds(off, tq), :])
                if masked:
                    qc = lax.broadcasted_iota(jnp.int32, (tq, tq), 0) // CHUNK
                    kc = lax.broadcasted_iota(jnp.int32, (tq, tq), 1) // CHUNK
                    sh = jnp.where(kc <= qc, sh, NEG)
                s[n, hh] = sh
        for n, (blk, _) in enumerate(blocks):
            off = pl.multiple_of(blk * tq, tq)
            for hh in heads:
                sh = s[n, hh]
                m_prev = m_scr[hh]
                m_new = jnp.maximum(m_prev, jnp.max(sh, axis=-1, keepdims=True))
                alpha = jnp.exp2(m_prev - m_new)
                p = jnp.concatenate([jnp.exp2(sh[:, c * LANES:(c + 1) * LANES] - m_new)
                                     for c in range(tq // LANES)], axis=1).astype(BF16)
                acc_scr[hh] = alpha * acc_scr[hh] + _dot(p, v_ref[hh, pl.ds(off, tq), :])
                m_scr[hh] = m_new

    m_scr[...] = jnp.full(m_scr.shape, NEG, F32)
    acc_scr[...] = jnp.zeros(acc_scr.shape, F32)

    def body(i, carry):
        group([(2 * i, False), (2 * i + 1, False)])
        return carry

    lax.fori_loop(0, t // 2, body, 0)

    @pl.when(t % 2 == 1)
    def _():
        group([(t - 1, False), (t, True)])

    @pl.when(t % 2 == 0)
    def _():
        group([(t, True)])

    acc0, acc1 = acc_scr[0], acc_scr[1]
    out0 = acc0 / acc0[:, ROWSUM_LANE_EVEN:ROWSUM_LANE_EVEN + 1]
    out1 = acc1 / acc1[:, ROWSUM_LANE_ODD:ROWSUM_LANE_ODD + 1]
    lane = lax.broadcasted_iota(jnp.int32, (tq, LANES), 1)
    o_ref[...] = jnp.where(lane < V_HEAD, out0, out1).astype(o_ref.dtype)


def _flash_attention(q, k, v, *, tq):
    b, h, s, _ = q.shape
    assert tq % CHUNK == 0 and s % tq == 0
    pair = lambda n: pl.BlockSpec((None, 2, n, LANES), lambda bi, hp, ti: (bi, hp, 0, 0))
    return pl.pallas_call(
        functools.partial(_flash_kernel, tq=tq),
        out_shape=jax.ShapeDtypeStruct((b, s, MLA_OUT), BF16),
        grid=(b, h // 2, s // tq),
        in_specs=[
            pl.BlockSpec((None, 2, tq, HEAD_PAD), lambda bi, hp, ti: (bi, hp, ti, 0)),
            pair(s),
            pair(s),
        ],
        out_specs=pl.BlockSpec((None, tq, 2 * V_HEAD), lambda bi, hp, ti: (bi, ti, hp)),
        scratch_shapes=[pltpu.VMEM((2, tq, LANES), F32), pltpu.VMEM((2, tq, LANES), F32)],
        compiler_params=pltpu.CompilerParams(dimension_semantics=("arbitrary", "arbitrary", "arbitrary"),
                                             vmem_limit_bytes=VMEM_LIMIT),
        name="flash_attention",
    )(q, k, v)


def _latent_attn_kernel(q_ref, wabs_ref, cckv_ref, ckpe_ref, nckv_ref, nkpe_ref, wuv_ref, o_ref,
                        *, sq, past, pos0):
    rows = MLA_HEADS * sq
    qcat = jnp.concatenate([_dot(q_ref[h], wabs_ref[h]) for h in range(MLA_HEADS)], axis=0)
    qa = qcat[:, :KV_LORA].astype(BF16)
    qp = qcat[:, KV_LORA:].astype(BF16)
    ck = cckv_ref[...].astype(BF16)
    kp = ckpe_ref[...].astype(BF16)
    nk = nckv_ref[...].astype(BF16)
    np_ = nkpe_ref[...].astype(BF16)
    s_past = _dot_nt(qa, ck) + _dot_nt(qp, kp)
    s_new = _dot_nt(qa, nk) + _dot_nt(qp, np_)

    q_chunk = (pos0 + lax.broadcasted_iota(jnp.int32, (rows, 1), 0) % sq) // CHUNK
    kc_past = (pos0 - past + lax.broadcasted_iota(jnp.int32, (1, past), 1)) // CHUNK
    kc_new = (pos0 + lax.broadcasted_iota(jnp.int32, (1, sq), 1)) // CHUNK
    s_past = jnp.where(kc_past <= q_chunk, s_past, NEG)
    s_new = jnp.where(kc_new <= q_chunk, s_new, NEG)

    m = jnp.maximum(jnp.max(s_past, axis=-1, keepdims=True), jnp.max(s_new, axis=-1, keepdims=True))
    p_past = jnp.exp2(s_past - m)
    p_new = jnp.exp2(s_new - m)
    l = jnp.sum(p_past, axis=-1, keepdims=True) + jnp.sum(p_new, axis=-1, keepdims=True)
    o_lat = (_dot(p_past.astype(BF16), ck) + _dot(p_new.astype(BF16), nk)) / l
    o_lat = o_lat.astype(BF16)
    out = _dot(o_lat[0:sq], wuv_ref[0])
    for h in range(1, MLA_HEADS):
        out = out + _dot(o_lat[h * sq:(h + 1) * sq], wuv_ref[h])
    o_ref[...] = out.astype(o_ref.dtype)


def _prep_latent_weights(w_uk, w_uv):
    wabs = np.zeros((MLA_HEADS, HEAD_PAD, KV_LORA + QK_ROPE), np.float32)
    for i in range(QK_ROPE):
        wabs[:, QK_NOPE + i, KV_LORA + i] = 1.0
    wabs = jnp.asarray(wabs).at[:, :QK_NOPE, :KV_LORA].set(jnp.transpose(w_uk, (1, 2, 0)))
    wuv_place = jnp.zeros((MLA_HEADS, KV_LORA, MLA_OUT), F32)
    for h in range(MLA_HEADS):
        wuv_place = wuv_place.at[h, :, h * V_HEAD:(h + 1) * V_HEAD].set(w_uv[:, h, :])
    return wabs.astype(BF16), wuv_place.astype(BF16)


def _latent_attention(q, wabs, cache_ckv, cache_kpe, ckv_new, kpe_new, wuv_place, *, pos0):
    b, h, sq, _ = q.shape
    past = cache_ckv.shape[1]
    seq = lambda n, d: pl.BlockSpec((None, n, d), lambda bi: (bi, 0, 0))
    return pl.pallas_call(
        functools.partial(_latent_attn_kernel, sq=sq, past=past, pos0=pos0),
        out_shape=jax.ShapeDtypeStruct((b, sq, MLA_OUT), BF16),
        grid=(b,),
        in_specs=[
            pl.BlockSpec((None, h, sq, HEAD_PAD), lambda bi: (bi, 0, 0, 0)),
            _const_spec((h, HEAD_PAD, KV_LORA + QK_ROPE)),
            seq(past, KV_LORA), seq(past, QK_ROPE), seq(sq, KV_LORA), seq(sq, QK_ROPE),
            _const_spec((h, KV_LORA, MLA_OUT)),
        ],
        out_specs=seq(sq, MLA_OUT),
        compiler_params=pltpu.CompilerParams(dimension_semantics=("arbitrary",),
                                             vmem_limit_bytes=VMEM_LIMIT),
        name="latent_attention",
    )(q, wabs, cache_ckv, cache_kpe, ckv_new, kpe_new, wuv_place)


def _out_ffn_kernel(*refs, n_act, final_norm):
    x_ref = refs[0]
    act_refs = refs[1:1 + n_act]
    wout_refs = refs[1 + n_act:1 + 2 * n_act]
    gffn_ref, wg_ref, wu_ref, wd_ref = refs[1 + 2 * n_act:5 + 2 * n_act]
    rest = refs[5 + 2 * n_act:]
    if final_norm:
        gfin_ref, o_ref = rest
    else:
        (o_ref,) = rest

    y = _dot(act_refs[0][...], wout_refs[0][...])
    for a_ref, w_ref in zip(act_refs[1:], wout_refs[1:]):
        y = y + _dot(a_ref[...], w_ref[...])
    x1 = x_ref[...] + y
    h2 = _rms(x1, gffn_ref[...]).astype(BF16)
    gate = _dot(h2, wg_ref[...])
    up = _dot(h2, wu_ref[...])
    act = (gate * jax.nn.sigmoid(gate) * up).astype(BF16)
    x2 = x1 + _dot(act, wd_ref[...])
    if final_norm:
        x2 = _rms(x2, gfin_ref[...])
    o_ref[...] = x2


def _out_ffn(x, acts, wouts, g_ffn, w_gate, w_up, w_down, g_final, *, tm):
    t, _ = x.shape
    n_act = len(acts)
    row = lambda d: pl.BlockSpec((tm, d), lambda i: (i, 0))
    in_specs = ([row(D_MODEL)] + [row(a.shape[1]) for a in acts] + [_const_spec(w.shape) for w in wouts]
                + [_const_spec((1, D_MODEL)), _const_spec((D_MODEL, D_FF)), _const_spec((D_MODEL, D_FF)),
                   _const_spec((D_FF, D_MODEL))])
    args = [x, *acts, *wouts, g_ffn.reshape(1, -1), w_gate.astype(BF16), w_up.astype(BF16),
            w_down.astype(BF16)]
    if g_final is not None:
        in_specs.append(_const_spec((1, D_MODEL)))
        args.append(g_final.reshape(1, -1))
    return pl.pallas_call(
        functools.partial(_out_ffn_kernel, n_act=n_act, final_norm=g_final is not None),
        out_shape=jax.ShapeDtypeStruct((t, D_MODEL), F32),
        grid=(t // tm,),
        in_specs=in_specs,
        out_specs=row(D_MODEL),
        compiler_params=pltpu.CompilerParams(dimension_semantics=("arbitrary",),
                                             vmem_limit_bytes=VMEM_LIMIT),
        name="out_ffn",
    )(*args)


def _softplus(x):
    return jnp.maximum(x, 0.0) + jnp.log1p(jnp.exp(-jnp.abs(x)))


def _shift_rows(v, k):
    rot = pltpu.roll(v, k, axis=1)
    sub = lax.broadcasted_iota(jnp.int32, (1,) + v.shape[1:], 1)
    return jnp.where(sub < k, rot[:-1], rot[1:])


def _ssd_in_kernel(x_ref, g_ref, wz_ref, wxbc_ref, wdt_ref, wdtt_ref, cw_ref, cb_ref, dtb_ref, dtbt_ref,
                   hist_ref, z_ref, xs_ref, b_ref, c_ref, dt_ref, dtt_ref, clast_ref, carry_scr, *, tm, n_t):
    t = pl.program_id(1)
    hb = _rms(x_ref[...], g_ref[...]).astype(BF16)
    dt_ref[...] = _softplus(_dot(hb, wdt_ref[...])[:, :SSM_HEADS] + dtb_ref[...])
    dtt_ref[...] = _softplus(_dot_nt(wdtt_ref[...], hb) + dtbt_ref[...])

    @pl.when(t == 0)
    def _():
        carry_scr[...] = hist_ref[...]

    gn = SSM_GROUPS * D_STATE
    outs = ((xs_ref, 0, D_INNER), (b_ref, D_INNER, gn), (c_ref, D_INNER + gn, gn))
    n_grp = tm // SUBLANES
    starts = list(range(0, CONV_DIM, CONV_CHUNK))
    xc_next = _dot(hb, wxbc_ref[:, 0:CONV_CHUNK])
    for i, c0 in enumerate(starts):
        cols = slice(c0, c0 + CONV_CHUNK)
        xc = xc_next
        if i + 1 < len(starts):
            xc_next = _dot(hb, wxbc_ref[:, starts[i + 1]:starts[i + 1] + CONV_CHUNK])
        if c0 < D_INNER:
            z_ref[:, cols] = _dot(hb, wz_ref[:, cols]).astype(z_ref.dtype)
        ext = jnp.concatenate([carry_scr[:, cols], xc], axis=0)
        ext = ext.reshape(n_grp + CONV_CARRY // SUBLANES, SUBLANES, CONV_CHUNK)
        w = [cw_ref[k:k + 1, cols] for k in range(D_CONV)]
        ref, start = next((r, s0) for r, s0, width in outs if s0 <= c0 < s0 + width)
        x1 = _shift_rows(ext, 1)
        u = ext[2:] * w[3] + x1[1:] * w[2]
        v = ext[1:] * w[1] + x1 * w[0]
        conv = (u + _shift_rows(v, 2) + cb_ref[:, cols]).reshape(tm, CONV_CHUNK)
        act = conv * jax.nn.sigmoid(conv)
        ref[:, c0 - start:c0 - start + CONV_CHUNK] = act.astype(ref.dtype)
        carry_scr[:, cols] = xc[tm - CONV_CARRY:, :]
        clast_ref[:, cols] = xc[tm - CONV_CARRY:, :]


def _ssd_in(x, conv_hist, g_mix, w_in, conv_w, conv_b, dt_bias, *, tm):
    b, s, _ = x.shape
    n_t = s // tm
    gn = SSM_GROUPS * D_STATE
    wz = w_in[:, :D_INNER].astype(BF16)
    wxbc = w_in[:, D_INNER:D_INNER + CONV_DIM].astype(BF16)
    wdt = w_in[:, D_INNER + CONV_DIM:]
    wdt_p = jnp.pad(wdt, ((0, 0), (0, LANES - SSM_HEADS))).astype(BF16)
    wdt_t = wdt.T.astype(BF16)
    hist = jnp.pad(conv_hist.astype(F32), ((0, 0), (CONV_CARRY - (D_CONV - 1), 0), (0, 0)))
    row = lambda d: pl.BlockSpec((None, tm, d), lambda bi, ti: (bi, ti, 0))
    return pl.pallas_call(
        functools.partial(_ssd_in_kernel, tm=tm, n_t=n_t),
        out_shape=[
            jax.ShapeDtypeStruct((b, s, D_INNER), BF16),
            jax.ShapeDtypeStruct((b, s, D_INNER), BF16),
            jax.ShapeDtypeStruct((b, s, gn), BF16),
            jax.ShapeDtypeStruct((b, s, gn), BF16),
            jax.ShapeDtypeStruct((b, s, SSM_HEADS), F32),
            jax.ShapeDtypeStruct((b, SSM_HEADS, s), F32),
            jax.ShapeDtypeStruct((b, CONV_CARRY, CONV_DIM), F32),
        ],
        grid=(b, n_t),
        in_specs=[
            row(D_MODEL),
            _const_spec((1, D_MODEL)),
            _const_spec((D_MODEL, D_INNER)),
            _const_spec((D_MODEL, CONV_DIM)),
            _const_spec((D_MODEL, LANES)),
            _const_spec((SSM_HEADS, D_MODEL)),
            _const_spec((D_CONV, CONV_DIM)),
            _const_spec((1, CONV_DIM)),
            _const_spec((1, SSM_HEADS)),
            _const_spec((SSM_HEADS, 1)),
            pl.BlockSpec((None, CONV_CARRY, CONV_DIM), lambda bi, ti: (bi, 0, 0)),
        ],
        out_specs=[
            row(D_INNER), row(D_INNER), row(gn), row(gn), row(SSM_HEADS),
            pl.BlockSpec((None, SSM_HEADS, tm), lambda bi, ti: (bi, 0, ti)),
            pl.BlockSpec((None, CONV_CARRY, CONV_DIM), lambda bi, ti: (bi, 0, 0)),
        ],
        scratch_shapes=[pltpu.VMEM((CONV_CARRY, CONV_DIM), F32)],
        compiler_params=pltpu.CompilerParams(dimension_semantics=("arbitrary", "arbitrary"),
                                             vmem_limit_bytes=VMEM_LIMIT),
        name="ssd_in",
    )(x, g_mix.reshape(1, -1), wz, wxbc, wdt_p, wdt_t, conv_w, conv_b.reshape(1, -1),
      dt_bias.reshape(1, -1), dt_bias.reshape(-1, 1), hist)


def _ssd_scan_kernel(xs_ref, z_ref, b_ref, c_ref, dt_ref, dtt_ref, alog_ref, alogt_ref, dexp_ref, norm_ref,
                     expand_ref, h0_ref, y_ref, hfin_ref, h_scr, *, ln, n_c):
    c = pl.program_id(1)

    @pl.when(c == 0)
    def _():
        h_scr[...] = h0_ref[...]

    row_i = lax.broadcasted_iota(jnp.int32, (ln, ln), 0)
    col_i = lax.broadcasted_iota(jnp.int32, (ln, ln), 1)
    causal = row_i >= col_i
    tril = jnp.where(causal, 1.0, 0.0).astype(BF16)
    triu = jnp.where(row_i <= col_i, 1.0, 0.0).astype(BF16)
    lane = lax.broadcasted_iota(jnp.int32, (ln, LANES), 1)

    dt = dt_ref[...]
    dtt = dtt_ref[...]
    acum = _dot_sel_l(tril, dt * -jnp.exp(alog_ref[...]))
    acum_t = _dot_sel_r(dtt * -jnp.exp(alogt_ref[...]), triu)
    acum2 = acum * LOG2E
    key2_t = (acum_t - jnp.log(dtt)) * LOG2E
    last = acum[ln - 1:ln, :]
    ea_hi, ea_mid, _ = _split3(jnp.exp(acum))
    wj_b = (jnp.exp(last - acum) * dt).astype(BF16)
    pair = 2 * SSM_HEADDIM

    for g in range(SSM_GROUPS):
        gcols = slice(g * GROUP_COLS, (g + 1) * GROUP_COLS)
        expand = expand_ref[:, gcols]
        ea_e = _dot(ea_hi, expand) + _dot(ea_mid, expand)
        wj_e = _dot(wj_b, expand).astype(BF16)
        bg = b_ref[:, g * D_STATE:(g + 1) * D_STATE]
        cg = c_ref[:, g * D_STATE:(g + 1) * D_STATE]
        cb = _dot_nt(cg, bg)
        y_state = _dot(cg, h_scr[g].astype(BF16))
        gated = []
        for p in range(SSM_HPG // 2):
            h0 = g * SSM_HPG + 2 * p
            cols = slice(h0 * SSM_HEADDIM, (h0 + 2) * SSM_HEADDIM)
            pcols = slice(p * pair, (p + 1) * pair)
            scores = []
            for h in (h0, h0 + 1):
                seg = jnp.where(causal, acum2[:, h:h + 1] - key2_t[h:h + 1, :], NEG)
                scores.append((cb * jnp.exp2(seg)).astype(BF16))
            xp = xs_ref[:, cols]
            zero = jnp.zeros_like(xp)
            rhs = jnp.concatenate([jnp.where(lane < SSM_HEADDIM, xp, zero),
                                   jnp.where(lane >= SSM_HEADDIM, xp, zero)], axis=0)
            y = _dot(jnp.concatenate(scores, axis=1), rhs)
            y = y + y_state[:, pcols] * ea_e[:, pcols]
            y = y + xp.astype(F32) * dexp_ref[:, cols]
            zc = z_ref[:, cols].astype(F32)
            gated.append(y * (zc * jax.nn.sigmoid(zc)))
        yg = jnp.concatenate(gated, axis=1)
        yg = yg * lax.rsqrt(jnp.mean(yg * yg, axis=-1, keepdims=True) + EPS)
        y_ref[:, gcols] = (yg * norm_ref[:, gcols]).astype(y_ref.dtype)
        xw = xs_ref[:, gcols] * wj_e
        h_scr[g] = h_scr[g] * ea_e[ln - 1:ln, :] + _dot_tn(bg, xw)

    @pl.when(c == n_c - 1)
    def _():
        hfin_ref[...] = h_scr[...]


def _ssd_scan(xs, z, bm, cm, dt, dtt, a_log, d_skip, ssm_norm, h0, *, ln):
    b, s, _ = xs.shape
    n_c = s // ln
    gn = SSM_GROUPS * D_STATE
    expand = jnp.asarray(np.kron(np.eye(SSM_HEADS, dtype=np.float32), np.ones((1, SSM_HEADDIM), np.float32)),
                         BF16)
    d_exp = jnp.repeat(d_skip.astype(F32), SSM_HEADDIM).reshape(1, D_INNER)
    h0_t = jnp.transpose(h0.astype(F32).reshape(b, SSM_GROUPS, GROUP_COLS, D_STATE), (0, 1, 3, 2))
    row = lambda d: pl.BlockSpec((None, ln, d), lambda bi, ci: (bi, ci, 0))
    state_spec = pl.BlockSpec((None, SSM_GROUPS, D_STATE, GROUP_COLS), lambda bi, ci: (bi, 0, 0, 0))
    y, h_t = pl.pallas_call(
        functools.partial(_ssd_scan_kernel, ln=ln, n_c=n_c),
        out_shape=[jax.ShapeDtypeStruct((b, s, D_INNER), BF16),
                   jax.ShapeDtypeStruct((b, SSM_GROUPS, D_STATE, GROUP_COLS), F32)],
        grid=(b, n_c),
        in_specs=[
            row(D_INNER), row(D_INNER), row(gn), row(gn), row(SSM_HEADS),
            pl.BlockSpec((None, SSM_HEADS, ln), lambda bi, ci: (bi, 0, ci)),
            _const_spec((1, SSM_HEADS)), _const_spec((SSM_HEADS, 1)),
            _const_spec((1, D_INNER)), _const_spec((1, D_INNER)),
            _const_spec((SSM_HEADS, D_INNER)),
            state_spec,
        ],
        out_specs=[row(D_INNER), state_spec],
        scratch_shapes=[pltpu.VMEM((SSM_GROUPS, D_STATE, GROUP_COLS), F32)],
        compiler_params=pltpu.CompilerParams(dimension_semantics=("arbitrary", "arbitrary"),
                                             vmem_limit_bytes=VMEM_LIMIT),
        name="ssd_scan",
    )(xs, z, bm, cm, dt, dtt, a_log.reshape(1, -1), a_log.reshape(-1, 1), d_exp, ssm_norm.reshape(1, -1),
      expand, h0_t)
    h_fin = jnp.transpose(h_t, (0, 1, 3, 2)).reshape(b, SSM_HEADS, SSM_HEADDIM, D_STATE)
    return y, h_fin


def _trunk(x, pos0, cache, pool_hist, conv_hist, ssm_hist, w, *, tm_proj, tm_ffn, tq, ln):
    b, s, _ = x.shape
    prompt = cache is None
    win_p, wq, wk, wv, place = _prep_ab_weights(w["w_in_ab"][0], w["w_uq"][0], w["w_uk"][0], w["w_uv"][0])
    outs = _ab_in(x, pos0, pool_hist, w["norm_mix"][0], win_p, w["q_norm"][0], wq, w["kv_norm"][0],
                  w["w_pool"][0], w["pool_scale"][0], wk, wv, place, tm=tm_proj, emit_kv=prompt)
    q, ckv, kpe, pool_out, pool_last = outs[:5]
    if prompt:
        k, v = outs[5:]
        attn = _flash_attention(q, k, v, tq=tq)
    else:
        wabs, wuv_place = _prep_latent_weights(w["w_uk"][0], w["w_uv"][0])
        attn = _latent_attention(q, wabs, cache[0], cache[1], ckv, kpe, wuv_place, pos0=pos0)
    w_out = w["w_out_ab"][0].astype(BF16)
    x1 = _out_ffn(x.reshape(b * s, D_MODEL),
                  [pool_out.reshape(b * s, POOL_DIM), attn.reshape(b * s, MLA_OUT)],
                  [w_out[:POOL_DIM], w_out[POOL_DIM:]],
                  w["norm_ffn"][0], w["w_gate"][0], w["w_up"][0], w["w_down"][0], None, tm=tm_ffn)

    z, xs, bm, cm, dt, dtt, conv_last = _ssd_in(x1.reshape(b, s, D_MODEL), conv_hist, w["norm_mix"][1],
                                                w["w_in_c"][0], w["conv_w"][0], w["conv_b"][0],
                                                w["dt_bias"][0], tm=tm_proj)
    y, h_fin = _ssd_scan(xs, z, bm, cm, dt, dtt, w["a_log"][0], w["d_skip"][0], w["ssm_norm"][0], ssm_hist,
                         ln=ln)
    x2 = _out_ffn(x1, [y.reshape(b * s, D_INNER)], [w["w_out_c"][0].astype(BF16)],
                  w["norm_ffn"][1], w["w_gate"][1], w["w_up"][1], w["w_down"][1], w["norm_final"], tm=tm_ffn)
    return (x2.reshape(b, s, D_MODEL), ckv[None], kpe[None], pool_last[None, :, POOL_KEEP - POOL_HIST:],
            conv_last[None, :, CONV_CARRY - (D_CONV - 1):], h_fin[None])


def kernel(x_prompt, x_sample, cache_ckv, cache_kpe, state_pool, state_conv, state_ssm, norm_mix, norm_ffn, norm_final, w_in_ab, w_pool, pool_scale, q_norm, w_uq, kv_norm, w_uk, w_uv, w_out_ab, w_in_c, conv_w, conv_b, dt_bias, a_log, d_skip, ssm_norm, w_out_c, w_gate, w_up, w_down):
    assert norm_mix.shape[0] == 2 and w_in_ab.shape[0] == 1 and w_in_c.shape[0] == 1
    w = dict(norm_mix=norm_mix, norm_ffn=norm_ffn, norm_final=norm_final, w_in_ab=w_in_ab, w_pool=w_pool,
             pool_scale=pool_scale, q_norm=q_norm, w_uq=w_uq, kv_norm=kv_norm, w_uk=w_uk, w_uv=w_uv,
             w_out_ab=w_out_ab, w_in_c=w_in_c, conv_w=conv_w, conv_b=conv_b, dt_bias=dt_bias, a_log=a_log,
             d_skip=d_skip, ssm_norm=ssm_norm, w_out_c=w_out_c, w_gate=w_gate, w_up=w_up, w_down=w_down)
    bp, sp, _ = x_prompt.shape
    bs, ss, _ = x_sample.shape
    past = cache_ckv.shape[2]

    prompt = _trunk(x_prompt, 0, None,
                    jnp.zeros((bp, POOL_HIST, POOL_DIM), F32), jnp.zeros((bp, D_CONV - 1, CONV_DIM), F32),
                    jnp.zeros((bp, SSM_HEADS, SSM_HEADDIM, D_STATE), F32), w,
                    tm_proj=512, tm_ffn=512, tq=512, ln=128)
    sample = _trunk(x_sample, past, (cache_ckv[0], cache_kpe[0]), state_pool[0], state_conv[0], state_ssm[0], w,
                    tm_proj=ss, tm_ffn=bs * ss, tq=None, ln=ss)
    return (prompt[0], sample[0]) + prompt[1:] + sample[1:]
```

```python
import functools
import math

import numpy as np
import jax
import jax.numpy as jnp
from jax import lax
from jax.experimental import pallas as pl
from jax.experimental.pallas import tpu as pltpu

F32 = jnp.float32
BF16 = jnp.bfloat16

D_MODEL = 1024
CHUNK = 64
EPS = 1e-6
NEG = -1e30

POOL_WINDOWS = (2, 4, 8, 16)
POOL_GROUPS = 4
POOL_DIM = D_MODEL // 2
POOL_GDIM = POOL_DIM // POOL_GROUPS
POOL_HIST = max(POOL_WINDOWS) - 1

MLA_HEADS = 8
QK_NOPE = 64
QK_ROPE = 32
ROPE_HALF = QK_ROPE // 2
V_HEAD = 64
Q_LORA = 384
KV_LORA = 256
MLA_OUT = MLA_HEADS * V_HEAD
ROPE_THETA = 10000.0
SM_SCALE = (QK_NOPE + QK_ROPE) ** -0.5
LOG2E = math.log2(math.e)
Q_SCALE = SM_SCALE * LOG2E
IN_AB = POOL_DIM + Q_LORA + KV_LORA + QK_ROPE

D_INNER = 2 * D_MODEL
SSM_HEADDIM = 64
SSM_HEADS = D_INNER // SSM_HEADDIM
SSM_GROUPS = 4
SSM_HPG = SSM_HEADS // SSM_GROUPS
D_STATE = 128
D_CONV = 4
CONV_DIM = D_INNER + 2 * SSM_GROUPS * D_STATE
GROUP_COLS = D_INNER // SSM_GROUPS

D_FF = -(-8 * D_MODEL // (3 * 256)) * 256

LANES = 128
SUBLANES = 8
HEAD_PAD = LANES
ROWSUM_LANE_EVEN = V_HEAD
ROWSUM_LANE_ODD = 0
POOL_CARRY = 4 * SUBLANES
POOL_KEEP = 2 * SUBLANES
CONV_CARRY = 2 * SUBLANES
CONV_CHUNK = 256
IN_AB_PAD = 10 * LANES
VMEM_LIMIT = 56 * 1024 * 1024

OFF_Q = POOL_DIM
OFF_KV = OFF_Q + Q_LORA
OFF_KPE = OFF_KV + KV_LORA


def _const_spec(shape):
    nd = len(shape)
    return pl.BlockSpec(shape, lambda *_: (0,) * nd, pipeline_mode=pl.Buffered(1))


def _rms(x, g):
    return x * lax.rsqrt(jnp.mean(x * x, axis=-1, keepdims=True) + EPS) * g


def _dot(a, b):
    return jnp.dot(a, b, preferred_element_type=F32)


def _dot_nt(a, b):
    return lax.dot_general(a, b, (((1,), (1,)), ((), ())), preferred_element_type=F32)


def _dot_tn(a, b):
    return lax.dot_general(a, b, (((0,), (0,)), ((), ())), preferred_element_type=F32)


def _split3(x):
    hi = x.astype(BF16)
    r1 = x - hi.astype(F32)
    mid = r1.astype(BF16)
    lo = (r1 - mid.astype(F32)).astype(BF16)
    return hi, mid, lo


def _dot_sel_r(x, sel):
    hi, mid, lo = _split3(x)
    return _dot(hi, sel) + _dot(mid, sel) + _dot(lo, sel)


def _dot_sel_l(sel, x):
    hi, mid, lo = _split3(x)
    return _dot(sel, hi) + _dot(sel, mid) + _dot(sel, lo)


def _ab_in_kernel(*refs, tm, pos0, n_t, emit_kv):
    (x_ref, g_ref, win_ref, qn_ref, wuq_ref, kvn_ref, wpool_ref, pscale_ref, tab_ref, hist_ref) = refs[:10]
    refs = refs[10:]
    if emit_kv:
        wuk_ref, wuv_ref, place_ref = refs[:3]
        refs = refs[3:]
    q_ref, ckv_ref, kpe_ref, pool_ref, plast_ref = refs[:5]
    refs = refs[5:]
    if emit_kv:
        k_ref, v_ref = refs[:2]
        refs = refs[2:]
    (cat_scr,) = refs

    t = pl.program_id(1)
    x = x_ref[...]
    hb = _rms(x, g_ref[...]).astype(BF16)
    proj = _dot(hb, win_ref[...])
    u_pool = proj[:, :POOL_DIM]
    q_lat = proj[:, OFF_Q:OFF_KV]
    kv_lat = proj[:, OFF_KV:OFF_KPE]
    kblk = proj[:, OFF_KPE:OFF_KPE + LANES]

    @pl.when(t == 0)
    def _():
        cat_scr[...] = hist_ref[...]

    ext2d = jnp.concatenate([cat_scr[...], u_pool], axis=0)
    ext = ext2d.reshape((POOL_CARRY + tm) // SUBLANES, SUBLANES, POOL_DIM)
    lead = POOL_CARRY // SUBLANES
    pos = pos0 + t * tm + lax.broadcasted_iota(jnp.int32, (tm, 1), 0)
    for g, w in enumerate(POOL_WINDOWS):
        cols = slice(g * POOL_GDIM, (g + 1) * POOL_GDIM)
        win, span, dropped = ext[:, :, cols], 1, 0
        while span < w:
            prev = win[:-1] if span == SUBLANES else _shift_rows(win, span)
            win, span, dropped = win[1:] + prev, 2 * span, dropped + 1
        win = win[lead - dropped:].reshape(tm, POOL_GDIM)
        u_g = u_pool[:, cols]
        cnt = jnp.minimum(pos + 1, w).astype(F32)
        pooled = win / cnt - u_g
        mixed = _dot(pooled.astype(BF16), wpool_ref[g]) * pscale_ref[:, cols]
        pool_ref[:, cols] = mixed.astype(pool_ref.dtype)
    cat_scr[...] = ext2d[tm:, :]
    plast_ref[...] = ext2d[POOL_CARRY + tm - POOL_KEEP:, :]

    tab = tab_ref[...]
    up = pltpu.roll(tab, ROPE_HALF, 1)
    down = pltpu.roll(tab, LANES - ROPE_HALF, 1)
    ln_i = lax.broadcasted_iota(jnp.int32, (tm, LANES), 1)
    x1_q, x2_q = QK_NOPE, QK_NOPE + ROPE_HALF
    tq_c = jnp.where(ln_i < x1_q, 1.0, jnp.where(ln_i < x2_q, tab, jnp.where(ln_i < x2_q + ROPE_HALF, up, 0.0)))
    tq_a = jnp.where((ln_i >= x2_q) & (ln_i < x2_q + ROPE_HALF), tab, 0.0)
    tq_b = jnp.where((ln_i >= x1_q) & (ln_i < x2_q), -down, 0.0)
    tk_c = jnp.where(ln_i < ROPE_HALF, tab, jnp.where(ln_i < QK_ROPE, up, 0.0))
    tk_s = jnp.where(ln_i < ROPE_HALF, -down, jnp.where(ln_i < QK_ROPE, tab, 0.0))

    qn = _rms(q_lat, qn_ref[...]).astype(BF16)
    q = _dot(qn, wuq_ref[...])
    for h in range(MLA_HEADS):
        qh = q[:, h * HEAD_PAD:(h + 1) * HEAD_PAD]
        qr = (qh * tq_c + pltpu.roll(qh, ROPE_HALF, 1) * tq_a
              + pltpu.roll(qh, HEAD_PAD - ROPE_HALF, 1) * tq_b)
        q_ref[h] = qr.astype(q_ref.dtype)

    ckv = _rms(kv_lat, kvn_ref[...])
    ckv_ref[...] = ckv
    kpe_blk = kblk * tk_c + pltpu.roll(kblk, LANES - QK_ROPE, 1) * tk_s
    kpe_ref[...] = kpe_blk[:, :QK_ROPE]

    if emit_kv:
        ckv_b = ckv.astype(BF16)
        k_all = _dot(ckv_b, wuk_ref[...]) + _dot(kpe_blk.astype(BF16), place_ref[...])
        for h in range(MLA_HEADS):
            k_ref[h] = k_all[:, h * HEAD_PAD:(h + 1) * HEAD_PAD].astype(k_ref.dtype)
        v_all = _dot(ckv_b, wuv_ref[...])
        lane = lax.broadcasted_iota(jnp.int32, (tm, LANES), 1)
        for hp in range(MLA_HEADS // 2):
            vp = v_all[:, hp * LANES:(hp + 1) * LANES]
            even = jnp.where(lane < V_HEAD, vp, jnp.where(lane == ROWSUM_LANE_EVEN, 1.0, 0.0))
            odd = jnp.where(lane >= V_HEAD, vp, jnp.where(lane == ROWSUM_LANE_ODD, 1.0, 0.0))
            v_ref[2 * hp] = even.astype(v_ref.dtype)
            v_ref[2 * hp + 1] = odd.astype(v_ref.dtype)


def _rope_table(pos0, s):
    pos = (pos0 + jnp.arange(s)).astype(F32)
    inv = ROPE_THETA ** (-jnp.arange(ROPE_HALF, dtype=F32) / ROPE_HALF)
    ang = pos[:, None] * inv[None, :]
    return jnp.tile(jnp.concatenate([jnp.cos(ang), jnp.sin(ang)], axis=1), (1, LANES // QK_ROPE))


def _prep_ab_weights(w_in, w_uq, w_uk, w_uv):
    kpe_w = w_in[:, OFF_KPE:]
    swapped = jnp.concatenate([kpe_w[:, ROPE_HALF:], kpe_w[:, :ROPE_HALF]], axis=1)
    pad = jnp.zeros((D_MODEL, IN_AB_PAD - IN_AB - QK_ROPE), w_in.dtype)
    win_p = jnp.concatenate([w_in, swapped, pad], axis=1).astype(BF16)

    tail = HEAD_PAD - QK_NOPE - QK_ROPE
    wq = w_uq.reshape(Q_LORA, MLA_HEADS, QK_NOPE + QK_ROPE)
    wq = jnp.pad(wq, ((0, 0), (0, 0), (0, tail))).reshape(Q_LORA, MLA_HEADS * HEAD_PAD).astype(BF16)
    wk = jnp.pad(w_uk, ((0, 0), (0, 0), (0, HEAD_PAD - QK_NOPE))).reshape(KV_LORA, MLA_HEADS * HEAD_PAD)
    wk = wk.astype(BF16)
    wv = w_uv.reshape(KV_LORA, MLA_OUT).astype(BF16)
    place = np.zeros((LANES, MLA_HEADS * HEAD_PAD), np.float32)
    for h in range(MLA_HEADS):
        for i in range(QK_ROPE):
            place[i, h * HEAD_PAD + QK_NOPE + i] = 1.0
    return win_p, wq, wk, wv, jnp.asarray(place, BF16)


def _ab_in(x, pos0, pool_hist, g_mix, win_p, q_norm, wq, kv_norm, w_pool, pool_scale, wk, wv, place,
           *, tm, emit_kv):
    b, s, _ = x.shape
    n_t = s // tm
    tab = _rope_table(pos0, s)
    hist = jnp.pad(pool_hist.astype(F32), ((0, 0), (POOL_CARRY - POOL_HIST, 0), (0, 0)))
    row = lambda shape: pl.BlockSpec((None,) + shape, lambda bi, ti: (bi, ti, 0))

    in_specs = [
        row((tm, D_MODEL)),
        _const_spec((1, D_MODEL)),
        _const_spec((D_MODEL, IN_AB_PAD)),
        _const_spec((1, Q_LORA)),
        _const_spec((Q_LORA, MLA_HEADS * HEAD_PAD)),
        _const_spec((1, KV_LORA)),
        _const_spec((POOL_GROUPS, POOL_GDIM, POOL_GDIM)),
        _const_spec((1, POOL_DIM)),
        pl.BlockSpec((tm, LANES), lambda bi, ti: (ti, 0)),
        pl.BlockSpec((None, POOL_CARRY, POOL_DIM), lambda bi, ti: (bi, 0, 0)),
    ]
    args = [x, g_mix.reshape(1, -1), win_p, (q_norm * Q_SCALE).reshape(1, -1), wq, kv_norm.reshape(1, -1),
            w_pool.astype(BF16), pool_scale.reshape(1, -1), tab, hist]
    if emit_kv:
        in_specs += [_const_spec((KV_LORA, MLA_HEADS * HEAD_PAD)), _const_spec((KV_LORA, MLA_OUT)),
                     _const_spec((LANES, MLA_HEADS * HEAD_PAD))]
        args += [wk, wv, place]

    head_spec = pl.BlockSpec((None, MLA_HEADS, tm, HEAD_PAD), lambda bi, ti: (bi, 0, ti, 0))
    out_shape = [
        jax.ShapeDtypeStruct((b, MLA_HEADS, s, HEAD_PAD), BF16),
        jax.ShapeDtypeStruct((b, s, KV_LORA), F32),
        jax.ShapeDtypeStruct((b, s, QK_ROPE), F32),
        jax.ShapeDtypeStruct((b, s, POOL_DIM), BF16),
        jax.ShapeDtypeStruct((b, POOL_KEEP, POOL_DIM), F32),
    ]
    out_specs = [
        head_spec,
        row((tm, KV_LORA)),
        row((tm, QK_ROPE)),
        row((tm, POOL_DIM)),
        pl.BlockSpec((None, POOL_KEEP, POOL_DIM), lambda bi, ti: (bi, 0, 0)),
    ]
    if emit_kv:
        out_shape += [jax.ShapeDtypeStruct((b, MLA_HEADS, s, HEAD_PAD), BF16),
                      jax.ShapeDtypeStruct((b, MLA_HEADS, s, LANES), BF16)]
        out_specs += [head_spec, head_spec]

    return pl.pallas_call(
        functools.partial(_ab_in_kernel, tm=tm, pos0=pos0, n_t=n_t, emit_kv=emit_kv),
        out_shape=out_shape,
        grid=(b, n_t),
        in_specs=in_specs,
        out_specs=out_specs,
        scratch_shapes=[pltpu.VMEM((POOL_CARRY, POOL_DIM), F32)],
        compiler_params=pltpu.CompilerParams(dimension_semantics=("arbitrary", "arbitrary"),
                                             vmem_limit_bytes=VMEM_LIMIT),
        name="ab_in",
    )(*args)


def _flash_kernel(q_ref, k_ref, v_ref, o_ref, m_scr, acc_scr, *, tq, tk):
    t = pl.program_id(2)
    heads = (0, 1)

    def pair(first, diagonal):
        s = {}
        for n in range(2):
            off = pl.multiple_of((first + n) * tk, tk)
            r0 = n * tk if diagonal else 0
            for hh in heads:
                sh = _dot_nt(q_ref[hh, r0:, :], k_ref[hh, pl.ds(off, tk), :])
                if diagonal:
                    qc = (r0 + lax.broadcasted_iota(jnp.int32, (tq - r0, tk), 0)) // CHUNK
                    kc = (n * tk + lax.broadcasted_iota(jnp.int32, (tq - r0, tk), 1)) // CHUNK
                    sh = jnp.where(kc <= qc, sh, NEG)
                s[n, hh] = sh
        for n in range(2):
            off = pl.multiple_of((first + n) * tk, tk)
            r0 = n * tk if diagonal else 0
            for hh in heads:
                sh = s[n, hh]
                m_prev = m_scr[hh, r0:, :]
                m_new = jnp.maximum(m_prev, jnp.max(sh, axis=-1, keepdims=True))
                alpha = jnp.exp2(m_prev - m_new)
                p = jnp.concatenate([jnp.exp2(sh[:, c * LANES:(c + 1) * LANES] - m_new)
                                     for c in range(tk // LANES)], axis=1).astype(BF16)
                acc_scr[hh, r0:, :] = alpha * acc_scr[hh, r0:, :] + _dot(p, v_ref[hh, pl.ds(off, tk), :])
                m_scr[hh, r0:, :] = m_new

    m_scr[...] = jnp.full(m_scr.shape, NEG, F32)
    acc_scr[...] = jnp.zeros(acc_scr.shape, F32)

    def body(i, carry):
        pair(2 * i, False)
        return carry

    lax.fori_loop(0, t, body, 0)
    pair(2 * t, True)

    acc0, acc1 = acc_scr[0], acc_scr[1]
    out0 = acc0 / acc0[:, ROWSUM_LANE_EVEN:ROWSUM_LANE_EVEN + 1]
    out1 = acc1 / acc1[:, ROWSUM_LANE_ODD:ROWSUM_LANE_ODD + 1]
    lane = lax.broadcasted_iota(jnp.int32, (tq, LANES), 1)
    o_ref[...] = jnp.where(lane < V_HEAD, out0, out1).astype(o_ref.dtype)


def _flash_attention(q, k, v, *, tq):
    b, h, s, _ = q.shape
    tk = tq // 2
    assert tk % CHUNK == 0 and s % tq == 0
    pair = lambda n: pl.BlockSpec((None, 2, n, LANES), lambda bi, hp, ti: (bi, hp, 0, 0))
    return pl.pallas_call(
        functools.partial(_flash_kernel, tq=tq, tk=tk),
        out_shape=jax.ShapeDtypeStruct((b, s, MLA_OUT), BF16),
        grid=(b, h // 2, s // tq),
        in_specs=[
            pl.BlockSpec((None, 2, tq, HEAD_PAD), lambda bi, hp, ti: (bi, hp, ti, 0)),
            pair(s),
            pair(s),
        ],
        out_specs=pl.BlockSpec((None, tq, 2 * V_HEAD), lambda bi, hp, ti: (bi, ti, hp)),
        scratch_shapes=[pltpu.VMEM((2, tq, LANES), F32), pltpu.VMEM((2, tq, LANES), F32)],
        compiler_params=pltpu.CompilerParams(dimension_semantics=("arbitrary", "arbitrary", "arbitrary"),
                                             vmem_limit_bytes=VMEM_LIMIT),
        name="flash_attention",
    )(q, k, v)


def _latent_attn_kernel(q_ref, wabs_ref, cckv_ref, ckpe_ref, nckv_ref, nkpe_ref, wuv_ref, o_ref,
                        *, sq, past, pos0):
    rows = MLA_HEADS * sq
    qcat = jnp.concatenate([_dot(q_ref[h], wabs_ref[h]) for h in range(MLA_HEADS)], axis=0)
    qa = qcat[:, :KV_LORA].astype(BF16)
    qp = qcat[:, KV_LORA:].astype(BF16)
    ck = cckv_ref[...].astype(BF16)
    kp = ckpe_ref[...].astype(BF16)
    nk = nckv_ref[...].astype(BF16)
    np_ = nkpe_ref[...].astype(BF16)
    s_past = _dot_nt(qa, ck) + _dot_nt(qp, kp)
    s_new = _dot_nt(qa, nk) + _dot_nt(qp, np_)

    q_chunk = (pos0 + lax.broadcasted_iota(jnp.int32, (rows, 1), 0) % sq) // CHUNK
    kc_past = (pos0 - past + lax.broadcasted_iota(jnp.int32, (1, past), 1)) // CHUNK
    kc_new = (pos0 + lax.broadcasted_iota(jnp.int32, (1, sq), 1)) // CHUNK
    s_past = jnp.where(kc_past <= q_chunk, s_past, NEG)
    s_new = jnp.where(kc_new <= q_chunk, s_new, NEG)

    m = jnp.maximum(jnp.max(s_past, axis=-1, keepdims=True), jnp.max(s_new, axis=-1, keepdims=True))
    p_past = jnp.exp2(s_past - m)
    p_new = jnp.exp2(s_new - m)
    l = jnp.sum(p_past, axis=-1, keepdims=True) + jnp.sum(p_new, axis=-1, keepdims=True)
    o_lat = (_dot(p_past.astype(BF16), ck) + _dot(p_new.astype(BF16), nk)) / l
    o_lat = o_lat.astype(BF16)
    out = _dot(o_lat[0:sq], wuv_ref[0])
    for h in range(1, MLA_HEADS):
        out = out + _dot(o_lat[h * sq:(h + 1) * sq], wuv_ref[h])
    o_ref[...] = out.astype(o_ref.dtype)


def _prep_latent_weights(w_uk, w_uv):
    wabs = np.zeros((MLA_HEADS, HEAD_PAD, KV_LORA + QK_ROPE), np.float32)
    for i in range(QK_ROPE):
        wabs[:, QK_NOPE + i, KV_LORA + i] = 1.0
    wabs = jnp.asarray(wabs).at[:, :QK_NOPE, :KV_LORA].set(jnp.transpose(w_uk, (1, 2, 0)))
    wuv_place = jnp.zeros((MLA_HEADS, KV_LORA, MLA_OUT), F32)
    for h in range(MLA_HEADS):
        wuv_place = wuv_place.at[h, :, h * V_HEAD:(h + 1) * V_HEAD].set(w_uv[:, h, :])
    return wabs.astype(BF16), wuv_place.astype(BF16)


def _latent_attention(q, wabs, cache_ckv, cache_kpe, ckv_new, kpe_new, wuv_place, *, pos0):
    b, h, sq, _ = q.shape
    past = cache_ckv.shape[1]
    seq = lambda n, d: pl.BlockSpec((None, n, d), lambda bi: (bi, 0, 0))
    return pl.pallas_call(
        functools.partial(_latent_attn_kernel, sq=sq, past=past, pos0=pos0),
        out_shape=jax.ShapeDtypeStruct((b, sq, MLA_OUT), BF16),
        grid=(b,),
        in_specs=[
            pl.BlockSpec((None, h, sq, HEAD_PAD), lambda bi: (bi, 0, 0, 0)),
            _const_spec((h, HEAD_PAD, KV_LORA + QK_ROPE)),
            seq(past, KV_LORA), seq(past, QK_ROPE), seq(sq, KV_LORA), seq(sq, QK_ROPE),
            _const_spec((h, KV_LORA, MLA_OUT)),
        ],
        out_specs=seq(sq, MLA_OUT),
        compiler_params=pltpu.CompilerParams(dimension_semantics=("arbitrary",),
                                             vmem_limit_bytes=VMEM_LIMIT),
        name="latent_attention",
    )(q, wabs, cache_ckv, cache_kpe, ckv_new, kpe_new, wuv_place)


def _out_ffn_kernel(*refs, n_act, final_norm):
    x_ref = refs[0]
    act_refs = refs[1:1 + n_act]
    wout_refs = refs[1 + n_act:1 + 2 * n_act]
    gffn_ref, wg_ref, wu_ref, wd_ref = refs[1 + 2 * n_act:5 + 2 * n_act]
    rest = refs[5 + 2 * n_act:]
    if final_norm:
        gfin_ref, o_ref = rest
    else:
        (o_ref,) = rest

    y = _dot(act_refs[0][...], wout_refs[0][...])
    for a_ref, w_ref in zip(act_refs[1:], wout_refs[1:]):
        y = y + _dot(a_ref[...], w_ref[...])
    x1 = x_ref[...] + y
    h2 = _rms(x1, gffn_ref[...]).astype(BF16)
    gate = _dot(h2, wg_ref[...])
    up = _dot(h2, wu_ref[...])
    act = (gate * jax.nn.sigmoid(gate) * up).astype(BF16)
    x2 = x1 + _dot(act, wd_ref[...])
    if final_norm:
        x2 = _rms(x2, gfin_ref[...])
    o_ref[...] = x2


def _out_ffn(x, acts, wouts, g_ffn, w_gate, w_up, w_down, g_final, *, tm):
    t, _ = x.shape
    n_act = len(acts)
    row = lambda d: pl.BlockSpec((tm, d), lambda i: (i, 0))
    in_specs = ([row(D_MODEL)] + [row(a.shape[1]) for a in acts] + [_const_spec(w.shape) for w in wouts]
                + [_const_spec((1, D_MODEL)), _const_spec((D_MODEL, D_FF)), _const_spec((D_MODEL, D_FF)),
                   _const_spec((D_FF, D_MODEL))])
    args = [x, *acts, *wouts, g_ffn.reshape(1, -1), w_gate.astype(BF16), w_up.astype(BF16),
            w_down.astype(BF16)]
    if g_final is not None:
        in_specs.append(_const_spec((1, D_MODEL)))
        args.append(g_final.reshape(1, -1))
    return pl.pallas_call(
        functools.partial(_out_ffn_kernel, n_act=n_act, final_norm=g_final is not None),
        out_shape=jax.ShapeDtypeStruct((t, D_MODEL), F32),
        grid=(t // tm,),
        in_specs=in_specs,
        out_specs=row(D_MODEL),
        compiler_params=pltpu.CompilerParams(dimension_semantics=("arbitrary",),
                                             vmem_limit_bytes=VMEM_LIMIT),
        name="out_ffn",
    )(*args)


def _softplus(x):
    return jnp.maximum(x, 0.0) + jnp.log1p(jnp.exp(-jnp.abs(x)))


def _shift_rows(v, k):
    rot = pltpu.roll(v, k, axis=1)
    sub = lax.broadcasted_iota(jnp.int32, (1,) + v.shape[1:], 1)
    return jnp.where(sub < k, rot[:-1], rot[1:])


def _ssd_in_kernel(x_ref, g_ref, wz_ref, wxbc_ref, wdt_ref, wdtt_ref, cw_ref, cb_ref, dtb_ref, dtbt_ref,
                   hist_ref, z_ref, xs_ref, b_ref, c_ref, dt_ref, dtt_ref, clast_ref, carry_scr, *, tm, n_t):
    t = pl.program_id(1)
    hb = _rms(x_ref[...], g_ref[...]).astype(BF16)
    dt_ref[...] = _softplus(_dot(hb, wdt_ref[...])[:, :SSM_HEADS] + dtb_ref[...])
    dtt_ref[...] = _softplus(_dot_nt(wdtt_ref[...], hb) + dtbt_ref[...])

    @pl.when(t == 0)
    def _():
        carry_scr[...] = hist_ref[...]

    gn = SSM_GROUPS * D_STATE
    outs = ((xs_ref, 0, D_INNER), (b_ref, D_INNER, gn), (c_ref, D_INNER + gn, gn))
    n_grp = tm // SUBLANES
    starts = list(range(0, CONV_DIM, CONV_CHUNK))
    xc_next = _dot(hb, wxbc_ref[:, 0:CONV_CHUNK])
    for i, c0 in enumerate(starts):
        cols = slice(c0, c0 + CONV_CHUNK)
        xc = xc_next
        if i + 1 < len(starts):
            xc_next = _dot(hb, wxbc_ref[:, starts[i + 1]:starts[i + 1] + CONV_CHUNK])
        if c0 < D_INNER:
            z_ref[:, cols] = _dot(hb, wz_ref[:, cols]).astype(z_ref.dtype)
        ext = jnp.concatenate([carry_scr[:, cols], xc], axis=0)
        ext = ext.reshape(n_grp + CONV_CARRY // SUBLANES, SUBLANES, CONV_CHUNK)
        w = [cw_ref[k:k + 1, cols] for k in range(D_CONV)]
        ref, start = next((r, s0) for r, s0, width in outs if s0 <= c0 < s0 + width)
        x1 = _shift_rows(ext, 1)
        u = ext[2:] * w[3] + x1[1:] * w[2]
        v = ext[1:] * w[1] + x1 * w[0]
        conv = (u + _shift_rows(v, 2) + cb_ref[:, cols]).reshape(tm, CONV_CHUNK)
        act = conv * jax.nn.sigmoid(conv)
        ref[:, c0 - start:c0 - start + CONV_CHUNK] = act.astype(ref.dtype)
        carry_scr[:, cols] = xc[tm - CONV_CARRY:, :]
        clast_ref[:, cols] = xc[tm - CONV_CARRY:, :]


def _ssd_in(x, conv_hist, g_mix, w_in, conv_w, conv_b, dt_bias, *, tm):
    b, s, _ = x.shape
    n_t = s // tm
    gn = SSM_GROUPS * D_STATE
    wz = w_in[:, :D_INNER].astype(BF16)
    wxbc = w_in[:, D_INNER:D_INNER + CONV_DIM].astype(BF16)
    wdt = w_in[:, D_INNER + CONV_DIM:]
    wdt_p = jnp.pad(wdt, ((0, 0), (0, LANES - SSM_HEADS))).astype(BF16)
    wdt_t = wdt.T.astype(BF16)
    hist = jnp.pad(conv_hist.astype(F32), ((0, 0), (CONV_CARRY - (D_CONV - 1), 0), (0, 0)))
    row = lambda d: pl.BlockSpec((None, tm, d), lambda bi, ti: (bi, ti, 0))
    return pl.pallas_call(
        functools.partial(_ssd_in_kernel, tm=tm, n_t=n_t),
        out_shape=[
            jax.ShapeDtypeStruct((b, s, D_INNER), BF16),
            jax.ShapeDtypeStruct((b, s, D_INNER), BF16),
            jax.ShapeDtypeStruct((b, s, gn), BF16),
            jax.ShapeDtypeStruct((b, s, gn), BF16),
            jax.ShapeDtypeStruct((b, s, SSM_HEADS), F32),
            jax.ShapeDtypeStruct((b, SSM_HEADS, s), F32),
            jax.ShapeDtypeStruct((b, CONV_CARRY, CONV_DIM), F32),
        ],
        grid=(b, n_t),
        in_specs=[
            row(D_MODEL),
            _const_spec((1, D_MODEL)),
            _const_spec((D_MODEL, D_INNER)),
            _const_spec((D_MODEL, CONV_DIM)),
            _const_spec((D_MODEL, LANES)),
            _const_spec((SSM_HEADS, D_MODEL)),
            _const_spec((D_CONV, CONV_DIM)),
            _const_spec((1, CONV_DIM)),
            _const_spec((1, SSM_HEADS)),
            _const_spec((SSM_HEADS, 1)),
            pl.BlockSpec((None, CONV_CARRY, CONV_DIM), lambda bi, ti: (bi, 0, 0)),
        ],
        out_specs=[
            row(D_INNER), row(D_INNER), row(gn), row(gn), row(SSM_HEADS),
            pl.BlockSpec((None, SSM_HEADS, tm), lambda bi, ti: (bi, 0, ti)),
            pl.BlockSpec((None, CONV_CARRY, CONV_DIM), lambda bi, ti: (bi, 0, 0)),
        ],
        scratch_shapes=[pltpu.VMEM((CONV_CARRY, CONV_DIM), F32)],
        compiler_params=pltpu.CompilerParams(dimension_semantics=("arbitrary", "arbitrary"),
                                             vmem_limit_bytes=VMEM_LIMIT),
        name="ssd_in",
    )(x, g_mix.reshape(1, -1), wz, wxbc, wdt_p, wdt_t, conv_w, conv_b.reshape(1, -1),
      dt_bias.reshape(1, -1), dt_bias.reshape(-1, 1), hist)


def _ssd_scan_kernel(xs_ref, z_ref, b_ref, c_ref, dt_ref, dtt_ref, alog_ref, alogt_ref, dexp_ref, norm_ref,
                     expand_ref, h0_ref, y_ref, hfin_ref, h_scr, *, ln, n_c):
    c = pl.program_id(1)

    @pl.when(c == 0)
    def _():
        for g in range(SSM_GROUPS):
            h_scr[g] = h0_ref[g * GROUP_COLS:(g + 1) * GROUP_COLS, :].T

    row_i = lax.broadcasted_iota(jnp.int32, (ln, ln), 0)
    col_i = lax.broadcasted_iota(jnp.int32, (ln, ln), 1)
    causal = row_i >= col_i
    tril = jnp.where(causal, 1.0, 0.0).astype(BF16)
    triu = jnp.where(row_i <= col_i, 1.0, 0.0).astype(BF16)
    lane = lax.broadcasted_iota(jnp.int32, (ln, LANES), 1)

    dt = dt_ref[...]
    dtt = dtt_ref[...]
    acum = _dot_sel_l(tril, dt * -jnp.exp(alog_ref[...]))
    acum_t = _dot_sel_r(dtt * -jnp.exp(alogt_ref[...]), triu)
    acum2 = acum * LOG2E
    key2_t = (acum_t - jnp.log(dtt)) * LOG2E
    last = acum[ln - 1:ln, :]
    ea_hi, ea_mid, _ = _split3(jnp.exp(acum))
    wj_b = (jnp.exp(last - acum) * dt).astype(BF16)
    pair = 2 * SSM_HEADDIM

    for g in range(SSM_GROUPS):
        gcols = slice(g * GROUP_COLS, (g + 1) * GROUP_COLS)
        expand = expand_ref[:, gcols]
        ea_e = _dot(ea_hi, expand) + _dot(ea_mid, expand)
        wj_e = _dot(wj_b, expand).astype(BF16)
        bg = b_ref[:, g * D_STATE:(g + 1) * D_STATE]
        cg = c_ref[:, g * D_STATE:(g + 1) * D_STATE]
        cb = _dot_nt(cg, bg)
        y_state = _dot(cg, h_scr[g].astype(BF16))
        gated = []
        for p in range(SSM_HPG // 2):
            h0 = g * SSM_HPG + 2 * p
            cols = slice(h0 * SSM_HEADDIM, (h0 + 2) * SSM_HEADDIM)
            pcols = slice(p * pair, (p + 1) * pair)
            scores = []
            for h in (h0, h0 + 1):
                seg = jnp.where(causal, acum2[:, h:h + 1] - key2_t[h:h + 1, :], NEG)
                scores.append((cb * jnp.exp2(seg)).astype(BF16))
            xp = xs_ref[:, cols]
            zero = jnp.zeros_like(xp)
            rhs = jnp.concatenate([jnp.where(lane < SSM_HEADDIM, xp, zero),
                                   jnp.where(lane >= SSM_HEADDIM, xp, zero)], axis=0)
            y = _dot(jnp.concatenate(scores, axis=1), rhs)
            y = y + y_state[:, pcols] * ea_e[:, pcols]
            y = y + xp.astype(F32) * dexp_ref[:, cols]
            zc = z_ref[:, cols].astype(F32)
            gated.append(y * (zc * jax.nn.sigmoid(zc)))
        yg = jnp.concatenate(gated, axis=1)
        yg = yg * lax.rsqrt(jnp.mean(yg * yg, axis=-1, keepdims=True) + EPS)
        y_ref[:, gcols] = (yg * norm_ref[:, gcols]).astype(y_ref.dtype)
        xw = xs_ref[:, gcols] * wj_e
        h_scr[g] = h_scr[g] * ea_e[ln - 1:ln, :] + _dot_tn(bg, xw)

    @pl.when(c == n_c - 1)
    def _():
        for g in range(SSM_GROUPS):
            hfin_ref[g * GROUP_COLS:(g + 1) * GROUP_COLS, :] = h_scr[g].T


def _ssd_scan(xs, z, bm, cm, dt, dtt, a_log, d_skip, ssm_norm, h0, *, ln):
    b, s, _ = xs.shape
    n_c = s // ln
    gn = SSM_GROUPS * D_STATE
    expand = jnp.asarray(np.kron(np.eye(SSM_HEADS, dtype=np.float32), np.ones((1, SSM_HEADDIM), np.float32)),
                         BF16)
    d_exp = jnp.repeat(d_skip.astype(F32), SSM_HEADDIM).reshape(1, D_INNER)
    row = lambda d: pl.BlockSpec((None, ln, d), lambda bi, ci: (bi, ci, 0))
    state_spec = pl.BlockSpec((None, D_INNER, D_STATE), lambda bi, ci: (bi, 0, 0))
    y, h_fin = pl.pallas_call(
        functools.partial(_ssd_scan_kernel, ln=ln, n_c=n_c),
        out_shape=[jax.ShapeDtypeStruct((b, s, D_INNER), BF16),
                   jax.ShapeDtypeStruct((b, D_INNER, D_STATE), F32)],
        grid=(b, n_c),
        in_specs=[
            row(D_INNER), row(D_INNER), row(gn), row(gn), row(SSM_HEADS),
            pl.BlockSpec((None, SSM_HEADS, ln), lambda bi, ci: (bi, 0, ci)),
            _const_spec((1, SSM_HEADS)), _const_spec((SSM_HEADS, 1)),
            _const_spec((1, D_INNER)), _const_spec((1, D_INNER)),
            _const_spec((SSM_HEADS, D_INNER)),
            state_spec,
        ],
        out_specs=[row(D_INNER), state_spec],
        scratch_shapes=[pltpu.VMEM((SSM_GROUPS, D_STATE, GROUP_COLS), F32)],
        compiler_params=pltpu.CompilerParams(dimension_semantics=("arbitrary", "arbitrary"),
                                             vmem_limit_bytes=VMEM_LIMIT),
        name="ssd_scan",
    )(xs, z, bm, cm, dt, dtt, a_log.reshape(1, -1), a_log.reshape(-1, 1), d_exp, ssm_norm.reshape(1, -1),
      expand, h0.astype(F32).reshape(b, D_INNER, D_STATE))
    return y, h_fin.reshape(b, SSM_HEADS, SSM_HEADDIM, D_STATE)


def _trunk(x, pos0, cache, pool_hist, conv_hist, ssm_hist, w, *, tm_proj, tm_ffn, tq, ln):
    b, s, _ = x.shape
    prompt = cache is None
    win_p, wq, wk, wv, place = _prep_ab_weights(w["w_in_ab"][0], w["w_uq"][0], w["w_uk"][0], w["w_uv"][0])
    outs = _ab_in(x, pos0, pool_hist, w["norm_mix"][0], win_p, w["q_norm"][0], wq, w["kv_norm"][0],
                  w["w_pool"][0], w["pool_scale"][0], wk, wv, place, tm=tm_proj, emit_kv=prompt)
    q, ckv, kpe, pool_out, pool_last = outs[:5]
    if prompt:
        k, v = outs[5:]
        attn = _flash_attention(q, k, v, tq=tq)
    else:
        wabs, wuv_place = _prep_latent_weights(w["w_uk"][0], w["w_uv"][0])
        attn = _latent_attention(q, wabs, cache[0], cache[1], ckv, kpe, wuv_place, pos0=pos0)
    w_out = w["w_out_ab"][0].astype(BF16)
    x1 = _out_ffn(x.reshape(b * s, D_MODEL),
                  [pool_out.reshape(b * s, POOL_DIM), attn.reshape(b * s, MLA_OUT)],
                  [w_out[:POOL_DIM], w_out[POOL_DIM:]],
                  w["norm_ffn"][0], w["w_gate"][0], w["w_up"][0], w["w_down"][0], None, tm=tm_ffn)

    z, xs, bm, cm, dt, dtt, conv_last = _ssd_in(x1.reshape(b, s, D_MODEL), conv_hist, w["norm_mix"][1],
                                                w["w_in_c"][0], w["conv_w"][0], w["conv_b"][0],
                                                w["dt_bias"][0], tm=tm_proj)
    y, h_fin = _ssd_scan(xs, z, bm, cm, dt, dtt, w["a_log"][0], w["d_skip"][0], w["ssm_norm"][0], ssm_hist,
                         ln=ln)
    x2 = _out_ffn(x1, [y.reshape(b * s, D_INNER)], [w["w_out_c"][0].astype(BF16)],
                  w["norm_ffn"][1], w["w_gate"][1], w["w_up"][1], w["w_down"][1], w["norm_final"], tm=tm_ffn)
    return (x2.reshape(b, s, D_MODEL), ckv[None], kpe[None], pool_last[None, :, POOL_KEEP - POOL_HIST:],
            conv_last[None, :, CONV_CARRY - (D_CONV - 1):], h_fin[None])


def kernel(x_prompt, x_sample, cache_ckv, cache_kpe, state_pool, state_conv, state_ssm, norm_mix, norm_ffn, norm_final, w_in_ab, w_pool, pool_scale, q_norm, w_uq, kv_norm, w_uk, w_uv, w_out_ab, w_in_c, conv_w, conv_b, dt_bias, a_log, d_skip, ssm_norm, w_out_c, w_gate, w_up, w_down):
    assert norm_mix.shape[0] == 2 and w_in_ab.shape[0] == 1 and w_in_c.shape[0] == 1
    w = dict(norm_mix=norm_mix, norm_ffn=norm_ffn, norm_final=norm_final, w_in_ab=w_in_ab, w_pool=w_pool,
             pool_scale=pool_scale, q_norm=q_norm, w_uq=w_uq, kv_norm=kv_norm, w_uk=w_uk, w_uv=w_uv,
             w_out_ab=w_out_ab, w_in_c=w_in_c, conv_w=conv_w, conv_b=conv_b, dt_bias=dt_bias, a_log=a_log,
             d_skip=d_skip, ssm_norm=ssm_norm, w_out_c=w_out_c, w_gate=w_gate, w_up=w_up, w_down=w_down)
    bp, sp, _ = x_prompt.shape
    bs, ss, _ = x_sample.shape
    past = cache_ckv.shape[2]

    prompt = _trunk(x_prompt, 0, None,
                    jnp.zeros((bp, POOL_HIST, POOL_DIM), F32), jnp.zeros((bp, D_CONV - 1, CONV_DIM), F32),
                    jnp.zeros((bp, SSM_HEADS, SSM_HEADDIM, D_STATE), F32), w,
                    tm_proj=512, tm_ffn=512, tq=1024, ln=128)
    sample = _trunk(x_sample, past, (cache_ckv[0], cache_kpe[0]), state_pool[0], state_conv[0], state_ssm[0], w,
                    tm_proj=ss, tm_ffn=bs * ss, tq=None, ln=ss)
    return (prompt[0], sample[0]) + prompt[1:] + sample[1:]
```

```python
import functools
import math

import numpy as np
import jax
import jax.numpy as jnp
from jax import lax
from jax.experimental import pallas as pl
from jax.experimental.pallas import tpu as pltpu

F32 = jnp.float32
BF16 = jnp.bfloat16

D_MODEL = 1024
CHUNK = 64
EPS = 1e-6
NEG = -1e30

POOL_WINDOWS = (2, 4, 8, 16)
POOL_GROUPS = 4
POOL_DIM = D_MODEL // 2
POOL_GDIM = POOL_DIM // POOL_GROUPS
POOL_HIST = max(POOL_WINDOWS) - 1

MLA_HEADS = 8
QK_NOPE = 64
QK_ROPE = 32
ROPE_HALF = QK_ROPE // 2
V_HEAD = 64
Q_LORA = 384
KV_LORA = 256
MLA_OUT = MLA_HEADS * V_HEAD
ROPE_THETA = 10000.0
SM_SCALE = (QK_NOPE + QK_ROPE) ** -0.5
LOG2E = math.log2(math.e)
Q_SCALE = SM_SCALE * LOG2E
IN_AB = POOL_DIM + Q_LORA + KV_LORA + QK_ROPE

D_INNER = 2 * D_MODEL
SSM_HEADDIM = 64
SSM_HEADS = D_INNER // SSM_HEADDIM
SSM_GROUPS = 4
SSM_HPG = SSM_HEADS // SSM_GROUPS
D_STATE = 128
D_CONV = 4
CONV_DIM = D_INNER + 2 * SSM_GROUPS * D_STATE
IN_C = D_INNER + CONV_DIM + SSM_HEADS
GROUP_COLS = D_INNER // SSM_GROUPS

D_FF = -(-8 * D_MODEL // (3 * 256)) * 256

LANES = 128
SUBLANES = 8
HEAD_PAD = LANES
ROWSUM_LANE_EVEN = V_HEAD
ROWSUM_LANE_ODD = 0
POOL_CARRY = 4 * SUBLANES
POOL_KEEP = 2 * SUBLANES
CONV_CARRY = 2 * SUBLANES
CONV_CHUNK = 256
IN_AB_PAD = 10 * LANES
VMEM_LIMIT = 56 * 1024 * 1024

OFF_Q = POOL_DIM
OFF_KV = OFF_Q + Q_LORA
OFF_KPE = OFF_KV + KV_LORA


def _const_spec(shape):
    nd = len(shape)
    return pl.BlockSpec(shape, lambda *_: (0,) * nd, pipeline_mode=pl.Buffered(1))


def _rms(x, g):
    return x * lax.rsqrt(jnp.mean(x * x, axis=-1, keepdims=True) + EPS) * g


def _dot(a, b):
    return jnp.dot(a, b, preferred_element_type=F32)


def _dot_nt(a, b):
    return lax.dot_general(a, b, (((1,), (1,)), ((), ())), preferred_element_type=F32)


def _dot_tn(a, b):
    return lax.dot_general(a, b, (((0,), (0,)), ((), ())), preferred_element_type=F32)


def _split3(x):
    hi = x.astype(BF16)
    r1 = x - hi.astype(F32)
    mid = r1.astype(BF16)
    lo = (r1 - mid.astype(F32)).astype(BF16)
    return hi, mid, lo


def _dot_sel_r(x, sel):
    hi, mid, lo = _split3(x)
    return _dot(hi, sel) + _dot(mid, sel) + _dot(lo, sel)


def _dot_sel_l(sel, x):
    hi, mid, lo = _split3(x)
    return _dot(sel, hi) + _dot(sel, mid) + _dot(sel, lo)


def _ab_in_kernel(*refs, tm, pos0, n_t, emit_kv):
    (x_ref, g_ref, win_ref, qn_ref, wuq_ref, kvn_ref, wpool_ref, pscale_ref, tab_ref, hist_ref) = refs[:10]
    refs = refs[10:]
    if emit_kv:
        wuk_ref, wuv_ref, place_ref = refs[:3]
        refs = refs[3:]
    q_ref, ckv_ref, kpe_ref, pool_ref, plast_ref = refs[:5]
    refs = refs[5:]
    if emit_kv:
        k_ref, v_ref = refs[:2]
        refs = refs[2:]
    (cat_scr,) = refs

    t = pl.program_id(1)
    x = x_ref[...]
    hb = _rms(x, g_ref[...]).astype(BF16)
    proj = _dot(hb, win_ref[...])
    u_pool = proj[:, :POOL_DIM]
    q_lat = proj[:, OFF_Q:OFF_KV]
    kv_lat = proj[:, OFF_KV:OFF_KPE]
    kblk = proj[:, OFF_KPE:OFF_KPE + LANES]

    @pl.when(t == 0)
    def _():
        cat_scr[...] = hist_ref[...]

    ext2d = jnp.concatenate([cat_scr[...], u_pool], axis=0)
    ext = ext2d.reshape((POOL_CARRY + tm) // SUBLANES, SUBLANES, POOL_DIM)
    lead = POOL_CARRY // SUBLANES
    pos = pos0 + t * tm + lax.broadcasted_iota(jnp.int32, (tm, 1), 0)
    for g, w in enumerate(POOL_WINDOWS):
        cols = slice(g * POOL_GDIM, (g + 1) * POOL_GDIM)
        win, span, dropped = ext[:, :, cols], 1, 0
        while span < w:
            prev = win[:-1] if span == SUBLANES else _shift_rows(win, span)
            win, span, dropped = win[1:] + prev, 2 * span, dropped + 1
        win = win[lead - dropped:].reshape(tm, POOL_GDIM)
        u_g = u_pool[:, cols]
        cnt = jnp.minimum(pos + 1, w).astype(F32)
        pooled = win / cnt - u_g
        mixed = _dot(pooled.astype(BF16), wpool_ref[g]) * pscale_ref[:, cols]
        pool_ref[:, cols] = mixed.astype(pool_ref.dtype)
    cat_scr[...] = ext2d[tm:, :]
    plast_ref[...] = ext2d[POOL_CARRY + tm - POOL_KEEP:, :]

    tab = tab_ref[...]
    up = pltpu.roll(tab, ROPE_HALF, 1)
    down = pltpu.roll(tab, LANES - ROPE_HALF, 1)
    ln_i = lax.broadcasted_iota(jnp.int32, (tm, LANES), 1)
    x1_q, x2_q = QK_NOPE, QK_NOPE + ROPE_HALF
    tq_c = jnp.where(ln_i < x1_q, 1.0, jnp.where(ln_i < x2_q, tab, jnp.where(ln_i < x2_q + ROPE_HALF, up, 0.0)))
    tq_a = jnp.where((ln_i >= x2_q) & (ln_i < x2_q + ROPE_HALF), tab, 0.0)
    tq_b = jnp.where((ln_i >= x1_q) & (ln_i < x2_q), -down, 0.0)
    tk_c = jnp.where(ln_i < ROPE_HALF, tab, jnp.where(ln_i < QK_ROPE, up, 0.0))
    tk_s = jnp.where(ln_i < ROPE_HALF, -down, jnp.where(ln_i < QK_ROPE, tab, 0.0))

    qn = _rms(q_lat, qn_ref[...]).astype(BF16)
    q = _dot(qn, wuq_ref[...])
    for h in range(MLA_HEADS):
        qh = q[:, h * HEAD_PAD:(h + 1) * HEAD_PAD]
        qr = (qh * tq_c + pltpu.roll(qh, ROPE_HALF, 1) * tq_a
              + pltpu.roll(qh, HEAD_PAD - ROPE_HALF, 1) * tq_b)
        q_ref[h] = qr.astype(q_ref.dtype)

    ckv = _rms(kv_lat, kvn_ref[...])
    ckv_ref[...] = ckv
    kpe_blk = kblk * tk_c + pltpu.roll(kblk, LANES - QK_ROPE, 1) * tk_s
    if emit_kv:
        kpe_ref[...] = kpe_blk.T[:QK_ROPE, :]
    else:
        kpe_ref[...] = kpe_blk[:, :QK_ROPE]

    if emit_kv:
        ckv_b = ckv.astype(BF16)
        k_all = _dot(ckv_b, wuk_ref[...]) + _dot(kpe_blk.astype(BF16), place_ref[...])
        for h in range(MLA_HEADS):
            k_ref[h] = k_all[:, h * HEAD_PAD:(h + 1) * HEAD_PAD].astype(k_ref.dtype)
        v_all = _dot(ckv_b, wuv_ref[...])
        lane = lax.broadcasted_iota(jnp.int32, (tm, LANES), 1)
        for hp in range(MLA_HEADS // 2):
            vp = v_all[:, hp * LANES:(hp + 1) * LANES]
            even = jnp.where(lane < V_HEAD, vp, jnp.where(lane == ROWSUM_LANE_EVEN, 1.0, 0.0))
            odd = jnp.where(lane >= V_HEAD, vp, jnp.where(lane == ROWSUM_LANE_ODD, 1.0, 0.0))
            v_ref[2 * hp] = even.astype(v_ref.dtype)
            v_ref[2 * hp + 1] = odd.astype(v_ref.dtype)


def _rope_table(pos0, s):
    pos = (pos0 + jnp.arange(s)).astype(F32)
    inv = ROPE_THETA ** (-jnp.arange(ROPE_HALF, dtype=F32) / ROPE_HALF)
    ang = pos[:, None] * inv[None, :]
    return jnp.tile(jnp.concatenate([jnp.cos(ang), jnp.sin(ang)], axis=1), (1, LANES // QK_ROPE))


def _prep_ab_weights(w_in, w_uq, w_uk, w_uv):
    kpe_w = w_in[:, OFF_KPE:]
    swapped = jnp.concatenate([kpe_w[:, ROPE_HALF:], kpe_w[:, :ROPE_HALF]], axis=1)
    pad = jnp.zeros((D_MODEL, IN_AB_PAD - IN_AB - QK_ROPE), w_in.dtype)
    win_p = jnp.concatenate([w_in, swapped, pad], axis=1).astype(BF16)

    tail = HEAD_PAD - QK_NOPE - QK_ROPE
    wq = w_uq.reshape(Q_LORA, MLA_HEADS, QK_NOPE + QK_ROPE)
    wq = jnp.pad(wq, ((0, 0), (0, 0), (0, tail))).reshape(Q_LORA, MLA_HEADS * HEAD_PAD).astype(BF16)
    wk = jnp.pad(w_uk, ((0, 0), (0, 0), (0, HEAD_PAD - QK_NOPE))).reshape(KV_LORA, MLA_HEADS * HEAD_PAD)
    wk = wk.astype(BF16)
    wv = w_uv.reshape(KV_LORA, MLA_OUT).astype(BF16)
    place = np.zeros((LANES, MLA_HEADS * HEAD_PAD), np.float32)
    for h in range(MLA_HEADS):
        for i in range(QK_ROPE):
            place[i, h * HEAD_PAD + QK_NOPE + i] = 1.0
    return win_p, wq, wk, wv, jnp.asarray(place, BF16)


def _ab_in(x, pos0, pool_hist, g_mix, win_p, q_norm, wq, kv_norm, w_pool, pool_scale, wk, wv, place,
           *, tm, emit_kv):
    b, s, _ = x.shape
    n_t = s // tm
    tab = _rope_table(pos0, s)
    hist = jnp.pad(pool_hist.astype(F32), ((0, 0), (POOL_CARRY - POOL_HIST, 0), (0, 0)))
    row = lambda shape: pl.BlockSpec((None,) + shape, lambda bi, ti: (bi, ti, 0))

    in_specs = [
        row((tm, D_MODEL)),
        _const_spec((1, D_MODEL)),
        _const_spec((D_MODEL, IN_AB_PAD)),
        _const_spec((1, Q_LORA)),
        _const_spec((Q_LORA, MLA_HEADS * HEAD_PAD)),
        _const_spec((1, KV_LORA)),
        _const_spec((POOL_GROUPS, POOL_GDIM, POOL_GDIM)),
        _const_spec((1, POOL_DIM)),
        pl.BlockSpec((tm, LANES), lambda bi, ti: (ti, 0)),
        pl.BlockSpec((None, POOL_CARRY, POOL_DIM), lambda bi, ti: (bi, 0, 0)),
    ]
    args = [x, g_mix.reshape(1, -1), win_p, (q_norm * Q_SCALE).reshape(1, -1), wq, kv_norm.reshape(1, -1),
            w_pool.astype(BF16), pool_scale.reshape(1, -1), tab, hist]
    if emit_kv:
        in_specs += [_const_spec((KV_LORA, MLA_HEADS * HEAD_PAD)), _const_spec((KV_LORA, MLA_OUT)),
                     _const_spec((LANES, MLA_HEADS * HEAD_PAD))]
        args += [wk, wv, place]

    head_spec = pl.BlockSpec((None, MLA_HEADS, tm, HEAD_PAD), lambda bi, ti: (bi, 0, ti, 0))
    out_shape = [
        jax.ShapeDtypeStruct((b, MLA_HEADS, s, HEAD_PAD), BF16),
        jax.ShapeDtypeStruct((b, s, KV_LORA), F32),
        jax.ShapeDtypeStruct((b, QK_ROPE, s) if emit_kv else (b, s, QK_ROPE), F32),
        jax.ShapeDtypeStruct((b, s, POOL_DIM), BF16),
        jax.ShapeDtypeStruct((b, POOL_KEEP, POOL_DIM), F32),
    ]
    out_specs = [
        head_spec,
        row((tm, KV_LORA)),
        pl.BlockSpec((None, QK_ROPE, tm), lambda bi, ti: (bi, 0, ti)) if emit_kv else row((tm, QK_ROPE)),
        row((tm, POOL_DIM)),
        pl.BlockSpec((None, POOL_KEEP, POOL_DIM), lambda bi, ti: (bi, 0, 0)),
    ]
    if emit_kv:
        out_shape += [jax.ShapeDtypeStruct((b, MLA_HEADS, s, HEAD_PAD), BF16),
                      jax.ShapeDtypeStruct((b, MLA_HEADS, s, LANES), BF16)]
        out_specs += [head_spec, head_spec]

    return pl.pallas_call(
        functools.partial(_ab_in_kernel, tm=tm, pos0=pos0, n_t=n_t, emit_kv=emit_kv),
        out_shape=out_shape,
        grid=(b, n_t),
        in_specs=in_specs,
        out_specs=out_specs,
        scratch_shapes=[pltpu.VMEM((POOL_CARRY, POOL_DIM), F32)],
        compiler_params=pltpu.CompilerParams(dimension_semantics=("arbitrary", "arbitrary"),
                                             vmem_limit_bytes=VMEM_LIMIT),
        name="ab_in",
    )(*args)


def _flash_kernel(q_ref, k_ref, v_ref, o_ref, m_scr, acc_scr, *, tq, tk):
    t = pl.program_id(2)
    heads = (0, 1)

    def pair(first, diagonal):
        s = {}
        for n in range(2):
            off = pl.multiple_of((first + n) * tk, tk)
            r0 = n * tk if diagonal else 0
            for hh in heads:
                sh = _dot_nt(q_ref[hh, r0:, :], k_ref[hh, pl.ds(off, tk), :])
                if diagonal:
                    qc = (r0 + lax.broadcasted_iota(jnp.int32, (tq - r0, tk), 0)) // CHUNK
                    kc = (n * tk + lax.broadcasted_iota(jnp.int32, (tq - r0, tk), 1)) // CHUNK
                    sh = jnp.where(kc <= qc, sh, NEG)
                s[n, hh] = sh
        for n in range(2):
            off = pl.multiple_of((first + n) * tk, tk)
            r0 = n * tk if diagonal else 0
            for hh in heads:
                sh = s[n, hh]
                m_prev = m_scr[hh, r0:, :]
                m_new = jnp.maximum(m_prev, jnp.max(sh, axis=-1, keepdims=True))
                alpha = jnp.exp2(m_prev - m_new)
                p = jnp.concatenate([jnp.exp2(sh[:, c * LANES:(c + 1) * LANES] - m_new)
                                     for c in range(tk // LANES)], axis=1).astype(BF16)
                acc_scr[hh, r0:, :] = alpha * acc_scr[hh, r0:, :] + _dot(p, v_ref[hh, pl.ds(off, tk), :])
                m_scr[hh, r0:, :] = m_new

    m_scr[...] = jnp.full(m_scr.shape, NEG, F32)
    acc_scr[...] = jnp.zeros(acc_scr.shape, F32)

    def body(i, carry):
        pair(2 * i, False)
        return carry

    lax.fori_loop(0, t, body, 0)
    pair(2 * t, True)

    acc0, acc1 = acc_scr[0], acc_scr[1]
    out0 = acc0 / acc0[:, ROWSUM_LANE_EVEN:ROWSUM_LANE_EVEN + 1]
    out1 = acc1 / acc1[:, ROWSUM_LANE_ODD:ROWSUM_LANE_ODD + 1]
    lane = lax.broadcasted_iota(jnp.int32, (tq, LANES), 1)
    o_ref[...] = jnp.where(lane < V_HEAD, out0, out1).astype(o_ref.dtype)


def _flash_attention(q, k, v, *, tq):
    b, h, s, _ = q.shape
    tk = tq // 2
    assert tk % CHUNK == 0 and s % tq == 0
    pair = lambda n: pl.BlockSpec((None, 2, n, LANES), lambda bi, hp, ti: (bi, hp, 0, 0))
    return pl.pallas_call(
        functools.partial(_flash_kernel, tq=tq, tk=tk),
        out_shape=jax.ShapeDtypeStruct((b, s, MLA_OUT), BF16),
        grid=(b, h // 2, s // tq),
        in_specs=[
            pl.BlockSpec((None, 2, tq, HEAD_PAD), lambda bi, hp, ti: (bi, hp, ti, 0)),
            pair(s),
            pair(s),
        ],
        out_specs=pl.BlockSpec((None, tq, 2 * V_HEAD), lambda bi, hp, ti: (bi, ti, hp)),
        scratch_shapes=[pltpu.VMEM((2, tq, LANES), F32), pltpu.VMEM((2, tq, LANES), F32)],
        compiler_params=pltpu.CompilerParams(dimension_semantics=("arbitrary", "arbitrary", "arbitrary"),
                                             vmem_limit_bytes=VMEM_LIMIT),
        name="flash_attention",
    )(q, k, v)


def _latent_attn_kernel(q_ref, wabs_ref, cckv_ref, ckpe_ref, nckv_ref, nkpe_ref, wuv_ref, o_ref,
                        *, sq, past, pos0):
    rows = MLA_HEADS * sq
    qcat = jnp.concatenate([_dot(q_ref[h], wabs_ref[h]) for h in range(MLA_HEADS)], axis=0)
    qa = qcat[:, :KV_LORA].astype(BF16)
    qp = qcat[:, KV_LORA:].astype(BF16)
    ck = cckv_ref[...].astype(BF16)
    kp_t = ckpe_ref[...].astype(BF16)
    nk = nckv_ref[...].astype(BF16)
    np_ = nkpe_ref[...].astype(BF16)
    s_past = _dot_nt(qa, ck) + _dot(qp, kp_t)
    s_new = _dot_nt(qa, nk) + _dot_nt(qp, np_)

    q_chunk = (pos0 + lax.broadcasted_iota(jnp.int32, (rows, 1), 0) % sq) // CHUNK
    kc_past = (pos0 - past + lax.broadcasted_iota(jnp.int32, (1, past), 1)) // CHUNK
    kc_new = (pos0 + lax.broadcasted_iota(jnp.int32, (1, sq), 1)) // CHUNK
    s_past = jnp.where(kc_past <= q_chunk, s_past, NEG)
    s_new = jnp.where(kc_new <= q_chunk, s_new, NEG)

    m = jnp.maximum(jnp.max(s_past, axis=-1, keepdims=True), jnp.max(s_new, axis=-1, keepdims=True))
    p_past = jnp.exp2(s_past - m)
    p_new = jnp.exp2(s_new - m)
    l = jnp.sum(p_past, axis=-1, keepdims=True) + jnp.sum(p_new, axis=-1, keepdims=True)
    o_lat = (_dot(p_past.astype(BF16), ck) + _dot(p_new.astype(BF16), nk)) / l
    o_lat = o_lat.astype(BF16)
    out = _dot(o_lat[0:sq], wuv_ref[0])
    for h in range(1, MLA_HEADS):
        out = out + _dot(o_lat[h * sq:(h + 1) * sq], wuv_ref[h])
    o_ref[...] = out.astype(o_ref.dtype)


def _prep_latent_weights(w_uk, w_uv):
    wabs = np.zeros((MLA_HEADS, HEAD_PAD, KV_LORA + QK_ROPE), np.float32)
    for i in range(QK_ROPE):
        wabs[:, QK_NOPE + i, KV_LORA + i] = 1.0
    wabs = jnp.asarray(wabs).at[:, :QK_NOPE, :KV_LORA].set(jnp.transpose(w_uk, (1, 2, 0)))
    wuv_place = jnp.zeros((MLA_HEADS, KV_LORA, MLA_OUT), F32)
    for h in range(MLA_HEADS):
        wuv_place = wuv_place.at[h, :, h * V_HEAD:(h + 1) * V_HEAD].set(w_uv[:, h, :])
    return wabs.astype(BF16), wuv_place.astype(BF16)


def _latent_attention(q, wabs, cache_ckv, cache_kpe, ckv_new, kpe_new, wuv_place, *, pos0):
    b, h, sq, _ = q.shape
    past = cache_ckv.shape[1]
    seq = lambda n, d: pl.BlockSpec((None, n, d), lambda bi: (bi, 0, 0))
    return pl.pallas_call(
        functools.partial(_latent_attn_kernel, sq=sq, past=past, pos0=pos0),
        out_shape=jax.ShapeDtypeStruct((b, sq, MLA_OUT), BF16),
        grid=(b,),
        in_specs=[
            pl.BlockSpec((None, h, sq, HEAD_PAD), lambda bi: (bi, 0, 0, 0)),
            _const_spec((h, HEAD_PAD, KV_LORA + QK_ROPE)),
            seq(past, KV_LORA), seq(QK_ROPE, past), seq(sq, KV_LORA), seq(sq, QK_ROPE),
            _const_spec((h, KV_LORA, MLA_OUT)),
        ],
        out_specs=seq(sq, MLA_OUT),
        compiler_params=pltpu.CompilerParams(dimension_semantics=("arbitrary",),
                                             vmem_limit_bytes=VMEM_LIMIT),
        name="latent_attention",
    )(q, wabs, cache_ckv, cache_kpe, ckv_new, kpe_new, wuv_place)


def _out_ffn_kernel(*refs, n_act, final_norm):
    x_ref = refs[0]
    act_refs = refs[1:1 + n_act]
    wout_refs = refs[1 + n_act:1 + 2 * n_act]
    gffn_ref, wg_ref, wu_ref, wd_ref = refs[1 + 2 * n_act:5 + 2 * n_act]
    rest = refs[5 + 2 * n_act:]
    if final_norm:
        gfin_ref, o_ref = rest
    else:
        (o_ref,) = rest

    y = _dot(act_refs[0][...], wout_refs[0][...])
    for a_ref, w_ref in zip(act_refs[1:], wout_refs[1:]):
        y = y + _dot(a_ref[...], w_ref[...])
    x1 = x_ref[...] + y
    h2 = _rms(x1, gffn_ref[...]).astype(BF16)
    gate = _dot(h2, wg_ref[...])
    up = _dot(h2, wu_ref[...])
    act = (gate * jax.nn.sigmoid(gate) * up).astype(BF16)
    x2 = x1 + _dot(act, wd_ref[...])
    if final_norm:
        x2 = _rms(x2, gfin_ref[...])
    o_ref[...] = x2


def _out_ffn(x, acts, wouts, g_ffn, ffn_w, layer, g_final, *, tm):
    t, _ = x.shape
    n_act = len(acts)
    row = lambda d: pl.BlockSpec((tm, d), lambda i: (i, 0))
    slab = lambda w: pl.BlockSpec((None,) + w.shape[1:], lambda i: (layer, 0, 0), pipeline_mode=pl.Buffered(1))
    in_specs = ([row(D_MODEL)] + [row(a.shape[1]) for a in acts] + [_const_spec(w.shape) for w in wouts]
                + [_const_spec((1, D_MODEL))] + [slab(w) for w in ffn_w])
    args = [x, *acts, *wouts, g_ffn.reshape(1, -1), *ffn_w]
    if g_final is not None:
        in_specs.append(_const_spec((1, D_MODEL)))
        args.append(g_final.reshape(1, -1))
    return pl.pallas_call(
        functools.partial(_out_ffn_kernel, n_act=n_act, final_norm=g_final is not None),
        out_shape=jax.ShapeDtypeStruct((t, D_MODEL), F32),
        grid=(t // tm,),
        in_specs=in_specs,
        out_specs=row(D_MODEL),
        compiler_params=pltpu.CompilerParams(dimension_semantics=("arbitrary",),
                                             vmem_limit_bytes=VMEM_LIMIT),
        name="out_ffn",
    )(*args)


def _softplus(x):
    return jnp.maximum(x, 0.0) + jnp.log1p(jnp.exp(-jnp.abs(x)))


def _shift_rows(v, k):
    rot = pltpu.roll(v, k, axis=1)
    sub = lax.broadcasted_iota(jnp.int32, (1,) + v.shape[1:], 1)
    return jnp.where(sub < k, rot[:-1], rot[1:])


def _ssd_in_kernel(x_ref, g_ref, win_ref, wdtt_ref, cw_ref, cb_ref, dtb_ref, dtbt_ref,
                   hist_ref, z_ref, xs_ref, b_ref, c_ref, dt_ref, dtt_ref, clast_ref, carry_scr, *, tm):
    t = pl.program_id(1)
    hb = _rms(x_ref[...], g_ref[...]).astype(BF16)
    dt_ref[...] = _softplus(_dot(hb, win_ref[:, D_INNER + CONV_DIM:]) + dtb_ref[...])
    dtt_ref[...] = _softplus(_dot_nt(wdtt_ref[...], hb) + dtbt_ref[...])

    @pl.when(t == 0)
    def _():
        carry_scr[...] = hist_ref[...]

    gn = SSM_GROUPS * D_STATE
    outs = ((xs_ref, 0, D_INNER), (b_ref, D_INNER, gn), (c_ref, D_INNER + gn, gn))
    n_grp = tm // SUBLANES
    starts = list(range(0, CONV_DIM, CONV_CHUNK))
    xbc_w = lambda c0: win_ref[:, D_INNER + c0:D_INNER + c0 + CONV_CHUNK]
    xc_next = _dot(hb, xbc_w(0))
    for i, c0 in enumerate(starts):
        cols = slice(c0, c0 + CONV_CHUNK)
        xc = xc_next
        if i + 1 < len(starts):
            xc_next = _dot(hb, xbc_w(starts[i + 1]))
        if c0 < D_INNER:
            z_ref[:, cols] = _dot(hb, win_ref[:, cols]).astype(z_ref.dtype)
        ext = jnp.concatenate([carry_scr[:, cols], xc], axis=0)
        ext = ext.reshape(n_grp + CONV_CARRY // SUBLANES, SUBLANES, CONV_CHUNK)
        w = [cw_ref[k:k + 1, cols] for k in range(D_CONV)]
        ref, start = next((r, s0) for r, s0, width in outs if s0 <= c0 < s0 + width)
        x1 = _shift_rows(ext, 1)
        u = ext[2:] * w[3] + x1[1:] * w[2]
        v = ext[1:] * w[1] + x1 * w[0]
        conv = (u + _shift_rows(v, 2) + cb_ref[:, cols]).reshape(tm, CONV_CHUNK)
        act = conv * jax.nn.sigmoid(conv)
        ref[:, c0 - start:c0 - start + CONV_CHUNK] = act.astype(ref.dtype)
        carry_scr[:, cols] = xc[tm - CONV_CARRY:, :]
        clast_ref[:, cols] = xc[tm - CONV_CARRY:, :]


def _ssd_in(x, conv_hist, g_mix, w_in, conv_w, conv_b, dt_bias, *, tm):
    b, s, _ = x.shape
    n_t = s // tm
    gn = SSM_GROUPS * D_STATE
    win_b = w_in.astype(BF16)
    wdt_t = w_in[:, D_INNER + CONV_DIM:].T.astype(BF16)
    hist = jnp.pad(conv_hist.astype(F32), ((0, 0), (CONV_CARRY - (D_CONV - 1), 0), (0, 0)))
    row = lambda d: pl.BlockSpec((None, tm, d), lambda bi, ti: (bi, ti, 0))
    return pl.pallas_call(
        functools.partial(_ssd_in_kernel, tm=tm),
        out_shape=[
            jax.ShapeDtypeStruct((b, s, D_INNER), BF16),
            jax.ShapeDtypeStruct((b, s, D_INNER), BF16),
            jax.ShapeDtypeStruct((b, s, gn), BF16),
            jax.ShapeDtypeStruct((b, s, gn), BF16),
            jax.ShapeDtypeStruct((b, s, SSM_HEADS), F32),
            jax.ShapeDtypeStruct((b, SSM_HEADS, s), F32),
            jax.ShapeDtypeStruct((b, CONV_CARRY, CONV_DIM), F32),
        ],
        grid=(b, n_t),
        in_specs=[
            row(D_MODEL),
            _const_spec((1, D_MODEL)),
            _const_spec((D_MODEL, IN_C)),
            _const_spec((SSM_HEADS, D_MODEL)),
            _const_spec((D_CONV, CONV_DIM)),
            _const_spec((1, CONV_DIM)),
            _const_spec((1, SSM_HEADS)),
            _const_spec((SSM_HEADS, 1)),
            pl.BlockSpec((None, CONV_CARRY, CONV_DIM), lambda bi, ti: (bi, 0, 0)),
        ],
        out_specs=[
            row(D_INNER), row(D_INNER), row(gn), row(gn), row(SSM_HEADS),
            pl.BlockSpec((None, SSM_HEADS, tm), lambda bi, ti: (bi, 0, ti)),
            pl.BlockSpec((None, CONV_CARRY, CONV_DIM), lambda bi, ti: (bi, 0, 0)),
        ],
        scratch_shapes=[pltpu.VMEM((CONV_CARRY, CONV_DIM), F32)],
        compiler_params=pltpu.CompilerParams(dimension_semantics=("arbitrary", "arbitrary"),
                                             vmem_limit_bytes=VMEM_LIMIT),
        name="ssd_in",
    )(x, g_mix.reshape(1, -1), win_b, wdt_t, conv_w, conv_b.reshape(1, -1),
      dt_bias.reshape(1, -1), dt_bias.reshape(-1, 1), hist)


def _ssd_scan_kernel(xs_ref, z_ref, b_ref, c_ref, dt_ref, dtt_ref, alog_ref, alogt_ref, dexp_ref, norm_ref,
                     expand_ref, h0_ref, y_ref, hfin_ref, h_scr, *, ln, n_c):
    c = pl.program_id(1)

    @pl.when(c == 0)
    def _():
        for g in range(SSM_GROUPS):
            h_scr[g] = h0_ref[g * GROUP_COLS:(g + 1) * GROUP_COLS, :].T

    row_i = lax.broadcasted_iota(jnp.int32, (ln, ln), 0)
    col_i = lax.broadcasted_iota(jnp.int32, (ln, ln), 1)
    causal = row_i >= col_i
    tril = jnp.where(causal, 1.0, 0.0).astype(BF16)
    triu = jnp.where(row_i <= col_i, 1.0, 0.0).astype(BF16)
    lane = lax.broadcasted_iota(jnp.int32, (ln, LANES), 1)

    dt = dt_ref[...]
    dtt = dtt_ref[...]
    acum = _dot_sel_l(tril, dt * -jnp.exp(alog_ref[...]))
    acum_t = _dot_sel_r(dtt * -jnp.exp(alogt_ref[...]), triu)
    acum2 = acum * LOG2E
    key2_t = (acum_t - jnp.log(dtt)) * LOG2E
    last = acum[ln - 1:ln, :]
    ea_hi, ea_mid, _ = _split3(jnp.exp(acum))
    wj_b = (jnp.exp(last - acum) * dt).astype(BF16)
    pair = 2 * SSM_HEADDIM

    for g in range(SSM_GROUPS):
        gcols = slice(g * GROUP_COLS, (g + 1) * GROUP_COLS)
        expand = expand_ref[:, gcols]
        ea_e = _dot(ea_hi, expand) + _dot(ea_mid, expand)
        wj_e = _dot(wj_b, expand).astype(BF16)
        bg = b_ref[:, g * D_STATE:(g + 1) * D_STATE]
        cg = c_ref[:, g * D_STATE:(g + 1) * D_STATE]
        cb = _dot_nt(cg, bg)
        y_state = _dot(cg, h_scr[g].astype(BF16))
        gated = []
        for p in range(SSM_HPG // 2):
            h0 = g * SSM_HPG + 2 * p
            cols = slice(h0 * SSM_HEADDIM, (h0 + 2) * SSM_HEADDIM)
            pcols = slice(p * pair, (p + 1) * pair)
            scores = []
            for h in (h0, h0 + 1):
                seg = jnp.where(causal, acum2[:, h:h + 1] - key2_t[h:h + 1, :], NEG)
                scores.append((cb * jnp.exp2(seg)).astype(BF16))
            xp = xs_ref[:, cols]
            zero = jnp.zeros_like(xp)
            rhs = jnp.concatenate([jnp.where(lane < SSM_HEADDIM, xp, zero),
                                   jnp.where(lane >= SSM_HEADDIM, xp, zero)], axis=0)
            y = _dot(jnp.concatenate(scores, axis=1), rhs)
            y = y + y_state[:, pcols] * ea_e[:, pcols]
            y = y + xp.astype(F32) * dexp_ref[:, cols]
            zc = z_ref[:, cols].astype(F32)
            gated.append(y * (zc * jax.nn.sigmoid(zc)))
        yg = jnp.concatenate(gated, axis=1)
        yg = yg * lax.rsqrt(jnp.mean(yg * yg, axis=-1, keepdims=True) + EPS)
        y_ref[:, gcols] = (yg * norm_ref[:, gcols]).astype(y_ref.dtype)
        xw = xs_ref[:, gcols] * wj_e
        h_scr[g] = h_scr[g] * ea_e[ln - 1:ln, :] + _dot_tn(bg, xw)

    @pl.when(c == n_c - 1)
    def _():
        for g in range(SSM_GROUPS):
            hfin_ref[g * GROUP_COLS:(g + 1) * GROUP_COLS, :] = h_scr[g].T


def _ssd_scan(xs, z, bm, cm, dt, dtt, a_log, d_skip, ssm_norm, h0, *, ln):
    b, s, _ = xs.shape
    n_c = s // ln
    gn = SSM_GROUPS * D_STATE
    expand = jnp.asarray(np.kron(np.eye(SSM_HEADS, dtype=np.float32), np.ones((1, SSM_HEADDIM), np.float32)),
                         BF16)
    d_exp = jnp.repeat(d_skip.astype(F32), SSM_HEADDIM).reshape(1, D_INNER)
    row = lambda d: pl.BlockSpec((None, ln, d), lambda bi, ci: (bi, ci, 0))
    state_spec = pl.BlockSpec((None, D_INNER, D_STATE), lambda bi, ci: (bi, 0, 0))
    y, h_fin = pl.pallas_call(
        functools.partial(_ssd_scan_kernel, ln=ln, n_c=n_c),
        out_shape=[jax.ShapeDtypeStruct((b, s, D_INNER), BF16),
                   jax.ShapeDtypeStruct((b, D_INNER, D_STATE), F32)],
        grid=(b, n_c),
        in_specs=[
            row(D_INNER), row(D_INNER), row(gn), row(gn), row(SSM_HEADS),
            pl.BlockSpec((None, SSM_HEADS, ln), lambda bi, ci: (bi, 0, ci)),
            _const_spec((1, SSM_HEADS)), _const_spec((SSM_HEADS, 1)),
            _const_spec((1, D_INNER)), _const_spec((1, D_INNER)),
            _const_spec((SSM_HEADS, D_INNER)),
            state_spec,
        ],
        out_specs=[row(D_INNER), state_spec],
        scratch_shapes=[pltpu.VMEM((SSM_GROUPS, D_STATE, GROUP_COLS), F32)],
        compiler_params=pltpu.CompilerParams(dimension_semantics=("arbitrary", "arbitrary"),
                                             vmem_limit_bytes=VMEM_LIMIT),
        name="ssd_scan",
    )(xs, z, bm, cm, dt, dtt, a_log.reshape(1, -1), a_log.reshape(-1, 1), d_exp, ssm_norm.reshape(1, -1),
      expand, h0.astype(F32).reshape(b, D_INNER, D_STATE))
    return y, h_fin.reshape(b, SSM_HEADS, SSM_HEADDIM, D_STATE)


def _trunk(x, pos0, cache, pool_hist, conv_hist, ssm_hist, w, *, tm_proj, tm_ffn, tq, ln):
    b, s, _ = x.shape
    prompt = cache is None
    win_p, wq, wk, wv, place = _prep_ab_weights(w["w_in_ab"][0], w["w_uq"][0], w["w_uk"][0], w["w_uv"][0])
    outs = _ab_in(x, pos0, pool_hist, w["norm_mix"][0], win_p, w["q_norm"][0], wq, w["kv_norm"][0],
                  w["w_pool"][0], w["pool_scale"][0], wk, wv, place, tm=tm_proj, emit_kv=prompt)
    q, ckv, kpe, pool_out, pool_last = outs[:5]
    if prompt:
        k, v = outs[5:]
        attn = _flash_attention(q, k, v, tq=tq)
        kpe = jnp.swapaxes(kpe, 1, 2)
    else:
        wabs, wuv_place = _prep_latent_weights(w["w_uk"][0], w["w_uv"][0])
        attn = _latent_attention(q, wabs, cache[0], jnp.swapaxes(cache[1], 1, 2), ckv, kpe, wuv_place,
                                 pos0=pos0)
    ffn_w = (w["w_gate"].astype(BF16), w["w_up"].astype(BF16), w["w_down"].astype(BF16))
    w_out = w["w_out_ab"][0].astype(BF16)
    x1 = _out_ffn(x.reshape(b * s, D_MODEL),
                  [pool_out.reshape(b * s, POOL_DIM), attn.reshape(b * s, MLA_OUT)],
                  [w_out[:POOL_DIM], w_out[POOL_DIM:]],
                  w["norm_ffn"][0], ffn_w, 0, None, tm=tm_ffn)

    z, xs, bm, cm, dt, dtt, conv_last = _ssd_in(x1.reshape(b, s, D_MODEL), conv_hist, w["norm_mix"][1],
                                                w["w_in_c"][0], w["conv_w"][0], w["conv_b"][0],
                                                w["dt_bias"][0], tm=tm_proj)
    y, h_fin = _ssd_scan(xs, z, bm, cm, dt, dtt, w["a_log"][0], w["d_skip"][0], w["ssm_norm"][0], ssm_hist,
                         ln=ln)
    x2 = _out_ffn(x1, [y.reshape(b * s, D_INNER)], [w["w_out_c"][0].astype(BF16)],
                  w["norm_ffn"][1], ffn_w, 1, w["norm_final"], tm=tm_ffn)
    return (x2.reshape(b, s, D_MODEL), ckv[None], kpe[None], pool_last[None, :, POOL_KEEP - POOL_HIST:],
            conv_last[None, :, CONV_CARRY - (D_CONV - 1):], h_fin[None])


def kernel(x_prompt, x_sample, cache_ckv, cache_kpe, state_pool, state_conv, state_ssm, norm_mix, norm_ffn, norm_final, w_in_ab, w_pool, pool_scale, q_norm, w_uq, kv_norm, w_uk, w_uv, w_out_ab, w_in_c, conv_w, conv_b, dt_bias, a_log, d_skip, ssm_norm, w_out_c, w_gate, w_up, w_down):
    assert norm_mix.shape[0] == 2 and w_in_ab.shape[0] == 1 and w_in_c.shape[0] == 1
    w = dict(norm_mix=norm_mix, norm_ffn=norm_ffn, norm_final=norm_final, w_in_ab=w_in_ab, w_pool=w_pool,
             pool_scale=pool_scale, q_norm=q_norm, w_uq=w_uq, kv_norm=kv_norm, w_uk=w_uk, w_uv=w_uv,
             w_out_ab=w_out_ab, w_in_c=w_in_c, conv_w=conv_w, conv_b=conv_b, dt_bias=dt_bias, a_log=a_log,
             d_skip=d_skip, ssm_norm=ssm_norm, w_out_c=w_out_c, w_gate=w_gate, w_up=w_up, w_down=w_down)
    bp, sp, _ = x_prompt.shape
    bs, ss, _ = x_sample.shape
    past = cache_ckv.shape[2]

    prompt = _trunk(x_prompt, 0, None,
                    jnp.zeros((bp, POOL_HIST, POOL_DIM), F32), jnp.zeros((bp, D_CONV - 1, CONV_DIM), F32),
                    jnp.zeros((bp, SSM_HEADS, SSM_HEADDIM, D_STATE), F32), w,
                    tm_proj=512, tm_ffn=512, tq=1024, ln=128)
    sample = _trunk(x_sample, past, (cache_ckv[0], cache_kpe[0]), state_pool[0], state_conv[0], state_ssm[0], w,
                    tm_proj=ss, tm_ffn=bs * ss, tq=None, ln=ss)
    return (prompt[0], sample[0]) + prompt[1:] + sample[1:]
```

```python
import functools
import math

import numpy as np
import jax
import jax.numpy as jnp
from jax import lax
from jax.experimental import pallas as pl
from jax.experimental.pallas import tpu as pltpu

F32 = jnp.float32
BF16 = jnp.bfloat16

D_MODEL = 1024
CHUNK = 64
EPS = 1e-6
NEG = -1e30

POOL_WINDOWS = (2, 4, 8, 16)
POOL_GROUPS = 4
POOL_DIM = D_MODEL // 2
POOL_GDIM = POOL_DIM // POOL_GROUPS
POOL_HIST = max(POOL_WINDOWS) - 1

MLA_HEADS = 8
QK_NOPE = 64
QK_ROPE = 32
ROPE_HALF = QK_ROPE // 2
V_HEAD = 64
Q_LORA = 384
KV_LORA = 256
MLA_OUT = MLA_HEADS * V_HEAD
ROPE_THETA = 10000.0
SM_SCALE = (QK_NOPE + QK_ROPE) ** -0.5
LOG2E = math.log2(math.e)
Q_SCALE = SM_SCALE * LOG2E
IN_AB = POOL_DIM + Q_LORA + KV_LORA + QK_ROPE

D_INNER = 2 * D_MODEL
SSM_HEADDIM = 64
SSM_HEADS = D_INNER // SSM_HEADDIM
SSM_GROUPS = 4
SSM_HPG = SSM_HEADS // SSM_GROUPS
D_STATE = 128
D_CONV = 4
CONV_DIM = D_INNER + 2 * SSM_GROUPS * D_STATE
IN_C = D_INNER + CONV_DIM + SSM_HEADS
GROUP_COLS = D_INNER // SSM_GROUPS

D_FF = -(-8 * D_MODEL // (3 * 256)) * 256

LANES = 128
SUBLANES = 8
HEAD_PAD = LANES
ROWSUM_LANE_EVEN = V_HEAD
ROWSUM_LANE_ODD = 0
POOL_CARRY = 4 * SUBLANES
POOL_KEEP = 2 * SUBLANES
CONV_CARRY = 2 * SUBLANES
CONV_CHUNK = 256
IN_AB_PAD = 10 * LANES
VMEM_LIMIT = 56 * 1024 * 1024

OFF_Q = POOL_DIM
OFF_KV = OFF_Q + Q_LORA
OFF_KPE = OFF_KV + KV_LORA


def _const_spec(shape):
    nd = len(shape)
    return pl.BlockSpec(shape, lambda *_: (0,) * nd, pipeline_mode=pl.Buffered(1))


def _rms(x, g):
    return x * lax.rsqrt(jnp.mean(x * x, axis=-1, keepdims=True) + EPS) * g


def _dot(a, b):
    return jnp.dot(a, b, preferred_element_type=F32)


def _dot_nt(a, b):
    return lax.dot_general(a, b, (((1,), (1,)), ((), ())), preferred_element_type=F32)


def _dot_tn(a, b):
    return lax.dot_general(a, b, (((0,), (0,)), ((), ())), preferred_element_type=F32)


def _split3(x):
    hi = x.astype(BF16)
    r1 = x - hi.astype(F32)
    mid = r1.astype(BF16)
    lo = (r1 - mid.astype(F32)).astype(BF16)
    return hi, mid, lo


def _dot_sel_r(x, sel):
    hi, mid, lo = _split3(x)
    return _dot(hi, sel) + _dot(mid, sel) + _dot(lo, sel)


def _dot_sel_l(sel, x):
    hi, mid, lo = _split3(x)
    return _dot(sel, hi) + _dot(sel, mid) + _dot(sel, lo)


def _ab_in_kernel(*refs, tm, pos0, emit_kv):
    (x_ref, g_ref, win_ref, qn_ref, wuq_ref, kvn_ref, wpool_ref, pscale_ref, tab_ref, hist_ref) = refs[:10]
    refs = refs[10:]
    if emit_kv:
        wuk_ref, wuv_ref, place_ref = refs[:3]
        refs = refs[3:]
    q_ref, ckv_ref, kpe_ref, pool_ref, plast_ref = refs[:5]
    refs = refs[5:]
    if emit_kv:
        k_ref, v_ref = refs[:2]
        refs = refs[2:]
    (cat_scr,) = refs

    t = pl.program_id(1)
    x = x_ref[...]
    hb = _rms(x, g_ref[...]).astype(BF16)
    proj = _dot(hb, win_ref[...])
    u_pool = proj[:, :POOL_DIM]
    q_lat = proj[:, OFF_Q:OFF_KV]
    kv_lat = proj[:, OFF_KV:OFF_KPE]
    kblk = proj[:, OFF_KPE:OFF_KPE + LANES]

    @pl.when(t == 0)
    def _():
        cat_scr[...] = hist_ref[...]

    ext2d = jnp.concatenate([cat_scr[...], u_pool], axis=0)
    ext = ext2d.reshape((POOL_CARRY + tm) // SUBLANES, SUBLANES, POOL_DIM)
    lead = POOL_CARRY // SUBLANES
    pos = pos0 + t * tm + lax.broadcasted_iota(jnp.int32, (tm, 1), 0)
    for g, w in enumerate(POOL_WINDOWS):
        cols = slice(g * POOL_GDIM, (g + 1) * POOL_GDIM)
        win, span, dropped = ext[:, :, cols], 1, 0
        while span < w:
            prev = win[:-1] if span == SUBLANES else _shift_rows(win, span)
            win, span, dropped = win[1:] + prev, 2 * span, dropped + 1
        win = win[lead - dropped:].reshape(tm, POOL_GDIM)
        u_g = u_pool[:, cols]
        cnt = jnp.minimum(pos + 1, w).astype(F32)
        pooled = win / cnt - u_g
        mixed = _dot(pooled.astype(BF16), wpool_ref[g]) * pscale_ref[:, cols]
        pool_ref[:, cols] = mixed.astype(pool_ref.dtype)
    cat_scr[...] = ext2d[tm:, :]
    plast_ref[...] = ext2d[POOL_CARRY + tm - POOL_KEEP:, :]

    tab = tab_ref[...]
    up = pltpu.roll(tab, ROPE_HALF, 1)
    down = pltpu.roll(tab, LANES - ROPE_HALF, 1)
    ln_i = lax.broadcasted_iota(jnp.int32, (tm, LANES), 1)
    x1_q, x2_q = QK_NOPE, QK_NOPE + ROPE_HALF
    tq_c = jnp.where(ln_i < x1_q, 1.0, jnp.where(ln_i < x2_q, tab, jnp.where(ln_i < x2_q + ROPE_HALF, up, 0.0)))
    tq_a = jnp.where((ln_i >= x2_q) & (ln_i < x2_q + ROPE_HALF), tab, 0.0)
    tq_b = jnp.where((ln_i >= x1_q) & (ln_i < x2_q), -down, 0.0)
    tk_c = jnp.where(ln_i < ROPE_HALF, tab, jnp.where(ln_i < QK_ROPE, up, 0.0))
    tk_s = jnp.where(ln_i < ROPE_HALF, -down, jnp.where(ln_i < QK_ROPE, tab, 0.0))

    qn = _rms(q_lat, qn_ref[...]).astype(BF16)
    q = _dot(qn, wuq_ref[...])
    for h in range(MLA_HEADS):
        qh = q[:, h * HEAD_PAD:(h + 1) * HEAD_PAD]
        qr = (qh * tq_c + pltpu.roll(qh, ROPE_HALF, 1) * tq_a
              + pltpu.roll(qh, HEAD_PAD - ROPE_HALF, 1) * tq_b)
        q_ref[h] = qr.astype(q_ref.dtype)

    ckv = _rms(kv_lat, kvn_ref[...])
    ckv_ref[...] = ckv
    kpe_blk = kblk * tk_c + pltpu.roll(kblk, LANES - QK_ROPE, 1) * tk_s
    if emit_kv:
        kpe_ref[...] = kpe_blk.T[:QK_ROPE, :]
    else:
        kpe_ref[...] = kpe_blk[:, :QK_ROPE]

    if emit_kv:
        ckv_b = ckv.astype(BF16)
        k_all = _dot(ckv_b, wuk_ref[...]) + _dot(kpe_blk.astype(BF16), place_ref[...])
        for h in range(MLA_HEADS):
            k_ref[h] = k_all[:, h * HEAD_PAD:(h + 1) * HEAD_PAD].astype(k_ref.dtype)
        v_all = _dot(ckv_b, wuv_ref[...])
        lane = lax.broadcasted_iota(jnp.int32, (tm, LANES), 1)
        for hp in range(MLA_HEADS // 2):
            vp = v_all[:, hp * LANES:(hp + 1) * LANES]
            even = jnp.where(lane < V_HEAD, vp, jnp.where(lane == ROWSUM_LANE_EVEN, 1.0, 0.0))
            odd = jnp.where(lane >= V_HEAD, vp, jnp.where(lane == ROWSUM_LANE_ODD, 1.0, 0.0))
            v_ref[2 * hp] = even.astype(v_ref.dtype)
            v_ref[2 * hp + 1] = odd.astype(v_ref.dtype)


def _rope_table(pos0, s):
    pos = (pos0 + jnp.arange(s)).astype(F32)
    inv = ROPE_THETA ** (-jnp.arange(ROPE_HALF, dtype=F32) / ROPE_HALF)
    ang = pos[:, None] * inv[None, :]
    return jnp.concatenate([jnp.cos(ang), jnp.sin(ang)] * (LANES // QK_ROPE), axis=1)


def _prep_ab_weights(w_in, w_uq, w_uk, w_uv):
    kpe_w = w_in[:, OFF_KPE:]
    swapped = jnp.concatenate([kpe_w[:, ROPE_HALF:], kpe_w[:, :ROPE_HALF]], axis=1)
    pad = jnp.zeros((D_MODEL, IN_AB_PAD - IN_AB - QK_ROPE), w_in.dtype)
    win_p = jnp.concatenate([w_in, swapped, pad], axis=1).astype(BF16)

    tail = HEAD_PAD - QK_NOPE - QK_ROPE
    wq = w_uq.reshape(Q_LORA, MLA_HEADS, QK_NOPE + QK_ROPE)
    wq = jnp.pad(wq, ((0, 0), (0, 0), (0, tail))).reshape(Q_LORA, MLA_HEADS * HEAD_PAD).astype(BF16)
    wk = jnp.pad(w_uk, ((0, 0), (0, 0), (0, HEAD_PAD - QK_NOPE))).reshape(KV_LORA, MLA_HEADS * HEAD_PAD)
    wk = wk.astype(BF16)
    wv = w_uv.reshape(KV_LORA, MLA_OUT).astype(BF16)
    place = np.zeros((LANES, MLA_HEADS * HEAD_PAD), np.float32)
    for h in range(MLA_HEADS):
        for i in range(QK_ROPE):
            place[i, h * HEAD_PAD + QK_NOPE + i] = 1.0
    return win_p, wq, wk, wv, jnp.asarray(place, BF16)


def _ab_in(x, pos0, pool_hist, g_mix, win_p, q_norm, wq, kv_norm, w_pool, pool_scale, wk, wv, place,
           *, tm, emit_kv):
    b, s, _ = x.shape
    n_t = s // tm
    tab = _rope_table(pos0, s)
    hist = jnp.pad(pool_hist.astype(F32), ((0, 0), (POOL_CARRY - POOL_HIST, 0), (0, 0)))
    row = lambda shape: pl.BlockSpec((None,) + shape, lambda bi, ti: (bi, ti, 0))

    in_specs = [
        row((tm, D_MODEL)),
        _const_spec((1, D_MODEL)),
        _const_spec((D_MODEL, IN_AB_PAD)),
        _const_spec((1, Q_LORA)),
        _const_spec((Q_LORA, MLA_HEADS * HEAD_PAD)),
        _const_spec((1, KV_LORA)),
        _const_spec((POOL_GROUPS, POOL_GDIM, POOL_GDIM)),
        _const_spec((1, POOL_DIM)),
        pl.BlockSpec((tm, LANES), lambda bi, ti: (ti, 0)),
        pl.BlockSpec((None, POOL_CARRY, POOL_DIM), lambda bi, ti: (bi, 0, 0)),
    ]
    args = [x, g_mix.reshape(1, -1), win_p, (q_norm * Q_SCALE).reshape(1, -1), wq, kv_norm.reshape(1, -1),
            w_pool.astype(BF16), pool_scale.reshape(1, -1), tab, hist]
    if emit_kv:
        in_specs += [_const_spec((KV_LORA, MLA_HEADS * HEAD_PAD)), _const_spec((KV_LORA, MLA_OUT)),
                     _const_spec((LANES, MLA_HEADS * HEAD_PAD))]
        args += [wk, wv, place]

    head_spec = pl.BlockSpec((None, MLA_HEADS, tm, HEAD_PAD), lambda bi, ti: (bi, 0, ti, 0))
    out_shape = [
        jax.ShapeDtypeStruct((b, MLA_HEADS, s, HEAD_PAD), BF16),
        jax.ShapeDtypeStruct((b, s, KV_LORA), F32),
        jax.ShapeDtypeStruct((b, QK_ROPE, s) if emit_kv else (b, s, QK_ROPE), F32),
        jax.ShapeDtypeStruct((b, s, POOL_DIM), BF16),
        jax.ShapeDtypeStruct((b, POOL_KEEP, POOL_DIM), F32),
    ]
    out_specs = [
        head_spec,
        row((tm, KV_LORA)),
        pl.BlockSpec((None, QK_ROPE, tm), lambda bi, ti: (bi, 0, ti)) if emit_kv else row((tm, QK_ROPE)),
        row((tm, POOL_DIM)),
        pl.BlockSpec((None, POOL_KEEP, POOL_DIM), lambda bi, ti: (bi, 0, 0)),
    ]
    if emit_kv:
        out_shape += [jax.ShapeDtypeStruct((b, MLA_HEADS, s, HEAD_PAD), BF16),
                      jax.ShapeDtypeStruct((b, MLA_HEADS, s, LANES), BF16)]
        out_specs += [head_spec, head_spec]

    return pl.pallas_call(
        functools.partial(_ab_in_kernel, tm=tm, pos0=pos0, emit_kv=emit_kv),
        out_shape=out_shape,
        grid=(b, n_t),
        in_specs=in_specs,
        out_specs=out_specs,
        scratch_shapes=[pltpu.VMEM((POOL_CARRY, POOL_DIM), F32)],
        compiler_params=pltpu.CompilerParams(dimension_semantics=("arbitrary", "arbitrary"),
                                             vmem_limit_bytes=VMEM_LIMIT),
        name="ab_in",
    )(*args)


def _flash_kernel(q_ref, k_ref, v_ref, o_ref, m_scr, acc_scr, *, tq, tk):
    t = pl.program_id(2)
    heads = (0, 1)

    def group(blocks):
        s = {}
        for n, (blk, d) in enumerate(blocks):
            off = pl.multiple_of(blk * tk, tk)
            r0 = 0 if d is None else d * tk
            for hh in heads:
                sh = _dot_nt(q_ref[hh, r0:, :], k_ref[hh, pl.ds(off, tk), :])
                if d is not None:
                    qc = (r0 + lax.broadcasted_iota(jnp.int32, (tq - r0, tk), 0)) // CHUNK
                    kc = (d * tk + lax.broadcasted_iota(jnp.int32, (tq - r0, tk), 1)) // CHUNK
                    sh = jnp.where(kc <= qc, sh, NEG)
                s[n, hh] = sh
        for n, (blk, d) in enumerate(blocks):
            off = pl.multiple_of(blk * tk, tk)
            r0 = 0 if d is None else d * tk
            for hh in heads:
                sh = s[n, hh]
                m_prev = m_scr[hh, r0:, :]
                m_new = jnp.maximum(m_prev, jnp.max(sh, axis=-1, keepdims=True))
                alpha = jnp.exp2(m_prev - m_new)
                p = jnp.concatenate([jnp.exp2(sh[:, c * LANES:(c + 1) * LANES] - m_new)
                                     for c in range(tk // LANES)], axis=1).astype(BF16)
                acc_scr[hh, r0:, :] = alpha * acc_scr[hh, r0:, :] + _dot(p, v_ref[hh, pl.ds(off, tk), :])
                m_scr[hh, r0:, :] = m_new

    m_scr[...] = jnp.full(m_scr.shape, NEG, F32)
    acc_scr[...] = jnp.zeros(acc_scr.shape, F32)

    def body(i, carry):
        group([(2 * i, None), (2 * i + 1, None)])
        return carry

    lax.fori_loop(0, t, body, 0)
    group([(2 * t, 0), (2 * t + 1, 1)])

    acc0, acc1 = acc_scr[0], acc_scr[1]
    out0 = acc0 / acc0[:, ROWSUM_LANE_EVEN:ROWSUM_LANE_EVEN + 1]
    out1 = acc1 / acc1[:, ROWSUM_LANE_ODD:ROWSUM_LANE_ODD + 1]
    lane = lax.broadcasted_iota(jnp.int32, (tq, LANES), 1)
    o_ref[...] = jnp.where(lane < V_HEAD, out0, out1).astype(o_ref.dtype)


def _flash_attention(q, k, v, *, tq):
    b, h, s, _ = q.shape
    tk = tq // 2
    assert tk % CHUNK == 0 and s % tq == 0
    pair = lambda n: pl.BlockSpec((None, 2, n, LANES), lambda bi, hp, ti: (bi, hp, 0, 0))
    return pl.pallas_call(
        functools.partial(_flash_kernel, tq=tq, tk=tk),
        out_shape=jax.ShapeDtypeStruct((b, s, MLA_OUT), BF16),
        grid=(b, h // 2, s // tq),
        in_specs=[
            pl.BlockSpec((None, 2, tq, HEAD_PAD), lambda bi, hp, ti: (bi, hp, ti, 0)),
            pair(s),
            pair(s),
        ],
        out_specs=pl.BlockSpec((None, tq, 2 * V_HEAD), lambda bi, hp, ti: (bi, ti, hp)),
        scratch_shapes=[pltpu.VMEM((2, tq, LANES), F32), pltpu.VMEM((2, tq, LANES), F32)],
        compiler_params=pltpu.CompilerParams(dimension_semantics=("arbitrary", "arbitrary", "arbitrary"),
                                             vmem_limit_bytes=VMEM_LIMIT),
        name="flash_attention",
    )(q, k, v)


def _latent_attn_kernel(q_ref, wabs_ref, cckv_ref, ckpe_ref, nckv_ref, nkpe_ref, wuv_ref, o_ref,
                        *, sq, past, pos0):
    rows = MLA_HEADS * sq
    qcat = jnp.concatenate([_dot(q_ref[h], wabs_ref[h]) for h in range(MLA_HEADS)], axis=0)
    qa = qcat[:, :KV_LORA].astype(BF16)
    qp = qcat[:, KV_LORA:].astype(BF16)
    ck = cckv_ref[...].astype(BF16)
    kp_t = ckpe_ref[...].astype(BF16)
    nk = nckv_ref[...].astype(BF16)
    np_ = nkpe_ref[...].astype(BF16)
    s_past = _dot_nt(qa, ck) + _dot(qp, kp_t)
    s_new = _dot_nt(qa, nk) + _dot_nt(qp, np_)

    q_chunk = (pos0 + lax.broadcasted_iota(jnp.int32, (rows, 1), 0) % sq) // CHUNK
    kc_past = (pos0 - past + lax.broadcasted_iota(jnp.int32, (1, past), 1)) // CHUNK
    kc_new = (pos0 + lax.broadcasted_iota(jnp.int32, (1, sq), 1)) // CHUNK
    s_past = jnp.where(kc_past <= q_chunk, s_past, NEG)
    s_new = jnp.where(kc_new <= q_chunk, s_new, NEG)

    m = jnp.maximum(jnp.max(s_past, axis=-1, keepdims=True), jnp.max(s_new, axis=-1, keepdims=True))
    p_past = jnp.exp2(s_past - m)
    p_new = jnp.exp2(s_new - m)
    l = jnp.sum(p_past, axis=-1, keepdims=True) + jnp.sum(p_new, axis=-1, keepdims=True)
    o_lat = (_dot(p_past.astype(BF16), ck) + _dot(p_new.astype(BF16), nk)) / l
    o_lat = o_lat.astype(BF16)
    out = _dot(o_lat[0:sq], wuv_ref[0])
    for h in range(1, MLA_HEADS):
        out = out + _dot(o_lat[h * sq:(h + 1) * sq], wuv_ref[h])
    o_ref[...] = out.astype(o_ref.dtype)


def _prep_latent_weights(w_uk, w_uv):
    wabs = np.zeros((MLA_HEADS, HEAD_PAD, KV_LORA + QK_ROPE), np.float32)
    for i in range(QK_ROPE):
        wabs[:, QK_NOPE + i, KV_LORA + i] = 1.0
    wabs = jnp.asarray(wabs).at[:, :QK_NOPE, :KV_LORA].set(jnp.transpose(w_uk, (1, 2, 0)))
    wuv_place = jnp.zeros((MLA_HEADS, KV_LORA, MLA_OUT), F32)
    for h in range(MLA_HEADS):
        wuv_place = wuv_place.at[h, :, h * V_HEAD:(h + 1) * V_HEAD].set(w_uv[:, h, :])
    return wabs.astype(BF16), wuv_place.astype(BF16)


def _latent_attention(q, wabs, cache_ckv, cache_kpe, ckv_new, kpe_new, wuv_place, *, pos0):
    b, h, sq, _ = q.shape
    past = cache_ckv.shape[1]
    seq = lambda n, d: pl.BlockSpec((None, n, d), lambda bi: (bi, 0, 0))
    return pl.pallas_call(
        functools.partial(_latent_attn_kernel, sq=sq, past=past, pos0=pos0),
        out_shape=jax.ShapeDtypeStruct((b, sq, MLA_OUT), BF16),
        grid=(b,),
        in_specs=[
            pl.BlockSpec((None, h, sq, HEAD_PAD), lambda bi: (bi, 0, 0, 0)),
            _const_spec((h, HEAD_PAD, KV_LORA + QK_ROPE)),
            seq(past, KV_LORA), seq(QK_ROPE, past), seq(sq, KV_LORA), seq(sq, QK_ROPE),
            _const_spec((h, KV_LORA, MLA_OUT)),
        ],
        out_specs=seq(sq, MLA_OUT),
        compiler_params=pltpu.CompilerParams(dimension_semantics=("arbitrary",),
                                             vmem_limit_bytes=VMEM_LIMIT),
        name="latent_attention",
    )(q, wabs, cache_ckv, cache_kpe, ckv_new, kpe_new, wuv_place)


def _out_ffn_kernel(*refs, n_act, final_norm):
    x_ref = refs[0]
    act_refs = refs[1:1 + n_act]
    wout_refs = refs[1 + n_act:1 + 2 * n_act]
    gffn_ref, wg_ref, wu_ref, wd_ref = refs[1 + 2 * n_act:5 + 2 * n_act]
    rest = refs[5 + 2 * n_act:]
    if final_norm:
        gfin_ref, o_ref = rest
    else:
        (o_ref,) = rest

    y = _dot(act_refs[0][...], wout_refs[0][...])
    for a_ref, w_ref in zip(act_refs[1:], wout_refs[1:]):
        y = y + _dot(a_ref[...], w_ref[...])
    x1 = x_ref[...] + y
    h2 = _rms(x1, gffn_ref[...]).astype(BF16)
    gate = _dot(h2, wg_ref[...])
    up = _dot(h2, wu_ref[...])
    act = (gate * jax.nn.sigmoid(gate) * up).astype(BF16)
    x2 = x1 + _dot(act, wd_ref[...])
    if final_norm:
        x2 = _rms(x2, gfin_ref[...])
    o_ref[...] = x2


def _out_ffn(x, acts, wouts, g_ffn, ffn_w, layer, g_final, *, tm):
    t, _ = x.shape
    n_act = len(acts)
    row = lambda d: pl.BlockSpec((tm, d), lambda i: (i, 0))
    slab = lambda w: pl.BlockSpec((None,) + w.shape[1:], lambda i: (layer, 0, 0), pipeline_mode=pl.Buffered(1))
    in_specs = ([row(D_MODEL)] + [row(a.shape[1]) for a in acts] + [_const_spec(w.shape) for w in wouts]
                + [_const_spec((1, D_MODEL))] + [slab(w) for w in ffn_w])
    args = [x, *acts, *wouts, g_ffn.reshape(1, -1), *ffn_w]
    if g_final is not None:
        in_specs.append(_const_spec((1, D_MODEL)))
        args.append(g_final.reshape(1, -1))
    return pl.pallas_call(
        functools.partial(_out_ffn_kernel, n_act=n_act, final_norm=g_final is not None),
        out_shape=jax.ShapeDtypeStruct((t, D_MODEL), F32),
        grid=(t // tm,),
        in_specs=in_specs,
        out_specs=row(D_MODEL),
        compiler_params=pltpu.CompilerParams(dimension_semantics=("arbitrary",),
                                             vmem_limit_bytes=VMEM_LIMIT),
        name="out_ffn",
    )(*args)


def _softplus(x):
    return jnp.maximum(x, 0.0) + jnp.log1p(jnp.exp(-jnp.abs(x)))


def _shift_rows(v, k):
    sub = lax.broadcasted_iota(jnp.int32, (1,) + v.shape[1:], 1)
    mixed = jnp.where(sub >= SUBLANES - k, v[:-1], v[1:])
    return pltpu.roll(mixed, k, axis=1)


def _ssd_in_kernel(x_ref, g_ref, win_ref, wdtt_ref, cw_ref, cb_ref, dtb_ref, dtbt_ref,
                   hist_ref, z_ref, xs_ref, b_ref, c_ref, dt_ref, dtt_ref, clast_ref, carry_scr, *, tm):
    t = pl.program_id(1)
    hb = _rms(x_ref[...], g_ref[...]).astype(BF16)
    dt_ref[...] = _softplus(_dot(hb, win_ref[:, D_INNER + CONV_DIM:]) + dtb_ref[...])
    dtt_ref[...] = _softplus(_dot_nt(wdtt_ref[...].astype(BF16), hb) + dtbt_ref[...])

    @pl.when(t == 0)
    def _():
        carry_scr[...] = hist_ref[...]

    gn = SSM_GROUPS * D_STATE
    outs = ((xs_ref, 0, D_INNER), (b_ref, D_INNER, gn), (c_ref, D_INNER + gn, gn))
    n_grp = tm // SUBLANES
    starts = list(range(0, CONV_DIM, CONV_CHUNK))
    xbc_w = lambda c0: win_ref[:, D_INNER + c0:D_INNER + c0 + CONV_CHUNK]
    xc_next = _dot(hb, xbc_w(0))
    for i, c0 in enumerate(starts):
        cols = slice(c0, c0 + CONV_CHUNK)
        xc = xc_next
        if i + 1 < len(starts):
            xc_next = _dot(hb, xbc_w(starts[i + 1]))
        if c0 < D_INNER:
            z_ref[:, cols] = _dot(hb, win_ref[:, cols]).astype(z_ref.dtype)
        ext = jnp.concatenate([carry_scr[:, cols], xc], axis=0)
        ext = ext.reshape(n_grp + CONV_CARRY // SUBLANES, SUBLANES, CONV_CHUNK)
        w = [cw_ref[k:k + 1, cols] for k in range(D_CONV)]
        ref, start = next((r, s0) for r, s0, width in outs if s0 <= c0 < s0 + width)
        x1 = _shift_rows(ext, 1)
        u = ext[2:] * w[3] + x1[1:] * w[2]
        v = ext[1:] * w[1] + x1 * w[0]
        half = (u + _shift_rows(v, 2) + cb_ref[:, cols]).reshape(tm, CONV_CHUNK)
        act = half + half * jnp.tanh(half)
        ref[:, c0 - start:c0 - start + CONV_CHUNK] = act.astype(ref.dtype)
        carry_scr[:, cols] = xc[tm - CONV_CARRY:, :]
        clast_ref[:, cols] = xc[tm - CONV_CARRY:, :]


def _ssd_in(x, conv_hist, g_mix, w_in, conv_w, conv_b, dt_bias, *, tm):
    b, s, _ = x.shape
    n_t = s // tm
    gn = SSM_GROUPS * D_STATE
    win_b = w_in.astype(BF16)
    wdt_t = w_in[:, D_INNER + CONV_DIM:].T
    hist = jnp.pad(conv_hist.astype(F32), ((0, 0), (CONV_CARRY - (D_CONV - 1), 0), (0, 0)))
    row = lambda d: pl.BlockSpec((None, tm, d), lambda bi, ti: (bi, ti, 0))
    return pl.pallas_call(
        functools.partial(_ssd_in_kernel, tm=tm),
        out_shape=[
            jax.ShapeDtypeStruct((b, s, D_INNER), BF16),
            jax.ShapeDtypeStruct((b, s, D_INNER), BF16),
            jax.ShapeDtypeStruct((b, s, gn), BF16),
            jax.ShapeDtypeStruct((b, s, gn), BF16),
            jax.ShapeDtypeStruct((b, s, SSM_HEADS), F32),
            jax.ShapeDtypeStruct((b, SSM_HEADS, s), F32),
            jax.ShapeDtypeStruct((b, CONV_CARRY, CONV_DIM), F32),
        ],
        grid=(b, n_t),
        in_specs=[
            row(D_MODEL),
            _const_spec((1, D_MODEL)),
            _const_spec((D_MODEL, IN_C)),
            _const_spec((SSM_HEADS, D_MODEL)),
            _const_spec((D_CONV, CONV_DIM)),
            _const_spec((1, CONV_DIM)),
            _const_spec((1, SSM_HEADS)),
            _const_spec((SSM_HEADS, 1)),
            pl.BlockSpec((None, CONV_CARRY, CONV_DIM), lambda bi, ti: (bi, 0, 0)),
        ],
        out_specs=[
            row(D_INNER), row(D_INNER), row(gn), row(gn), row(SSM_HEADS),
            pl.BlockSpec((None, SSM_HEADS, tm), lambda bi, ti: (bi, 0, ti)),
            pl.BlockSpec((None, CONV_CARRY, CONV_DIM), lambda bi, ti: (bi, 0, 0)),
        ],
        scratch_shapes=[pltpu.VMEM((CONV_CARRY, CONV_DIM), F32)],
        compiler_params=pltpu.CompilerParams(dimension_semantics=("arbitrary", "arbitrary"),
                                             vmem_limit_bytes=VMEM_LIMIT),
        name="ssd_in",
    )(x, g_mix.reshape(1, -1), win_b, wdt_t, 0.5 * conv_w, 0.5 * conv_b.reshape(1, -1),
      dt_bias.reshape(1, -1), dt_bias.reshape(-1, 1), hist)


def _ssd_scan_kernel(xs_ref, z_ref, b_ref, c_ref, dt_ref, dtt_ref, alog_ref, alogt_ref, dexp_ref, norm_ref,
                     expand_ref, h0_ref, y_ref, hfin_ref, h_scr, *, ln, n_c):
    c = pl.program_id(1)

    @pl.when(c == 0)
    def _():
        for g in range(SSM_GROUPS):
            h_scr[g] = h0_ref[g * GROUP_COLS:(g + 1) * GROUP_COLS, :].T

    row_i = lax.broadcasted_iota(jnp.int32, (ln, ln), 0)
    col_i = lax.broadcasted_iota(jnp.int32, (ln, ln), 1)
    causal = row_i >= col_i
    tril = jnp.where(causal, 1.0, 0.0).astype(BF16)
    triu = jnp.where(row_i <= col_i, 1.0, 0.0).astype(BF16)
    lane = lax.broadcasted_iota(jnp.int32, (ln, LANES), 1)

    dt = dt_ref[...]
    dtt = dtt_ref[...]
    acum = _dot_sel_l(tril, dt * -jnp.exp(alog_ref[...]))
    acum_t = _dot_sel_r(dtt * -jnp.exp(alogt_ref[...]), triu)
    acum2 = acum * LOG2E
    key2_t = (acum_t - jnp.log(dtt)) * LOG2E
    last = acum[ln - 1:ln, :]
    ea_hi, ea_mid, _ = _split3(jnp.exp(acum))
    wj_b = (jnp.exp(last - acum) * dt).astype(BF16)
    pair = 2 * SSM_HEADDIM

    for g in range(SSM_GROUPS):
        gcols = slice(g * GROUP_COLS, (g + 1) * GROUP_COLS)
        expand = expand_ref[:, gcols]
        ea_e = _dot(ea_hi, expand) + _dot(ea_mid, expand)
        wj_e = _dot(wj_b, expand).astype(BF16)
        bg = b_ref[:, g * D_STATE:(g + 1) * D_STATE]
        cg = c_ref[:, g * D_STATE:(g + 1) * D_STATE]
        cb = _dot_nt(cg, bg)
        y_state = _dot(cg, h_scr[g].astype(BF16))
        gated = []
        for p in range(SSM_HPG // 2):
            h0 = g * SSM_HPG + 2 * p
            cols = slice(h0 * SSM_HEADDIM, (h0 + 2) * SSM_HEADDIM)
            pcols = slice(p * pair, (p + 1) * pair)
            scores = []
            for h in (h0, h0 + 1):
                seg = jnp.where(causal, acum2[:, h:h + 1] - key2_t[h:h + 1, :], NEG)
                scores.append((cb * jnp.exp2(seg)).astype(BF16))
            xp = xs_ref[:, cols]
            zero = jnp.zeros_like(xp)
            rhs = jnp.concatenate([jnp.where(lane < SSM_HEADDIM, xp, zero),
                                   jnp.where(lane >= SSM_HEADDIM, xp, zero)], axis=0)
            y = _dot(jnp.concatenate(scores, axis=1), rhs)
            y = y + y_state[:, pcols] * ea_e[:, pcols]
            y = y + xp.astype(F32) * dexp_ref[:, cols]
            zc = z_ref[:, cols].astype(F32)
            gated.append(y * (zc * jax.nn.sigmoid(zc)))
        yg = jnp.concatenate(gated, axis=1)
        yg = yg * lax.rsqrt(jnp.mean(yg * yg, axis=-1, keepdims=True) + EPS)
        y_ref[:, gcols] = (yg * norm_ref[:, gcols]).astype(y_ref.dtype)
        xw = xs_ref[:, gcols] * wj_e
        h_scr[g] = h_scr[g] * ea_e[ln - 1:ln, :] + _dot_tn(bg, xw)

    @pl.when(c == n_c - 1)
    def _():
        for g in range(SSM_GROUPS):
            hfin_ref[g * GROUP_COLS:(g + 1) * GROUP_COLS, :] = h_scr[g].T


def _ssd_scan(xs, z, bm, cm, dt, dtt, a_log, d_skip, ssm_norm, h0, *, ln):
    b, s, _ = xs.shape
    n_c = s // ln
    gn = SSM_GROUPS * D_STATE
    expand = jnp.asarray(np.kron(np.eye(SSM_HEADS, dtype=np.float32), np.ones((1, SSM_HEADDIM), np.float32)),
                         BF16)
    d_exp = jnp.repeat(d_skip.astype(F32), SSM_HEADDIM).reshape(1, D_INNER)
    row = lambda d: pl.BlockSpec((None, ln, d), lambda bi, ci: (bi, ci, 0))
    state_spec = pl.BlockSpec((None, D_INNER, D_STATE), lambda bi, ci: (bi, 0, 0))
    y, h_fin = pl.pallas_call(
        functools.partial(_ssd_scan_kernel, ln=ln, n_c=n_c),
        out_shape=[jax.ShapeDtypeStruct((b, s, D_INNER), BF16),
                   jax.ShapeDtypeStruct((b, D_INNER, D_STATE), F32)],
        grid=(b, n_c),
        in_specs=[
            row(D_INNER), row(D_INNER), row(gn), row(gn), row(SSM_HEADS),
            pl.BlockSpec((None, SSM_HEADS, ln), lambda bi, ci: (bi, 0, ci)),
            _const_spec((1, SSM_HEADS)), _const_spec((SSM_HEADS, 1)),
            _const_spec((1, D_INNER)), _const_spec((1, D_INNER)),
            _const_spec((SSM_HEADS, D_INNER)),
            state_spec,
        ],
        out_specs=[row(D_INNER), state_spec],
        scratch_shapes=[pltpu.VMEM((SSM_GROUPS, D_STATE, GROUP_COLS), F32)],
        compiler_params=pltpu.CompilerParams(dimension_semantics=("arbitrary", "arbitrary"),
                                             vmem_limit_bytes=VMEM_LIMIT),
        name="ssd_scan",
    )(xs, z, bm, cm, dt, dtt, a_log.reshape(1, -1), a_log.reshape(-1, 1), d_exp, ssm_norm.reshape(1, -1),
      expand, h0.astype(F32).reshape(b, D_INNER, D_STATE))
    return y, h_fin.reshape(b, SSM_HEADS, SSM_HEADDIM, D_STATE)


def _trunk(x, pos0, cache, pool_hist, conv_hist, ssm_hist, w, *, tm_proj, tm_ffn, tq, ln):
    b, s, _ = x.shape
    prompt = cache is None
    win_p, wq, wk, wv, place = _prep_ab_weights(w["w_in_ab"][0], w["w_uq"][0], w["w_uk"][0], w["w_uv"][0])
    outs = _ab_in(x, pos0, pool_hist, w["norm_mix"][0], win_p, w["q_norm"][0], wq, w["kv_norm"][0],
                  w["w_pool"][0], w["pool_scale"][0], wk, wv, place, tm=tm_proj, emit_kv=prompt)
    q, ckv, kpe, pool_out, pool_last = outs[:5]
    if prompt:
        k, v = outs[5:]
        attn = _flash_attention(q, k, v, tq=tq)
        kpe = jnp.swapaxes(kpe, 1, 2)
    else:
        wabs, wuv_place = _prep_latent_weights(w["w_uk"][0], w["w_uv"][0])
        attn = _latent_attention(q, wabs, cache[0], jnp.swapaxes(cache[1], 1, 2), ckv, kpe, wuv_place,
                                 pos0=pos0)
    ffn_w = (w["w_gate"].astype(BF16), w["w_up"].astype(BF16), w["w_down"].astype(BF16))
    w_out = w["w_out_ab"][0].astype(BF16)
    x1 = _out_ffn(x.reshape(b * s, D_MODEL),
                  [pool_out.reshape(b * s, POOL_DIM), attn.reshape(b * s, MLA_OUT)],
                  [w_out[:POOL_DIM], w_out[POOL_DIM:]],
                  w["norm_ffn"][0], ffn_w, 0, None, tm=tm_ffn)

    z, xs, bm, cm, dt, dtt, conv_last = _ssd_in(x1.reshape(b, s, D_MODEL), conv_hist, w["norm_mix"][1],
                                                w["w_in_c"][0], w["conv_w"][0], w["conv_b"][0],
                                                w["dt_bias"][0], tm=tm_proj)
    y, h_fin = _ssd_scan(xs, z, bm, cm, dt, dtt, w["a_log"][0], w["d_skip"][0], w["ssm_norm"][0], ssm_hist,
                         ln=ln)
    x2 = _out_ffn(x1, [y.reshape(b * s, D_INNER)], [w["w_out_c"][0].astype(BF16)],
                  w["norm_ffn"][1], ffn_w, 1, w["norm_final"], tm=tm_ffn)
    return (x2.reshape(b, s, D_MODEL), ckv[None], kpe[None], pool_last[None, :, POOL_KEEP - POOL_HIST:],
            conv_last[None, :, CONV_CARRY - (D_CONV - 1):], h_fin[None])


def kernel(x_prompt, x_sample, cache_ckv, cache_kpe, state_pool, state_conv, state_ssm, norm_mix, norm_ffn, norm_final, w_in_ab, w_pool, pool_scale, q_norm, w_uq, kv_norm, w_uk, w_uv, w_out_ab, w_in_c, conv_w, conv_b, dt_bias, a_log, d_skip, ssm_norm, w_out_c, w_gate, w_up, w_down):
    assert norm_mix.shape[0] == 2 and w_in_ab.shape[0] == 1 and w_in_c.shape[0] == 1
    w = dict(norm_mix=norm_mix, norm_ffn=norm_ffn, norm_final=norm_final, w_in_ab=w_in_ab, w_pool=w_pool,
             pool_scale=pool_scale, q_norm=q_norm, w_uq=w_uq, kv_norm=kv_norm, w_uk=w_uk, w_uv=w_uv,
             w_out_ab=w_out_ab, w_in_c=w_in_c, conv_w=conv_w, conv_b=conv_b, dt_bias=dt_bias, a_log=a_log,
             d_skip=d_skip, ssm_norm=ssm_norm, w_out_c=w_out_c, w_gate=w_gate, w_up=w_up, w_down=w_down)
    bp, sp, _ = x_prompt.shape
    bs, ss, _ = x_sample.shape
    past = cache_ckv.shape[2]

    prompt = _trunk(x_prompt, 0, None,
                    jnp.zeros((bp, POOL_HIST, POOL_DIM), F32), jnp.zeros((bp, D_CONV - 1, CONV_DIM), F32),
                    jnp.zeros((bp, SSM_HEADS, SSM_HEADDIM, D_STATE), F32), w,
                    tm_proj=512, tm_ffn=512, tq=1024, ln=128)
    sample = _trunk(x_sample, past, (cache_ckv[0], cache_kpe[0]), state_pool[0], state_conv[0], state_ssm[0], w,
                    tm_proj=ss, tm_ffn=bs * ss, tq=None, ln=ss)
    return (prompt[0], sample[0]) + prompt[1:] + sample[1:]
```

```python
import functools
import math

import numpy as np
import jax
import jax.numpy as jnp
from jax import lax
from jax.experimental import pallas as pl
from jax.experimental.pallas import tpu as pltpu

F32 = jnp.float32
BF16 = jnp.bfloat16

D_MODEL = 1024
CHUNK = 64
EPS = 1e-6
NEG = -1e30

POOL_WINDOWS = (2, 4, 8, 16)
POOL_GROUPS = 4
POOL_DIM = D_MODEL // 2
POOL_GDIM = POOL_DIM // POOL_GROUPS
POOL_HIST = max(POOL_WINDOWS) - 1

MLA_HEADS = 8
QK_NOPE = 64
QK_ROPE = 32
ROPE_HALF = QK_ROPE // 2
V_HEAD = 64
Q_LORA = 384
KV_LORA = 256
MLA_OUT = MLA_HEADS * V_HEAD
ROPE_THETA = 10000.0
SM_SCALE = (QK_NOPE + QK_ROPE) ** -0.5
LOG2E = math.log2(math.e)
Q_SCALE = SM_SCALE * LOG2E
IN_AB = POOL_DIM + Q_LORA + KV_LORA + QK_ROPE

D_INNER = 2 * D_MODEL
SSM_HEADDIM = 64
SSM_HEADS = D_INNER // SSM_HEADDIM
SSM_GROUPS = 4
SSM_HPG = SSM_HEADS // SSM_GROUPS
D_STATE = 128
D_CONV = 4
CONV_DIM = D_INNER + 2 * SSM_GROUPS * D_STATE
IN_C = D_INNER + CONV_DIM + SSM_HEADS
GROUP_COLS = D_INNER // SSM_GROUPS

D_FF = -(-8 * D_MODEL // (3 * 256)) * 256

LANES = 128
SUBLANES = 8
HEAD_PAD = LANES
ROWSUM_LANE_EVEN = V_HEAD
ROWSUM_LANE_ODD = 0
POOL_CARRY = 4 * SUBLANES
POOL_KEEP = 2 * SUBLANES
LATENT_SEQS_PER_STEP = 2
CONV_CARRY = 2 * SUBLANES
CONV_CHUNK = 256
IN_AB_PAD = 10 * LANES
VMEM_LIMIT = 56 * 1024 * 1024

OFF_Q = POOL_DIM
OFF_KV = OFF_Q + Q_LORA
OFF_KPE = OFF_KV + KV_LORA


def _const_spec(shape):
    nd = len(shape)
    return pl.BlockSpec(shape, lambda *_: (0,) * nd, pipeline_mode=pl.Buffered(1))


def _rms(x, g):
    return x * lax.rsqrt(jnp.mean(x * x, axis=-1, keepdims=True) + EPS) * g


def _dot(a, b):
    return jnp.dot(a, b, preferred_element_type=F32)


def _dot_nt(a, b):
    return lax.dot_general(a, b, (((1,), (1,)), ((), ())), preferred_element_type=F32)


def _dot_tn(a, b):
    return lax.dot_general(a, b, (((0,), (0,)), ((), ())), preferred_element_type=F32)


def _split3(x):
    hi = x.astype(BF16)
    r1 = x - hi.astype(F32)
    mid = r1.astype(BF16)
    lo = (r1 - mid.astype(F32)).astype(BF16)
    return hi, mid, lo


def _dot_sel_r(x, sel):
    hi, mid, lo = _split3(x)
    return _dot(hi, sel) + _dot(mid, sel) + _dot(lo, sel)


def _dot_sel_l(sel, x):
    hi, mid, lo = _split3(x)
    return _dot(sel, hi) + _dot(sel, mid) + _dot(sel, lo)


def _ab_in_kernel(*refs, tm, pos0, emit_kv, seq_rows):
    (x_ref, g_ref, win_ref, qn_ref, wuq_ref, kvn_ref, wpool_ref, pscale_ref, tab_ref, hist_ref) = refs[:10]
    refs = refs[10:]
    if emit_kv:
        wuk_ref, wuv_ref, place_ref = refs[:3]
        refs = refs[3:]
    q_ref, ckv_ref, kpe_ref, pool_ref, plast_ref = refs[:5]
    refs = refs[5:]
    if emit_kv:
        k_ref, v_ref = refs[:2]
        refs = refs[2:]
    (cat_scr,) = refs

    t = pl.program_id(1)
    x = x_ref[...]
    hb = _rms(x, g_ref[...]).astype(BF16)
    proj = _dot(hb, win_ref[...])
    u_pool = proj[:, :POOL_DIM]
    q_lat = proj[:, OFF_Q:OFF_KV]
    kv_lat = proj[:, OFF_KV:OFF_KPE]
    kblk = proj[:, OFF_KPE:OFF_KPE + LANES]

    lead = POOL_CARRY // SUBLANES
    row_i = lax.broadcasted_iota(jnp.int32, (tm, 1), 0)
    if seq_rows is None:
        @pl.when(t == 0)
        def _():
            cat_scr[...] = hist_ref[...]

        ext2d = jnp.concatenate([cat_scr[...], u_pool], axis=0)
        ext = ext2d.reshape((POOL_CARRY + tm) // SUBLANES, SUBLANES, POOL_DIM)
        pos = pos0 + t * tm + row_i
        cat_scr[...] = ext2d[tm:, :]
        plast_ref[...] = ext2d[POOL_CARRY + tm - POOL_KEEP:, :]
    else:
        n_seq, per_seq = tm // seq_rows, (POOL_CARRY + seq_rows) // SUBLANES
        u_seq = u_pool.reshape(n_seq, seq_rows, POOL_DIM)
        ext = jnp.concatenate([hist_ref[...], u_seq], axis=1)
        ext = ext.reshape(n_seq * per_seq, SUBLANES, POOL_DIM)
        pos = pos0 + row_i % seq_rows
        plast_ref[...] = u_seq[:, seq_rows - POOL_KEEP:, :]

    def tile_rows(win, dropped):
        if seq_rows is None:
            return win[lead - dropped:].reshape(tm, POOL_GDIM)
        pad = jnp.zeros((dropped,) + win.shape[1:], win.dtype)
        full = jnp.concatenate([pad, win], axis=0).reshape(n_seq, per_seq, SUBLANES, POOL_GDIM)
        return full[:, lead:].reshape(tm, POOL_GDIM)

    for g, w in enumerate(POOL_WINDOWS):
        cols = slice(g * POOL_GDIM, (g + 1) * POOL_GDIM)
        win, span, dropped = ext[:, :, cols], 1, 0
        while span < w:
            prev = win[:-1] if span == SUBLANES else _shift_rows(win, span)
            win, span, dropped = win[1:] + prev, 2 * span, dropped + 1
        win = tile_rows(win, dropped)
        u_g = u_pool[:, cols]
        cnt = jnp.minimum(pos + 1, w).astype(F32)
        pooled = win / cnt - u_g
        mixed = _dot(pooled.astype(BF16), wpool_ref[g]) * pscale_ref[:, cols]
        pool_ref[:, cols] = mixed.astype(pool_ref.dtype)

    tab = tab_ref[...]
    up = pltpu.roll(tab, ROPE_HALF, 1)
    down = pltpu.roll(tab, LANES - ROPE_HALF, 1)
    ln_i = lax.broadcasted_iota(jnp.int32, (tm, LANES), 1)
    x1_q, x2_q = QK_NOPE, QK_NOPE + ROPE_HALF
    tq_c = jnp.where(ln_i < x1_q, 1.0, jnp.where(ln_i < x2_q, tab, jnp.where(ln_i < x2_q + ROPE_HALF, up, 0.0)))
    tq_a = jnp.where((ln_i >= x2_q) & (ln_i < x2_q + ROPE_HALF), tab, 0.0)
    tq_b = jnp.where((ln_i >= x1_q) & (ln_i < x2_q), -down, 0.0)
    tk_c = jnp.where(ln_i < ROPE_HALF, tab, jnp.where(ln_i < QK_ROPE, up, 0.0))
    tk_s = jnp.where(ln_i < ROPE_HALF, -down, jnp.where(ln_i < QK_ROPE, tab, 0.0))

    qn = _rms(q_lat, qn_ref[...]).astype(BF16)
    q = _dot(qn, wuq_ref[...])
    for h in range(MLA_HEADS):
        qh = q[:, h * HEAD_PAD:(h + 1) * HEAD_PAD]
        qr = (qh * tq_c + pltpu.roll(qh, ROPE_HALF, 1) * tq_a
              + pltpu.roll(qh, HEAD_PAD - ROPE_HALF, 1) * tq_b)
        q_ref[h] = qr.astype(q_ref.dtype)

    ckv = _rms(kv_lat, kvn_ref[...])
    ckv_ref[...] = ckv
    kpe_blk = kblk * tk_c + pltpu.roll(kblk, LANES - QK_ROPE, 1) * tk_s
    if emit_kv:
        kpe_ref[...] = kpe_blk.T[:QK_ROPE, :]
    else:
        kpe_ref[...] = kpe_blk[:, :QK_ROPE]

    if emit_kv:
        ckv_b = ckv.astype(BF16)
        k_all = _dot(ckv_b, wuk_ref[...]) + _dot(kpe_blk.astype(BF16), place_ref[...])
        for h in range(MLA_HEADS):
            k_ref[h] = k_all[:, h * HEAD_PAD:(h + 1) * HEAD_PAD].astype(k_ref.dtype)
        v_all = _dot(ckv_b, wuv_ref[...])
        lane = lax.broadcasted_iota(jnp.int32, (tm, LANES), 1)
        for hp in range(MLA_HEADS // 2):
            vp = v_all[:, hp * LANES:(hp + 1) * LANES]
            even = jnp.where(lane < V_HEAD, vp, jnp.where(lane == ROWSUM_LANE_EVEN, 1.0, 0.0))
            odd = jnp.where(lane >= V_HEAD, vp, jnp.where(lane == ROWSUM_LANE_ODD, 1.0, 0.0))
            v_ref[2 * hp] = even.astype(v_ref.dtype)
            v_ref[2 * hp + 1] = odd.astype(v_ref.dtype)


def _rope_table(pos0, s):
    pos = (pos0 + jnp.arange(s)).astype(F32)
    inv = ROPE_THETA ** (-jnp.arange(ROPE_HALF, dtype=F32) / ROPE_HALF)
    ang = pos[:, None] * inv[None, :]
    return jnp.tile(jnp.concatenate([jnp.cos(ang), jnp.sin(ang)], axis=1), (1, LANES // QK_ROPE))


def _prep_ab_weights(w_in, w_uq, w_uk, w_uv):
    kpe_w = w_in[:, OFF_KPE:]
    swapped = jnp.concatenate([kpe_w[:, ROPE_HALF:], kpe_w[:, :ROPE_HALF]], axis=1)
    pad = jnp.zeros((D_MODEL, IN_AB_PAD - IN_AB - QK_ROPE), w_in.dtype)
    win_p = jnp.concatenate([w_in, swapped, pad], axis=1).astype(BF16)

    tail = HEAD_PAD - QK_NOPE - QK_ROPE
    wq = w_uq.reshape(Q_LORA, MLA_HEADS, QK_NOPE + QK_ROPE)
    wq = jnp.pad(wq, ((0, 0), (0, 0), (0, tail))).reshape(Q_LORA, MLA_HEADS * HEAD_PAD).astype(BF16)
    wk = jnp.pad(w_uk, ((0, 0), (0, 0), (0, HEAD_PAD - QK_NOPE))).reshape(KV_LORA, MLA_HEADS * HEAD_PAD)
    wk = wk.astype(BF16)
    wv = w_uv.reshape(KV_LORA, MLA_OUT).astype(BF16)
    place = np.zeros((LANES, MLA_HEADS * HEAD_PAD), np.float32)
    for h in range(MLA_HEADS):
        for i in range(QK_ROPE):
            place[i, h * HEAD_PAD + QK_NOPE + i] = 1.0
    return win_p, wq, wk, wv, jnp.asarray(place, BF16)


def _ab_in(x, pos0, pool_hist, g_mix, win_p, q_norm, wq, kv_norm, w_pool, pool_scale, wk, wv, place,
           *, tm, emit_kv, seq_rows=None):
    b, s, _ = x.shape
    n_t = s // tm
    n_seq = pool_hist.shape[0]
    tab = _rope_table(pos0, s) if seq_rows is None else jnp.tile(_rope_table(pos0, seq_rows), (n_seq, 1))
    hist = jnp.pad(pool_hist.astype(F32), ((0, 0), (POOL_CARRY - POOL_HIST, 0), (0, 0)))
    row = lambda shape: pl.BlockSpec((None,) + shape, lambda bi, ti: (bi, ti, 0))
    if seq_rows is None:
        per_seq = lambda rows: pl.BlockSpec((None, rows, POOL_DIM), lambda bi, ti: (bi, 0, 0))
    else:
        assert b == 1 and n_t == 1 and tm == n_seq * seq_rows and seq_rows >= POOL_KEEP
        per_seq = lambda rows: pl.BlockSpec((n_seq, rows, POOL_DIM), lambda bi, ti: (0, 0, 0))

    in_specs = [
        row((tm, D_MODEL)),
        _const_spec((1, D_MODEL)),
        _const_spec((D_MODEL, IN_AB_PAD)),
        _const_spec((1, Q_LORA)),
        _const_spec((Q_LORA, MLA_HEADS * HEAD_PAD)),
        _const_spec((1, KV_LORA)),
        _const_spec((POOL_GROUPS, POOL_GDIM, POOL_GDIM)),
        _const_spec((1, POOL_DIM)),
        pl.BlockSpec((tm, LANES), lambda bi, ti: (ti, 0)),
        per_seq(POOL_CARRY),
    ]
    args = [x, g_mix.reshape(1, -1), win_p, (q_norm * Q_SCALE).reshape(1, -1), wq, kv_norm.reshape(1, -1),
            w_pool.astype(BF16), pool_scale.reshape(1, -1), tab, hist]
    if emit_kv:
        in_specs += [_const_spec((KV_LORA, MLA_HEADS * HEAD_PAD)), _const_spec((KV_LORA, MLA_OUT)),
                     _const_spec((LANES, MLA_HEADS * HEAD_PAD))]
        args += [wk, wv, place]

    head_spec = pl.BlockSpec((None, MLA_HEADS, tm, HEAD_PAD), lambda bi, ti: (bi, 0, ti, 0))
    out_shape = [
        jax.ShapeDtypeStruct((b, MLA_HEADS, s, HEAD_PAD), BF16),
        jax.ShapeDtypeStruct((b, s, KV_LORA), F32),
        jax.ShapeDtypeStruct((b, QK_ROPE, s) if emit_kv else (b, s, QK_ROPE), F32),
        jax.ShapeDtypeStruct((b, s, POOL_DIM), BF16),
        jax.ShapeDtypeStruct((n_seq, POOL_KEEP, POOL_DIM), F32),
    ]
    out_specs = [
        head_spec,
        row((tm, KV_LORA)),
        pl.BlockSpec((None, QK_ROPE, tm), lambda bi, ti: (bi, 0, ti)) if emit_kv else row((tm, QK_ROPE)),
        row((tm, POOL_DIM)),
        per_seq(POOL_KEEP),
    ]
    if emit_kv:
        out_shape += [jax.ShapeDtypeStruct((b, MLA_HEADS, s, HEAD_PAD), BF16),
                      jax.ShapeDtypeStruct((b, MLA_HEADS, s, LANES), BF16)]
        out_specs += [head_spec, head_spec]

    return pl.pallas_call(
        functools.partial(_ab_in_kernel, tm=tm, pos0=pos0, emit_kv=emit_kv, seq_rows=seq_rows),
        out_shape=out_shape,
        grid=(b, n_t),
        in_specs=in_specs,
        out_specs=out_specs,
        scratch_shapes=[pltpu.VMEM((POOL_CARRY, POOL_DIM), F32)],
        compiler_params=pltpu.CompilerParams(dimension_semantics=("arbitrary", "arbitrary"),
                                             vmem_limit_bytes=VMEM_LIMIT),
        name="ab_in",
    )(*args)


def _flash_kernel(q_ref, k_ref, v_ref, o_ref, m_scr, acc_scr, *, tq, tk):
    t = pl.program_id(2)
    heads = (0, 1)

    def group(blocks):
        s = {}
        for n, (blk, d) in enumerate(blocks):
            off = pl.multiple_of(blk * tk, tk)
            r0 = 0 if d is None else d * tk
            for hh in heads:
                sh = _dot_nt(q_ref[hh, r0:, :], k_ref[hh, pl.ds(off, tk), :])
                if d is not None:
                    qc = (r0 + lax.broadcasted_iota(jnp.int32, (tq - r0, tk), 0)) // CHUNK
                    kc = (d * tk + lax.broadcasted_iota(jnp.int32, (tq - r0, tk), 1)) // CHUNK
                    sh = jnp.where(kc <= qc, sh, NEG)
                s[n, hh] = sh
        for n, (blk, d) in enumerate(blocks):
            off = pl.multiple_of(blk * tk, tk)
            r0 = 0 if d is None else d * tk
            for hh in heads:
                sh = s[n, hh]
                m_prev = m_scr[hh, r0:, :]
                m_new = jnp.maximum(m_prev, jnp.max(sh, axis=-1, keepdims=True))
                alpha = jnp.exp2(m_prev - m_new)
                p = jnp.concatenate([jnp.exp2(sh[:, c * LANES:(c + 1) * LANES] - m_new)
                                     for c in range(tk // LANES)], axis=1).astype(BF16)
                acc_scr[hh, r0:, :] = alpha * acc_scr[hh, r0:, :] + _dot(p, v_ref[hh, pl.ds(off, tk), :])
                m_scr[hh, r0:, :] = m_new

    m_scr[...] = jnp.full(m_scr.shape, NEG, F32)
    acc_scr[...] = jnp.zeros(acc_scr.shape, F32)

    def body(i, carry):
        group([(2 * i, None), (2 * i + 1, None)])
        return carry

    lax.fori_loop(0, t, body, 0)
    group([(2 * t, 0), (2 * t + 1, 1)])

    acc0, acc1 = acc_scr[0], acc_scr[1]
    out0 = acc0 / acc0[:, ROWSUM_LANE_EVEN:ROWSUM_LANE_EVEN + 1]
    out1 = acc1 / acc1[:, ROWSUM_LANE_ODD:ROWSUM_LANE_ODD + 1]
    lane = lax.broadcasted_iota(jnp.int32, (tq, LANES), 1)
    o_ref[...] = jnp.where(lane < V_HEAD, out0, out1).astype(o_ref.dtype)


def _flash_attention(q, k, v, *, tq):
    b, h, s, _ = q.shape
    tk = tq // 2
    assert tk % CHUNK == 0 and s % tq == 0
    pair = lambda n: pl.BlockSpec((None, 2, n, LANES), lambda bi, hp, ti: (bi, hp, 0, 0))
    return pl.pallas_call(
        functools.partial(_flash_kernel, tq=tq, tk=tk),
        out_shape=jax.ShapeDtypeStruct((b, s, MLA_OUT), BF16),
        grid=(b, h // 2, s // tq),
        in_specs=[
            pl.BlockSpec((None, 2, tq, HEAD_PAD), lambda bi, hp, ti: (bi, hp, ti, 0)),
            pair(s),
            pair(s),
        ],
        out_specs=pl.BlockSpec((None, tq, 2 * V_HEAD), lambda bi, hp, ti: (bi, ti, hp)),
        scratch_shapes=[pltpu.VMEM((2, tq, LANES), F32), pltpu.VMEM((2, tq, LANES), F32)],
        compiler_params=pltpu.CompilerParams(dimension_semantics=("arbitrary", "arbitrary", "arbitrary"),
                                             vmem_limit_bytes=VMEM_LIMIT),
        name="flash_attention",
    )(q, k, v)


def _latent_attn_kernel(q_ref, wabs_ref, cckv_ref, ckpe_ref, nckv_ref, nkpe_ref, wuv_ref, o_ref,
                        *, sq, past, pos0, n_seq):
    rows = MLA_HEADS * sq
    q_chunk = (pos0 + np.arange(rows)[:, None] % sq) // CHUNK
    vis_past = (pos0 - past + np.arange(past)[None, :]) // CHUNK <= q_chunk
    vis_new = (pos0 + np.arange(sq)[None, :]) // CHUNK <= q_chunk

    def masked(s, vis, n_keys, k_pos0):
        if vis.all():
            return s
        qc = (pos0 + lax.broadcasted_iota(jnp.int32, (rows, 1), 0) % sq) // CHUNK
        kc = (k_pos0 + lax.broadcasted_iota(jnp.int32, (1, n_keys), 1)) // CHUNK
        return jnp.where(kc <= qc, s, NEG)

    for i in range(n_seq):
        qcat = jnp.concatenate([_dot(q_ref[h, i * sq:(i + 1) * sq, :], wabs_ref[h]) for h in range(MLA_HEADS)],
                               axis=0)
        qa = qcat[:, :KV_LORA].astype(BF16)
        qp = qcat[:, KV_LORA:].astype(BF16)
        ck = cckv_ref[i].astype(BF16)
        kp_t = ckpe_ref[i].astype(BF16)
        nk = nckv_ref[i].astype(BF16)
        np_ = nkpe_ref[i].astype(BF16)
        s_past = masked(_dot_nt(qa, ck) + _dot(qp, kp_t), vis_past, past, pos0 - past)
        s_new = masked(_dot_nt(qa, nk) + _dot_nt(qp, np_), vis_new, sq, pos0)

        m = jnp.maximum(jnp.max(s_past, axis=-1, keepdims=True), jnp.max(s_new, axis=-1, keepdims=True))
        p_past = jnp.exp2(s_past - m)
        p_new = jnp.exp2(s_new - m)
        l = jnp.sum(p_past, axis=-1, keepdims=True) + jnp.sum(p_new, axis=-1, keepdims=True)
        o_lat = (_dot(p_past.astype(BF16), ck) + _dot(p_new.astype(BF16), nk)) / l
        o_lat = o_lat.astype(BF16)
        out = _dot(o_lat[0:sq], wuv_ref[0])
        for h in range(1, MLA_HEADS):
            out = out + _dot(o_lat[h * sq:(h + 1) * sq], wuv_ref[h])
        o_ref[i] = out.astype(o_ref.dtype)


def _prep_latent_weights(w_uk, w_uv):
    wabs = np.zeros((MLA_HEADS, HEAD_PAD, KV_LORA + QK_ROPE), np.float32)
    for i in range(QK_ROPE):
        wabs[:, QK_NOPE + i, KV_LORA + i] = 1.0
    wabs = jnp.asarray(wabs).at[:, :QK_NOPE, :KV_LORA].set(jnp.transpose(w_uk, (1, 2, 0)))
    wuv_place = jnp.zeros((MLA_HEADS, KV_LORA, MLA_OUT), F32)
    for h in range(MLA_HEADS):
        wuv_place = wuv_place.at[h, :, h * V_HEAD:(h + 1) * V_HEAD].set(w_uv[:, h, :])
    return wabs.astype(BF16), wuv_place.astype(BF16)


def _latent_attention(q, wabs, cache_ckv, cache_kpe, ckv_new, kpe_new, wuv_place, *, pos0):
    h = q.shape[0]
    b, sq, _ = ckv_new.shape
    past = cache_ckv.shape[1]
    n_seq = LATENT_SEQS_PER_STEP if b % LATENT_SEQS_PER_STEP == 0 else 1
    seq = lambda n, d: pl.BlockSpec((n_seq, n, d), lambda bi: (bi, 0, 0))
    return pl.pallas_call(
        functools.partial(_latent_attn_kernel, sq=sq, past=past, pos0=pos0, n_seq=n_seq),
        out_shape=jax.ShapeDtypeStruct((b, sq, MLA_OUT), BF16),
        grid=(b // n_seq,),
        in_specs=[
            pl.BlockSpec((h, n_seq * sq, HEAD_PAD), lambda bi: (0, bi, 0)),
            _const_spec((h, HEAD_PAD, KV_LORA + QK_ROPE)),
            seq(past, KV_LORA), seq(QK_ROPE, past), seq(sq, KV_LORA), seq(sq, QK_ROPE),
            _const_spec((h, KV_LORA, MLA_OUT)),
        ],
        out_specs=seq(sq, MLA_OUT),
        compiler_params=pltpu.CompilerParams(dimension_semantics=("arbitrary",),
                                             vmem_limit_bytes=VMEM_LIMIT),
        name="latent_attention",
    )(q, wabs, cache_ckv, cache_kpe, ckv_new, kpe_new, wuv_place)


def _out_ffn_kernel(*refs, n_act, final_norm):
    x_ref = refs[0]
    act_refs = refs[1:1 + n_act]
    wout_refs = refs[1 + n_act:1 + 2 * n_act]
    gffn_ref, wg_ref, wu_ref, wd_ref = refs[1 + 2 * n_act:5 + 2 * n_act]
    rest = refs[5 + 2 * n_act:]
    if final_norm:
        gfin_ref, o_ref = rest
    else:
        (o_ref,) = rest

    y = _dot(act_refs[0][...], wout_refs[0][...])
    for a_ref, w_ref in zip(act_refs[1:], wout_refs[1:]):
        y = y + _dot(a_ref[...], w_ref[...])
    x1 = x_ref[...] + y
    h2 = _rms(x1, gffn_ref[...]).astype(BF16)
    gate = _dot(h2, wg_ref[...])
    up = _dot(h2, wu_ref[...])
    act = (gate * jax.nn.sigmoid(gate) * up).astype(BF16)
    x2 = x1 + _dot(act, wd_ref[...])
    if final_norm:
        x2 = _rms(x2, gfin_ref[...])
    o_ref[...] = x2


def _out_ffn(x, acts, wouts, g_ffn, ffn_w, layer, g_final, *, tm):
    t, _ = x.shape
    n_act = len(acts)
    row = lambda d: pl.BlockSpec((tm, d), lambda i: (i, 0))
    slab = lambda w: pl.BlockSpec((None,) + w.shape[1:], lambda i: (layer, 0, 0), pipeline_mode=pl.Buffered(1))
    in_specs = ([row(D_MODEL)] + [row(a.shape[1]) for a in acts] + [_const_spec(w.shape) for w in wouts]
                + [_const_spec((1, D_MODEL))] + [slab(w) for w in ffn_w])
    args = [x, *acts, *wouts, g_ffn.reshape(1, -1), *ffn_w]
    if g_final is not None:
        in_specs.append(_const_spec((1, D_MODEL)))
        args.append(g_final.reshape(1, -1))
    return pl.pallas_call(
        functools.partial(_out_ffn_kernel, n_act=n_act, final_norm=g_final is not None),
        out_shape=jax.ShapeDtypeStruct((t, D_MODEL), F32),
        grid=(t // tm,),
        in_specs=in_specs,
        out_specs=row(D_MODEL),
        compiler_params=pltpu.CompilerParams(dimension_semantics=("arbitrary",),
                                             vmem_limit_bytes=VMEM_LIMIT),
        name="out_ffn",
    )(*args)


def _softplus(x):
    return jnp.maximum(x, 0.0) + jnp.log1p(jnp.exp(-jnp.abs(x)))


def _shift_rows(v, k):
    sub = lax.broadcasted_iota(jnp.int32, (1,) + v.shape[1:], 1)
    mixed = jnp.where(sub >= SUBLANES - k, v[:-1], v[1:])
    return pltpu.roll(mixed, k, axis=1)


def _ssd_in_kernel(x_ref, g_ref, win_ref, wdtt_ref, cw_ref, cb_ref, dtb_ref, dtbt_ref,
                   hist_ref, z_ref, xs_ref, b_ref, c_ref, dt_ref, dtt_ref, clast_ref, carry_scr,
                   *, tm, seq_rows):
    t = pl.program_id(1)
    hb = _rms(x_ref[...], g_ref[...]).astype(BF16)
    dt_ref[...] = _softplus(_dot(hb, win_ref[:, D_INNER + CONV_DIM:]) + dtb_ref[...])
    dtt_ref[...] = _softplus(_dot_nt(wdtt_ref[...].astype(BF16), hb) + dtbt_ref[...])

    lead = CONV_CARRY // SUBLANES
    if seq_rows is None:
        @pl.when(t == 0)
        def _():
            carry_scr[...] = hist_ref[...]
    else:
        n_seq, per_seq = tm // seq_rows, (CONV_CARRY + seq_rows) // SUBLANES

    gn = SSM_GROUPS * D_STATE
    outs = ((xs_ref, 0, D_INNER), (b_ref, D_INNER, gn), (c_ref, D_INNER + gn, gn))
    n_grp = tm // SUBLANES
    starts = list(range(0, CONV_DIM, CONV_CHUNK))
    xbc_w = lambda c0: win_ref[:, D_INNER + c0:D_INNER + c0 + CONV_CHUNK]
    xc_next = _dot(hb, xbc_w(0))
    for i, c0 in enumerate(starts):
        cols = slice(c0, c0 + CONV_CHUNK)
        xc = xc_next
        if i + 1 < len(starts):
            xc_next = _dot(hb, xbc_w(starts[i + 1]))
        if c0 < D_INNER:
            z_ref[:, cols] = _dot(hb, win_ref[:, cols]).astype(z_ref.dtype)
        if seq_rows is None:
            ext = jnp.concatenate([carry_scr[:, cols], xc], axis=0)
            ext = ext.reshape(n_grp + lead, SUBLANES, CONV_CHUNK)
            carry_scr[:, cols] = xc[tm - CONV_CARRY:, :]
            clast_ref[:, cols] = xc[tm - CONV_CARRY:, :]
        else:
            x_seq = xc.reshape(n_seq, seq_rows, CONV_CHUNK)
            ext = jnp.concatenate([hist_ref[:, :, cols], x_seq], axis=1)
            ext = ext.reshape(n_seq * per_seq, SUBLANES, CONV_CHUNK)
            clast_ref[:, :, cols] = x_seq[:, seq_rows - CONV_CARRY:, :]
        w = [cw_ref[k:k + 1, cols] for k in range(D_CONV)]
        ref, start = next((r, s0) for r, s0, width in outs if s0 <= c0 < s0 + width)
        x1 = _shift_rows(ext, 1)
        u = ext[2:] * w[3] + x1[1:] * w[2]
        v = ext[1:] * w[1] + x1 * w[0]
        half = u + _shift_rows(v, 2) + cb_ref[:, cols]
        if seq_rows is None:
            half = half.reshape(tm, CONV_CHUNK)
        else:
            pad = jnp.zeros((lead,) + half.shape[1:], half.dtype)
            half = jnp.concatenate([pad, half], axis=0).reshape(n_seq, per_seq, SUBLANES, CONV_CHUNK)
            half = half[:, lead:].reshape(tm, CONV_CHUNK)
        act = half + half * jnp.tanh(half)
        ref[:, c0 - start:c0 - start + CONV_CHUNK] = act.astype(ref.dtype)


def _ssd_in(x, conv_hist, g_mix, w_in, conv_w, conv_b, dt_bias, *, tm, seq_rows=None):
    b, s, _ = x.shape
    n_t = s // tm
    n_seq = conv_hist.shape[0]
    gn = SSM_GROUPS * D_STATE
    win_b = w_in.astype(BF16)
    wdt_t = w_in[:, D_INNER + CONV_DIM:].T
    hist = jnp.pad(conv_hist.astype(F32), ((0, 0), (CONV_CARRY - (D_CONV - 1), 0), (0, 0)))
    row = lambda d: pl.BlockSpec((None, tm, d), lambda bi, ti: (bi, ti, 0))
    if seq_rows is None:
        per_seq = pl.BlockSpec((None, CONV_CARRY, CONV_DIM), lambda bi, ti: (bi, 0, 0))
    else:
        assert b == 1 and n_t == 1 and tm == n_seq * seq_rows and seq_rows >= CONV_CARRY
        per_seq = pl.BlockSpec((n_seq, CONV_CARRY, CONV_DIM), lambda bi, ti: (0, 0, 0))
    return pl.pallas_call(
        functools.partial(_ssd_in_kernel, tm=tm, seq_rows=seq_rows),
        out_shape=[
            jax.ShapeDtypeStruct((b, s, D_INNER), BF16),
            jax.ShapeDtypeStruct((b, s, D_INNER), BF16),
            jax.ShapeDtypeStruct((b, s, gn), BF16),
            jax.ShapeDtypeStruct((b, s, gn), BF16),
            jax.ShapeDtypeStruct((b, s, SSM_HEADS), F32),
            jax.ShapeDtypeStruct((b, SSM_HEADS, s), F32),
            jax.ShapeDtypeStruct((n_seq, CONV_CARRY, CONV_DIM), F32),
        ],
        grid=(b, n_t),
        in_specs=[
            row(D_MODEL),
            _const_spec((1, D_MODEL)),
            _const_spec((D_MODEL, IN_C)),
            _const_spec((SSM_HEADS, D_MODEL)),
            _const_spec((D_CONV, CONV_DIM)),
            _const_spec((1, CONV_DIM)),
            _const_spec((1, SSM_HEADS)),
            _const_spec((SSM_HEADS, 1)),
            per_seq,
        ],
        out_specs=[
            row(D_INNER), row(D_INNER), row(gn), row(gn), row(SSM_HEADS),
            pl.BlockSpec((None, SSM_HEADS, tm), lambda bi, ti: (bi, 0, ti)),
            per_seq,
        ],
        scratch_shapes=[pltpu.VMEM((CONV_CARRY, CONV_DIM), F32)],
        compiler_params=pltpu.CompilerParams(dimension_semantics=("arbitrary", "arbitrary"),
                                             vmem_limit_bytes=VMEM_LIMIT),
        name="ssd_in",
    )(x, g_mix.reshape(1, -1), win_b, wdt_t, 0.5 * conv_w, 0.5 * conv_b.reshape(1, -1),
      dt_bias.reshape(1, -1), dt_bias.reshape(-1, 1), hist)


def _ssd_scan_kernel(xs_ref, z_ref, b_ref, c_ref, dt_ref, dtt_ref, alog_ref, alogt_ref, dexp_ref, norm_ref,
                     expand_ref, h0_ref, y_ref, hfin_ref, h_scr, *, ln, n_c):
    c = pl.program_id(1)

    @pl.when(c == 0)
    def _():
        for g in range(SSM_GROUPS):
            h_scr[g] = h0_ref[g * GROUP_COLS:(g + 1) * GROUP_COLS, :].T

    row_i = lax.broadcasted_iota(jnp.int32, (ln, ln), 0)
    col_i = lax.broadcasted_iota(jnp.int32, (ln, ln), 1)
    causal = row_i >= col_i
    tril = jnp.where(causal, 1.0, 0.0).astype(BF16)
    triu = jnp.where(row_i <= col_i, 1.0, 0.0).astype(BF16)
    lane = lax.broadcasted_iota(jnp.int32, (ln, LANES), 1)

    dt = dt_ref[...]
    dtt = dtt_ref[...]
    acum = _dot_sel_l(tril, dt * -jnp.exp(alog_ref[...]))
    acum_t = _dot_sel_r(dtt * -jnp.exp(alogt_ref[...]), triu)
    acum2 = acum * LOG2E
    key2_t = (acum_t - jnp.log(dtt)) * LOG2E
    last = acum[ln - 1:ln, :]
    ea_hi, ea_mid, _ = _split3(jnp.exp(acum))
    wj_b = (jnp.exp(last - acum) * dt).astype(BF16)
    pair = 2 * SSM_HEADDIM

    for g in range(SSM_GROUPS):
        gcols = slice(g * GROUP_COLS, (g + 1) * GROUP_COLS)
        expand = expand_ref[:, gcols]
        ea_e = _dot(ea_hi, expand) + _dot(ea_mid, expand)
        wj_e = _dot(wj_b, expand).astype(BF16)
        bg = b_ref[:, g * D_STATE:(g + 1) * D_STATE]
        cg = c_ref[:, g * D_STATE:(g + 1) * D_STATE]
        cb = _dot_nt(cg, bg)
        y_state = _dot(cg, h_scr[g].astype(BF16))
        gated = []
        for p in range(SSM_HPG // 2):
            h0 = g * SSM_HPG + 2 * p
            cols = slice(h0 * SSM_HEADDIM, (h0 + 2) * SSM_HEADDIM)
            pcols = slice(p * pair, (p + 1) * pair)
            scores = []
            for h in (h0, h0 + 1):
                seg = jnp.where(causal, acum2[:, h:h + 1] - key2_t[h:h + 1, :], NEG)
                scores.append((cb * jnp.exp2(seg)).astype(BF16))
            xp = xs_ref[:, cols]
            zero = jnp.zeros_like(xp)
            rhs = jnp.concatenate([jnp.where(lane < SSM_HEADDIM, xp, zero),
                                   jnp.where(lane >= SSM_HEADDIM, xp, zero)], axis=0)
            y = _dot(jnp.concatenate(scores, axis=1), rhs)
            y = y + y_state[:, pcols] * ea_e[:, pcols]
            y = y + xp.astype(F32) * dexp_ref[:, cols]
            zc = z_ref[:, cols].astype(F32)
            gated.append(y * (zc * jax.nn.sigmoid(zc)))
        yg = jnp.concatenate(gated, axis=1)
        yg = yg * lax.rsqrt(jnp.mean(yg * yg, axis=-1, keepdims=True) + EPS)
        y_ref[:, gcols] = (yg * norm_ref[:, gcols]).astype(y_ref.dtype)
        xw = xs_ref[:, gcols] * wj_e
        h_scr[g] = h_scr[g] * ea_e[ln - 1:ln, :] + _dot_tn(bg, xw)

    @pl.when(c == n_c - 1)
    def _():
        for g in range(SSM_GROUPS):
            hfin_ref[g * GROUP_COLS:(g + 1) * GROUP_COLS, :] = h_scr[g].T


def _ssd_scan(xs, z, bm, cm, dt, dtt, a_log, d_skip, ssm_norm, h0, *, ln):
    b, s, _ = xs.shape
    n_c = s // ln
    gn = SSM_GROUPS * D_STATE
    expand = jnp.asarray(np.kron(np.eye(SSM_HEADS, dtype=np.float32), np.ones((1, SSM_HEADDIM), np.float32)),
                         BF16)
    d_exp = jnp.repeat(d_skip.astype(F32), SSM_HEADDIM).reshape(1, D_INNER)
    row = lambda d: pl.BlockSpec((None, ln, d), lambda bi, ci: (bi, ci, 0))
    state_spec = pl.BlockSpec((None, D_INNER, D_STATE), lambda bi, ci: (bi, 0, 0))
    y, h_fin = pl.pallas_call(
        functools.partial(_ssd_scan_kernel, ln=ln, n_c=n_c),
        out_shape=[jax.ShapeDtypeStruct((b, s, D_INNER), BF16),
                   jax.ShapeDtypeStruct((b, D_INNER, D_STATE), F32)],
        grid=(b, n_c),
        in_specs=[
            row(D_INNER), row(D_INNER), row(gn), row(gn), row(SSM_HEADS),
            pl.BlockSpec((None, SSM_HEADS, ln), lambda bi, ci: (bi, 0, ci)),
            _const_spec((1, SSM_HEADS)), _const_spec((SSM_HEADS, 1)),
            _const_spec((1, D_INNER)), _const_spec((1, D_INNER)),
            _const_spec((SSM_HEADS, D_INNER)),
            state_spec,
        ],
        out_specs=[row(D_INNER), state_spec],
        scratch_shapes=[pltpu.VMEM((SSM_GROUPS, D_STATE, GROUP_COLS), F32)],
        compiler_params=pltpu.CompilerParams(dimension_semantics=("arbitrary", "arbitrary"),
                                             vmem_limit_bytes=VMEM_LIMIT),
        name="ssd_scan",
    )(xs, z, bm, cm, dt, dtt, a_log.reshape(1, -1), a_log.reshape(-1, 1), d_exp, ssm_norm.reshape(1, -1),
      expand, h0.astype(F32).reshape(b, D_INNER, D_STATE))
    return y, h_fin.reshape(b, SSM_HEADS, SSM_HEADDIM, D_STATE)


def _trunk(x, pos0, cache, pool_hist, conv_hist, ssm_hist, w, *, tm_proj, tm_ffn, tq, ln):
    b, s, _ = x.shape
    prompt = cache is None
    stacked = dict(tm=b * s, seq_rows=s) if not prompt else dict(tm=tm_proj)
    as_rows = (lambda a: a.reshape(1, b * s, a.shape[-1])) if not prompt else (lambda a: a)
    win_p, wq, wk, wv, place = _prep_ab_weights(w["w_in_ab"][0], w["w_uq"][0], w["w_uk"][0], w["w_uv"][0])
    outs = _ab_in(as_rows(x), pos0, pool_hist, w["norm_mix"][0], win_p, w["q_norm"][0], wq, w["kv_norm"][0],
                  w["w_pool"][0], w["pool_scale"][0], wk, wv, place, emit_kv=prompt, **stacked)
    q, ckv, kpe, pool_out, pool_last = outs[:5]
    if prompt:
        k, v = outs[5:]
        attn = _flash_attention(q, k, v, tq=tq)
        kpe = jnp.swapaxes(kpe, 1, 2)
    else:
        ckv, kpe = ckv.reshape(b, s, KV_LORA), kpe.reshape(b, s, QK_ROPE)
        wabs, wuv_place = _prep_latent_weights(w["w_uk"][0], w["w_uv"][0])
        attn = _latent_attention(q[0], wabs, cache[0], jnp.swapaxes(cache[1], 1, 2), ckv, kpe, wuv_place,
                                 pos0=pos0)
    ffn_w = (w["w_gate"].astype(BF16), w["w_up"].astype(BF16), w["w_down"].astype(BF16))
    w_out = w["w_out_ab"][0].astype(BF16)
    x1 = _out_ffn(x.reshape(b * s, D_MODEL),
                  [pool_out.reshape(b * s, POOL_DIM), attn.reshape(b * s, MLA_OUT)],
                  [w_out[:POOL_DIM], w_out[POOL_DIM:]],
                  w["norm_ffn"][0], ffn_w, 0, None, tm=tm_ffn)

    z, xs, bm, cm, dt, dtt, conv_last = _ssd_in(as_rows(x1.reshape(b, s, D_MODEL)), conv_hist, w["norm_mix"][1],
                                                w["w_in_c"][0], w["conv_w"][0], w["conv_b"][0],
                                                w["dt_bias"][0], **stacked)
    if not prompt:
        z, xs, bm, cm, dt = (a.reshape(b, s, a.shape[-1]) for a in (z, xs, bm, cm, dt))
        dtt = jnp.transpose(dtt.reshape(SSM_HEADS, b, s), (1, 0, 2))
    y, h_fin = _ssd_scan(xs, z, bm, cm, dt, dtt, w["a_log"][0], w["d_skip"][0], w["ssm_norm"][0], ssm_hist,
                         ln=ln)
    x2 = _out_ffn(x1, [y.reshape(b * s, D_INNER)], [w["w_out_c"][0].astype(BF16)],
                  w["norm_ffn"][1], ffn_w, 1, w["norm_final"], tm=tm_ffn)
    return (x2.reshape(b, s, D_MODEL), ckv[None], kpe[None], pool_last[None, :, POOL_KEEP - POOL_HIST:],
            conv_last[None, :, CONV_CARRY - (D_CONV - 1):], h_fin[None])


def kernel(x_prompt, x_sample, cache_ckv, cache_kpe, state_pool, state_conv, state_ssm, norm_mix, norm_ffn, norm_final, w_in_ab, w_pool, pool_scale, q_norm, w_uq, kv_norm, w_uk, w_uv, w_out_ab, w_in_c, conv_w, conv_b, dt_bias, a_log, d_skip, ssm_norm, w_out_c, w_gate, w_up, w_down):
    assert norm_mix.shape[0] == 2 and w_in_ab.shape[0] == 1 and w_in_c.shape[0] == 1
    w = dict(norm_mix=norm_mix, norm_ffn=norm_ffn, norm_final=norm_final, w_in_ab=w_in_ab, w_pool=w_pool,
             pool_scale=pool_scale, q_norm=q_norm, w_uq=w_uq, kv_norm=kv_norm, w_uk=w_uk, w_uv=w_uv,
             w_out_ab=w_out_ab, w_in_c=w_in_c, conv_w=conv_w, conv_b=conv_b, dt_bias=dt_bias, a_log=a_log,
             d_skip=d_skip, ssm_norm=ssm_norm, w_out_c=w_out_c, w_gate=w_gate, w_up=w_up, w_down=w_down)
    bp, sp, _ = x_prompt.shape
    bs, ss, _ = x_sample.shape
    past = cache_ckv.shape[2]

    prompt = _trunk(x_prompt, 0, None,
                    jnp.zeros((bp, POOL_HIST, POOL_DIM), F32), jnp.zeros((bp, D_CONV - 1, CONV_DIM), F32),
                    jnp.zeros((bp, SSM_HEADS, SSM_HEADDIM, D_STATE), F32), w,
                    tm_proj=512, tm_ffn=512, tq=1024, ln=128)
    sample = _trunk(x_sample, past, (cache_ckv[0], cache_kpe[0]), state_pool[0], state_conv[0], state_ssm[0], w,
                    tm_proj=ss, tm_ffn=bs * ss, tq=None, ln=ss)
    return (prompt[0], sample[0]) + prompt[1:] + sample[1:]
```

```python
import functools
import math

import numpy as np
import jax
import jax.numpy as jnp
from jax import lax
from jax.experimental import pallas as pl
from jax.experimental.pallas import tpu as pltpu

F32 = jnp.float32
BF16 = jnp.bfloat16

D_MODEL = 1024
CHUNK = 64
EPS = 1e-6
NEG = -1e30

POOL_WINDOWS = (2, 4, 8, 16)
POOL_GROUPS = 4
POOL_DIM = D_MODEL // 2
POOL_GDIM = POOL_DIM // POOL_GROUPS
POOL_HIST = max(POOL_WINDOWS) - 1

MLA_HEADS = 8
QK_NOPE = 64
QK_ROPE = 32
ROPE_HALF = QK_ROPE // 2
V_HEAD = 64
Q_LORA = 384
KV_LORA = 256
MLA_OUT = MLA_HEADS * V_HEAD
ROPE_THETA = 10000.0
SM_SCALE = (QK_NOPE + QK_ROPE) ** -0.5
LOG2E = math.log2(math.e)
Q_SCALE = SM_SCALE * LOG2E
IN_AB = POOL_DIM + Q_LORA + KV_LORA + QK_ROPE

D_INNER = 2 * D_MODEL
SSM_HEADDIM = 64
SSM_HEADS = D_INNER // SSM_HEADDIM
SSM_GROUPS = 4
SSM_HPG = SSM_HEADS // SSM_GROUPS
D_STATE = 128
D_CONV = 4
CONV_DIM = D_INNER + 2 * SSM_GROUPS * D_STATE
IN_C = D_INNER + CONV_DIM + SSM_HEADS
GROUP_COLS = D_INNER // SSM_GROUPS

D_FF = -(-8 * D_MODEL // (3 * 256)) * 256

LANES = 128
SUBLANES = 8
HEAD_PAD = LANES
ROWSUM_LANE_EVEN = V_HEAD
ROWSUM_LANE_ODD = 0
POOL_CARRY = 4 * SUBLANES
POOL_KEEP = 2 * SUBLANES
LATENT_SEQS_PER_STEP = 2
CONV_CARRY = 2 * SUBLANES
CONV_CHUNK = 256
IN_AB_PAD = 10 * LANES
VMEM_LIMIT = 56 * 1024 * 1024

OFF_Q = POOL_DIM
OFF_KV = OFF_Q + Q_LORA
OFF_KPE = OFF_KV + KV_LORA


def _const_spec(shape):
    nd = len(shape)
    return pl.BlockSpec(shape, lambda *_: (0,) * nd, pipeline_mode=pl.Buffered(1))


def _rms(x, g):
    return x * lax.rsqrt(jnp.mean(x * x, axis=-1, keepdims=True) + EPS) * g


def _dot(a, b):
    return jnp.dot(a, b, preferred_element_type=F32)


def _dot_nt(a, b):
    return lax.dot_general(a, b, (((1,), (1,)), ((), ())), preferred_element_type=F32)


def _dot_tn(a, b):
    return lax.dot_general(a, b, (((0,), (0,)), ((), ())), preferred_element_type=F32)


def _split3(x):
    hi = x.astype(BF16)
    r1 = x - hi.astype(F32)
    mid = r1.astype(BF16)
    lo = (r1 - mid.astype(F32)).astype(BF16)
    return hi, mid, lo


def _dot_sel_r(x, sel):
    hi, mid, lo = _split3(x)
    return _dot(hi, sel) + _dot(mid, sel) + _dot(lo, sel)


def _dot_sel_l(sel, x):
    hi, mid, lo = _split3(x)
    return _dot(sel, hi) + _dot(sel, mid) + _dot(sel, lo)


def _ab_in_kernel(*refs, tm, pos0, emit_kv, seq_rows):
    (x_ref, g_ref, win_ref, qn_ref, wuq_ref, kvn_ref, wpool_ref, pscale_ref, tab_ref, hist_ref) = refs[:10]
    refs = refs[10:]
    if emit_kv:
        wuk_ref, wuv_ref, place_ref = refs[:3]
        refs = refs[3:]
    q_ref, ckv_ref, kpe_ref, pool_ref, plast_ref = refs[:5]
    refs = refs[5:]
    if emit_kv:
        k_ref, v_ref = refs[:2]
        refs = refs[2:]
    (cat_scr,) = refs

    t = pl.program_id(1)
    x = x_ref[...]
    hb = _rms(x, g_ref[...]).astype(BF16)
    proj = _dot(hb, win_ref[...])
    u_pool = proj[:, :POOL_DIM]
    q_lat = proj[:, OFF_Q:OFF_KV]
    kv_lat = proj[:, OFF_KV:OFF_KPE]
    kblk = proj[:, OFF_KPE:OFF_KPE + LANES]

    lead = POOL_CARRY // SUBLANES
    row_i = lax.broadcasted_iota(jnp.int32, (tm, 1), 0)
    if seq_rows is None:
        @pl.when(t == 0)
        def _():
            cat_scr[...] = hist_ref[...]

        ext2d = jnp.concatenate([cat_scr[...], u_pool], axis=0)
        ext = ext2d.reshape((POOL_CARRY + tm) // SUBLANES, SUBLANES, POOL_DIM)
        pos = pos0 + t * tm + row_i
        cat_scr[...] = ext2d[tm:, :]
        plast_ref[...] = ext2d[POOL_CARRY + tm - POOL_KEEP:, :]
    else:
        n_seq, per_seq = tm // seq_rows, (POOL_CARRY + seq_rows) // SUBLANES
        u_seq = u_pool.reshape(n_seq, seq_rows, POOL_DIM)
        ext = jnp.concatenate([hist_ref[...], u_seq], axis=1)
        ext = ext.reshape(n_seq * per_seq, SUBLANES, POOL_DIM)
        pos = pos0 + row_i % seq_rows
        plast_ref[...] = u_seq[:, seq_rows - POOL_KEEP:, :]

    def tile_rows(win, dropped):
        if seq_rows is None:
            return win[lead - dropped:].reshape(tm, POOL_GDIM)
        pad = jnp.zeros((dropped,) + win.shape[1:], win.dtype)
        full = jnp.concatenate([pad, win], axis=0).reshape(n_seq, per_seq, SUBLANES, POOL_GDIM)
        return full[:, lead:].reshape(tm, POOL_GDIM)

    for g, w in enumerate(POOL_WINDOWS):
        cols = slice(g * POOL_GDIM, (g + 1) * POOL_GDIM)
        win, span, dropped = ext[:, :, cols], 1, 0
        while span < w:
            prev = win[:-1] if span == SUBLANES else _shift_rows(win, span)
            win, span, dropped = win[1:] + prev, 2 * span, dropped + 1
        win = tile_rows(win, dropped)
        u_g = u_pool[:, cols]
        cnt = jnp.minimum(pos + 1, w).astype(F32)
        pooled = win / cnt - u_g
        mixed = _dot(pooled.astype(BF16), wpool_ref[g]) * pscale_ref[:, cols]
        pool_ref[:, cols] = mixed.astype(pool_ref.dtype)

    tab = tab_ref[...]
    up = pltpu.roll(tab, ROPE_HALF, 1)
    down = pltpu.roll(tab, LANES - ROPE_HALF, 1)
    ln_i = lax.broadcasted_iota(jnp.int32, (tm, LANES), 1)
    x1_q, x2_q = QK_NOPE, QK_NOPE + ROPE_HALF
    tq_c = jnp.where(ln_i < x1_q, 1.0, jnp.where(ln_i < x2_q, tab, jnp.where(ln_i < x2_q + ROPE_HALF, up, 0.0)))
    tq_a = jnp.where((ln_i >= x2_q) & (ln_i < x2_q + ROPE_HALF), tab, 0.0)
    tq_b = jnp.where((ln_i >= x1_q) & (ln_i < x2_q), -down, 0.0)
    tk_c = jnp.where(ln_i < ROPE_HALF, tab, jnp.where(ln_i < QK_ROPE, up, 0.0))
    tk_s = jnp.where(ln_i < ROPE_HALF, -down, jnp.where(ln_i < QK_ROPE, tab, 0.0))

    qn = _rms(q_lat, qn_ref[...]).astype(BF16)
    q = _dot(qn, wuq_ref[...])
    for h in range(MLA_HEADS):
        qh = q[:, h * HEAD_PAD:(h + 1) * HEAD_PAD]
        qr = (qh * tq_c + pltpu.roll(qh, ROPE_HALF, 1) * tq_a
              + pltpu.roll(qh, HEAD_PAD - ROPE_HALF, 1) * tq_b)
        q_ref[h] = qr.astype(q_ref.dtype)

    ckv = _rms(kv_lat, kvn_ref[...])
    ckv_ref[...] = ckv
    kpe_blk = kblk * tk_c + pltpu.roll(kblk, LANES - QK_ROPE, 1) * tk_s
    if emit_kv:
        kpe_ref[...] = kpe_blk.T[:QK_ROPE, :]
    else:
        kpe_ref[...] = kpe_blk[:, :QK_ROPE]

    if emit_kv:
        ckv_b = ckv.astype(BF16)
        k_all = _dot(ckv_b, wuk_ref[...]) + _dot(kpe_blk.astype(BF16), place_ref[...])
        for h in range(MLA_HEADS):
            k_ref[h] = k_all[:, h * HEAD_PAD:(h + 1) * HEAD_PAD].astype(k_ref.dtype)
        v_all = _dot(ckv_b, wuv_ref[...])
        lane = lax.broadcasted_iota(jnp.int32, (tm, LANES), 1)
        for hp in range(MLA_HEADS // 2):
            vp = v_all[:, hp * LANES:(hp + 1) * LANES]
            even = jnp.where(lane < V_HEAD, vp, jnp.where(lane == ROWSUM_LANE_EVEN, 1.0, 0.0))
            odd = jnp.where(lane >= V_HEAD, vp, jnp.where(lane == ROWSUM_LANE_ODD, 1.0, 0.0))
            v_ref[2 * hp] = even.astype(v_ref.dtype)
            v_ref[2 * hp + 1] = odd.astype(v_ref.dtype)


def _rope_table(pos0, s):
    pos = (pos0 + jnp.arange(s)).astype(F32)
    inv = ROPE_THETA ** (-jnp.arange(ROPE_HALF, dtype=F32) / ROPE_HALF)
    ang = pos[:, None] * inv[None, :]
    return jnp.tile(jnp.concatenate([jnp.cos(ang), jnp.sin(ang)], axis=1), (1, LANES // QK_ROPE))


def _prep_ab_weights(w_in, w_uq, w_uk, w_uv):
    kpe_w = w_in[:, OFF_KPE:]
    swapped = jnp.concatenate([kpe_w[:, ROPE_HALF:], kpe_w[:, :ROPE_HALF]], axis=1)
    pad = jnp.zeros((D_MODEL, IN_AB_PAD - IN_AB - QK_ROPE), w_in.dtype)
    win_p = jnp.concatenate([w_in, swapped, pad], axis=1).astype(BF16)

    tail = HEAD_PAD - QK_NOPE - QK_ROPE
    wq = w_uq.reshape(Q_LORA, MLA_HEADS, QK_NOPE + QK_ROPE)
    wq = jnp.pad(wq, ((0, 0), (0, 0), (0, tail))).reshape(Q_LORA, MLA_HEADS * HEAD_PAD).astype(BF16)
    wk = jnp.pad(w_uk, ((0, 0), (0, 0), (0, HEAD_PAD - QK_NOPE))).reshape(KV_LORA, MLA_HEADS * HEAD_PAD)
    wk = wk.astype(BF16)
    wv = w_uv.reshape(KV_LORA, MLA_OUT).astype(BF16)
    place = np.zeros((LANES, MLA_HEADS * HEAD_PAD), np.float32)
    for h in range(MLA_HEADS):
        for i in range(QK_ROPE):
            place[i, h * HEAD_PAD + QK_NOPE + i] = 1.0
    return win_p, wq, wk, wv, jnp.asarray(place, BF16)


def _ab_in(x, pos0, pool_hist, g_mix, win_p, q_norm, wq, kv_norm, w_pool, pool_scale, wk, wv, place,
           *, tm, emit_kv, seq_rows=None):
    b, s, _ = x.shape
    n_t = s // tm
    n_seq = pool_hist.shape[0]
    tab = _rope_table(pos0, s) if seq_rows is None else jnp.tile(_rope_table(pos0, seq_rows), (n_seq, 1))
    hist = jnp.pad(pool_hist.astype(F32), ((0, 0), (POOL_CARRY - POOL_HIST, 0), (0, 0)))
    row = lambda shape: pl.BlockSpec((None,) + shape, lambda bi, ti: (bi, ti, 0))
    if seq_rows is None:
        per_seq = lambda rows: pl.BlockSpec((None, rows, POOL_DIM), lambda bi, ti: (bi, 0, 0))
    else:
        assert b == 1 and n_t == 1 and tm == n_seq * seq_rows and seq_rows >= POOL_KEEP
        per_seq = lambda rows: pl.BlockSpec((n_seq, rows, POOL_DIM), lambda bi, ti: (0, 0, 0))

    in_specs = [
        row((tm, D_MODEL)),
        _const_spec((1, D_MODEL)),
        _const_spec((D_MODEL, IN_AB_PAD)),
        _const_spec((1, Q_LORA)),
        _const_spec((Q_LORA, MLA_HEADS * HEAD_PAD)),
        _const_spec((1, KV_LORA)),
        _const_spec((POOL_GROUPS, POOL_GDIM, POOL_GDIM)),
        _const_spec((1, POOL_DIM)),
        pl.BlockSpec((tm, LANES), lambda bi, ti: (ti, 0)),
        per_seq(POOL_CARRY),
    ]
    args = [x, g_mix.reshape(1, -1), win_p, (q_norm * Q_SCALE).reshape(1, -1), wq, kv_norm.reshape(1, -1),
            w_pool.astype(BF16), pool_scale.reshape(1, -1), tab, hist]
    if emit_kv:
        in_specs += [_const_spec((KV_LORA, MLA_HEADS * HEAD_PAD)), _const_spec((KV_LORA, MLA_OUT)),
                     _const_spec((LANES, MLA_HEADS * HEAD_PAD))]
        args += [wk, wv, place]

    head_spec = pl.BlockSpec((None, MLA_HEADS, tm, HEAD_PAD), lambda bi, ti: (bi, 0, ti, 0))
    out_shape = [
        jax.ShapeDtypeStruct((b, MLA_HEADS, s, HEAD_PAD), BF16),
        jax.ShapeDtypeStruct((b, s, KV_LORA), F32),
        jax.ShapeDtypeStruct((b, QK_ROPE, s) if emit_kv else (b, s, QK_ROPE), F32),
        jax.ShapeDtypeStruct((b, s, POOL_DIM), BF16),
        jax.ShapeDtypeStruct((n_seq, POOL_KEEP, POOL_DIM), F32),
    ]
    out_specs = [
        head_spec,
        row((tm, KV_LORA)),
        pl.BlockSpec((None, QK_ROPE, tm), lambda bi, ti: (bi, 0, ti)) if emit_kv else row((tm, QK_ROPE)),
        row((tm, POOL_DIM)),
        per_seq(POOL_KEEP),
    ]
    if emit_kv:
        out_shape += [jax.ShapeDtypeStruct((b, MLA_HEADS, s, HEAD_PAD), BF16),
                      jax.ShapeDtypeStruct((b, MLA_HEADS, s, LANES), BF16)]
        out_specs += [head_spec, head_spec]

    return pl.pallas_call(
        functools.partial(_ab_in_kernel, tm=tm, pos0=pos0, emit_kv=emit_kv, seq_rows=seq_rows),
        out_shape=out_shape,
        grid=(b, n_t),
        in_specs=in_specs,
        out_specs=out_specs,
        scratch_shapes=[pltpu.VMEM((POOL_CARRY, POOL_DIM), F32)],
        compiler_params=pltpu.CompilerParams(dimension_semantics=("arbitrary", "arbitrary"),
                                             vmem_limit_bytes=VMEM_LIMIT),
        name="ab_in",
    )(*args)


def _flash_kernel(q_ref, k_ref, v_ref, o_ref, m_scr, acc_scr, *, tq, tk):
    t = pl.program_id(2)
    heads = (0, 1)

    def group(blocks):
        s = {}
        for n, (blk, d) in enumerate(blocks):
            off = pl.multiple_of(blk * tk, tk)
            r0 = 0 if d is None else d * tk
            for hh in heads:
                sh = _dot_nt(q_ref[hh, r0:, :], k_ref[hh, pl.ds(off, tk), :])
                if d is not None:
                    qc = (r0 + lax.broadcasted_iota(jnp.int32, (tq - r0, tk), 0)) // CHUNK
                    kc = (d * tk + lax.broadcasted_iota(jnp.int32, (tq - r0, tk), 1)) // CHUNK
                    sh = jnp.where(kc <= qc, sh, NEG)
                s[n, hh] = sh
        for n, (blk, d) in enumerate(blocks):
            off = pl.multiple_of(blk * tk, tk)
            r0 = 0 if d is None else d * tk
            for hh in heads:
                sh = s[n, hh]
                m_prev = m_scr[hh, r0:, :]
                m_new = jnp.maximum(m_prev, jnp.max(sh, axis=-1, keepdims=True))
                alpha = jnp.exp2(m_prev - m_new)
                p = jnp.concatenate([jnp.exp2(sh[:, c * LANES:(c + 1) * LANES] - m_new)
                                     for c in range(tk // LANES)], axis=1).astype(BF16)
                acc_scr[hh, r0:, :] = alpha * acc_scr[hh, r0:, :] + _dot(p, v_ref[hh, pl.ds(off, tk), :])
                m_scr[hh, r0:, :] = m_new

    m_scr[...] = jnp.full(m_scr.shape, NEG, F32)
    acc_scr[...] = jnp.zeros(acc_scr.shape, F32)

    def body(i, carry):
        group([(2 * i, None), (2 * i + 1, None)])
        return carry

    lax.fori_loop(0, t, body, 0)
    group([(2 * t, 0), (2 * t + 1, 1)])

    acc0, acc1 = acc_scr[0], acc_scr[1]
    out0 = acc0 / acc0[:, ROWSUM_LANE_EVEN:ROWSUM_LANE_EVEN + 1]
    out1 = acc1 / acc1[:, ROWSUM_LANE_ODD:ROWSUM_LANE_ODD + 1]
    lane = lax.broadcasted_iota(jnp.int32, (tq, LANES), 1)
    o_ref[...] = jnp.where(lane < V_HEAD, out0, out1).astype(o_ref.dtype)


def _flash_attention(q, k, v, *, tq):
    b, h, s, _ = q.shape
    tk = tq // 2
    assert tk % CHUNK == 0 and s % tq == 0
    pair = lambda n: pl.BlockSpec((None, 2, n, LANES), lambda bi, hp, ti: (bi, hp, 0, 0))
    return pl.pallas_call(
        functools.partial(_flash_kernel, tq=tq, tk=tk),
        out_shape=jax.ShapeDtypeStruct((b, s, MLA_OUT), BF16),
        grid=(b, h // 2, s // tq),
        in_specs=[
            pl.BlockSpec((None, 2, tq, HEAD_PAD), lambda bi, hp, ti: (bi, hp, ti, 0)),
            pair(s),
            pair(s),
        ],
        out_specs=pl.BlockSpec((None, tq, 2 * V_HEAD), lambda bi, hp, ti: (bi, ti, hp)),
        scratch_shapes=[pltpu.VMEM((2, tq, LANES), F32), pltpu.VMEM((2, tq, LANES), F32)],
        compiler_params=pltpu.CompilerParams(dimension_semantics=("arbitrary", "arbitrary", "arbitrary"),
                                             vmem_limit_bytes=VMEM_LIMIT),
        name="flash_attention",
    )(q, k, v)


def _latent_attn_kernel(q_ref, wabs_ref, cckv_ref, ckpe_ref, nckv_ref, nkpe_ref, wuv_ref, o_ref,
                        *, sq, past, pos0, n_seq):
    rows = MLA_HEADS * sq
    q_chunk = (pos0 + np.arange(rows)[:, None] % sq) // CHUNK
    vis_past = (pos0 - past + np.arange(past)[None, :]) // CHUNK <= q_chunk
    vis_new = (pos0 + np.arange(sq)[None, :]) // CHUNK <= q_chunk

    def masked(s, vis, n_keys, k_pos0):
        if vis.all():
            return s
        qc = (pos0 + lax.broadcasted_iota(jnp.int32, (rows, 1), 0) % sq) // CHUNK
        kc = (k_pos0 + lax.broadcasted_iota(jnp.int32, (1, n_keys), 1)) // CHUNK
        return jnp.where(kc <= qc, s, NEG)

    for i in range(n_seq):
        qcat = jnp.concatenate([_dot(q_ref[h, i * sq:(i + 1) * sq, :], wabs_ref[h]) for h in range(MLA_HEADS)],
                               axis=0)
        qa = qcat[:, :KV_LORA].astype(BF16)
        qp = qcat[:, KV_LORA:].astype(BF16)
        ck = cckv_ref[i].astype(BF16)
        kp_t = ckpe_ref[i].astype(BF16)
        nk = nckv_ref[i].astype(BF16)
        np_ = nkpe_ref[i].astype(BF16)
        s_past = masked(_dot_nt(qa, ck) + _dot(qp, kp_t), vis_past, past, pos0 - past)
        s_new = masked(_dot_nt(qa, nk) + _dot_nt(qp, np_), vis_new, sq, pos0)

        m = jnp.maximum(jnp.max(s_past, axis=-1, keepdims=True), jnp.max(s_new, axis=-1, keepdims=True))
        p_past = jnp.exp2(s_past - m)
        p_new = jnp.exp2(s_new - m)
        l = jnp.sum(p_past, axis=-1, keepdims=True) + jnp.sum(p_new, axis=-1, keepdims=True)
        o_lat = (_dot(p_past.astype(BF16), ck) + _dot(p_new.astype(BF16), nk)) / l
        o_lat = o_lat.astype(BF16)
        out = _dot(o_lat[0:sq], wuv_ref[0])
        for h in range(1, MLA_HEADS):
            out = out + _dot(o_lat[h * sq:(h + 1) * sq], wuv_ref[h])
        o_ref[i] = out.astype(o_ref.dtype)


def _prep_latent_weights(w_uk, w_uv):
    wabs = np.zeros((MLA_HEADS, HEAD_PAD, KV_LORA + QK_ROPE), np.float32)
    for i in range(QK_ROPE):
        wabs[:, QK_NOPE + i, KV_LORA + i] = 1.0
    wabs = jnp.asarray(wabs).at[:, :QK_NOPE, :KV_LORA].set(jnp.transpose(w_uk, (1, 2, 0)))
    wuv_place = jnp.zeros((MLA_HEADS, KV_LORA, MLA_OUT), F32)
    for h in range(MLA_HEADS):
        wuv_place = wuv_place.at[h, :, h * V_HEAD:(h + 1) * V_HEAD].set(w_uv[:, h, :])
    return wabs.astype(BF16), wuv_place.astype(BF16)


def _latent_attention(q, wabs, cache_ckv, cache_kpe, ckv_new, kpe_new, wuv_place, *, pos0):
    h = q.shape[0]
    b, sq, _ = ckv_new.shape
    past = cache_ckv.shape[1]
    n_seq = LATENT_SEQS_PER_STEP if b % LATENT_SEQS_PER_STEP == 0 else 1
    seq = lambda n, d: pl.BlockSpec((n_seq, n, d), lambda bi: (bi, 0, 0))
    return pl.pallas_call(
        functools.partial(_latent_attn_kernel, sq=sq, past=past, pos0=pos0, n_seq=n_seq),
        out_shape=jax.ShapeDtypeStruct((b, sq, MLA_OUT), BF16),
        grid=(b // n_seq,),
        in_specs=[
            pl.BlockSpec((h, n_seq * sq, HEAD_PAD), lambda bi: (0, bi, 0)),
            _const_spec((h, HEAD_PAD, KV_LORA + QK_ROPE)),
            seq(past, KV_LORA), seq(QK_ROPE, past), seq(sq, KV_LORA), seq(sq, QK_ROPE),
            _const_spec((h, KV_LORA, MLA_OUT)),
        ],
        out_specs=seq(sq, MLA_OUT),
        compiler_params=pltpu.CompilerParams(dimension_semantics=("arbitrary",),
                                             vmem_limit_bytes=VMEM_LIMIT),
        name="latent_attention",
    )(q, wabs, cache_ckv, cache_kpe, ckv_new, kpe_new, wuv_place)


def _out_ffn_kernel(*refs, n_act, final_norm):
    x_ref = refs[0]
    act_refs = refs[1:1 + n_act]
    wout_refs = refs[1 + n_act:1 + 2 * n_act]
    gffn_ref, wg_ref, wu_ref, wd_ref = refs[1 + 2 * n_act:5 + 2 * n_act]
    rest = refs[5 + 2 * n_act:]
    if final_norm:
        gfin_ref, o_ref = rest
    else:
        (o_ref,) = rest

    y = _dot(act_refs[0][...], wout_refs[0][...])
    for a_ref, w_ref in zip(act_refs[1:], wout_refs[1:]):
        y = y + _dot(a_ref[...], w_ref[...])
    x1 = x_ref[...] + y
    h2 = _rms(x1, gffn_ref[...]).astype(BF16)
    gate = _dot(h2, wg_ref[...])
    up = _dot(h2, wu_ref[...])
    act = (gate * jax.nn.sigmoid(gate) * up).astype(BF16)
    x2 = x1 + _dot(act, wd_ref[...])
    if final_norm:
        x2 = _rms(x2, gfin_ref[...])
    o_ref[...] = x2


def _out_ffn(x, acts, wouts, g_ffn, ffn_w, layer, g_final, *, tm):
    t, _ = x.shape
    n_act = len(acts)
    row = lambda d: pl.BlockSpec((tm, d), lambda i: (i, 0))
    slab = lambda w: pl.BlockSpec((None,) + w.shape[1:], lambda i: (layer, 0, 0), pipeline_mode=pl.Buffered(1))
    in_specs = ([row(D_MODEL)] + [row(a.shape[1]) for a in acts] + [_const_spec(w.shape) for w in wouts]
                + [_const_spec((1, D_MODEL))] + [slab(w) for w in ffn_w])
    args = [x, *acts, *wouts, g_ffn.reshape(1, -1), *ffn_w]
    if g_final is not None:
        in_specs.append(_const_spec((1, D_MODEL)))
        args.append(g_final.reshape(1, -1))
    return pl.pallas_call(
        functools.partial(_out_ffn_kernel, n_act=n_act, final_norm=g_final is not None),
        out_shape=jax.ShapeDtypeStruct((t, D_MODEL), F32),
        grid=(t // tm,),
        in_specs=in_specs,
        out_specs=row(D_MODEL),
        compiler_params=pltpu.CompilerParams(dimension_semantics=("arbitrary",),
                                             vmem_limit_bytes=VMEM_LIMIT),
        name="out_ffn",
    )(*args)


def _softplus(x):
    return jnp.maximum(x, 0.0) + jnp.log1p(jnp.exp(-jnp.abs(x)))


def _shift_rows(v, k):
    sub = lax.broadcasted_iota(jnp.int32, (1,) + v.shape[1:], 1)
    mixed = jnp.where(sub >= SUBLANES - k, v[:-1], v[1:])
    return pltpu.roll(mixed, k, axis=1)


def _ssd_in_kernel(x_ref, g_ref, win_ref, wdtt_ref, cw_ref, cb_ref, dtb_ref, dtbt_ref,
                   hist_ref, z_ref, xs_ref, b_ref, c_ref, dt_ref, dtt_ref, clast_ref, carry_scr,
                   *, tm, seq_rows):
    t = pl.program_id(1)
    hb = _rms(x_ref[...], g_ref[...]).astype(BF16)
    dt_ref[...] = _softplus(_dot(hb, win_ref[:, D_INNER + CONV_DIM:]) + dtb_ref[...])
    dtt_ref[...] = _softplus(_dot_nt(wdtt_ref[...].astype(BF16), hb) + dtbt_ref[...])

    lead = CONV_CARRY // SUBLANES
    if seq_rows is None:
        @pl.when(t == 0)
        def _():
            carry_scr[...] = hist_ref[...]
    else:
        n_seq, per_seq = tm // seq_rows, (CONV_CARRY + seq_rows) // SUBLANES

    gn = SSM_GROUPS * D_STATE
    outs = ((xs_ref, 0, D_INNER), (b_ref, D_INNER, gn), (c_ref, D_INNER + gn, gn))
    n_grp = tm // SUBLANES
    starts = list(range(0, CONV_DIM, CONV_CHUNK))
    xbc_w = lambda c0: win_ref[:, D_INNER + c0:D_INNER + c0 + CONV_CHUNK]
    xc_next = _dot(hb, xbc_w(0))
    for i, c0 in enumerate(starts):
        cols = slice(c0, c0 + CONV_CHUNK)
        xc = xc_next
        if i + 1 < len(starts):
            xc_next = _dot(hb, xbc_w(starts[i + 1]))
        if c0 < D_INNER:
            z_ref[:, cols] = _dot(hb, win_ref[:, cols]).astype(z_ref.dtype)
        if seq_rows is None:
            ext = jnp.concatenate([carry_scr[:, cols], xc], axis=0)
            ext = ext.reshape(n_grp + lead, SUBLANES, CONV_CHUNK)
            carry_scr[:, cols] = xc[tm - CONV_CARRY:, :]
            clast_ref[:, cols] = xc[tm - CONV_CARRY:, :]
        else:
            x_seq = xc.reshape(n_seq, seq_rows, CONV_CHUNK)
            ext = jnp.concatenate([hist_ref[:, :, cols], x_seq], axis=1)
            ext = ext.reshape(n_seq * per_seq, SUBLANES, CONV_CHUNK)
            clast_ref[:, :, cols] = x_seq[:, seq_rows - CONV_CARRY:, :]
        w = [cw_ref[k:k + 1, cols] for k in range(D_CONV)]
        ref, start = next((r, s0) for r, s0, width in outs if s0 <= c0 < s0 + width)
        x1 = _shift_rows(ext, 1)
        u = ext[2:] * w[3] + x1[1:] * w[2]
        v = ext[1:] * w[1] + x1 * w[0]
        half = u + _shift_rows(v, 2) + cb_ref[:, cols]
        if seq_rows is None:
            half = half.reshape(tm, CONV_CHUNK)
        else:
            pad = jnp.zeros((lead,) + half.shape[1:], half.dtype)
            half = jnp.concatenate([pad, half], axis=0).reshape(n_seq, per_seq, SUBLANES, CONV_CHUNK)
            half = half[:, lead:].reshape(tm, CONV_CHUNK)
        act = half + half * jnp.tanh(half)
        ref[:, c0 - start:c0 - start + CONV_CHUNK] = act.astype(ref.dtype)


def _ssd_in(x, conv_hist, g_mix, w_in, conv_w, conv_b, dt_bias, *, tm, seq_rows=None):
    b, s, _ = x.shape
    n_t = s // tm
    n_seq = conv_hist.shape[0]
    gn = SSM_GROUPS * D_STATE
    win_b = w_in.astype(BF16)
    wdt_t = w_in[:, D_INNER + CONV_DIM:].T
    hist = jnp.pad(conv_hist.astype(F32), ((0, 0), (CONV_CARRY - (D_CONV - 1), 0), (0, 0)))
    row = lambda d: pl.BlockSpec((None, tm, d), lambda bi, ti: (bi, ti, 0))
    if seq_rows is None:
        per_seq = pl.BlockSpec((None, CONV_CARRY, CONV_DIM), lambda bi, ti: (bi, 0, 0))
    else:
        assert b == 1 and n_t == 1 and tm == n_seq * seq_rows and seq_rows >= CONV_CARRY
        per_seq = pl.BlockSpec((n_seq, CONV_CARRY, CONV_DIM), lambda bi, ti: (0, 0, 0))
    return pl.pallas_call(
        functools.partial(_ssd_in_kernel, tm=tm, seq_rows=seq_rows),
        out_shape=[
            jax.ShapeDtypeStruct((b, s, D_INNER), BF16),
            jax.ShapeDtypeStruct((b, s, D_INNER), BF16),
            jax.ShapeDtypeStruct((b, s, gn), BF16),
            jax.ShapeDtypeStruct((b, s, gn), BF16),
            jax.ShapeDtypeStruct((b, s, SSM_HEADS), F32),
            jax.ShapeDtypeStruct((b, SSM_HEADS, s), F32),
            jax.ShapeDtypeStruct((n_seq, CONV_CARRY, CONV_DIM), F32),
        ],
        grid=(b, n_t),
        in_specs=[
            row(D_MODEL),
            _const_spec((1, D_MODEL)),
            _const_spec((D_MODEL, IN_C)),
            _const_spec((SSM_HEADS, D_MODEL)),
            _const_spec((D_CONV, CONV_DIM)),
            _const_spec((1, CONV_DIM)),
            _const_spec((1, SSM_HEADS)),
            _const_spec((SSM_HEADS, 1)),
            per_seq,
        ],
        out_specs=[
            row(D_INNER), row(D_INNER), row(gn), row(gn), row(SSM_HEADS),
            pl.BlockSpec((None, SSM_HEADS, tm), lambda bi, ti: (bi, 0, ti)),
            per_seq,
        ],
        scratch_shapes=[pltpu.VMEM((CONV_CARRY, CONV_DIM), F32)],
        compiler_params=pltpu.CompilerParams(dimension_semantics=("arbitrary", "arbitrary"),
                                             vmem_limit_bytes=VMEM_LIMIT),
        name="ssd_in",
    )(x, g_mix.reshape(1, -1), win_b, wdt_t, 0.5 * conv_w, 0.5 * conv_b.reshape(1, -1),
      dt_bias.reshape(1, -1), dt_bias.reshape(-1, 1), hist)


def _ssd_scan_kernel(xs_ref, z_ref, b_ref, c_ref, dt_ref, dtt_ref, alog_ref, alogt_ref, dexp_ref, norm_ref,
                     expand_ref, h0_ref, y_ref, hfin_ref, h_scr, *, ln, n_c):
    c = pl.program_id(1)

    @pl.when(c == 0)
    def _():
        for g in range(SSM_GROUPS):
            h_scr[g] = h0_ref[g * GROUP_COLS:(g + 1) * GROUP_COLS, :].T

    row_i = lax.broadcasted_iota(jnp.int32, (ln, ln), 0)
    col_i = lax.broadcasted_iota(jnp.int32, (ln, ln), 1)
    causal = row_i >= col_i
    tril = jnp.where(causal, 1.0, 0.0).astype(BF16)
    triu = jnp.where(row_i <= col_i, 1.0, 0.0).astype(BF16)
    lane = lax.broadcasted_iota(jnp.int32, (ln, LANES), 1)

    dt = dt_ref[...]
    dtt = dtt_ref[...]
    acum = _dot_sel_l(tril, dt * -jnp.exp(alog_ref[...]))
    acum_t = _dot_sel_r(dtt * -jnp.exp(alogt_ref[...]), triu)
    acum2 = acum * LOG2E
    key2_t = (acum_t - jnp.log(dtt)) * LOG2E
    last = acum[ln - 1:ln, :]
    ea_hi, ea_mid, _ = _split3(jnp.exp(acum))
    wj_b = (jnp.exp(last - acum) * dt).astype(BF16)
    pair = 2 * SSM_HEADDIM

    for g in range(SSM_GROUPS):
        gcols = slice(g * GROUP_COLS, (g + 1) * GROUP_COLS)
        expand = expand_ref[:, gcols]
        ea_e = _dot(ea_hi, expand) + _dot(ea_mid, expand)
        wj_e = _dot(wj_b, expand).astype(BF16)
        bg = b_ref[:, g * D_STATE:(g + 1) * D_STATE]
        cg = c_ref[:, g * D_STATE:(g + 1) * D_STATE]
        cb = _dot_nt(cg, bg)
        y_state = _dot(cg, h_scr[g].astype(BF16))
        gated = []
        for p in range(SSM_HPG // 2):
            h0 = g * SSM_HPG + 2 * p
            cols = slice(h0 * SSM_HEADDIM, (h0 + 2) * SSM_HEADDIM)
            pcols = slice(p * pair, (p + 1) * pair)
            scores = []
            for h in (h0, h0 + 1):
                seg = jnp.where(causal, acum2[:, h:h + 1] - key2_t[h:h + 1, :], NEG)
                scores.append((cb * jnp.exp2(seg)).astype(BF16))
            xp = xs_ref[:, cols]
            zero = jnp.zeros_like(xp)
            rhs = jnp.concatenate([jnp.where(lane < SSM_HEADDIM, xp, zero),
                                   jnp.where(lane >= SSM_HEADDIM, xp, zero)], axis=0)
            y = _dot(jnp.concatenate(scores, axis=1), rhs)
            y = y + y_state[:, pcols] * ea_e[:, pcols]
            y = y + xp.astype(F32) * dexp_ref[:, cols]
            zc = z_ref[:, cols].astype(F32)
            gated.append(y * (zc * jax.nn.sigmoid(zc)))
        yg = jnp.concatenate(gated, axis=1)
        yg = yg * lax.rsqrt(jnp.mean(yg * yg, axis=-1, keepdims=True) + EPS)
        y_ref[:, gcols] = (yg * norm_ref[:, gcols]).astype(y_ref.dtype)
        xw = xs_ref[:, gcols] * wj_e
        h_scr[g] = h_scr[g] * ea_e[ln - 1:ln, :] + _dot_tn(bg, xw)

    @pl.when(c == n_c - 1)
    def _():
        for g in range(SSM_GROUPS):
            hfin_ref[g * GROUP_COLS:(g + 1) * GROUP_COLS, :] = h_scr[g].T


def _ssd_scan(xs, z, bm, cm, dt, dtt, a_log, d_skip, ssm_norm, h0, *, ln):
    b, s, _ = xs.shape
    n_c = s // ln
    gn = SSM_GROUPS * D_STATE
    expand = jnp.asarray(np.kron(np.eye(SSM_HEADS, dtype=np.float32), np.ones((1, SSM_HEADDIM), np.float32)),
                         BF16)
    d_exp = jnp.repeat(d_skip.astype(F32), SSM_HEADDIM).reshape(1, D_INNER)
    row = lambda d: pl.BlockSpec((None, ln, d), lambda bi, ci: (bi, ci, 0))
    state_spec = pl.BlockSpec((None, D_INNER, D_STATE), lambda bi, ci: (bi, 0, 0))
    y, h_fin = pl.pallas_call(
        functools.partial(_ssd_scan_kernel, ln=ln, n_c=n_c),
        out_shape=[jax.ShapeDtypeStruct((b, s, D_INNER), BF16),
                   jax.ShapeDtypeStruct((b, D_INNER, D_STATE), F32)],
        grid=(b, n_c),
        in_specs=[
            row(D_INNER), row(D_INNER), row(gn), row(gn), row(SSM_HEADS),
            pl.BlockSpec((None, SSM_HEADS, ln), lambda bi, ci: (bi, 0, ci)),
            _const_spec((1, SSM_HEADS)), _const_spec((SSM_HEADS, 1)),
            _const_spec((1, D_INNER)), _const_spec((1, D_INNER)),
            _const_spec((SSM_HEADS, D_INNER)),
            state_spec,
        ],
        out_specs=[row(D_INNER), state_spec],
        scratch_shapes=[pltpu.VMEM((SSM_GROUPS, D_STATE, GROUP_COLS), F32)],
        compiler_params=pltpu.CompilerParams(dimension_semantics=("arbitrary", "arbitrary"),
                                             vmem_limit_bytes=VMEM_LIMIT),
        name="ssd_scan",
    )(xs, z, bm, cm, dt, dtt, a_log.reshape(1, -1), a_log.reshape(-1, 1), d_exp, ssm_norm.reshape(1, -1),
      expand, h0.astype(F32).reshape(b, D_INNER, D_STATE))
    return y, h_fin.reshape(b, SSM_HEADS, SSM_HEADDIM, D_STATE)


def _trunk(x, pos0, cache, pool_hist, conv_hist, ssm_hist, w, *, tm_proj, tm_ffn, tq, ln):
    b, s, _ = x.shape
    prompt = cache is None
    stacked = dict(tm=b * s, seq_rows=s) if not prompt else dict(tm=tm_proj)
    as_rows = (lambda a: a.reshape(1, b * s, a.shape[-1])) if not prompt else (lambda a: a)
    win_p, wq, wk, wv, place = _prep_ab_weights(w["w_in_ab"][0], w["w_uq"][0], w["w_uk"][0], w["w_uv"][0])
    outs = _ab_in(as_rows(x), pos0, pool_hist, w["norm_mix"][0], win_p, w["q_norm"][0], wq, w["kv_norm"][0],
                  w["w_pool"][0], w["pool_scale"][0], wk, wv, place, emit_kv=prompt, **stacked)
    q, ckv, kpe, pool_out, pool_last = outs[:5]
    if prompt:
        k, v = outs[5:]
        attn = _flash_attention(q, k, v, tq=tq)
        kpe = jnp.swapaxes(kpe, 1, 2)
    else:
        ckv, kpe = ckv.reshape(b, s, KV_LORA), kpe.reshape(b, s, QK_ROPE)
        wabs, wuv_place = _prep_latent_weights(w["w_uk"][0], w["w_uv"][0])
        attn = _latent_attention(q[0], wabs, cache[0], jnp.swapaxes(cache[1], 1, 2), ckv, kpe, wuv_place,
                                 pos0=pos0)
    ffn_w = (w["w_gate"].astype(BF16), w["w_up"].astype(BF16), w["w_down"].astype(BF16))
    w_out = w["w_out_ab"][0].astype(BF16)
    x1 = _out_ffn(x.reshape(b * s, D_MODEL),
                  [pool_out.reshape(b * s, POOL_DIM), attn.reshape(b * s, MLA_OUT)],
                  [w_out[:POOL_DIM], w_out[POOL_DIM:]],
                  w["norm_ffn"][0], ffn_w, 0, None, tm=tm_ffn)

    z, xs, bm, cm, dt, dtt, conv_last = _ssd_in(as_rows(x1.reshape(b, s, D_MODEL)), conv_hist, w["norm_mix"][1],
                                                w["w_in_c"][0], w["conv_w"][0], w["conv_b"][0],
                                                w["dt_bias"][0], **stacked)
    if not prompt:
        z, xs, bm, cm, dt = (a.reshape(b, s, a.shape[-1]) for a in (z, xs, bm, cm, dt))
        dtt = jnp.transpose(dtt.reshape(SSM_HEADS, b, s), (1, 0, 2))
    y, h_fin = _ssd_scan(xs, z, bm, cm, dt, dtt, w["a_log"][0], w["d_skip"][0], w["ssm_norm"][0], ssm_hist,
                         ln=ln)
    x2 = _out_ffn(x1, [y.reshape(b * s, D_INNER)], [w["w_out_c"][0].astype(BF16)],
                  w["norm_ffn"][1], ffn_w, 1, w["norm_final"], tm=tm_ffn)
    return (x2.reshape(b, s, D_MODEL), ckv[None], kpe[None], pool_last[None, :, POOL_KEEP - POOL_HIST:],
            conv_last[None, :, CONV_CARRY - (D_CONV - 1):], h_fin[None])


def kernel(x_prompt, x_sample, cache_ckv, cache_kpe, state_pool, state_conv, state_ssm, norm_mix, norm_ffn, norm_final, w_in_ab, w_pool, pool_scale, q_norm, w_uq, kv_norm, w_uk, w_uv, w_out_ab, w_in_c, conv_w, conv_b, dt_bias, a_log, d_skip, ssm_norm, w_out_c, w_gate, w_up, w_down):
    assert norm_mix.shape[0] == 2 and w_in_ab.shape[0] == 1 and w_in_c.shape[0] == 1
    w = dict(norm_mix=norm_mix, norm_ffn=norm_ffn, norm_final=norm_final, w_in_ab=w_in_ab, w_pool=w_pool,
             pool_scale=pool_scale, q_norm=q_norm, w_uq=w_uq, kv_norm=kv_norm, w_uk=w_uk, w_uv=w_uv,
             w_out_ab=w_out_ab, w_in_c=w_in_c, conv_w=conv_w, conv_b=conv_b, dt_bias=dt_bias, a_log=a_log,
             d_skip=d_skip, ssm_norm=ssm_norm, w_out_c=w_out_c, w_gate=w_gate, w_up=w_up, w_down=w_down)
    bp, sp, _ = x_prompt.shape
    bs, ss, _ = x_sample.shape
    past = cache_ckv.shape[2]

    prompt = _trunk(x_prompt, 0, None,
                    jnp.zeros((bp, POOL_HIST, POOL_DIM), F32), jnp.zeros((bp, D_CONV - 1, CONV_DIM), F32),
                    jnp.zeros((bp, SSM_HEADS, SSM_HEADDIM, D_STATE), F32), w,
                    tm_proj=1024, tm_ffn=512, tq=1024, ln=128)
    sample = _trunk(x_sample, past, (cache_ckv[0], cache_kpe[0]), state_pool[0], state_conv[0], state_ssm[0], w,
                    tm_proj=ss, tm_ffn=bs * ss, tq=None, ln=ss)
    return (prompt[0], sample[0]) + prompt[1:] + sample[1:]
```

```python
import functools
import math

import numpy as np
import jax
import jax.numpy as jnp
from jax import lax
from jax.experimental import pallas as pl
from jax.experimental.pallas import tpu as pltpu

F32 = jnp.float32
BF16 = jnp.bfloat16

D_MODEL = 1024
CHUNK = 64
EPS = 1e-6
NEG = -1e30

POOL_WINDOWS = (2, 4, 8, 16)
POOL_GROUPS = 4
POOL_DIM = D_MODEL // 2
POOL_GDIM = POOL_DIM // POOL_GROUPS
POOL_HIST = max(POOL_WINDOWS) - 1

MLA_HEADS = 8
QK_NOPE = 64
QK_ROPE = 32
ROPE_HALF = QK_ROPE // 2
V_HEAD = 64
Q_LORA = 384
KV_LORA = 256
MLA_OUT = MLA_HEADS * V_HEAD
ROPE_THETA = 10000.0
SM_SCALE = (QK_NOPE + QK_ROPE) ** -0.5
LOG2E = math.log2(math.e)
Q_SCALE = SM_SCALE * LOG2E
IN_AB = POOL_DIM + Q_LORA + KV_LORA + QK_ROPE

D_INNER = 2 * D_MODEL
SSM_HEADDIM = 64
SSM_HEADS = D_INNER // SSM_HEADDIM
SSM_GROUPS = 4
SSM_HPG = SSM_HEADS // SSM_GROUPS
D_STATE = 128
D_CONV = 4
CONV_DIM = D_INNER + 2 * SSM_GROUPS * D_STATE
IN_C = D_INNER + CONV_DIM + SSM_HEADS
GROUP_COLS = D_INNER // SSM_GROUPS

D_FF = -(-8 * D_MODEL // (3 * 256)) * 256

LANES = 128
SUBLANES = 8
HEAD_PAD = LANES
ROWSUM_LANE_EVEN = V_HEAD
ROWSUM_LANE_ODD = 0
POOL_CARRY = 4 * SUBLANES
POOL_KEEP = 2 * SUBLANES
LATENT_SEQS_PER_STEP = 2
CONV_CARRY = 2 * SUBLANES
CONV_CHUNK = 256
IN_AB_PAD = 10 * LANES
VMEM_LIMIT = 56 * 1024 * 1024

OFF_Q = POOL_DIM
OFF_KV = OFF_Q + Q_LORA
OFF_KPE = OFF_KV + KV_LORA


def _const_spec(shape):
    nd = len(shape)
    return pl.BlockSpec(shape, lambda *_: (0,) * nd, pipeline_mode=pl.Buffered(1))


def _rms(x, g):
    return x * lax.rsqrt(jnp.mean(x * x, axis=-1, keepdims=True) + EPS) * g


def _dot(a, b):
    return jnp.dot(a, b, preferred_element_type=F32)


def _dot_nt(a, b):
    return lax.dot_general(a, b, (((1,), (1,)), ((), ())), preferred_element_type=F32)


def _dot_tn(a, b):
    return lax.dot_general(a, b, (((0,), (0,)), ((), ())), preferred_element_type=F32)


def _split3(x):
    hi = x.astype(BF16)
    r1 = x - hi.astype(F32)
    mid = r1.astype(BF16)
    lo = (r1 - mid.astype(F32)).astype(BF16)
    return hi, mid, lo


def _dot_sel_r(x, sel):
    hi, mid, lo = _split3(x)
    return _dot(hi, sel) + _dot(mid, sel) + _dot(lo, sel)


def _dot_sel_l(sel, x):
    hi, mid, lo = _split3(x)
    return _dot(sel, hi) + _dot(sel, mid) + _dot(sel, lo)


def _ab_in_kernel(*refs, tm, pos0, emit_kv, seq_rows):
    (x_ref, g_ref, win_ref, qn_ref, wuq_ref, kvn_ref, wpool_ref, pscale_ref, tab_ref, hist_ref) = refs[:10]
    refs = refs[10:]
    if emit_kv:
        wuk_ref, wuv_ref, place_ref = refs[:3]
        refs = refs[3:]
    q_ref, ckv_ref, kpe_ref, pool_ref, plast_ref = refs[:5]
    refs = refs[5:]
    if emit_kv:
        k_ref, v_ref = refs[:2]
        refs = refs[2:]
    (cat_scr,) = refs

    t = pl.program_id(1)
    x = x_ref[...]
    hb = _rms(x, g_ref[...]).astype(BF16)
    proj = _dot(hb, win_ref[...])
    u_pool = proj[:, :POOL_DIM]
    q_lat = proj[:, OFF_Q:OFF_KV]
    kv_lat = proj[:, OFF_KV:OFF_KPE]
    kblk = proj[:, OFF_KPE:OFF_KPE + LANES]

    lead = POOL_CARRY // SUBLANES
    row_i = lax.broadcasted_iota(jnp.int32, (tm, 1), 0)
    if seq_rows is None:
        @pl.when(t == 0)
        def _():
            cat_scr[...] = hist_ref[...]

        ext2d = jnp.concatenate([cat_scr[...], u_pool], axis=0)
        ext = ext2d.reshape((POOL_CARRY + tm) // SUBLANES, SUBLANES, POOL_DIM)
        pos = pos0 + t * tm + row_i
        cat_scr[...] = ext2d[tm:, :]
        plast_ref[...] = ext2d[POOL_CARRY + tm - POOL_KEEP:, :]
    else:
        n_seq, per_seq = tm // seq_rows, (POOL_CARRY + seq_rows) // SUBLANES
        u_seq = u_pool.reshape(n_seq, seq_rows, POOL_DIM)
        ext = jnp.concatenate([hist_ref[...], u_seq], axis=1)
        ext = ext.reshape(n_seq * per_seq, SUBLANES, POOL_DIM)
        pos = pos0 + row_i % seq_rows
        plast_ref[...] = u_seq[:, seq_rows - POOL_KEEP:, :]

    def tile_rows(win, dropped):
        if seq_rows is None:
            return win[lead - dropped:].reshape(tm, POOL_GDIM)
        pad = jnp.zeros((dropped,) + win.shape[1:], win.dtype)
        full = jnp.concatenate([pad, win], axis=0).reshape(n_seq, per_seq, SUBLANES, POOL_GDIM)
        return full[:, lead:].reshape(tm, POOL_GDIM)

    for g, w in enumerate(POOL_WINDOWS):
        cols = slice(g * POOL_GDIM, (g + 1) * POOL_GDIM)
        win, span, dropped = ext[:, :, cols], 1, 0
        while span < w:
            prev = win[:-1] if span == SUBLANES else _shift_rows(win, span)
            win, span, dropped = win[1:] + prev, 2 * span, dropped + 1
        win = tile_rows(win, dropped)
        u_g = u_pool[:, cols]
        cnt = jnp.minimum(pos + 1, w).astype(F32)
        pooled = win / cnt - u_g
        mixed = _dot(pooled.astype(BF16), wpool_ref[g]) * pscale_ref[:, cols]
        pool_ref[:, cols] = mixed.astype(pool_ref.dtype)

    tab = tab_ref[...]
    up = pltpu.roll(tab, ROPE_HALF, 1)
    down = pltpu.roll(tab, LANES - ROPE_HALF, 1)
    ln_i = lax.broadcasted_iota(jnp.int32, (tm, LANES), 1)
    x1_q, x2_q = QK_NOPE, QK_NOPE + ROPE_HALF
    tq_c = jnp.where(ln_i < x1_q, 1.0, jnp.where(ln_i < x2_q, tab, jnp.where(ln_i < x2_q + ROPE_HALF, up, 0.0)))
    tq_a = jnp.where((ln_i >= x2_q) & (ln_i < x2_q + ROPE_HALF), tab, 0.0)
    tq_b = jnp.where((ln_i >= x1_q) & (ln_i < x2_q), -down, 0.0)
    tk_c = jnp.where(ln_i < ROPE_HALF, tab, jnp.where(ln_i < QK_ROPE, up, 0.0))
    tk_s = jnp.where(ln_i < ROPE_HALF, -down, jnp.where(ln_i < QK_ROPE, tab, 0.0))

    qn = _rms(q_lat, qn_ref[...]).astype(BF16)
    q = _dot(qn, wuq_ref[...])
    for h in range(MLA_HEADS):
        qh = q[:, h * HEAD_PAD:(h + 1) * HEAD_PAD]
        qr = (qh * tq_c + pltpu.roll(qh, ROPE_HALF, 1) * tq_a
              + pltpu.roll(qh, HEAD_PAD - ROPE_HALF, 1) * tq_b)
        q_ref[h] = qr.astype(q_ref.dtype)

    ckv = _rms(kv_lat, kvn_ref[...])
    ckv_ref[...] = ckv
    kpe_blk = kblk * tk_c + pltpu.roll(kblk, LANES - QK_ROPE, 1) * tk_s
    if emit_kv:
        kpe_ref[...] = kpe_blk.T[:QK_ROPE, :]
    else:
        kpe_ref[...] = kpe_blk[:, :QK_ROPE]

    if emit_kv:
        ckv_b = ckv.astype(BF16)
        k_all = _dot(ckv_b, wuk_ref[...]) + _dot(kpe_blk.astype(BF16), place_ref[...])
        for h in range(MLA_HEADS):
            k_ref[h] = k_all[:, h * HEAD_PAD:(h + 1) * HEAD_PAD].astype(k_ref.dtype)
        v_all = _dot(ckv_b, wuv_ref[...])
        lane = lax.broadcasted_iota(jnp.int32, (tm, LANES), 1)
        for hp in range(MLA_HEADS // 2):
            vp = v_all[:, hp * LANES:(hp + 1) * LANES]
            even = jnp.where(lane < V_HEAD, vp, jnp.where(lane == ROWSUM_LANE_EVEN, 1.0, 0.0))
            odd = jnp.where(lane >= V_HEAD, vp, jnp.where(lane == ROWSUM_LANE_ODD, 1.0, 0.0))
            v_ref[2 * hp] = even.astype(v_ref.dtype)
            v_ref[2 * hp + 1] = odd.astype(v_ref.dtype)


def _rope_table(pos0, s):
    pos = (pos0 + jnp.arange(s)).astype(F32)
    inv = ROPE_THETA ** (-jnp.arange(ROPE_HALF, dtype=F32) / ROPE_HALF)
    ang = pos[:, None] * inv[None, :]
    return jnp.tile(jnp.concatenate([jnp.cos(ang), jnp.sin(ang)], axis=1), (1, LANES // QK_ROPE))


def _prep_ab_weights(w_in, w_uq, w_uk, w_uv):
    kpe_w = w_in[:, OFF_KPE:]
    swapped = jnp.concatenate([kpe_w[:, ROPE_HALF:], kpe_w[:, :ROPE_HALF]], axis=1)
    pad = jnp.zeros((D_MODEL, IN_AB_PAD - IN_AB - QK_ROPE), w_in.dtype)
    win_p = jnp.concatenate([w_in, swapped, pad], axis=1).astype(BF16)

    tail = HEAD_PAD - QK_NOPE - QK_ROPE
    wq = w_uq.reshape(Q_LORA, MLA_HEADS, QK_NOPE + QK_ROPE)
    wq = jnp.pad(wq, ((0, 0), (0, 0), (0, tail))).reshape(Q_LORA, MLA_HEADS * HEAD_PAD).astype(BF16)
    wk = jnp.pad(w_uk, ((0, 0), (0, 0), (0, HEAD_PAD - QK_NOPE))).reshape(KV_LORA, MLA_HEADS * HEAD_PAD)
    wk = wk.astype(BF16)
    wv = w_uv.reshape(KV_LORA, MLA_OUT).astype(BF16)
    place = np.zeros((LANES, MLA_HEADS * HEAD_PAD), np.float32)
    for h in range(MLA_HEADS):
        for i in range(QK_ROPE):
            place[i, h * HEAD_PAD + QK_NOPE + i] = 1.0
    return win_p, wq, wk, wv, jnp.asarray(place, BF16)


def _ab_in(x, pos0, pool_hist, g_mix, win_p, q_norm, wq, kv_norm, w_pool, pool_scale, wk, wv, place,
           *, tm, emit_kv, seq_rows=None):
    b, s, _ = x.shape
    n_t = s // tm
    n_seq = pool_hist.shape[0]
    tab = _rope_table(pos0, s) if seq_rows is None else jnp.tile(_rope_table(pos0, seq_rows), (n_seq, 1))
    hist = jnp.pad(pool_hist.astype(F32), ((0, 0), (POOL_CARRY - POOL_HIST, 0), (0, 0)))
    row = lambda shape: pl.BlockSpec((None,) + shape, lambda bi, ti: (bi, ti, 0))
    if seq_rows is None:
        per_seq = lambda rows: pl.BlockSpec((None, rows, POOL_DIM), lambda bi, ti: (bi, 0, 0))
    else:
        assert b == 1 and n_t == 1 and tm == n_seq * seq_rows and seq_rows >= POOL_KEEP
        per_seq = lambda rows: pl.BlockSpec((n_seq, rows, POOL_DIM), lambda bi, ti: (0, 0, 0))

    in_specs = [
        row((tm, D_MODEL)),
        _const_spec((1, D_MODEL)),
        _const_spec((D_MODEL, IN_AB_PAD)),
        _const_spec((1, Q_LORA)),
        _const_spec((Q_LORA, MLA_HEADS * HEAD_PAD)),
        _const_spec((1, KV_LORA)),
        _const_spec((POOL_GROUPS, POOL_GDIM, POOL_GDIM)),
        _const_spec((1, POOL_DIM)),
        pl.BlockSpec((tm, LANES), lambda bi, ti: (ti, 0)),
        per_seq(POOL_CARRY),
    ]
    args = [x, g_mix.reshape(1, -1), win_p, (q_norm * Q_SCALE).reshape(1, -1), wq, kv_norm.reshape(1, -1),
            w_pool.astype(BF16), pool_scale.reshape(1, -1), tab, hist]
    if emit_kv:
        in_specs += [_const_spec((KV_LORA, MLA_HEADS * HEAD_PAD)), _const_spec((KV_LORA, MLA_OUT)),
                     _const_spec((LANES, MLA_HEADS * HEAD_PAD))]
        args += [wk, wv, place]

    head_spec = pl.BlockSpec((None, MLA_HEADS, tm, HEAD_PAD), lambda bi, ti: (bi, 0, ti, 0))
    out_shape = [
        jax.ShapeDtypeStruct((b, MLA_HEADS, s, HEAD_PAD), BF16),
        jax.ShapeDtypeStruct((b, s, KV_LORA), F32),
        jax.ShapeDtypeStruct((b, QK_ROPE, s) if emit_kv else (b, s, QK_ROPE), F32),
        jax.ShapeDtypeStruct((b, s, POOL_DIM), BF16),
        jax.ShapeDtypeStruct((n_seq, POOL_KEEP, POOL_DIM), F32),
    ]
    out_specs = [
        head_spec,
        row((tm, KV_LORA)),
        pl.BlockSpec((None, QK_ROPE, tm), lambda bi, ti: (bi, 0, ti)) if emit_kv else row((tm, QK_ROPE)),
        row((tm, POOL_DIM)),
        per_seq(POOL_KEEP),
    ]
    if emit_kv:
        out_shape += [jax.ShapeDtypeStruct((b, MLA_HEADS, s, HEAD_PAD), BF16),
                      jax.ShapeDtypeStruct((b, MLA_HEADS, s, LANES), BF16)]
        out_specs += [head_spec, head_spec]

    return pl.pallas_call(
        functools.partial(_ab_in_kernel, tm=tm, pos0=pos0, emit_kv=emit_kv, seq_rows=seq_rows),
        out_shape=out_shape,
        grid=(b, n_t),
        in_specs=in_specs,
        out_specs=out_specs,
        scratch_shapes=[pltpu.VMEM((POOL_CARRY, POOL_DIM), F32)],
        compiler_params=pltpu.CompilerParams(dimension_semantics=("arbitrary", "arbitrary"),
                                             vmem_limit_bytes=VMEM_LIMIT),
        name="ab_in",
    )(*args)


def _flash_kernel(q_ref, k_ref, v_ref, o_ref, m_scr, acc_scr, *, tq, tk):
    t = pl.program_id(2)
    heads = (0, 1)

    def group(blocks):
        s = {}
        for n, (blk, d) in enumerate(blocks):
            off = pl.multiple_of(blk * tk, tk)
            r0 = 0 if d is None else d * tk
            for hh in heads:
                sh = _dot_nt(q_ref[hh, r0:, :], k_ref[hh, pl.ds(off, tk), :])
                if d is not None:
                    qc = (r0 + lax.broadcasted_iota(jnp.int32, (tq - r0, tk), 0)) // CHUNK
                    kc = (d * tk + lax.broadcasted_iota(jnp.int32, (tq - r0, tk), 1)) // CHUNK
                    sh = jnp.where(kc <= qc, sh, NEG)
                s[n, hh] = sh
        for n, (blk, d) in enumerate(blocks):
            off = pl.multiple_of(blk * tk, tk)
            r0 = 0 if d is None else d * tk
            for hh in heads:
                sh = s[n, hh]
                m_prev = m_scr[hh, r0:, :]
                m_new = jnp.maximum(m_prev, jnp.max(sh, axis=-1, keepdims=True))
                alpha = jnp.exp2(m_prev - m_new)
                p = jnp.concatenate([jnp.exp2(sh[:, c * LANES:(c + 1) * LANES] - m_new)
                                     for c in range(tk // LANES)], axis=1).astype(BF16)
                acc_scr[hh, r0:, :] = alpha * acc_scr[hh, r0:, :] + _dot(p, v_ref[hh, pl.ds(off, tk), :])
                m_scr[hh, r0:, :] = m_new

    m_scr[...] = jnp.full(m_scr.shape, NEG, F32)
    acc_scr[...] = jnp.zeros(acc_scr.shape, F32)

    def body(i, carry):
        group([(2 * i, None), (2 * i + 1, None)])
        return carry

    lax.fori_loop(0, t, body, 0)
    group([(2 * t, 0), (2 * t + 1, 1)])

    acc0, acc1 = acc_scr[0], acc_scr[1]
    out0 = acc0 / acc0[:, ROWSUM_LANE_EVEN:ROWSUM_LANE_EVEN + 1]
    out1 = acc1 / acc1[:, ROWSUM_LANE_ODD:ROWSUM_LANE_ODD + 1]
    lane = lax.broadcasted_iota(jnp.int32, (tq, LANES), 1)
    o_ref[...] = jnp.where(lane < V_HEAD, out0, out1).astype(o_ref.dtype)


def _flash_attention(q, k, v, *, tq):
    b, h, s, _ = q.shape
    tk = tq // 2
    assert tk % CHUNK == 0 and s % tq == 0
    pair = lambda n: pl.BlockSpec((None, 2, n, LANES), lambda bi, hp, ti: (bi, hp, 0, 0))
    return pl.pallas_call(
        functools.partial(_flash_kernel, tq=tq, tk=tk),
        out_shape=jax.ShapeDtypeStruct((b, s, MLA_OUT), BF16),
        grid=(b, h // 2, s // tq),
        in_specs=[
            pl.BlockSpec((None, 2, tq, HEAD_PAD), lambda bi, hp, ti: (bi, hp, ti, 0)),
            pair(s),
            pair(s),
        ],
        out_specs=pl.BlockSpec((None, tq, 2 * V_HEAD), lambda bi, hp, ti: (bi, ti, hp)),
        scratch_shapes=[pltpu.VMEM((2, tq, LANES), F32), pltpu.VMEM((2, tq, LANES), F32)],
        compiler_params=pltpu.CompilerParams(dimension_semantics=("arbitrary", "arbitrary", "arbitrary"),
                                             vmem_limit_bytes=VMEM_LIMIT),
        name="flash_attention",
    )(q, k, v)


def _latent_attn_kernel(q_ref, wabs_ref, cckv_ref, ckpe_ref, nckv_ref, nkpe_ref, wuv_ref, o_ref,
                        *, sq, past, pos0, n_seq):
    rows = MLA_HEADS * sq
    q_chunk = (pos0 + np.arange(rows)[:, None] % sq) // CHUNK
    vis_past = (pos0 - past + np.arange(past)[None, :]) // CHUNK <= q_chunk
    vis_new = (pos0 + np.arange(sq)[None, :]) // CHUNK <= q_chunk

    def masked(s, vis, n_keys, k_pos0):
        if vis.all():
            return s
        qc = (pos0 + lax.broadcasted_iota(jnp.int32, (rows, 1), 0) % sq) // CHUNK
        kc = (k_pos0 + lax.broadcasted_iota(jnp.int32, (1, n_keys), 1)) // CHUNK
        return jnp.where(kc <= qc, s, NEG)

    for i in range(n_seq):
        qcat = jnp.concatenate([_dot(q_ref[h, i * sq:(i + 1) * sq, :], wabs_ref[h]) for h in range(MLA_HEADS)],
                               axis=0)
        qa = qcat[:, :KV_LORA].astype(BF16)
        qp = qcat[:, KV_LORA:].astype(BF16)
        ck = cckv_ref[i].astype(BF16)
        kp_t = ckpe_ref[i].astype(BF16)
        nk = nckv_ref[i].astype(BF16)
        np_ = nkpe_ref[i].astype(BF16)
        s_past = masked(_dot_nt(qa, ck) + _dot(qp, kp_t), vis_past, past, pos0 - past)
        s_new = masked(_dot_nt(qa, nk) + _dot_nt(qp, np_), vis_new, sq, pos0)

        m = jnp.maximum(jnp.max(s_past, axis=-1, keepdims=True), jnp.max(s_new, axis=-1, keepdims=True))
        p_past = jnp.exp2(s_past - m)
        p_new = jnp.exp2(s_new - m)
        l = jnp.sum(p_past, axis=-1, keepdims=True) + jnp.sum(p_new, axis=-1, keepdims=True)
        o_lat = (_dot(p_past.astype(BF16), ck) + _dot(p_new.astype(BF16), nk)) / l
        o_lat = o_lat.astype(BF16)
        out = _dot(o_lat[0:sq], wuv_ref[0])
        for h in range(1, MLA_HEADS):
            out = out + _dot(o_lat[h * sq:(h + 1) * sq], wuv_ref[h])
        o_ref[i] = out.astype(o_ref.dtype)


def _prep_latent_weights(w_uk, w_uv):
    wabs = np.zeros((MLA_HEADS, HEAD_PAD, KV_LORA + QK_ROPE), np.float32)
    for i in range(QK_ROPE):
        wabs[:, QK_NOPE + i, KV_LORA + i] = 1.0
    wabs = jnp.asarray(wabs).at[:, :QK_NOPE, :KV_LORA].set(jnp.transpose(w_uk, (1, 2, 0)))
    wuv_place = jnp.zeros((MLA_HEADS, KV_LORA, MLA_OUT), F32)
    for h in range(MLA_HEADS):
        wuv_place = wuv_place.at[h, :, h * V_HEAD:(h + 1) * V_HEAD].set(w_uv[:, h, :])
    return wabs.astype(BF16), wuv_place.astype(BF16)


def _latent_attention(q, wabs, cache_ckv, cache_kpe, ckv_new, kpe_new, wuv_place, *, pos0):
    h = q.shape[0]
    b, sq, _ = ckv_new.shape
    past = cache_ckv.shape[1]
    n_seq = LATENT_SEQS_PER_STEP if b % LATENT_SEQS_PER_STEP == 0 else 1
    seq = lambda n, d: pl.BlockSpec((n_seq, n, d), lambda bi: (bi, 0, 0))
    return pl.pallas_call(
        functools.partial(_latent_attn_kernel, sq=sq, past=past, pos0=pos0, n_seq=n_seq),
        out_shape=jax.ShapeDtypeStruct((b, sq, MLA_OUT), BF16),
        grid=(b // n_seq,),
        in_specs=[
            pl.BlockSpec((h, n_seq * sq, HEAD_PAD), lambda bi: (0, bi, 0)),
            _const_spec((h, HEAD_PAD, KV_LORA + QK_ROPE)),
            seq(past, KV_LORA), seq(QK_ROPE, past), seq(sq, KV_LORA), seq(sq, QK_ROPE),
            _const_spec((h, KV_LORA, MLA_OUT)),
        ],
        out_specs=seq(sq, MLA_OUT),
        compiler_params=pltpu.CompilerParams(dimension_semantics=("arbitrary",),
                                             vmem_limit_bytes=VMEM_LIMIT),
        name="latent_attention",
    )(q, wabs, cache_ckv, cache_kpe, ckv_new, kpe_new, wuv_place)


def _out_ffn_kernel(*refs, n_act, final_norm):
    x_ref = refs[0]
    act_refs = refs[1:1 + n_act]
    wout_refs = refs[1 + n_act:1 + 2 * n_act]
    gffn_ref, wg_ref, wu_ref, wd_ref = refs[1 + 2 * n_act:5 + 2 * n_act]
    rest = refs[5 + 2 * n_act:]
    if final_norm:
        gfin_ref, o_ref = rest
    else:
        (o_ref,) = rest

    y = _dot(act_refs[0][...], wout_refs[0][...])
    for a_ref, w_ref in zip(act_refs[1:], wout_refs[1:]):
        y = y + _dot(a_ref[...], w_ref[...])
    x1 = x_ref[...] + y
    h2 = _rms(x1, gffn_ref[...]).astype(BF16)
    gate = _dot(h2, wg_ref[...])
    up = _dot(h2, wu_ref[...])
    act = (gate * jax.nn.sigmoid(gate) * up).astype(BF16)
    x2 = x1 + _dot(act, wd_ref[...])
    if final_norm:
        x2 = _rms(x2, gfin_ref[...])
    o_ref[...] = x2


def _out_ffn(x, acts, wouts, g_ffn, ffn_w, layer, g_final, *, tm):
    t, _ = x.shape
    n_act = len(acts)
    row = lambda d: pl.BlockSpec((tm, d), lambda i: (i, 0))
    slab = lambda w: pl.BlockSpec((None,) + w.shape[1:], lambda i: (layer, 0, 0), pipeline_mode=pl.Buffered(1))
    in_specs = ([row(D_MODEL)] + [row(a.shape[1]) for a in acts] + [_const_spec(w.shape) for w in wouts]
                + [_const_spec((1, D_MODEL))] + [slab(w) for w in ffn_w])
    args = [x, *acts, *wouts, g_ffn.reshape(1, -1), *ffn_w]
    if g_final is not None:
        in_specs.append(_const_spec((1, D_MODEL)))
        args.append(g_final.reshape(1, -1))
    return pl.pallas_call(
        functools.partial(_out_ffn_kernel, n_act=n_act, final_norm=g_final is not None),
        out_shape=jax.ShapeDtypeStruct((t, D_MODEL), F32),
        grid=(t // tm,),
        in_specs=in_specs,
        out_specs=row(D_MODEL),
        compiler_params=pltpu.CompilerParams(dimension_semantics=("arbitrary",),
                                             vmem_limit_bytes=VMEM_LIMIT),
        name="out_ffn",
    )(*args)


def _softplus(x):
    return jnp.maximum(x, 0.0) + jnp.log1p(jnp.exp(-jnp.abs(x)))


def _shift_rows(v, k):
    sub = lax.broadcasted_iota(jnp.int32, (1,) + v.shape[1:], 1)
    mixed = jnp.where(sub >= SUBLANES - k, v[:-1], v[1:])
    return pltpu.roll(mixed, k, axis=1)


def _ssd_in_kernel(x_ref, g_ref, win_ref, wdtt_ref, cw_ref, cb_ref, dtb_ref, dtbt_ref,
                   hist_ref, z_ref, xs_ref, b_ref, c_ref, dt_ref, dtt_ref, clast_ref, carry_scr,
                   *, tm, seq_rows):
    t = pl.program_id(1)
    hb = _rms(x_ref[...], g_ref[...]).astype(BF16)
    dt_ref[...] = _softplus(_dot(hb, win_ref[:, D_INNER + CONV_DIM:]) + dtb_ref[...])
    dtt_ref[...] = _softplus(_dot_nt(wdtt_ref[...].astype(BF16), hb) + dtbt_ref[...])

    lead = CONV_CARRY // SUBLANES
    if seq_rows is None:
        @pl.when(t == 0)
        def _():
            carry_scr[...] = hist_ref[...]
    else:
        n_seq, per_seq = tm // seq_rows, (CONV_CARRY + seq_rows) // SUBLANES

    gn = SSM_GROUPS * D_STATE
    outs = ((xs_ref, 0, D_INNER), (b_ref, D_INNER, gn), (c_ref, D_INNER + gn, gn))
    n_grp = tm // SUBLANES
    starts = list(range(0, CONV_DIM, CONV_CHUNK))
    xbc_w = lambda c0: win_ref[:, D_INNER + c0:D_INNER + c0 + CONV_CHUNK]
    xc_next = _dot(hb, xbc_w(0))
    for i, c0 in enumerate(starts):
        cols = slice(c0, c0 + CONV_CHUNK)
        xc = xc_next
        if i + 1 < len(starts):
            xc_next = _dot(hb, xbc_w(starts[i + 1]))
        if c0 < D_INNER:
            z_ref[:, cols] = _dot(hb, win_ref[:, cols]).astype(z_ref.dtype)
        if seq_rows is None:
            ext = jnp.concatenate([carry_scr[:, cols], xc], axis=0)
            ext = ext.reshape(n_grp + lead, SUBLANES, CONV_CHUNK)
            carry_scr[:, cols] = xc[tm - CONV_CARRY:, :]
            clast_ref[:, cols] = xc[tm - CONV_CARRY:, :]
        else:
            x_seq = xc.reshape(n_seq, seq_rows, CONV_CHUNK)
            ext = jnp.concatenate([hist_ref[:, :, cols], x_seq], axis=1)
            ext = ext.reshape(n_seq * per_seq, SUBLANES, CONV_CHUNK)
            clast_ref[:, :, cols] = x_seq[:, seq_rows - CONV_CARRY:, :]
        w = [cw_ref[k:k + 1, cols] for k in range(D_CONV)]
        ref, start = next((r, s0) for r, s0, width in outs if s0 <= c0 < s0 + width)
        x1 = _shift_rows(ext, 1)
        u = ext[2:] * w[3] + x1[1:] * w[2]
        v = ext[1:] * w[1] + x1 * w[0]
        half = u + _shift_rows(v, 2) + cb_ref[:, cols]
        if seq_rows is None:
            half = half.reshape(tm, CONV_CHUNK)
        else:
            pad = jnp.zeros((lead,) + half.shape[1:], half.dtype)
            half = jnp.concatenate([pad, half], axis=0).reshape(n_seq, per_seq, SUBLANES, CONV_CHUNK)
            half = half[:, lead:].reshape(tm, CONV_CHUNK)
        act = half + half * jnp.tanh(half)
        ref[:, c0 - start:c0 - start + CONV_CHUNK] = act.astype(ref.dtype)


def _ssd_in(x, conv_hist, g_mix, w_in, conv_w, conv_b, dt_bias, *, tm, seq_rows=None):
    b, s, _ = x.shape
    n_t = s // tm
    n_seq = conv_hist.shape[0]
    gn = SSM_GROUPS * D_STATE
    win_b = w_in.astype(BF16)
    wdt_t = w_in[:, D_INNER + CONV_DIM:].T
    hist = jnp.pad(conv_hist.astype(F32), ((0, 0), (CONV_CARRY - (D_CONV - 1), 0), (0, 0)))
    row = lambda d: pl.BlockSpec((None, tm, d), lambda bi, ti: (bi, ti, 0))
    if seq_rows is None:
        per_seq = pl.BlockSpec((None, CONV_CARRY, CONV_DIM), lambda bi, ti: (bi, 0, 0))
    else:
        assert b == 1 and n_t == 1 and tm == n_seq * seq_rows and seq_rows >= CONV_CARRY
        per_seq = pl.BlockSpec((n_seq, CONV_CARRY, CONV_DIM), lambda bi, ti: (0, 0, 0))
    return pl.pallas_call(
        functools.partial(_ssd_in_kernel, tm=tm, seq_rows=seq_rows),
        out_shape=[
            jax.ShapeDtypeStruct((b, s, D_INNER), BF16),
            jax.ShapeDtypeStruct((b, s, D_INNER), BF16),
            jax.ShapeDtypeStruct((b, s, gn), BF16),
            jax.ShapeDtypeStruct((b, s, gn), BF16),
            jax.ShapeDtypeStruct((b, s, SSM_HEADS), F32),
            jax.ShapeDtypeStruct((b, SSM_HEADS, s), F32),
            jax.ShapeDtypeStruct((n_seq, CONV_CARRY, CONV_DIM), F32),
        ],
        grid=(b, n_t),
        in_specs=[
            row(D_MODEL),
            _const_spec((1, D_MODEL)),
            _const_spec((D_MODEL, IN_C)),
            _const_spec((SSM_HEADS, D_MODEL)),
            _const_spec((D_CONV, CONV_DIM)),
            _const_spec((1, CONV_DIM)),
            _const_spec((1, SSM_HEADS)),
            _const_spec((SSM_HEADS, 1)),
            per_seq,
        ],
        out_specs=[
            row(D_INNER), row(D_INNER), row(gn), row(gn), row(SSM_HEADS),
            pl.BlockSpec((None, SSM_HEADS, tm), lambda bi, ti: (bi, 0, ti)),
            per_seq,
        ],
        scratch_shapes=[pltpu.VMEM((CONV_CARRY, CONV_DIM), F32)],
        compiler_params=pltpu.CompilerParams(dimension_semantics=("arbitrary", "arbitrary"),
                                             vmem_limit_bytes=VMEM_LIMIT),
        name="ssd_in",
    )(x, g_mix.reshape(1, -1), win_b, wdt_t, 0.5 * conv_w, 0.5 * conv_b.reshape(1, -1),
      dt_bias.reshape(1, -1), dt_bias.reshape(-1, 1), hist)


def _ssd_scan_kernel(xs_ref, z_ref, b_ref, c_ref, dt_ref, dtt_ref, alog_ref, alogt_ref, dexp_ref, norm_ref,
                     expand_ref, h0_ref, y_ref, hfin_ref, h_scr, *, ln, n_sub, n_steps):
    c = pl.program_id(1)

    @pl.when(c == 0)
    def _():
        for g in range(SSM_GROUPS):
            h_scr[g] = h0_ref[g * GROUP_COLS:(g + 1) * GROUP_COLS, :].T

    row_i = lax.broadcasted_iota(jnp.int32, (ln, ln), 0)
    col_i = lax.broadcasted_iota(jnp.int32, (ln, ln), 1)
    causal = row_i >= col_i
    tril = jnp.where(causal, 1.0, 0.0).astype(BF16)
    triu = jnp.where(row_i <= col_i, 1.0, 0.0).astype(BF16)
    lane = lax.broadcasted_iota(jnp.int32, (ln, LANES), 1)

    for sc in range(n_sub):
        rows = slice(sc * ln, (sc + 1) * ln)
        dt = dt_ref[rows, :]
        dtt = dtt_ref[:, rows]
        acum = _dot_sel_l(tril, dt * -jnp.exp(alog_ref[...]))
        acum_t = _dot_sel_r(dtt * -jnp.exp(alogt_ref[...]), triu)
        acum2 = acum * LOG2E
        key2_t = (acum_t - jnp.log(dtt)) * LOG2E
        last = acum[ln - 1:ln, :]
        ea_hi, ea_mid, _ = _split3(jnp.exp(acum))
        wj_b = (jnp.exp(last - acum) * dt).astype(BF16)
        pair = 2 * SSM_HEADDIM

        for g in range(SSM_GROUPS):
            gcols = slice(g * GROUP_COLS, (g + 1) * GROUP_COLS)
            expand = expand_ref[:, gcols]
            ea_e = _dot(ea_hi, expand) + _dot(ea_mid, expand)
            wj_e = _dot(wj_b, expand).astype(BF16)
            bg = b_ref[rows, g * D_STATE:(g + 1) * D_STATE]
            cg = c_ref[rows, g * D_STATE:(g + 1) * D_STATE]
            cb = _dot_nt(cg, bg)
            y_state = _dot(cg, h_scr[g].astype(BF16))
            gated = []
            for p in range(SSM_HPG // 2):
                h0 = g * SSM_HPG + 2 * p
                cols = slice(h0 * SSM_HEADDIM, (h0 + 2) * SSM_HEADDIM)
                pcols = slice(p * pair, (p + 1) * pair)
                scores = []
                for h in (h0, h0 + 1):
                    seg = jnp.where(causal, acum2[:, h:h + 1] - key2_t[h:h + 1, :], NEG)
                    scores.append((cb * jnp.exp2(seg)).astype(BF16))
                xp = xs_ref[rows, cols]
                zero = jnp.zeros_like(xp)
                rhs = jnp.concatenate([jnp.where(lane < SSM_HEADDIM, xp, zero),
                                       jnp.where(lane >= SSM_HEADDIM, xp, zero)], axis=0)
                y = _dot(jnp.concatenate(scores, axis=1), rhs)
                y = y + y_state[:, pcols] * ea_e[:, pcols]
                y = y + xp.astype(F32) * dexp_ref[:, cols]
                zc = z_ref[rows, cols].astype(F32)
                gated.append(y * (zc * jax.nn.sigmoid(zc)))
            yg = jnp.concatenate(gated, axis=1)
            yg = yg * lax.rsqrt(jnp.mean(yg * yg, axis=-1, keepdims=True) + EPS)
            y_ref[rows, gcols] = (yg * norm_ref[:, gcols]).astype(y_ref.dtype)
            xw = xs_ref[rows, gcols] * wj_e
            h_scr[g] = h_scr[g] * ea_e[ln - 1:ln, :] + _dot_tn(bg, xw)

    @pl.when(c == n_steps - 1)
    def _():
        for g in range(SSM_GROUPS):
            hfin_ref[g * GROUP_COLS:(g + 1) * GROUP_COLS, :] = h_scr[g].T


def _ssd_scan(xs, z, bm, cm, dt, dtt, a_log, d_skip, ssm_norm, h0, *, ln, n_sub=1):
    b, s, _ = xs.shape
    step = ln * n_sub
    n_steps = s // step
    assert s % step == 0
    gn = SSM_GROUPS * D_STATE
    expand = jnp.asarray(np.kron(np.eye(SSM_HEADS, dtype=np.float32), np.ones((1, SSM_HEADDIM), np.float32)),
                         BF16)
    d_exp = jnp.repeat(d_skip.astype(F32), SSM_HEADDIM).reshape(1, D_INNER)
    row = lambda d: pl.BlockSpec((None, step, d), lambda bi, ci: (bi, ci, 0))
    state_spec = pl.BlockSpec((None, D_INNER, D_STATE), lambda bi, ci: (bi, 0, 0))
    y, h_fin = pl.pallas_call(
        functools.partial(_ssd_scan_kernel, ln=ln, n_sub=n_sub, n_steps=n_steps),
        out_shape=[jax.ShapeDtypeStruct((b, s, D_INNER), BF16),
                   jax.ShapeDtypeStruct((b, D_INNER, D_STATE), F32)],
        grid=(b, n_steps),
        in_specs=[
            row(D_INNER), row(D_INNER), row(gn), row(gn), row(SSM_HEADS),
            pl.BlockSpec((None, SSM_HEADS, step), lambda bi, ci: (bi, 0, ci)),
            _const_spec((1, SSM_HEADS)), _const_spec((SSM_HEADS, 1)),
            _const_spec((1, D_INNER)), _const_spec((1, D_INNER)),
            _const_spec((SSM_HEADS, D_INNER)),
            state_spec,
        ],
        out_specs=[row(D_INNER), state_spec],
        scratch_shapes=[pltpu.VMEM((SSM_GROUPS, D_STATE, GROUP_COLS), F32)],
        compiler_params=pltpu.CompilerParams(dimension_semantics=("arbitrary", "arbitrary"),
                                             vmem_limit_bytes=VMEM_LIMIT),
        name="ssd_scan",
    )(xs, z, bm, cm, dt, dtt, a_log.reshape(1, -1), a_log.reshape(-1, 1), d_exp, ssm_norm.reshape(1, -1),
      expand, h0.astype(F32).reshape(b, D_INNER, D_STATE))
    return y, h_fin.reshape(b, SSM_HEADS, SSM_HEADDIM, D_STATE)


def _trunk(x, pos0, cache, pool_hist, conv_hist, ssm_hist, w, *, tm_proj, tm_ffn, tq, ln, n_sub=1):
    b, s, _ = x.shape
    prompt = cache is None
    stacked = dict(tm=b * s, seq_rows=s) if not prompt else dict(tm=tm_proj)
    as_rows = (lambda a: a.reshape(1, b * s, a.shape[-1])) if not prompt else (lambda a: a)
    win_p, wq, wk, wv, place = _prep_ab_weights(w["w_in_ab"][0], w["w_uq"][0], w["w_uk"][0], w["w_uv"][0])
    outs = _ab_in(as_rows(x), pos0, pool_hist, w["norm_mix"][0], win_p, w["q_norm"][0], wq, w["kv_norm"][0],
                  w["w_pool"][0], w["pool_scale"][0], wk, wv, place, emit_kv=prompt, **stacked)
    q, ckv, kpe, pool_out, pool_last = outs[:5]
    if prompt:
        k, v = outs[5:]
        attn = _flash_attention(q, k, v, tq=tq)
        kpe = jnp.swapaxes(kpe, 1, 2)
    else:
        ckv, kpe = ckv.reshape(b, s, KV_LORA), kpe.reshape(b, s, QK_ROPE)
        wabs, wuv_place = _prep_latent_weights(w["w_uk"][0], w["w_uv"][0])
        attn = _latent_attention(q[0], wabs, cache[0], jnp.swapaxes(cache[1], 1, 2), ckv, kpe, wuv_place,
                                 pos0=pos0)
    ffn_w = (w["w_gate"].astype(BF16), w["w_up"].astype(BF16), w["w_down"].astype(BF16))
    w_out = w["w_out_ab"][0].astype(BF16)
    x1 = _out_ffn(x.reshape(b * s, D_MODEL),
                  [pool_out.reshape(b * s, POOL_DIM), attn.reshape(b * s, MLA_OUT)],
                  [w_out[:POOL_DIM], w_out[POOL_DIM:]],
                  w["norm_ffn"][0], ffn_w, 0, None, tm=tm_ffn)

    z, xs, bm, cm, dt, dtt, conv_last = _ssd_in(as_rows(x1.reshape(b, s, D_MODEL)), conv_hist, w["norm_mix"][1],
                                                w["w_in_c"][0], w["conv_w"][0], w["conv_b"][0],
                                                w["dt_bias"][0], **stacked)
    if not prompt:
        z, xs, bm, cm, dt = (a.reshape(b, s, a.shape[-1]) for a in (z, xs, bm, cm, dt))
        dtt = jnp.transpose(dtt.reshape(SSM_HEADS, b, s), (1, 0, 2))
    y, h_fin = _ssd_scan(xs, z, bm, cm, dt, dtt, w["a_log"][0], w["d_skip"][0], w["ssm_norm"][0], ssm_hist,
                         ln=ln, n_sub=n_sub)
    x2 = _out_ffn(x1, [y.reshape(b * s, D_INNER)], [w["w_out_c"][0].astype(BF16)],
                  w["norm_ffn"][1], ffn_w, 1, w["norm_final"], tm=tm_ffn)
    return (x2.reshape(b, s, D_MODEL), ckv[None], kpe[None], pool_last[None, :, POOL_KEEP - POOL_HIST:],
            conv_last[None, :, CONV_CARRY - (D_CONV - 1):], h_fin[None])


def kernel(x_prompt, x_sample, cache_ckv, cache_kpe, state_pool, state_conv, state_ssm, norm_mix, norm_ffn, norm_final, w_in_ab, w_pool, pool_scale, q_norm, w_uq, kv_norm, w_uk, w_uv, w_out_ab, w_in_c, conv_w, conv_b, dt_bias, a_log, d_skip, ssm_norm, w_out_c, w_gate, w_up, w_down):
    assert norm_mix.shape[0] == 2 and w_in_ab.shape[0] == 1 and w_in_c.shape[0] == 1
    w = dict(norm_mix=norm_mix, norm_ffn=norm_ffn, norm_final=norm_final, w_in_ab=w_in_ab, w_pool=w_pool,
             pool_scale=pool_scale, q_norm=q_norm, w_uq=w_uq, kv_norm=kv_norm, w_uk=w_uk, w_uv=w_uv,
             w_out_ab=w_out_ab, w_in_c=w_in_c, conv_w=conv_w, conv_b=conv_b, dt_bias=dt_bias, a_log=a_log,
             d_skip=d_skip, ssm_norm=ssm_norm, w_out_c=w_out_c, w_gate=w_gate, w_up=w_up, w_down=w_down)
    bp, sp, _ = x_prompt.shape
    bs, ss, _ = x_sample.shape
    past = cache_ckv.shape[2]

    prompt = _trunk(x_prompt, 0, None,
                    jnp.zeros((bp, POOL_HIST, POOL_DIM), F32), jnp.zeros((bp, D_CONV - 1, CONV_DIM), F32),
                    jnp.zeros((bp, SSM_HEADS, SSM_HEADDIM, D_STATE), F32), w,
                    tm_proj=1024, tm_ffn=512, tq=1024, ln=128, n_sub=4)
    sample = _trunk(x_sample, past, (cache_ckv[0], cache_kpe[0]), state_pool[0], state_conv[0], state_ssm[0], w,
                    tm_proj=ss, tm_ffn=bs * ss, tq=None, ln=ss)
    return (prompt[0], sample[0]) + prompt[1:] + sample[1:]
```

```python
import functools
import math

import numpy as np
import jax
import jax.numpy as jnp
from jax import lax
from jax.experimental import pallas as pl
from jax.experimental.pallas import tpu as pltpu

F32 = jnp.float32
BF16 = jnp.bfloat16

D_MODEL = 1024
CHUNK = 64
EPS = 1e-6
NEG = -1e30

POOL_WINDOWS = (2, 4, 8, 16)
POOL_GROUPS = 4
POOL_DIM = D_MODEL // 2
POOL_GDIM = POOL_DIM // POOL_GROUPS
POOL_HIST = max(POOL_WINDOWS) - 1

MLA_HEADS = 8
QK_NOPE = 64
QK_ROPE = 32
ROPE_HALF = QK_ROPE // 2
V_HEAD = 64
Q_LORA = 384
KV_LORA = 256
MLA_OUT = MLA_HEADS * V_HEAD
ROPE_THETA = 10000.0
SM_SCALE = (QK_NOPE + QK_ROPE) ** -0.5
LOG2E = math.log2(math.e)
Q_SCALE = SM_SCALE * LOG2E
IN_AB = POOL_DIM + Q_LORA + KV_LORA + QK_ROPE

D_INNER = 2 * D_MODEL
SSM_HEADDIM = 64
SSM_HEADS = D_INNER // SSM_HEADDIM
SSM_GROUPS = 4
SSM_HPG = SSM_HEADS // SSM_GROUPS
D_STATE = 128
D_CONV = 4
CONV_DIM = D_INNER + 2 * SSM_GROUPS * D_STATE
IN_C = D_INNER + CONV_DIM + SSM_HEADS
GROUP_COLS = D_INNER // SSM_GROUPS

LANES = 128
SUBLANES = 8
HEAD_PAD = LANES
ROWSUM_LANE_EVEN = V_HEAD
ROWSUM_LANE_ODD = 0
POOL_CARRY = 4 * SUBLANES
POOL_KEEP = 2 * SUBLANES
LATENT_SEQS_PER_STEP = 2
CONV_CARRY = 2 * SUBLANES
CONV_CHUNK = 256
IN_AB_PAD = 10 * LANES
VMEM_LIMIT = 56 * 1024 * 1024

PROMPT_PROJ_ROWS = 1024
PROMPT_FFN_ROWS = 512
PROMPT_Q_ROWS = 1024
SSD_CHUNK = 128
SSD_CHUNKS_PER_STEP = 8

OFF_Q = POOL_DIM
OFF_KV = OFF_Q + Q_LORA
OFF_KPE = OFF_KV + KV_LORA


def _const_spec(shape):
    nd = len(shape)
    return pl.BlockSpec(shape, lambda *_: (0,) * nd, pipeline_mode=pl.Buffered(1))


def _rms(x, g):
    return x * lax.rsqrt(jnp.mean(x * x, axis=-1, keepdims=True) + EPS) * g


def _dot(a, b):
    return jnp.dot(a, b, preferred_element_type=F32)


def _dot_nt(a, b):
    return lax.dot_general(a, b, (((1,), (1,)), ((), ())), preferred_element_type=F32)


def _dot_tn(a, b):
    return lax.dot_general(a, b, (((0,), (0,)), ((), ())), preferred_element_type=F32)


def _split3(x):
    hi = x.astype(BF16)
    r1 = x - hi.astype(F32)
    mid = r1.astype(BF16)
    lo = (r1 - mid.astype(F32)).astype(BF16)
    return hi, mid, lo


def _dot_sel_r(x, sel):
    hi, mid, lo = _split3(x)
    return _dot(hi, sel) + _dot(mid, sel) + _dot(lo, sel)


def _dot_sel_l(sel, x):
    hi, mid, lo = _split3(x)
    return _dot(sel, hi) + _dot(sel, mid) + _dot(sel, lo)


def _ab_in_kernel(*refs, tm, pos0, emit_kv, seq_rows):
    (x_ref, g_ref, win_ref, qn_ref, wuq_ref, kvn_ref, wpool_ref, pscale_ref, tab_ref, hist_ref) = refs[:10]
    refs = refs[10:]
    if emit_kv:
        wuk_ref, wuv_ref, place_ref = refs[:3]
        refs = refs[3:]
    q_ref, ckv_ref, kpe_ref, pool_ref, plast_ref = refs[:5]
    refs = refs[5:]
    if emit_kv:
        k_ref, v_ref = refs[:2]
        refs = refs[2:]
    (cat_scr,) = refs

    t = pl.program_id(1)
    x = x_ref[...]
    hb = _rms(x, g_ref[...]).astype(BF16)
    proj = _dot(hb, win_ref[...])
    u_pool = proj[:, :POOL_DIM]
    q_lat = proj[:, OFF_Q:OFF_KV]
    kv_lat = proj[:, OFF_KV:OFF_KPE]
    kblk = proj[:, OFF_KPE:OFF_KPE + LANES]

    lead = POOL_CARRY // SUBLANES
    row_i = lax.broadcasted_iota(jnp.int32, (tm, 1), 0)
    if seq_rows is None:
        @pl.when(t == 0)
        def _():
            cat_scr[...] = hist_ref[...]

        ext2d = jnp.concatenate([cat_scr[...], u_pool], axis=0)
        ext = ext2d.reshape((POOL_CARRY + tm) // SUBLANES, SUBLANES, POOL_DIM)
        pos = pos0 + t * tm + row_i
        cat_scr[...] = ext2d[tm:, :]
        plast_ref[...] = ext2d[POOL_CARRY + tm - POOL_KEEP:, :]
    else:
        n_seq, per_seq = tm // seq_rows, (POOL_CARRY + seq_rows) // SUBLANES
        u_seq = u_pool.reshape(n_seq, seq_rows, POOL_DIM)
        ext = jnp.concatenate([hist_ref[...], u_seq], axis=1)
        ext = ext.reshape(n_seq * per_seq, SUBLANES, POOL_DIM)
        pos = pos0 + row_i % seq_rows
        plast_ref[...] = u_seq[:, seq_rows - POOL_KEEP:, :]

    def tile_rows(win, dropped):
        if seq_rows is None:
            return win[lead - dropped:].reshape(tm, POOL_GDIM)
        pad = jnp.zeros((dropped,) + win.shape[1:], win.dtype)
        full = jnp.concatenate([pad, win], axis=0).reshape(n_seq, per_seq, SUBLANES, POOL_GDIM)
        return full[:, lead:].reshape(tm, POOL_GDIM)

    for g, w in enumerate(POOL_WINDOWS):
        cols = slice(g * POOL_GDIM, (g + 1) * POOL_GDIM)
        win, span, dropped = ext[:, :, cols], 1, 0
        while span < w:
            prev = win[:-1] if span == SUBLANES else _shift_rows(win, span)
            win, span, dropped = win[1:] + prev, 2 * span, dropped + 1
        win = tile_rows(win, dropped)
        u_g = u_pool[:, cols]
        cnt = jnp.minimum(pos + 1, w).astype(F32)
        pooled = win / cnt - u_g
        mixed = _dot(pooled.astype(BF16), wpool_ref[g]) * pscale_ref[:, cols]
        pool_ref[:, cols] = mixed.astype(pool_ref.dtype)

    tab = tab_ref[...]
    up = pltpu.roll(tab, ROPE_HALF, 1)
    down = pltpu.roll(tab, LANES - ROPE_HALF, 1)
    ln_i = lax.broadcasted_iota(jnp.int32, (tm, LANES), 1)
    x1_q, x2_q = QK_NOPE, QK_NOPE + ROPE_HALF
    tq_c = jnp.where(ln_i < x1_q, 1.0, jnp.where(ln_i < x2_q, tab, jnp.where(ln_i < x2_q + ROPE_HALF, up, 0.0)))
    tq_a = jnp.where((ln_i >= x2_q) & (ln_i < x2_q + ROPE_HALF), tab, 0.0)
    tq_b = jnp.where((ln_i >= x1_q) & (ln_i < x2_q), -down, 0.0)
    tk_c = jnp.where(ln_i < ROPE_HALF, tab, jnp.where(ln_i < QK_ROPE, up, 0.0))
    tk_s = jnp.where(ln_i < ROPE_HALF, -down, jnp.where(ln_i < QK_ROPE, tab, 0.0))

    qn = _rms(q_lat, qn_ref[...]).astype(BF16)
    q = _dot(qn, wuq_ref[...])
    for h in range(MLA_HEADS):
        qh = q[:, h * HEAD_PAD:(h + 1) * HEAD_PAD]
        qr = (qh * tq_c + pltpu.roll(qh, ROPE_HALF, 1) * tq_a
              + pltpu.roll(qh, HEAD_PAD - ROPE_HALF, 1) * tq_b)
        q_ref[h] = qr.astype(q_ref.dtype)

    ckv = _rms(kv_lat, kvn_ref[...])
    ckv_ref[...] = ckv
    kpe_blk = kblk * tk_c + pltpu.roll(kblk, LANES - QK_ROPE, 1) * tk_s
    if emit_kv:
        kpe_ref[...] = kpe_blk.T[:QK_ROPE, :]
    else:
        kpe_ref[...] = kpe_blk[:, :QK_ROPE]

    if emit_kv:
        ckv_b = ckv.astype(BF16)
        k_all = _dot(ckv_b, wuk_ref[...]) + _dot(kpe_blk.astype(BF16), place_ref[...])
        for h in range(MLA_HEADS):
            k_ref[h] = k_all[:, h * HEAD_PAD:(h + 1) * HEAD_PAD].astype(k_ref.dtype)
        v_all = _dot(ckv_b, wuv_ref[...])
        lane = lax.broadcasted_iota(jnp.int32, (tm, LANES), 1)
        for hp in range(MLA_HEADS // 2):
            vp = v_all[:, hp * LANES:(hp + 1) * LANES]
            even = jnp.where(lane < V_HEAD, vp, jnp.where(lane == ROWSUM_LANE_EVEN, 1.0, 0.0))
            odd = jnp.where(lane >= V_HEAD, vp, jnp.where(lane == ROWSUM_LANE_ODD, 1.0, 0.0))
            v_ref[2 * hp] = even.astype(v_ref.dtype)
            v_ref[2 * hp + 1] = odd.astype(v_ref.dtype)


def _rope_table(pos0, s):
    pos = (pos0 + jnp.arange(s)).astype(F32)
    inv = ROPE_THETA ** (-jnp.arange(ROPE_HALF, dtype=F32) / ROPE_HALF)
    ang = pos[:, None] * inv[None, :]
    return jnp.tile(jnp.concatenate([jnp.cos(ang), jnp.sin(ang)], axis=1), (1, LANES // QK_ROPE))


def _prep_ab_weights(w_in, w_uq, w_uk, w_uv):
    kpe_w = w_in[:, OFF_KPE:]
    swapped = jnp.concatenate([kpe_w[:, ROPE_HALF:], kpe_w[:, :ROPE_HALF]], axis=1)
    pad = jnp.zeros((D_MODEL, IN_AB_PAD - IN_AB - QK_ROPE), w_in.dtype)
    win_p = jnp.concatenate([w_in, swapped, pad], axis=1).astype(BF16)

    tail = HEAD_PAD - QK_NOPE - QK_ROPE
    wq = w_uq.reshape(Q_LORA, MLA_HEADS, QK_NOPE + QK_ROPE)
    wq = jnp.pad(wq, ((0, 0), (0, 0), (0, tail))).reshape(Q_LORA, MLA_HEADS * HEAD_PAD).astype(BF16)
    wk = jnp.pad(w_uk, ((0, 0), (0, 0), (0, HEAD_PAD - QK_NOPE))).reshape(KV_LORA, MLA_HEADS * HEAD_PAD)
    wk = wk.astype(BF16)
    wv = w_uv.reshape(KV_LORA, MLA_OUT).astype(BF16)
    place = np.zeros((LANES, MLA_HEADS * HEAD_PAD), np.float32)
    for h in range(MLA_HEADS):
        for i in range(QK_ROPE):
            place[i, h * HEAD_PAD + QK_NOPE + i] = 1.0
    return win_p, wq, wk, wv, jnp.asarray(place, BF16)


def _ab_in(x, pos0, pool_hist, g_mix, win_p, q_norm, wq, kv_norm, w_pool, pool_scale, wk, wv, place,
           *, tm, emit_kv, seq_rows=None):
    b, s, _ = x.shape
    n_t = s // tm
    n_seq = pool_hist.shape[0]
    tab = _rope_table(pos0, s) if seq_rows is None else jnp.tile(_rope_table(pos0, seq_rows), (n_seq, 1))
    hist = jnp.pad(pool_hist.astype(F32), ((0, 0), (POOL_CARRY - POOL_HIST, 0), (0, 0)))
    row = lambda shape: pl.BlockSpec((None,) + shape, lambda bi, ti: (bi, ti, 0))
    if seq_rows is None:
        per_seq = lambda rows: pl.BlockSpec((None, rows, POOL_DIM), lambda bi, ti: (bi, 0, 0))
    else:
        assert b == 1 and n_t == 1 and tm == n_seq * seq_rows and seq_rows >= POOL_KEEP
        per_seq = lambda rows: pl.BlockSpec((n_seq, rows, POOL_DIM), lambda bi, ti: (0, 0, 0))

    in_specs = [
        row((tm, D_MODEL)),
        _const_spec((1, D_MODEL)),
        _const_spec((D_MODEL, IN_AB_PAD)),
        _const_spec((1, Q_LORA)),
        _const_spec((Q_LORA, MLA_HEADS * HEAD_PAD)),
        _const_spec((1, KV_LORA)),
        _const_spec((POOL_GROUPS, POOL_GDIM, POOL_GDIM)),
        _const_spec((1, POOL_DIM)),
        pl.BlockSpec((tm, LANES), lambda bi, ti: (ti, 0)),
        per_seq(POOL_CARRY),
    ]
    args = [x, g_mix.reshape(1, -1), win_p, (q_norm * Q_SCALE).reshape(1, -1), wq, kv_norm.reshape(1, -1),
            w_pool.astype(BF16), pool_scale.reshape(1, -1), tab, hist]
    if emit_kv:
        in_specs += [_const_spec((KV_LORA, MLA_HEADS * HEAD_PAD)), _const_spec((KV_LORA, MLA_OUT)),
                     _const_spec((LANES, MLA_HEADS * HEAD_PAD))]
        args += [wk, wv, place]

    head_spec = pl.BlockSpec((None, MLA_HEADS, tm, HEAD_PAD), lambda bi, ti: (bi, 0, ti, 0))
    out_shape = [
        jax.ShapeDtypeStruct((b, MLA_HEADS, s, HEAD_PAD), BF16),
        jax.ShapeDtypeStruct((b, s, KV_LORA), F32),
        jax.ShapeDtypeStruct((b, QK_ROPE, s) if emit_kv else (b, s, QK_ROPE), F32),
        jax.ShapeDtypeStruct((b, s, POOL_DIM), BF16),
        jax.ShapeDtypeStruct((n_seq, POOL_KEEP, POOL_DIM), F32),
    ]
    out_specs = [
        head_spec,
        row((tm, KV_LORA)),
        pl.BlockSpec((None, QK_ROPE, tm), lambda bi, ti: (bi, 0, ti)) if emit_kv else row((tm, QK_ROPE)),
        row((tm, POOL_DIM)),
        per_seq(POOL_KEEP),
    ]
    if emit_kv:
        out_shape += [jax.ShapeDtypeStruct((b, MLA_HEADS, s, HEAD_PAD), BF16),
                      jax.ShapeDtypeStruct((b, MLA_HEADS, s, LANES), BF16)]
        out_specs += [head_spec, head_spec]

    return pl.pallas_call(
        functools.partial(_ab_in_kernel, tm=tm, pos0=pos0, emit_kv=emit_kv, seq_rows=seq_rows),
        out_shape=out_shape,
        grid=(b, n_t),
        in_specs=in_specs,
        out_specs=out_specs,
        scratch_shapes=[pltpu.VMEM((POOL_CARRY, POOL_DIM), F32)],
        compiler_params=pltpu.CompilerParams(dimension_semantics=("arbitrary", "arbitrary"),
                                             vmem_limit_bytes=VMEM_LIMIT),
        name="ab_in",
    )(*args)


def _flash_kernel(q_ref, k_ref, v_ref, o_ref, m_scr, acc_scr, *, tq, tk):
    t = pl.program_id(2)
    heads = (0, 1)

    def group(blocks):
        s = {}
        for n, (blk, d) in enumerate(blocks):
            off = pl.multiple_of(blk * tk, tk)
            r0 = 0 if d is None else d * tk
            for hh in heads:
                sh = _dot_nt(q_ref[hh, r0:, :], k_ref[hh, pl.ds(off, tk), :])
                if d is not None:
                    qc = (r0 + lax.broadcasted_iota(jnp.int32, (tq - r0, tk), 0)) // CHUNK
                    kc = (d * tk + lax.broadcasted_iota(jnp.int32, (tq - r0, tk), 1)) // CHUNK
                    sh = jnp.where(kc <= qc, sh, NEG)
                s[n, hh] = sh
        for n, (blk, d) in enumerate(blocks):
            off = pl.multiple_of(blk * tk, tk)
            r0 = 0 if d is None else d * tk
            for hh in heads:
                sh = s[n, hh]
                m_prev = m_scr[hh, r0:, :]
                m_new = jnp.maximum(m_prev, jnp.max(sh, axis=-1, keepdims=True))
                alpha = jnp.exp2(m_prev - m_new)
                p = jnp.concatenate([jnp.exp2(sh[:, c * LANES:(c + 1) * LANES] - m_new)
                                     for c in range(tk // LANES)], axis=1).astype(BF16)
                acc_scr[hh, r0:, :] = alpha * acc_scr[hh, r0:, :] + _dot(p, v_ref[hh, pl.ds(off, tk), :])
                m_scr[hh, r0:, :] = m_new

    m_scr[...] = jnp.full(m_scr.shape, NEG, F32)
    acc_scr[...] = jnp.zeros(acc_scr.shape, F32)

    def body(i, carry):
        group([(2 * i, None), (2 * i + 1, None)])
        return carry

    lax.fori_loop(0, t, body, 0)
    group([(2 * t, 0), (2 * t + 1, 1)])

    acc0, acc1 = acc_scr[0], acc_scr[1]
    out0 = acc0 / acc0[:, ROWSUM_LANE_EVEN:ROWSUM_LANE_EVEN + 1]
    out1 = acc1 / acc1[:, ROWSUM_LANE_ODD:ROWSUM_LANE_ODD + 1]
    lane = lax.broadcasted_iota(jnp.int32, (tq, LANES), 1)
    o_ref[...] = jnp.where(lane < V_HEAD, out0, out1).astype(o_ref.dtype)


def _flash_attention(q, k, v, *, tq):
    b, h, s, _ = q.shape
    tk = tq // 2
    assert tk % CHUNK == 0 and s % tq == 0
    pair = lambda n: pl.BlockSpec((None, 2, n, LANES), lambda bi, hp, ti: (bi, hp, 0, 0))
    return pl.pallas_call(
        functools.partial(_flash_kernel, tq=tq, tk=tk),
        out_shape=jax.ShapeDtypeStruct((b, s, MLA_OUT), BF16),
        grid=(b, h // 2, s // tq),
        in_specs=[
            pl.BlockSpec((None, 2, tq, HEAD_PAD), lambda bi, hp, ti: (bi, hp, ti, 0)),
            pair(s),
            pair(s),
        ],
        out_specs=pl.BlockSpec((None, tq, 2 * V_HEAD), lambda bi, hp, ti: (bi, ti, hp)),
        scratch_shapes=[pltpu.VMEM((2, tq, LANES), F32), pltpu.VMEM((2, tq, LANES), F32)],
        compiler_params=pltpu.CompilerParams(dimension_semantics=("arbitrary", "arbitrary", "arbitrary"),
                                             vmem_limit_bytes=VMEM_LIMIT),
        name="flash_attention",
    )(q, k, v)


def _latent_attn_kernel(q_ref, wabs_ref, cckv_ref, ckpe_ref, nckv_ref, nkpe_ref, wuv_ref, o_ref,
                        *, sq, past, pos0, n_seq):
    rows = MLA_HEADS * sq
    q_chunk = (pos0 + np.arange(rows)[:, None] % sq) // CHUNK
    vis_past = (pos0 - past + np.arange(past)[None, :]) // CHUNK <= q_chunk
    vis_new = (pos0 + np.arange(sq)[None, :]) // CHUNK <= q_chunk

    def masked(s, vis, n_keys, k_pos0):
        if vis.all():
            return s
        qc = (pos0 + lax.broadcasted_iota(jnp.int32, (rows, 1), 0) % sq) // CHUNK
        kc = (k_pos0 + lax.broadcasted_iota(jnp.int32, (1, n_keys), 1)) // CHUNK
        return jnp.where(kc <= qc, s, NEG)

    for i in range(n_seq):
        qcat = jnp.concatenate([_dot(q_ref[h, i * sq:(i + 1) * sq, :], wabs_ref[h]) for h in range(MLA_HEADS)],
                               axis=0)
        qa = qcat[:, :KV_LORA].astype(BF16)
        qp = qcat[:, KV_LORA:].astype(BF16)
        ck = cckv_ref[i].astype(BF16)
        kp_t = ckpe_ref[i].astype(BF16)
        nk = nckv_ref[i].astype(BF16)
        np_ = nkpe_ref[i].astype(BF16)
        s_past = masked(_dot_nt(qa, ck) + _dot(qp, kp_t), vis_past, past, pos0 - past)
        s_new = masked(_dot_nt(qa, nk) + _dot_nt(qp, np_), vis_new, sq, pos0)

        m = jnp.maximum(jnp.max(s_past, axis=-1, keepdims=True), jnp.max(s_new, axis=-1, keepdims=True))
        p_past = jnp.exp2(s_past - m)
        p_new = jnp.exp2(s_new - m)
        l = jnp.sum(p_past, axis=-1, keepdims=True) + jnp.sum(p_new, axis=-1, keepdims=True)
        o_lat = (_dot(p_past.astype(BF16), ck) + _dot(p_new.astype(BF16), nk)) / l
        o_lat = o_lat.astype(BF16)
        out = _dot(o_lat[0:sq], wuv_ref[0])
        for h in range(1, MLA_HEADS):
            out = out + _dot(o_lat[h * sq:(h + 1) * sq], wuv_ref[h])
        o_ref[i] = out.astype(o_ref.dtype)


def _prep_latent_weights(w_uk, w_uv):
    wabs = np.zeros((MLA_HEADS, HEAD_PAD, KV_LORA + QK_ROPE), np.float32)
    for i in range(QK_ROPE):
        wabs[:, QK_NOPE + i, KV_LORA + i] = 1.0
    wabs = jnp.asarray(wabs).at[:, :QK_NOPE, :KV_LORA].set(jnp.transpose(w_uk, (1, 2, 0)))
    wuv_place = jnp.zeros((MLA_HEADS, KV_LORA, MLA_OUT), F32)
    for h in range(MLA_HEADS):
        wuv_place = wuv_place.at[h, :, h * V_HEAD:(h + 1) * V_HEAD].set(w_uv[:, h, :])
    return wabs.astype(BF16), wuv_place.astype(BF16)


def _latent_attention(q, wabs, cache_ckv, cache_kpe, ckv_new, kpe_new, wuv_place, *, pos0):
    h = q.shape[0]
    b, sq, _ = ckv_new.shape
    past = cache_ckv.shape[1]
    n_seq = LATENT_SEQS_PER_STEP if b % LATENT_SEQS_PER_STEP == 0 else 1
    seq = lambda n, d: pl.BlockSpec((n_seq, n, d), lambda bi: (bi, 0, 0))
    return pl.pallas_call(
        functools.partial(_latent_attn_kernel, sq=sq, past=past, pos0=pos0, n_seq=n_seq),
        out_shape=jax.ShapeDtypeStruct((b, sq, MLA_OUT), BF16),
        grid=(b // n_seq,),
        in_specs=[
            pl.BlockSpec((h, n_seq * sq, HEAD_PAD), lambda bi: (0, bi, 0)),
            _const_spec((h, HEAD_PAD, KV_LORA + QK_ROPE)),
            seq(past, KV_LORA), seq(QK_ROPE, past), seq(sq, KV_LORA), seq(sq, QK_ROPE),
            _const_spec((h, KV_LORA, MLA_OUT)),
        ],
        out_specs=seq(sq, MLA_OUT),
        compiler_params=pltpu.CompilerParams(dimension_semantics=("arbitrary",),
                                             vmem_limit_bytes=VMEM_LIMIT),
        name="latent_attention",
    )(q, wabs, cache_ckv, cache_kpe, ckv_new, kpe_new, wuv_place)


def _out_ffn_kernel(*refs, n_act, final_norm):
    x_ref = refs[0]
    act_refs = refs[1:1 + n_act]
    wout_refs = refs[1 + n_act:1 + 2 * n_act]
    gffn_ref, wg_ref, wu_ref, wd_ref = refs[1 + 2 * n_act:5 + 2 * n_act]
    rest = refs[5 + 2 * n_act:]
    if final_norm:
        gfin_ref, o_ref = rest
    else:
        (o_ref,) = rest

    y = _dot(act_refs[0][...], wout_refs[0][...])
    for a_ref, w_ref in zip(act_refs[1:], wout_refs[1:]):
        y = y + _dot(a_ref[...], w_ref[...])
    x1 = x_ref[...] + y
    h2 = _rms(x1, gffn_ref[...]).astype(BF16)
    gate = _dot(h2, wg_ref[...])
    up = _dot(h2, wu_ref[...])
    act = (gate * jax.nn.sigmoid(gate) * up).astype(BF16)
    x2 = x1 + _dot(act, wd_ref[...])
    if final_norm:
        x2 = _rms(x2, gfin_ref[...])
    o_ref[...] = x2


def _out_ffn(x, acts, wouts, g_ffn, ffn_w, layer, g_final, *, tm):
    t, _ = x.shape
    n_act = len(acts)
    row = lambda d: pl.BlockSpec((tm, d), lambda i: (i, 0))
    slab = lambda w: pl.BlockSpec((None,) + w.shape[1:], lambda i: (layer, 0, 0), pipeline_mode=pl.Buffered(1))
    in_specs = ([row(D_MODEL)] + [row(a.shape[1]) for a in acts] + [_const_spec(w.shape) for w in wouts]
                + [_const_spec((1, D_MODEL))] + [slab(w) for w in ffn_w])
    args = [x, *acts, *wouts, g_ffn.reshape(1, -1), *ffn_w]
    if g_final is not None:
        in_specs.append(_const_spec((1, D_MODEL)))
        args.append(g_final.reshape(1, -1))
    return pl.pallas_call(
        functools.partial(_out_ffn_kernel, n_act=n_act, final_norm=g_final is not None),
        out_shape=jax.ShapeDtypeStruct((t, D_MODEL), F32),
        grid=(t // tm,),
        in_specs=in_specs,
        out_specs=row(D_MODEL),
        compiler_params=pltpu.CompilerParams(dimension_semantics=("arbitrary",),
                                             vmem_limit_bytes=VMEM_LIMIT),
        name="out_ffn",
    )(*args)


def _softplus(x):
    return jnp.maximum(x, 0.0) + jnp.log1p(jnp.exp(-jnp.abs(x)))


def _shift_rows(v, k):
    sub = lax.broadcasted_iota(jnp.int32, (1,) + v.shape[1:], 1)
    mixed = jnp.where(sub >= SUBLANES - k, v[:-1], v[1:])
    return pltpu.roll(mixed, k, axis=1)


def _ssd_in_kernel(x_ref, g_ref, win_ref, wdtt_ref, cw_ref, cb_ref, dtb_ref, dtbt_ref,
                   hist_ref, z_ref, xs_ref, b_ref, c_ref, dt_ref, dtt_ref, clast_ref, carry_scr,
                   *, tm, seq_rows):
    t = pl.program_id(1)
    hb = _rms(x_ref[...], g_ref[...]).astype(BF16)
    dt_ref[...] = _softplus(_dot(hb, win_ref[:, D_INNER + CONV_DIM:]) + dtb_ref[...])
    dtt_ref[...] = _softplus(_dot_nt(wdtt_ref[...].astype(BF16), hb) + dtbt_ref[...])

    lead = CONV_CARRY // SUBLANES
    if seq_rows is None:
        @pl.when(t == 0)
        def _():
            carry_scr[...] = hist_ref[...]
    else:
        n_seq, per_seq = tm // seq_rows, (CONV_CARRY + seq_rows) // SUBLANES

    gn = SSM_GROUPS * D_STATE
    outs = ((xs_ref, 0, D_INNER), (b_ref, D_INNER, gn), (c_ref, D_INNER + gn, gn))
    n_grp = tm // SUBLANES
    starts = list(range(0, CONV_DIM, CONV_CHUNK))
    xbc_w = lambda c0: win_ref[:, D_INNER + c0:D_INNER + c0 + CONV_CHUNK]
    xc_next = _dot(hb, xbc_w(0))
    for i, c0 in enumerate(starts):
        cols = slice(c0, c0 + CONV_CHUNK)
        xc = xc_next
        if i + 1 < len(starts):
            xc_next = _dot(hb, xbc_w(starts[i + 1]))
        if c0 < D_INNER:
            z_ref[:, cols] = _dot(hb, win_ref[:, cols]).astype(z_ref.dtype)
        if seq_rows is None:
            ext = jnp.concatenate([carry_scr[:, cols], xc], axis=0)
            ext = ext.reshape(n_grp + lead, SUBLANES, CONV_CHUNK)
            carry_scr[:, cols] = xc[tm - CONV_CARRY:, :]
            clast_ref[:, cols] = xc[tm - CONV_CARRY:, :]
        else:
            x_seq = xc.reshape(n_seq, seq_rows, CONV_CHUNK)
            ext = jnp.concatenate([hist_ref[:, :, cols], x_seq], axis=1)
            ext = ext.reshape(n_seq * per_seq, SUBLANES, CONV_CHUNK)
            clast_ref[:, :, cols] = x_seq[:, seq_rows - CONV_CARRY:, :]
        w = [cw_ref[k:k + 1, cols] for k in range(D_CONV)]
        ref, start = next((r, s0) for r, s0, width in outs if s0 <= c0 < s0 + width)
        x1 = _shift_rows(ext, 1)
        u = ext[2:] * w[3] + x1[1:] * w[2]
        v = ext[1:] * w[1] + x1 * w[0]
        half = u + _shift_rows(v, 2) + cb_ref[:, cols]
        if seq_rows is None:
            half = half.reshape(tm, CONV_CHUNK)
        else:
            pad = jnp.zeros((lead,) + half.shape[1:], half.dtype)
            half = jnp.concatenate([pad, half], axis=0).reshape(n_seq, per_seq, SUBLANES, CONV_CHUNK)
            half = half[:, lead:].reshape(tm, CONV_CHUNK)
        act = half + half * jnp.tanh(half)
        ref[:, c0 - start:c0 - start + CONV_CHUNK] = act.astype(ref.dtype)


def _ssd_in(x, conv_hist, g_mix, w_in, conv_w, conv_b, dt_bias, *, tm, seq_rows=None):
    b, s, _ = x.shape
    n_t = s // tm
    n_seq = conv_hist.shape[0]
    gn = SSM_GROUPS * D_STATE
    win_b = w_in.astype(BF16)
    wdt_t = w_in[:, D_INNER + CONV_DIM:].T
    hist = jnp.pad(conv_hist.astype(F32), ((0, 0), (CONV_CARRY - (D_CONV - 1), 0), (0, 0)))
    row = lambda d: pl.BlockSpec((None, tm, d), lambda bi, ti: (bi, ti, 0))
    if seq_rows is None:
        per_seq = pl.BlockSpec((None, CONV_CARRY, CONV_DIM), lambda bi, ti: (bi, 0, 0))
    else:
        assert b == 1 and n_t == 1 and tm == n_seq * seq_rows and seq_rows >= CONV_CARRY
        per_seq = pl.BlockSpec((n_seq, CONV_CARRY, CONV_DIM), lambda bi, ti: (0, 0, 0))
    return pl.pallas_call(
        functools.partial(_ssd_in_kernel, tm=tm, seq_rows=seq_rows),
        out_shape=[
            jax.ShapeDtypeStruct((b, s, D_INNER), BF16),
            jax.ShapeDtypeStruct((b, s, D_INNER), BF16),
            jax.ShapeDtypeStruct((b, s, gn), BF16),
            jax.ShapeDtypeStruct((b, s, gn), BF16),
            jax.ShapeDtypeStruct((b, s, SSM_HEADS), F32),
            jax.ShapeDtypeStruct((b, SSM_HEADS, s), F32),
            jax.ShapeDtypeStruct((n_seq, CONV_CARRY, CONV_DIM), F32),
        ],
        grid=(b, n_t),
        in_specs=[
            row(D_MODEL),
            _const_spec((1, D_MODEL)),
            _const_spec((D_MODEL, IN_C)),
            _const_spec((SSM_HEADS, D_MODEL)),
            _const_spec((D_CONV, CONV_DIM)),
            _const_spec((1, CONV_DIM)),
            _const_spec((1, SSM_HEADS)),
            _const_spec((SSM_HEADS, 1)),
            per_seq,
        ],
        out_specs=[
            row(D_INNER), row(D_INNER), row(gn), row(gn), row(SSM_HEADS),
            pl.BlockSpec((None, SSM_HEADS, tm), lambda bi, ti: (bi, 0, ti)),
            per_seq,
        ],
        scratch_shapes=[pltpu.VMEM((CONV_CARRY, CONV_DIM), F32)],
        compiler_params=pltpu.CompilerParams(dimension_semantics=("arbitrary", "arbitrary"),
                                             vmem_limit_bytes=VMEM_LIMIT),
        name="ssd_in",
    )(x, g_mix.reshape(1, -1), win_b, wdt_t, 0.5 * conv_w, 0.5 * conv_b.reshape(1, -1),
      dt_bias.reshape(1, -1), dt_bias.reshape(-1, 1), hist)


def _ssd_scan_kernel(xs_ref, z_ref, b_ref, c_ref, dt_ref, dtt_ref, alog_ref, alogt_ref, dexp_ref, norm_ref,
                     expand_ref, h0_ref, y_ref, hfin_ref, h_scr, *, ln, n_sub, n_steps):
    c = pl.program_id(1)

    @pl.when(c == 0)
    def _():
        for g in range(SSM_GROUPS):
            h_scr[g] = h0_ref[g * GROUP_COLS:(g + 1) * GROUP_COLS, :].T

    row_i = lax.broadcasted_iota(jnp.int32, (ln, ln), 0)
    col_i = lax.broadcasted_iota(jnp.int32, (ln, ln), 1)
    causal = row_i >= col_i
    tril = jnp.where(causal, 1.0, 0.0).astype(BF16)
    triu = jnp.where(row_i <= col_i, 1.0, 0.0).astype(BF16)
    lane = lax.broadcasted_iota(jnp.int32, (ln, LANES), 1)

    for sc in range(n_sub):
        rows = slice(sc * ln, (sc + 1) * ln)
        dt = dt_ref[rows, :]
        dtt = dtt_ref[:, rows]
        acum = _dot_sel_l(tril, dt * -jnp.exp(alog_ref[...]))
        acum_t = _dot_sel_r(dtt * -jnp.exp(alogt_ref[...]), triu)
        acum2 = acum * LOG2E
        key2_t = (acum_t - jnp.log(dtt)) * LOG2E
        last = acum[ln - 1:ln, :]
        ea_hi, ea_mid, _ = _split3(jnp.exp(acum))
        wj_b = (jnp.exp(last - acum) * dt).astype(BF16)
        pair = 2 * SSM_HEADDIM

        for g in range(SSM_GROUPS):
            gcols = slice(g * GROUP_COLS, (g + 1) * GROUP_COLS)
            expand = expand_ref[:, gcols]
            ea_e = _dot(ea_hi, expand) + _dot(ea_mid, expand)
            wj_e = _dot(wj_b, expand).astype(BF16)
            bg = b_ref[rows, g * D_STATE:(g + 1) * D_STATE]
            cg = c_ref[rows, g * D_STATE:(g + 1) * D_STATE]
            cb = _dot_nt(cg, bg)
            y_state = _dot(cg, h_scr[g].astype(BF16))
            gated = []
            for p in range(SSM_HPG // 2):
                h0 = g * SSM_HPG + 2 * p
                cols = slice(h0 * SSM_HEADDIM, (h0 + 2) * SSM_HEADDIM)
                pcols = slice(p * pair, (p + 1) * pair)
                scores = []
                for h in (h0, h0 + 1):
                    seg = jnp.where(causal, acum2[:, h:h + 1] - key2_t[h:h + 1, :], NEG)
                    scores.append((cb * jnp.exp2(seg)).astype(BF16))
                xp = xs_ref[rows, cols]
                zero = jnp.zeros_like(xp)
                rhs = jnp.concatenate([jnp.where(lane < SSM_HEADDIM, xp, zero),
                                       jnp.where(lane >= SSM_HEADDIM, xp, zero)], axis=0)
                y = _dot(jnp.concatenate(scores, axis=1), rhs)
                y = y + y_state[:, pcols] * ea_e[:, pcols]
                y = y + xp.astype(F32) * dexp_ref[:, cols]
                zc = z_ref[rows, cols].astype(F32)
                gated.append(y * (zc * jax.nn.sigmoid(zc)))
            yg = jnp.concatenate(gated, axis=1)
            yg = yg * lax.rsqrt(jnp.mean(yg * yg, axis=-1, keepdims=True) + EPS)
            y_ref[rows, gcols] = (yg * norm_ref[:, gcols]).astype(y_ref.dtype)
            xw = xs_ref[rows, gcols] * wj_e
            h_scr[g] = h_scr[g] * ea_e[ln - 1:ln, :] + _dot_tn(bg, xw)

    @pl.when(c == n_steps - 1)
    def _():
        for g in range(SSM_GROUPS):
            hfin_ref[g * GROUP_COLS:(g + 1) * GROUP_COLS, :] = h_scr[g].T


def _ssd_scan(xs, z, bm, cm, dt, dtt, a_log, d_skip, ssm_norm, h0, *, ln, n_sub=1):
    b, s, _ = xs.shape
    step = ln * n_sub
    n_steps = s // step
    assert s % step == 0
    gn = SSM_GROUPS * D_STATE
    expand = jnp.asarray(np.kron(np.eye(SSM_HEADS, dtype=np.float32), np.ones((1, SSM_HEADDIM), np.float32)),
                         BF16)
    d_exp = jnp.repeat(d_skip.astype(F32), SSM_HEADDIM).reshape(1, D_INNER)
    row = lambda d: pl.BlockSpec((None, step, d), lambda bi, ci: (bi, ci, 0))
    state_spec = pl.BlockSpec((None, D_INNER, D_STATE), lambda bi, ci: (bi, 0, 0))
    y, h_fin = pl.pallas_call(
        functools.partial(_ssd_scan_kernel, ln=ln, n_sub=n_sub, n_steps=n_steps),
        out_shape=[jax.ShapeDtypeStruct((b, s, D_INNER), BF16),
                   jax.ShapeDtypeStruct((b, D_INNER, D_STATE), F32)],
        grid=(b, n_steps),
        in_specs=[
            row(D_INNER), row(D_INNER), row(gn), row(gn), row(SSM_HEADS),
            pl.BlockSpec((None, SSM_HEADS, step), lambda bi, ci: (bi, 0, ci)),
            _const_spec((1, SSM_HEADS)), _const_spec((SSM_HEADS, 1)),
            _const_spec((1, D_INNER)), _const_spec((1, D_INNER)),
            _const_spec((SSM_HEADS, D_INNER)),
            state_spec,
        ],
        out_specs=[row(D_INNER), state_spec],
        scratch_shapes=[pltpu.VMEM((SSM_GROUPS, D_STATE, GROUP_COLS), F32)],
        compiler_params=pltpu.CompilerParams(dimension_semantics=("arbitrary", "arbitrary"),
                                             vmem_limit_bytes=VMEM_LIMIT),
        name="ssd_scan",
    )(xs, z, bm, cm, dt, dtt, a_log.reshape(1, -1), a_log.reshape(-1, 1), d_exp, ssm_norm.reshape(1, -1),
      expand, h0.astype(F32).reshape(b, D_INNER, D_STATE))
    return y, h_fin.reshape(b, SSM_HEADS, SSM_HEADDIM, D_STATE)


def _trunk(x, pos0, cache, pool_hist, conv_hist, ssm_hist, w, *, tm_proj, tm_ffn, tq, ln, n_sub=1):
    b, s, _ = x.shape
    prompt = cache is None
    stacked = dict(tm=b * s, seq_rows=s) if not prompt else dict(tm=tm_proj)
    as_rows = (lambda a: a.reshape(1, b * s, a.shape[-1])) if not prompt else (lambda a: a)
    win_p, wq, wk, wv, place = _prep_ab_weights(w["w_in_ab"][0], w["w_uq"][0], w["w_uk"][0], w["w_uv"][0])
    outs = _ab_in(as_rows(x), pos0, pool_hist, w["norm_mix"][0], win_p, w["q_norm"][0], wq, w["kv_norm"][0],
                  w["w_pool"][0], w["pool_scale"][0], wk, wv, place, emit_kv=prompt, **stacked)
    q, ckv, kpe, pool_out, pool_last = outs[:5]
    if prompt:
        k, v = outs[5:]
        attn = _flash_attention(q, k, v, tq=tq)
        kpe = jnp.swapaxes(kpe, 1, 2)
    else:
        ckv, kpe = ckv.reshape(b, s, KV_LORA), kpe.reshape(b, s, QK_ROPE)
        wabs, wuv_place = _prep_latent_weights(w["w_uk"][0], w["w_uv"][0])
        attn = _latent_attention(q[0], wabs, cache[0], jnp.swapaxes(cache[1], 1, 2), ckv, kpe, wuv_place,
                                 pos0=pos0)
    ffn_w = (w["w_gate"].astype(BF16), w["w_up"].astype(BF16), w["w_down"].astype(BF16))
    w_out = w["w_out_ab"][0].astype(BF16)
    x1 = _out_ffn(x.reshape(b * s, D_MODEL),
                  [pool_out.reshape(b * s, POOL_DIM), attn.reshape(b * s, MLA_OUT)],
                  [w_out[:POOL_DIM], w_out[POOL_DIM:]],
                  w["norm_ffn"][0], ffn_w, 0, None, tm=tm_ffn)

    z, xs, bm, cm, dt, dtt, conv_last = _ssd_in(as_rows(x1.reshape(b, s, D_MODEL)), conv_hist, w["norm_mix"][1],
                                                w["w_in_c"][0], w["conv_w"][0], w["conv_b"][0],
                                                w["dt_bias"][0], **stacked)
    if not prompt:
        z, xs, bm, cm, dt = (a.reshape(b, s, a.shape[-1]) for a in (z, xs, bm, cm, dt))
        dtt = jnp.transpose(dtt.reshape(SSM_HEADS, b, s), (1, 0, 2))
    y, h_fin = _ssd_scan(xs, z, bm, cm, dt, dtt, w["a_log"][0], w["d_skip"][0], w["ssm_norm"][0], ssm_hist,
                         ln=ln, n_sub=n_sub)
    x2 = _out_ffn(x1, [y.reshape(b * s, D_INNER)], [w["w_out_c"][0].astype(BF16)],
                  w["norm_ffn"][1], ffn_w, 1, w["norm_final"], tm=tm_ffn)
    return (x2.reshape(b, s, D_MODEL), ckv[None], kpe[None], pool_last[None, :, POOL_KEEP - POOL_HIST:],
            conv_last[None, :, CONV_CARRY - (D_CONV - 1):], h_fin[None])


def kernel(x_prompt, x_sample, cache_ckv, cache_kpe, state_pool, state_conv, state_ssm, norm_mix, norm_ffn, norm_final, w_in_ab, w_pool, pool_scale, q_norm, w_uq, kv_norm, w_uk, w_uv, w_out_ab, w_in_c, conv_w, conv_b, dt_bias, a_log, d_skip, ssm_norm, w_out_c, w_gate, w_up, w_down):
    assert norm_mix.shape[0] == 2 and w_in_ab.shape[0] == 1 and w_in_c.shape[0] == 1
    w = dict(norm_mix=norm_mix, norm_ffn=norm_ffn, norm_final=norm_final, w_in_ab=w_in_ab, w_pool=w_pool,
             pool_scale=pool_scale, q_norm=q_norm, w_uq=w_uq, kv_norm=kv_norm, w_uk=w_uk, w_uv=w_uv,
             w_out_ab=w_out_ab, w_in_c=w_in_c, conv_w=conv_w, conv_b=conv_b, dt_bias=dt_bias, a_log=a_log,
             d_skip=d_skip, ssm_norm=ssm_norm, w_out_c=w_out_c, w_gate=w_gate, w_up=w_up, w_down=w_down)
    bp = x_prompt.shape[0]
    bs, ss, _ = x_sample.shape
    past = cache_ckv.shape[2]

    prompt = _trunk(x_prompt, 0, None,
                    jnp.zeros((bp, POOL_HIST, POOL_DIM), F32), jnp.zeros((bp, D_CONV - 1, CONV_DIM), F32),
                    jnp.zeros((bp, SSM_HEADS, SSM_HEADDIM, D_STATE), F32), w,
                    tm_proj=PROMPT_PROJ_ROWS, tm_ffn=PROMPT_FFN_ROWS, tq=PROMPT_Q_ROWS,
                    ln=SSD_CHUNK, n_sub=SSD_CHUNKS_PER_STEP)
    sample = _trunk(x_sample, past, (cache_ckv[0], cache_kpe[0]), state_pool[0], state_conv[0], state_ssm[0], w,
                    tm_proj=None, tm_ffn=bs * ss, tq=None, ln=ss)
    return (prompt[0], sample[0]) + prompt[1:] + sample[1:]
```

```python
import functools
import math

import numpy as np
import jax
import jax.numpy as jnp
from jax import lax
from jax.experimental import pallas as pl
from jax.experimental.pallas import tpu as pltpu

F32 = jnp.float32
BF16 = jnp.bfloat16

D_MODEL = 1024
CHUNK = 64
EPS = 1e-6
NEG = -1e30

POOL_WINDOWS = (2, 4, 8, 16)
POOL_GROUPS = 4
POOL_DIM = D_MODEL // 2
POOL_GDIM = POOL_DIM // POOL_GROUPS
POOL_HIST = max(POOL_WINDOWS) - 1

MLA_HEADS = 8
QK_NOPE = 64
QK_ROPE = 32
ROPE_HALF = QK_ROPE // 2
V_HEAD = 64
Q_LORA = 384
KV_LORA = 256
MLA_OUT = MLA_HEADS * V_HEAD
ROPE_THETA = 10000.0
SM_SCALE = (QK_NOPE + QK_ROPE) ** -0.5
LOG2E = math.log2(math.e)
Q_SCALE = SM_SCALE * LOG2E
IN_AB = POOL_DIM + Q_LORA + KV_LORA + QK_ROPE

D_INNER = 2 * D_MODEL
SSM_HEADDIM = 64
SSM_HEADS = D_INNER // SSM_HEADDIM
SSM_GROUPS = 4
SSM_HPG = SSM_HEADS // SSM_GROUPS
D_STATE = 128
D_CONV = 4
CONV_DIM = D_INNER + 2 * SSM_GROUPS * D_STATE
IN_C = D_INNER + CONV_DIM + SSM_HEADS
GROUP_COLS = D_INNER // SSM_GROUPS

LANES = 128
SUBLANES = 8
HEAD_PAD = LANES
ROWSUM_LANE_EVEN = V_HEAD
ROWSUM_LANE_ODD = 0
POOL_CARRY = 4 * SUBLANES
POOL_KEEP = 2 * SUBLANES
LATENT_SEQS_PER_STEP = 2
CONV_CARRY = 2 * SUBLANES
CONV_CHUNK = 256
IN_AB_PAD = 10 * LANES
VMEM_LIMIT = 56 * 1024 * 1024

PROMPT_PROJ_ROWS = 1024
PROMPT_FFN_ROWS = 512
PROMPT_Q_ROWS = 1024
SSD_CHUNK = 128
SSD_CHUNKS_PER_STEP = 4

OFF_Q = POOL_DIM
OFF_KV = OFF_Q + Q_LORA
OFF_KPE = OFF_KV + KV_LORA


def _const_spec(shape):
    nd = len(shape)
    return pl.BlockSpec(shape, lambda *_: (0,) * nd, pipeline_mode=pl.Buffered(1))


def _rms(x, g):
    return x * lax.rsqrt(jnp.mean(x * x, axis=-1, keepdims=True) + EPS) * g


def _dot(a, b):
    return jnp.dot(a, b, preferred_element_type=F32)


def _dot_nt(a, b):
    return lax.dot_general(a, b, (((1,), (1,)), ((), ())), preferred_element_type=F32)


def _dot_tn(a, b):
    return lax.dot_general(a, b, (((0,), (0,)), ((), ())), preferred_element_type=F32)


def _split3(x):
    hi = x.astype(BF16)
    r1 = x - hi.astype(F32)
    mid = r1.astype(BF16)
    lo = (r1 - mid.astype(F32)).astype(BF16)
    return hi, mid, lo


def _dot_sel_r(x, sel):
    hi, mid, lo = _split3(x)
    return _dot(hi, sel) + _dot(mid, sel) + _dot(lo, sel)


def _dot_sel_l(sel, x):
    hi, mid, lo = _split3(x)
    return _dot(sel, hi) + _dot(sel, mid) + _dot(sel, lo)


def _ab_in_kernel(*refs, tm, pos0, emit_kv, seq_rows):
    (x_ref, g_ref, win_ref, qn_ref, wuq_ref, kvn_ref, wpool_ref, pscale_ref, tab_ref, hist_ref) = refs[:10]
    refs = refs[10:]
    if emit_kv:
        wuk_ref, wuv_ref, place_ref = refs[:3]
        refs = refs[3:]
    q_ref, ckv_ref, kpe_ref, pool_ref, plast_ref = refs[:5]
    refs = refs[5:]
    if emit_kv:
        k_ref, v_ref = refs[:2]
        refs = refs[2:]
    (cat_scr,) = refs

    t = pl.program_id(1)
    x = x_ref[...]
    hb = _rms(x, g_ref[...]).astype(BF16)
    proj = _dot(hb, win_ref[...])
    u_pool = proj[:, :POOL_DIM]
    q_lat = proj[:, OFF_Q:OFF_KV]
    kv_lat = proj[:, OFF_KV:OFF_KPE]
    kblk = proj[:, OFF_KPE:OFF_KPE + LANES]

    lead = POOL_CARRY // SUBLANES
    row_i = lax.broadcasted_iota(jnp.int32, (tm, 1), 0)
    if seq_rows is None:
        @pl.when(t == 0)
        def _():
            cat_scr[...] = hist_ref[...]

        ext2d = jnp.concatenate([cat_scr[...], u_pool], axis=0)
        ext = ext2d.reshape((POOL_CARRY + tm) // SUBLANES, SUBLANES, POOL_DIM)
        pos = pos0 + t * tm + row_i
        cat_scr[...] = ext2d[tm:, :]
        plast_ref[...] = ext2d[POOL_CARRY + tm - POOL_KEEP:, :]
    else:
        n_seq, per_seq = tm // seq_rows, (POOL_CARRY + seq_rows) // SUBLANES
        u_seq = u_pool.reshape(n_seq, seq_rows, POOL_DIM)
        ext = jnp.concatenate([hist_ref[...], u_seq], axis=1)
        ext = ext.reshape(n_seq * per_seq, SUBLANES, POOL_DIM)
        pos = pos0 + row_i % seq_rows
        plast_ref[...] = u_seq[:, seq_rows - POOL_KEEP:, :]

    def tile_rows(win, dropped):
        if seq_rows is None:
            return win[lead - dropped:].reshape(tm, POOL_GDIM)
        pad = jnp.zeros((dropped,) + win.shape[1:], win.dtype)
        full = jnp.concatenate([pad, win], axis=0).reshape(n_seq, per_seq, SUBLANES, POOL_GDIM)
        return full[:, lead:].reshape(tm, POOL_GDIM)

    for g, w in enumerate(POOL_WINDOWS):
        cols = slice(g * POOL_GDIM, (g + 1) * POOL_GDIM)
        win, span, dropped = ext[:, :, cols], 1, 0
        while span < w:
            prev = win[:-1] if span == SUBLANES else _shift_rows(win, span)
            win, span, dropped = win[1:] + prev, 2 * span, dropped + 1
        win = tile_rows(win, dropped)
        u_g = u_pool[:, cols]
        cnt = jnp.minimum(pos + 1, w).astype(F32)
        pooled = win / cnt - u_g
        mixed = _dot(pooled.astype(BF16), wpool_ref[g]) * pscale_ref[:, cols]
        pool_ref[:, cols] = mixed.astype(pool_ref.dtype)

    tab = tab_ref[...]
    up = pltpu.roll(tab, ROPE_HALF, 1)
    down = pltpu.roll(tab, LANES - ROPE_HALF, 1)
    ln_i = lax.broadcasted_iota(jnp.int32, (tm, LANES), 1)
    x1_q, x2_q = QK_NOPE, QK_NOPE + ROPE_HALF
    tq_c = jnp.where(ln_i < x1_q, 1.0, jnp.where(ln_i < x2_q, tab, jnp.where(ln_i < x2_q + ROPE_HALF, up, 0.0)))
    tq_a = jnp.where((ln_i >= x2_q) & (ln_i < x2_q + ROPE_HALF), tab, 0.0)
    tq_b = jnp.where((ln_i >= x1_q) & (ln_i < x2_q), -down, 0.0)
    tk_c = jnp.where(ln_i < ROPE_HALF, tab, jnp.where(ln_i < QK_ROPE, up, 0.0))
    tk_s = jnp.where(ln_i < ROPE_HALF, -down, jnp.where(ln_i < QK_ROPE, tab, 0.0))

    qn = _rms(q_lat, qn_ref[...]).astype(BF16)
    q = _dot(qn, wuq_ref[...])
    for h in range(MLA_HEADS):
        qh = q[:, h * HEAD_PAD:(h + 1) * HEAD_PAD]
        qr = (qh * tq_c + pltpu.roll(qh, ROPE_HALF, 1) * tq_a
              + pltpu.roll(qh, HEAD_PAD - ROPE_HALF, 1) * tq_b)
        q_ref[h] = qr.astype(q_ref.dtype)

    ckv = _rms(kv_lat, kvn_ref[...])
    ckv_ref[...] = ckv
    kpe_blk = kblk * tk_c + pltpu.roll(kblk, LANES - QK_ROPE, 1) * tk_s
    if emit_kv:
        kpe_ref[...] = kpe_blk.T[:QK_ROPE, :]
    else:
        kpe_ref[...] = kpe_blk[:, :QK_ROPE]

    if emit_kv:
        ckv_b = ckv.astype(BF16)
        k_all = _dot(ckv_b, wuk_ref[...]) + _dot(kpe_blk.astype(BF16), place_ref[...])
        for h in range(MLA_HEADS):
            k_ref[h] = k_all[:, h * HEAD_PAD:(h + 1) * HEAD_PAD].astype(k_ref.dtype)
        v_all = _dot(ckv_b, wuv_ref[...])
        lane = lax.broadcasted_iota(jnp.int32, (tm, LANES), 1)
        for hp in range(MLA_HEADS // 2):
            vp = v_all[:, hp * LANES:(hp + 1) * LANES]
            even = jnp.where(lane < V_HEAD, vp, jnp.where(lane == ROWSUM_LANE_EVEN, 1.0, 0.0))
            odd = jnp.where(lane >= V_HEAD, vp, jnp.where(lane == ROWSUM_LANE_ODD, 1.0, 0.0))
            v_ref[2 * hp] = even.astype(v_ref.dtype)
            v_ref[2 * hp + 1] = odd.astype(v_ref.dtype)


def _rope_table(pos0, s):
    pos = (pos0 + jnp.arange(s)).astype(F32)
    inv = ROPE_THETA ** (-jnp.arange(ROPE_HALF, dtype=F32) / ROPE_HALF)
    ang = pos[:, None] * inv[None, :]
    return jnp.tile(jnp.concatenate([jnp.cos(ang), jnp.sin(ang)], axis=1), (1, LANES // QK_ROPE))


def _prep_ab_weights(w_in, w_uq, w_uk, w_uv):
    kpe_w = w_in[:, OFF_KPE:]
    swapped = jnp.concatenate([kpe_w[:, ROPE_HALF:], kpe_w[:, :ROPE_HALF]], axis=1)
    pad = jnp.zeros((D_MODEL, IN_AB_PAD - IN_AB - QK_ROPE), w_in.dtype)
    win_p = jnp.concatenate([w_in, swapped, pad], axis=1).astype(BF16)

    tail = HEAD_PAD - QK_NOPE - QK_ROPE
    wq = w_uq.reshape(Q_LORA, MLA_HEADS, QK_NOPE + QK_ROPE)
    wq = jnp.pad(wq, ((0, 0), (0, 0), (0, tail))).reshape(Q_LORA, MLA_HEADS * HEAD_PAD).astype(BF16)
    wk = jnp.pad(w_uk, ((0, 0), (0, 0), (0, HEAD_PAD - QK_NOPE))).reshape(KV_LORA, MLA_HEADS * HEAD_PAD)
    wk = wk.astype(BF16)
    wv = w_uv.reshape(KV_LORA, MLA_OUT).astype(BF16)
    place = np.zeros((LANES, MLA_HEADS * HEAD_PAD), np.float32)
    for h in range(MLA_HEADS):
        for i in range(QK_ROPE):
            place[i, h * HEAD_PAD + QK_NOPE + i] = 1.0
    return win_p, wq, wk, wv, jnp.asarray(place, BF16)


def _ab_in(x, pos0, pool_hist, g_mix, win_p, q_norm, wq, kv_norm, w_pool, pool_scale, wk, wv, place,
           *, tm, emit_kv, seq_rows=None):
    b, s, _ = x.shape
    n_t = s // tm
    n_seq = pool_hist.shape[0]
    tab = _rope_table(pos0, s) if seq_rows is None else jnp.tile(_rope_table(pos0, seq_rows), (n_seq, 1))
    hist = jnp.pad(pool_hist.astype(F32), ((0, 0), (POOL_CARRY - POOL_HIST, 0), (0, 0)))
    row = lambda shape: pl.BlockSpec((None,) + shape, lambda bi, ti: (bi, ti, 0))
    if seq_rows is None:
        per_seq = lambda rows: pl.BlockSpec((None, rows, POOL_DIM), lambda bi, ti: (bi, 0, 0))
    else:
        assert b == 1 and n_t == 1 and tm == n_seq * seq_rows and seq_rows >= POOL_KEEP
        per_seq = lambda rows: pl.BlockSpec((n_seq, rows, POOL_DIM), lambda bi, ti: (0, 0, 0))

    in_specs = [
        row((tm, D_MODEL)),
        _const_spec((1, D_MODEL)),
        _const_spec((D_MODEL, IN_AB_PAD)),
        _const_spec((1, Q_LORA)),
        _const_spec((Q_LORA, MLA_HEADS * HEAD_PAD)),
        _const_spec((1, KV_LORA)),
        _const_spec((POOL_GROUPS, POOL_GDIM, POOL_GDIM)),
        _const_spec((1, POOL_DIM)),
        pl.BlockSpec((tm, LANES), lambda bi, ti: (ti, 0)),
        per_seq(POOL_CARRY),
    ]
    args = [x, g_mix.reshape(1, -1), win_p, (q_norm * Q_SCALE).reshape(1, -1), wq, kv_norm.reshape(1, -1),
            w_pool.astype(BF16), pool_scale.reshape(1, -1), tab, hist]
    if emit_kv:
        in_specs += [_const_spec((KV_LORA, MLA_HEADS * HEAD_PAD)), _const_spec((KV_LORA, MLA_OUT)),
                     _const_spec((LANES, MLA_HEADS * HEAD_PAD))]
        args += [wk, wv, place]

    head_spec = pl.BlockSpec((None, MLA_HEADS, tm, HEAD_PAD), lambda bi, ti: (bi, 0, ti, 0))
    out_shape = [
        jax.ShapeDtypeStruct((b, MLA_HEADS, s, HEAD_PAD), BF16),
        jax.ShapeDtypeStruct((b, s, KV_LORA), F32),
        jax.ShapeDtypeStruct((b, QK_ROPE, s) if emit_kv else (b, s, QK_ROPE), F32),
        jax.ShapeDtypeStruct((b, s, POOL_DIM), BF16),
        jax.ShapeDtypeStruct((n_seq, POOL_KEEP, POOL_DIM), F32),
    ]
    out_specs = [
        head_spec,
        row((tm, KV_LORA)),
        pl.BlockSpec((None, QK_ROPE, tm), lambda bi, ti: (bi, 0, ti)) if emit_kv else row((tm, QK_ROPE)),
        row((tm, POOL_DIM)),
        per_seq(POOL_KEEP),
    ]
    if emit_kv:
        out_shape += [jax.ShapeDtypeStruct((b, MLA_HEADS, s, HEAD_PAD), BF16),
                      jax.ShapeDtypeStruct((b, MLA_HEADS, s, LANES), BF16)]
        out_specs += [head_spec, head_spec]

    return pl.pallas_call(
        functools.partial(_ab_in_kernel, tm=tm, pos0=pos0, emit_kv=emit_kv, seq_rows=seq_rows),
        out_shape=out_shape,
        grid=(b, n_t),
        in_specs=in_specs,
        out_specs=out_specs,
        scratch_shapes=[pltpu.VMEM((POOL_CARRY, POOL_DIM), F32)],
        compiler_params=pltpu.CompilerParams(dimension_semantics=("arbitrary", "arbitrary"),
                                             vmem_limit_bytes=VMEM_LIMIT),
        name="ab_in",
    )(*args)


def _flash_kernel(q_ref, k_ref, v_ref, o_ref, m_scr, acc_scr, *, tq, tk):
    t = pl.program_id(2)
    heads = (0, 1)

    def group(blocks):
        s = {}
        for n, (blk, d) in enumerate(blocks):
            off = pl.multiple_of(blk * tk, tk)
            r0 = 0 if d is None else d * tk
            for hh in heads:
                sh = _dot_nt(q_ref[hh, r0:, :], k_ref[hh, pl.ds(off, tk), :])
                if d is not None:
                    qc = (r0 + lax.broadcasted_iota(jnp.int32, (tq - r0, tk), 0)) // CHUNK
                    kc = (d * tk + lax.broadcasted_iota(jnp.int32, (tq - r0, tk), 1)) // CHUNK
                    sh = jnp.where(kc <= qc, sh, NEG)
                s[n, hh] = sh
        for n, (blk, d) in enumerate(blocks):
            off = pl.multiple_of(blk * tk, tk)
            r0 = 0 if d is None else d * tk
            for hh in heads:
                sh = s[n, hh]
                m_prev = m_scr[hh, r0:, :]
                m_new = jnp.maximum(m_prev, jnp.max(sh, axis=-1, keepdims=True))
                alpha = jnp.exp2(m_prev - m_new)
                p = jnp.concatenate([jnp.exp2(sh[:, c * LANES:(c + 1) * LANES] - m_new)
                                     for c in range(tk // LANES)], axis=1).astype(BF16)
                acc_scr[hh, r0:, :] = alpha * acc_scr[hh, r0:, :] + _dot(p, v_ref[hh, pl.ds(off, tk), :])
                m_scr[hh, r0:, :] = m_new

    m_scr[...] = jnp.full(m_scr.shape, NEG, F32)
    acc_scr[...] = jnp.zeros(acc_scr.shape, F32)

    def body(i, carry):
        group([(2 * i, None), (2 * i + 1, None)])
        return carry

    lax.fori_loop(0, t, body, 0)
    group([(2 * t, 0), (2 * t + 1, 1)])

    acc0, acc1 = acc_scr[0], acc_scr[1]
    out0 = acc0 / acc0[:, ROWSUM_LANE_EVEN:ROWSUM_LANE_EVEN + 1]
    out1 = acc1 / acc1[:, ROWSUM_LANE_ODD:ROWSUM_LANE_ODD + 1]
    lane = lax.broadcasted_iota(jnp.int32, (tq, LANES), 1)
    o_ref[...] = jnp.where(lane < V_HEAD, out0, out1).astype(o_ref.dtype)


def _flash_attention(q, k, v, *, tq):
    b, h, s, _ = q.shape
    tk = tq // 2
    assert tk % CHUNK == 0 and s % tq == 0
    pair = lambda n: pl.BlockSpec((None, 2, n, LANES), lambda bi, hp, ti: (bi, hp, 0, 0))
    return pl.pallas_call(
        functools.partial(_flash_kernel, tq=tq, tk=tk),
        out_shape=jax.ShapeDtypeStruct((b, s, MLA_OUT), BF16),
        grid=(b, h // 2, s // tq),
        in_specs=[
            pl.BlockSpec((None, 2, tq, HEAD_PAD), lambda bi, hp, ti: (bi, hp, ti, 0)),
            pair(s),
            pair(s),
        ],
        out_specs=pl.BlockSpec((None, tq, 2 * V_HEAD), lambda bi, hp, ti: (bi, ti, hp)),
        scratch_shapes=[pltpu.VMEM((2, tq, LANES), F32), pltpu.VMEM((2, tq, LANES), F32)],
        compiler_params=pltpu.CompilerParams(dimension_semantics=("arbitrary", "arbitrary", "arbitrary"),
                                             vmem_limit_bytes=VMEM_LIMIT),
        name="flash_attention",
    )(q, k, v)


def _latent_attn_kernel(q_ref, wabs_ref, cckv_ref, ckpe_ref, nckv_ref, nkpe_ref, wuv_ref, o_ref,
                        *, sq, past, pos0, n_seq):
    rows = MLA_HEADS * sq
    q_chunk = (pos0 + np.arange(rows)[:, None] % sq) // CHUNK
    vis_past = (pos0 - past + np.arange(past)[None, :]) // CHUNK <= q_chunk
    vis_new = (pos0 + np.arange(sq)[None, :]) // CHUNK <= q_chunk

    def masked(s, vis, n_keys, k_pos0):
        if vis.all():
            return s
        qc = (pos0 + lax.broadcasted_iota(jnp.int32, (rows, 1), 0) % sq) // CHUNK
        kc = (k_pos0 + lax.broadcasted_iota(jnp.int32, (1, n_keys), 1)) // CHUNK
        return jnp.where(kc <= qc, s, NEG)

    for i in range(n_seq):
        qcat = jnp.concatenate([_dot(q_ref[h, i * sq:(i + 1) * sq, :], wabs_ref[h]) for h in range(MLA_HEADS)],
                               axis=0)
        qa = qcat[:, :KV_LORA].astype(BF16)
        qp = qcat[:, KV_LORA:].astype(BF16)
        ck = cckv_ref[i].astype(BF16)
        kp_t = ckpe_ref[i].astype(BF16)
        nk = nckv_ref[i].astype(BF16)
        np_ = nkpe_ref[i].astype(BF16)
        s_past = masked(_dot_nt(qa, ck) + _dot(qp, kp_t), vis_past, past, pos0 - past)
        s_new = masked(_dot_nt(qa, nk) + _dot_nt(qp, np_), vis_new, sq, pos0)

        m = jnp.maximum(jnp.max(s_past, axis=-1, keepdims=True), jnp.max(s_new, axis=-1, keepdims=True))
        p_past = jnp.exp2(s_past - m)
        p_new = jnp.exp2(s_new - m)
        l = jnp.sum(p_past, axis=-1, keepdims=True) + jnp.sum(p_new, axis=-1, keepdims=True)
        o_lat = (_dot(p_past.astype(BF16), ck) + _dot(p_new.astype(BF16), nk)) / l
        o_lat = o_lat.astype(BF16)
        out = _dot(o_lat[0:sq], wuv_ref[0])
        for h in range(1, MLA_HEADS):
            out = out + _dot(o_lat[h * sq:(h + 1) * sq], wuv_ref[h])
        o_ref[i] = out.astype(o_ref.dtype)


def _prep_latent_weights(w_uk, w_uv):
    wabs = np.zeros((MLA_HEADS, HEAD_PAD, KV_LORA + QK_ROPE), np.float32)
    for i in range(QK_ROPE):
        wabs[:, QK_NOPE + i, KV_LORA + i] = 1.0
    wabs = jnp.asarray(wabs).at[:, :QK_NOPE, :KV_LORA].set(jnp.transpose(w_uk, (1, 2, 0)))
    wuv_place = jnp.zeros((MLA_HEADS, KV_LORA, MLA_OUT), F32)
    for h in range(MLA_HEADS):
        wuv_place = wuv_place.at[h, :, h * V_HEAD:(h + 1) * V_HEAD].set(w_uv[:, h, :])
    return wabs.astype(BF16), wuv_place.astype(BF16)


def _latent_attention(q, wabs, cache_ckv, cache_kpe, ckv_new, kpe_new, wuv_place, *, pos0):
    h = q.shape[0]
    b, sq, _ = ckv_new.shape
    past = cache_ckv.shape[1]
    n_seq = LATENT_SEQS_PER_STEP if b % LATENT_SEQS_PER_STEP == 0 else 1
    seq = lambda n, d: pl.BlockSpec((n_seq, n, d), lambda bi: (bi, 0, 0))
    return pl.pallas_call(
        functools.partial(_latent_attn_kernel, sq=sq, past=past, pos0=pos0, n_seq=n_seq),
        out_shape=jax.ShapeDtypeStruct((b, sq, MLA_OUT), BF16),
        grid=(b // n_seq,),
        in_specs=[
            pl.BlockSpec((h, n_seq * sq, HEAD_PAD), lambda bi: (0, bi, 0)),
            _const_spec((h, HEAD_PAD, KV_LORA + QK_ROPE)),
            seq(past, KV_LORA), seq(QK_ROPE, past), seq(sq, KV_LORA), seq(sq, QK_ROPE),
            _const_spec((h, KV_LORA, MLA_OUT)),
        ],
        out_specs=seq(sq, MLA_OUT),
        compiler_params=pltpu.CompilerParams(dimension_semantics=("arbitrary",),
                                             vmem_limit_bytes=VMEM_LIMIT),
        name="latent_attention",
    )(q, wabs, cache_ckv, cache_kpe, ckv_new, kpe_new, wuv_place)


def _out_ffn_kernel(*refs, n_act, final_norm):
    x_ref = refs[0]
    act_refs = refs[1:1 + n_act]
    wout_refs = refs[1 + n_act:1 + 2 * n_act]
    gffn_ref, wg_ref, wu_ref, wd_ref = refs[1 + 2 * n_act:5 + 2 * n_act]
    rest = refs[5 + 2 * n_act:]
    if final_norm:
        gfin_ref, o_ref = rest
    else:
        (o_ref,) = rest

    y = _dot(act_refs[0][...], wout_refs[0][...])
    for a_ref, w_ref in zip(act_refs[1:], wout_refs[1:]):
        y = y + _dot(a_ref[...], w_ref[...])
    x1 = x_ref[...] + y
    h2 = _rms(x1, gffn_ref[...]).astype(BF16)
    gate = _dot(h2, wg_ref[...])
    up = _dot(h2, wu_ref[...])
    act = (gate * jax.nn.sigmoid(gate) * up).astype(BF16)
    x2 = x1 + _dot(act, wd_ref[...])
    if final_norm:
        x2 = _rms(x2, gfin_ref[...])
    o_ref[...] = x2


def _out_ffn(x, acts, wouts, g_ffn, ffn_w, layer, g_final, *, tm):
    t, _ = x.shape
    n_act = len(acts)
    row = lambda d: pl.BlockSpec((tm, d), lambda i: (i, 0))
    slab = lambda w: pl.BlockSpec((None,) + w.shape[1:], lambda i: (layer, 0, 0), pipeline_mode=pl.Buffered(1))
    in_specs = ([row(D_MODEL)] + [row(a.shape[1]) for a in acts] + [_const_spec(w.shape) for w in wouts]
                + [_const_spec((1, D_MODEL))] + [slab(w) for w in ffn_w])
    args = [x, *acts, *wouts, g_ffn.reshape(1, -1), *ffn_w]
    if g_final is not None:
        in_specs.append(_const_spec((1, D_MODEL)))
        args.append(g_final.reshape(1, -1))
    return pl.pallas_call(
        functools.partial(_out_ffn_kernel, n_act=n_act, final_norm=g_final is not None),
        out_shape=jax.ShapeDtypeStruct((t, D_MODEL), F32),
        grid=(t // tm,),
        in_specs=in_specs,
        out_specs=row(D_MODEL),
        compiler_params=pltpu.CompilerParams(dimension_semantics=("arbitrary",),
                                             vmem_limit_bytes=VMEM_LIMIT),
        name="out_ffn",
    )(*args)


def _softplus(x):
    return jnp.maximum(x, 0.0) + jnp.log1p(jnp.exp(-jnp.abs(x)))


def _shift_rows(v, k):
    sub = lax.broadcasted_iota(jnp.int32, (1,) + v.shape[1:], 1)
    mixed = jnp.where(sub >= SUBLANES - k, v[:-1], v[1:])
    return pltpu.roll(mixed, k, axis=1)


def _ssd_in_kernel(x_ref, g_ref, win_ref, wdtt_ref, cw_ref, cb_ref, dtb_ref, dtbt_ref,
                   hist_ref, z_ref, xs_ref, b_ref, c_ref, dt_ref, dtt_ref, clast_ref, carry_scr,
                   *, tm, seq_rows):
    t = pl.program_id(1)
    hb = _rms(x_ref[...], g_ref[...]).astype(BF16)
    dt_ref[...] = _softplus(_dot(hb, win_ref[:, D_INNER + CONV_DIM:]) + dtb_ref[...])
    dtt_ref[...] = _softplus(_dot_nt(wdtt_ref[...].astype(BF16), hb) + dtbt_ref[...])

    lead = CONV_CARRY // SUBLANES
    if seq_rows is None:
        @pl.when(t == 0)
        def _():
            carry_scr[...] = hist_ref[...]
    else:
        n_seq, per_seq = tm // seq_rows, (CONV_CARRY + seq_rows) // SUBLANES

    gn = SSM_GROUPS * D_STATE
    outs = ((xs_ref, 0, D_INNER), (b_ref, D_INNER, gn), (c_ref, D_INNER + gn, gn))
    n_grp = tm // SUBLANES
    starts = list(range(0, CONV_DIM, CONV_CHUNK))
    xbc_w = lambda c0: win_ref[:, D_INNER + c0:D_INNER + c0 + CONV_CHUNK]
    xc_next = _dot(hb, xbc_w(0))
    for i, c0 in enumerate(starts):
        cols = slice(c0, c0 + CONV_CHUNK)
        xc = xc_next
        if i + 1 < len(starts):
            xc_next = _dot(hb, xbc_w(starts[i + 1]))
        if c0 < D_INNER:
            z_ref[:, cols] = _dot(hb, win_ref[:, cols]).astype(z_ref.dtype)
        if seq_rows is None:
            ext = jnp.concatenate([carry_scr[:, cols], xc], axis=0)
            ext = ext.reshape(n_grp + lead, SUBLANES, CONV_CHUNK)
            carry_scr[:, cols] = xc[tm - CONV_CARRY:, :]
            clast_ref[:, cols] = xc[tm - CONV_CARRY:, :]
        else:
            x_seq = xc.reshape(n_seq, seq_rows, CONV_CHUNK)
            ext = jnp.concatenate([hist_ref[:, :, cols], x_seq], axis=1)
            ext = ext.reshape(n_seq * per_seq, SUBLANES, CONV_CHUNK)
            clast_ref[:, :, cols] = x_seq[:, seq_rows - CONV_CARRY:, :]
        w = [cw_ref[k:k + 1, cols] for k in range(D_CONV)]
        ref, start = next((r, s0) for r, s0, width in outs if s0 <= c0 < s0 + width)
        x1 = _shift_rows(ext, 1)
        u = ext[2:] * w[3] + x1[1:] * w[2]
        v = ext[1:] * w[1] + x1 * w[0]
        half = u + _shift_rows(v, 2) + cb_ref[:, cols]
        if seq_rows is None:
            half = half.reshape(tm, CONV_CHUNK)
        else:
            pad = jnp.zeros((lead,) + half.shape[1:], half.dtype)
            half = jnp.concatenate([pad, half], axis=0).reshape(n_seq, per_seq, SUBLANES, CONV_CHUNK)
            half = half[:, lead:].reshape(tm, CONV_CHUNK)
        act = half + half * jnp.tanh(half)
        ref[:, c0 - start:c0 - start + CONV_CHUNK] = act.astype(ref.dtype)


def _ssd_in(x, conv_hist, g_mix, w_in, conv_w, conv_b, dt_bias, *, tm, seq_rows=None):
    b, s, _ = x.shape
    n_t = s // tm
    n_seq = conv_hist.shape[0]
    gn = SSM_GROUPS * D_STATE
    win_b = w_in.astype(BF16)
    wdt_t = w_in[:, D_INNER + CONV_DIM:].T
    hist = jnp.pad(conv_hist.astype(F32), ((0, 0), (CONV_CARRY - (D_CONV - 1), 0), (0, 0)))
    row = lambda d: pl.BlockSpec((None, tm, d), lambda bi, ti: (bi, ti, 0))
    if seq_rows is None:
        per_seq = pl.BlockSpec((None, CONV_CARRY, CONV_DIM), lambda bi, ti: (bi, 0, 0))
    else:
        assert b == 1 and n_t == 1 and tm == n_seq * seq_rows and seq_rows >= CONV_CARRY
        per_seq = pl.BlockSpec((n_seq, CONV_CARRY, CONV_DIM), lambda bi, ti: (0, 0, 0))
    return pl.pallas_call(
        functools.partial(_ssd_in_kernel, tm=tm, seq_rows=seq_rows),
        out_shape=[
            jax.ShapeDtypeStruct((b, s, D_INNER), BF16),
            jax.ShapeDtypeStruct((b, s, D_INNER), BF16),
            jax.ShapeDtypeStruct((b, s, gn), BF16),
            jax.ShapeDtypeStruct((b, s, gn), BF16),
            jax.ShapeDtypeStruct((b, s, SSM_HEADS), F32),
            jax.ShapeDtypeStruct((b, SSM_HEADS, s), F32),
            jax.ShapeDtypeStruct((n_seq, CONV_CARRY, CONV_DIM), F32),
        ],
        grid=(b, n_t),
        in_specs=[
            row(D_MODEL),
            _const_spec((1, D_MODEL)),
            _const_spec((D_MODEL, IN_C)),
            _const_spec((SSM_HEADS, D_MODEL)),
            _const_spec((D_CONV, CONV_DIM)),
            _const_spec((1, CONV_DIM)),
            _const_spec((1, SSM_HEADS)),
            _const_spec((SSM_HEADS, 1)),
            per_seq,
        ],
        out_specs=[
            row(D_INNER), row(D_INNER), row(gn), row(gn), row(SSM_HEADS),
            pl.BlockSpec((None, SSM_HEADS, tm), lambda bi, ti: (bi, 0, ti)),
            per_seq,
        ],
        scratch_shapes=[pltpu.VMEM((CONV_CARRY, CONV_DIM), F32)],
        compiler_params=pltpu.CompilerParams(dimension_semantics=("arbitrary", "arbitrary"),
                                             vmem_limit_bytes=VMEM_LIMIT),
        name="ssd_in",
    )(x, g_mix.reshape(1, -1), win_b, wdt_t, 0.5 * conv_w, 0.5 * conv_b.reshape(1, -1),
      dt_bias.reshape(1, -1), dt_bias.reshape(-1, 1), hist)


def _ssd_scan_kernel(xs_ref, z_ref, b_ref, c_ref, dt_ref, dtt_ref, alog_ref, alogt_ref, dexp_ref, norm_ref,
                     expand_ref, h0_ref, y_ref, hfin_ref, h_scr, *, ln, n_sub, n_steps):
    c = pl.program_id(1)

    @pl.when(c == 0)
    def _():
        for g in range(SSM_GROUPS):
            h_scr[g] = h0_ref[g * GROUP_COLS:(g + 1) * GROUP_COLS, :].T

    row_i = lax.broadcasted_iota(jnp.int32, (ln, ln), 0)
    col_i = lax.broadcasted_iota(jnp.int32, (ln, ln), 1)
    causal = row_i >= col_i
    tril = jnp.where(causal, 1.0, 0.0).astype(BF16)
    triu = jnp.where(row_i <= col_i, 1.0, 0.0).astype(BF16)
    lane = lax.broadcasted_iota(jnp.int32, (ln, LANES), 1)

    for sc in range(n_sub):
        rows = slice(sc * ln, (sc + 1) * ln)
        dt = dt_ref[rows, :]
        dtt = dtt_ref[:, rows]
        acum = _dot_sel_l(tril, dt * -jnp.exp(alog_ref[...]))
        acum_t = _dot_sel_r(dtt * -jnp.exp(alogt_ref[...]), triu)
        acum2 = acum * LOG2E
        key2_t = (acum_t - jnp.log(dtt)) * LOG2E
        last = acum[ln - 1:ln, :]
        ea_hi, ea_mid, _ = _split3(jnp.exp(acum))
        wj_b = (jnp.exp(last - acum) * dt).astype(BF16)
        pair = 2 * SSM_HEADDIM

        for g in range(SSM_GROUPS):
            gcols = slice(g * GROUP_COLS, (g + 1) * GROUP_COLS)
            expand = expand_ref[:, gcols]
            ea_e = _dot(ea_hi, expand) + _dot(ea_mid, expand)
            wj_e = _dot(wj_b, expand).astype(BF16)
            bg = b_ref[rows, g * D_STATE:(g + 1) * D_STATE]
            cg = c_ref[rows, g * D_STATE:(g + 1) * D_STATE]
            cb = _dot_nt(cg, bg)
            y_state = _dot(cg, h_scr[g].astype(BF16))
            gated = []
            for p in range(SSM_HPG // 2):
                h0 = g * SSM_HPG + 2 * p
                cols = slice(h0 * SSM_HEADDIM, (h0 + 2) * SSM_HEADDIM)
                pcols = slice(p * pair, (p + 1) * pair)
                scores = []
                for h in (h0, h0 + 1):
                    seg = jnp.where(causal, acum2[:, h:h + 1] - key2_t[h:h + 1, :], NEG)
                    scores.append((cb * jnp.exp2(seg)).astype(BF16))
                xp = xs_ref[rows, cols]
                zero = jnp.zeros_like(xp)
                rhs = jnp.concatenate([jnp.where(lane < SSM_HEADDIM, xp, zero),
                                       jnp.where(lane >= SSM_HEADDIM, xp, zero)], axis=0)
                y = _dot(jnp.concatenate(scores, axis=1), rhs)
                y = y + y_state[:, pcols] * ea_e[:, pcols]
                y = y + xp.astype(F32) * dexp_ref[:, cols]
                zc = z_ref[rows, cols].astype(F32)
                gated.append(y * (zc * jax.nn.sigmoid(zc)))
            yg = jnp.concatenate(gated, axis=1)
            yg = yg * lax.rsqrt(jnp.mean(yg * yg, axis=-1, keepdims=True) + EPS)
            y_ref[rows, gcols] = (yg * norm_ref[:, gcols]).astype(y_ref.dtype)
            xw = xs_ref[rows, gcols] * wj_e
            h_scr[g] = h_scr[g] * ea_e[ln - 1:ln, :] + _dot_tn(bg, xw)

    @pl.when(c == n_steps - 1)
    def _():
        for g in range(SSM_GROUPS):
            hfin_ref[g * GROUP_COLS:(g + 1) * GROUP_COLS, :] = h_scr[g].T


def _ssd_scan(xs, z, bm, cm, dt, dtt, a_log, d_skip, ssm_norm, h0, *, ln, n_sub=1):
    b, s, _ = xs.shape
    step = ln * n_sub
    n_steps = s // step
    assert s % step == 0
    gn = SSM_GROUPS * D_STATE
    expand = jnp.asarray(np.kron(np.eye(SSM_HEADS, dtype=np.float32), np.ones((1, SSM_HEADDIM), np.float32)),
                         BF16)
    d_exp = jnp.repeat(d_skip.astype(F32), SSM_HEADDIM).reshape(1, D_INNER)
    row = lambda d: pl.BlockSpec((None, step, d), lambda bi, ci: (bi, ci, 0))
    state_spec = pl.BlockSpec((None, D_INNER, D_STATE), lambda bi, ci: (bi, 0, 0))
    y, h_fin = pl.pallas_call(
        functools.partial(_ssd_scan_kernel, ln=ln, n_sub=n_sub, n_steps=n_steps),
        out_shape=[jax.ShapeDtypeStruct((b, s, D_INNER), BF16),
                   jax.ShapeDtypeStruct((b, D_INNER, D_STATE), F32)],
        grid=(b, n_steps),
        in_specs=[
            row(D_INNER), row(D_INNER), row(gn), row(gn), row(SSM_HEADS),
            pl.BlockSpec((None, SSM_HEADS, step), lambda bi, ci: (bi, 0, ci)),
            _const_spec((1, SSM_HEADS)), _const_spec((SSM_HEADS, 1)),
            _const_spec((1, D_INNER)), _const_spec((1, D_INNER)),
            _const_spec((SSM_HEADS, D_INNER)),
            state_spec,
        ],
        out_specs=[row(D_INNER), state_spec],
        scratch_shapes=[pltpu.VMEM((SSM_GROUPS, D_STATE, GROUP_COLS), F32)],
        compiler_params=pltpu.CompilerParams(dimension_semantics=("arbitrary", "arbitrary"),
                                             vmem_limit_bytes=VMEM_LIMIT),
        name="ssd_scan",
    )(xs, z, bm, cm, dt, dtt, a_log.reshape(1, -1), a_log.reshape(-1, 1), d_exp, ssm_norm.reshape(1, -1),
      expand, h0.astype(F32).reshape(b, D_INNER, D_STATE))
    return y, h_fin.reshape(b, SSM_HEADS, SSM_HEADDIM, D_STATE)


def _trunk(x, pos0, cache, pool_hist, conv_hist, ssm_hist, w, *, tm_proj, tm_ffn, tq, ln, n_sub=1):
    b, s, _ = x.shape
    prompt = cache is None
    stacked = dict(tm=b * s, seq_rows=s) if not prompt else dict(tm=tm_proj)
    as_rows = (lambda a: a.reshape(1, b * s, a.shape[-1])) if not prompt else (lambda a: a)
    win_p, wq, wk, wv, place = _prep_ab_weights(w["w_in_ab"][0], w["w_uq"][0], w["w_uk"][0], w["w_uv"][0])
    outs = _ab_in(as_rows(x), pos0, pool_hist, w["norm_mix"][0], win_p, w["q_norm"][0], wq, w["kv_norm"][0],
                  w["w_pool"][0], w["pool_scale"][0], wk, wv, place, emit_kv=prompt, **stacked)
    q, ckv, kpe, pool_out, pool_last = outs[:5]
    if prompt:
        k, v = outs[5:]
        attn = _flash_attention(q, k, v, tq=tq)
        kpe = jnp.swapaxes(kpe, 1, 2)
    else:
        ckv, kpe = ckv.reshape(b, s, KV_LORA), kpe.reshape(b, s, QK_ROPE)
        wabs, wuv_place = _prep_latent_weights(w["w_uk"][0], w["w_uv"][0])
        attn = _latent_attention(q[0], wabs, cache[0], jnp.swapaxes(cache[1], 1, 2), ckv, kpe, wuv_place,
                                 pos0=pos0)
    ffn_w = (w["w_gate"].astype(BF16), w["w_up"].astype(BF16), w["w_down"].astype(BF16))
    w_out = w["w_out_ab"][0].astype(BF16)
    x1 = _out_ffn(x.reshape(b * s, D_MODEL),
                  [pool_out.reshape(b * s, POOL_DIM), attn.reshape(b * s, MLA_OUT)],
                  [w_out[:POOL_DIM], w_out[POOL_DIM:]],
                  w["norm_ffn"][0], ffn_w, 0, None, tm=tm_ffn)

    z, xs, bm, cm, dt, dtt, conv_last = _ssd_in(as_rows(x1.reshape(b, s, D_MODEL)), conv_hist, w["norm_mix"][1],
                                                w["w_in_c"][0], w["conv_w"][0], w["conv_b"][0],
                                                w["dt_bias"][0], **stacked)
    if not prompt:
        z, xs, bm, cm, dt = (a.reshape(b, s, a.shape[-1]) for a in (z, xs, bm, cm, dt))
        dtt = jnp.transpose(dtt.reshape(SSM_HEADS, b, s), (1, 0, 2))
    y, h_fin = _ssd_scan(xs, z, bm, cm, dt, dtt, w["a_log"][0], w["d_skip"][0], w["ssm_norm"][0], ssm_hist,
                         ln=ln, n_sub=n_sub)
    x2 = _out_ffn(x1, [y.reshape(b * s, D_INNER)], [w["w_out_c"][0].astype(BF16)],
                  w["norm_ffn"][1], ffn_w, 1, w["norm_final"], tm=tm_ffn)
    return (x2.reshape(b, s, D_MODEL), ckv[None], kpe[None], pool_last[None, :, POOL_KEEP - POOL_HIST:],
            conv_last[None, :, CONV_CARRY - (D_CONV - 1):], h_fin[None])


def kernel(x_prompt, x_sample, cache_ckv, cache_kpe, state_pool, state_conv, state_ssm, norm_mix, norm_ffn, norm_final, w_in_ab, w_pool, pool_scale, q_norm, w_uq, kv_norm, w_uk, w_uv, w_out_ab, w_in_c, conv_w, conv_b, dt_bias, a_log, d_skip, ssm_norm, w_out_c, w_gate, w_up, w_down):
    assert norm_mix.shape[0] == 2 and w_in_ab.shape[0] == 1 and w_in_c.shape[0] == 1
    w = dict(norm_mix=norm_mix, norm_ffn=norm_ffn, norm_final=norm_final, w_in_ab=w_in_ab, w_pool=w_pool,
             pool_scale=pool_scale, q_norm=q_norm, w_uq=w_uq, kv_norm=kv_norm, w_uk=w_uk, w_uv=w_uv,
             w_out_ab=w_out_ab, w_in_c=w_in_c, conv_w=conv_w, conv_b=conv_b, dt_bias=dt_bias, a_log=a_log,
             d_skip=d_skip, ssm_norm=ssm_norm, w_out_c=w_out_c, w_gate=w_gate, w_up=w_up, w_down=w_down)
    bp = x_prompt.shape[0]
    bs, ss, _ = x_sample.shape
    past = cache_ckv.shape[2]

    prompt = _trunk(x_prompt, 0, None,
                    jnp.zeros((bp, POOL_HIST, POOL_DIM), F32), jnp.zeros((bp, D_CONV - 1, CONV_DIM), F32),
                    jnp.zeros((bp, SSM_HEADS, SSM_HEADDIM, D_STATE), F32), w,
                    tm_proj=PROMPT_PROJ_ROWS, tm_ffn=PROMPT_FFN_ROWS, tq=PROMPT_Q_ROWS,
                    ln=SSD_CHUNK, n_sub=SSD_CHUNKS_PER_STEP)
    sample = _trunk(x_sample, past, (cache_ckv[0], cache_kpe[0]), state_pool[0], state_conv[0], state_ssm[0], w,
                    tm_proj=None, tm_ffn=bs * ss, tq=None, ln=ss)
    return (prompt[0], sample[0]) + prompt[1:] + sample[1:]
```

```python
import functools
import math

import numpy as np
import jax
import jax.numpy as jnp
from jax import lax
from jax.experimental import pallas as pl
from jax.experimental.pallas import tpu as pltpu

F32 = jnp.float32
BF16 = jnp.bfloat16

D_MODEL = 1024
CHUNK = 64
EPS = 1e-6
NEG = -1e30

POOL_WINDOWS = (2, 4, 8, 16)
POOL_GROUPS = 4
POOL_DIM = D_MODEL // 2
POOL_GDIM = POOL_DIM // POOL_GROUPS
POOL_HIST = max(POOL_WINDOWS) - 1

MLA_HEADS = 8
QK_NOPE = 64
QK_ROPE = 32
ROPE_HALF = QK_ROPE // 2
V_HEAD = 64
Q_LORA = 384
KV_LORA = 256
MLA_OUT = MLA_HEADS * V_HEAD
ROPE_THETA = 10000.0
SM_SCALE = (QK_NOPE + QK_ROPE) ** -0.5
LOG2E = math.log2(math.e)
Q_SCALE = SM_SCALE * LOG2E
IN_AB = POOL_DIM + Q_LORA + KV_LORA + QK_ROPE

D_INNER = 2 * D_MODEL
SSM_HEADDIM = 64
SSM_HEADS = D_INNER // SSM_HEADDIM
SSM_GROUPS = 4
SSM_HPG = SSM_HEADS // SSM_GROUPS
D_STATE = 128
D_CONV = 4
CONV_DIM = D_INNER + 2 * SSM_GROUPS * D_STATE
IN_C = D_INNER + CONV_DIM + SSM_HEADS
GROUP_COLS = D_INNER // SSM_GROUPS

LANES = 128
SUBLANES = 8
HEAD_PAD = LANES
ROWSUM_LANE_EVEN = V_HEAD
ROWSUM_LANE_ODD = 0
POOL_CARRY = 4 * SUBLANES
POOL_KEEP = 2 * SUBLANES
LATENT_SEQS_PER_STEP = 2
CONV_CARRY = 2 * SUBLANES
CONV_CHUNK = 256
IN_AB_PAD = 10 * LANES
VMEM_LIMIT = 56 * 1024 * 1024

PROMPT_PROJ_ROWS = 1024
PROMPT_FFN_ROWS = 512
PROMPT_Q_ROWS = 1024
SSD_CHUNK = 128
SSD_CHUNKS_PER_STEP = 4

OFF_Q = POOL_DIM
OFF_KV = OFF_Q + Q_LORA
OFF_KPE = OFF_KV + KV_LORA


def _const_spec(shape):
    nd = len(shape)
    return pl.BlockSpec(shape, lambda *_: (0,) * nd, pipeline_mode=pl.Buffered(1))


def _rms(x, g):
    return x * lax.rsqrt(jnp.mean(x * x, axis=-1, keepdims=True) + EPS) * g


def _dot(a, b):
    return jnp.dot(a, b, preferred_element_type=F32)


def _dot_nt(a, b):
    return lax.dot_general(a, b, (((1,), (1,)), ((), ())), preferred_element_type=F32)


def _dot_tn(a, b):
    return lax.dot_general(a, b, (((0,), (0,)), ((), ())), preferred_element_type=F32)


def _split3(x):
    hi = x.astype(BF16)
    r1 = x - hi.astype(F32)
    mid = r1.astype(BF16)
    lo = (r1 - mid.astype(F32)).astype(BF16)
    return hi, mid, lo


def _dot_sel_r(x, sel):
    hi, mid, lo = _split3(x)
    return _dot(hi, sel) + _dot(mid, sel) + _dot(lo, sel)


def _dot_sel_l(sel, x):
    hi, mid, lo = _split3(x)
    return _dot(sel, hi) + _dot(sel, mid) + _dot(sel, lo)


def _ab_in_kernel(*refs, tm, pos0, emit_kv, seq_rows):
    (x_ref, g_ref, win_ref, qn_ref, wuq_ref, kvn_ref, wpool_ref, pscale_ref, tab_ref, hist_ref) = refs[:10]
    refs = refs[10:]
    if emit_kv:
        wuk_ref, wuv_ref, place_ref = refs[:3]
        refs = refs[3:]
    q_ref, ckv_ref, kpe_ref, pool_ref, plast_ref = refs[:5]
    refs = refs[5:]
    if emit_kv:
        k_ref, v_ref = refs[:2]
        refs = refs[2:]
    (cat_scr,) = refs

    t = pl.program_id(1)
    x = x_ref[...]
    hb = _rms(x, g_ref[...]).astype(BF16)
    proj = _dot(hb, win_ref[...])
    u_pool = proj[:, :POOL_DIM]
    q_lat = proj[:, OFF_Q:OFF_KV]
    kv_lat = proj[:, OFF_KV:OFF_KPE]
    kblk = proj[:, OFF_KPE:OFF_KPE + LANES]

    tab = tab_ref[...]
    up = pltpu.roll(tab, ROPE_HALF, 1)
    down = pltpu.roll(tab, LANES - ROPE_HALF, 1)
    ln_i = lax.broadcasted_iota(jnp.int32, (tm, LANES), 1)
    x1_q, x2_q = QK_NOPE, QK_NOPE + ROPE_HALF
    tq_c = jnp.where(ln_i < x1_q, 1.0, jnp.where(ln_i < x2_q, tab, jnp.where(ln_i < x2_q + ROPE_HALF, up, 0.0)))
    tq_a = jnp.where((ln_i >= x2_q) & (ln_i < x2_q + ROPE_HALF), tab, 0.0)
    tq_b = jnp.where((ln_i >= x1_q) & (ln_i < x2_q), -down, 0.0)
    tk_c = jnp.where(ln_i < ROPE_HALF, tab, jnp.where(ln_i < QK_ROPE, up, 0.0))
    tk_s = jnp.where(ln_i < ROPE_HALF, -down, jnp.where(ln_i < QK_ROPE, tab, 0.0))

    qn = _rms(q_lat, qn_ref[...]).astype(BF16)
    q = _dot(qn, wuq_ref[...])
    for h in range(MLA_HEADS):
        qh = q[:, h * HEAD_PAD:(h + 1) * HEAD_PAD]
        qr = (qh * tq_c + pltpu.roll(qh, ROPE_HALF, 1) * tq_a
              + pltpu.roll(qh, HEAD_PAD - ROPE_HALF, 1) * tq_b)
        q_ref[h] = qr.astype(q_ref.dtype)

    ckv = _rms(kv_lat, kvn_ref[...])
    ckv_ref[...] = ckv
    kpe_blk = kblk * tk_c + pltpu.roll(kblk, LANES - QK_ROPE, 1) * tk_s
    if emit_kv:
        kpe_ref[...] = kpe_blk.T[:QK_ROPE, :]
    else:
        kpe_ref[...] = kpe_blk[:, :QK_ROPE]

    if emit_kv:
        ckv_b = ckv.astype(BF16)
        k_all = _dot(ckv_b, wuk_ref[...]) + _dot(kpe_blk.astype(BF16), place_ref[...])
        for h in range(MLA_HEADS):
            k_ref[h] = k_all[:, h * HEAD_PAD:(h + 1) * HEAD_PAD].astype(k_ref.dtype)
        v_all = _dot(ckv_b, wuv_ref[...])
        lane = lax.broadcasted_iota(jnp.int32, (tm, LANES), 1)
        for hp in range(MLA_HEADS // 2):
            vp = v_all[:, hp * LANES:(hp + 1) * LANES]
            even = jnp.where(lane < V_HEAD, vp, jnp.where(lane == ROWSUM_LANE_EVEN, 1.0, 0.0))
            odd = jnp.where(lane >= V_HEAD, vp, jnp.where(lane == ROWSUM_LANE_ODD, 1.0, 0.0))
            v_ref[2 * hp] = even.astype(v_ref.dtype)
            v_ref[2 * hp + 1] = odd.astype(v_ref.dtype)

    lead = POOL_CARRY // SUBLANES
    row_i = lax.broadcasted_iota(jnp.int32, (tm, 1), 0)
    if seq_rows is None:
        @pl.when(t == 0)
        def _():
            cat_scr[...] = hist_ref[...]

        ext2d = jnp.concatenate([cat_scr[...], u_pool], axis=0)
        ext = ext2d.reshape((POOL_CARRY + tm) // SUBLANES, SUBLANES, POOL_DIM)
        pos = pos0 + t * tm + row_i
        cat_scr[...] = ext2d[tm:, :]
        plast_ref[...] = ext2d[POOL_CARRY + tm - POOL_KEEP:, :]
    else:
        n_seq, per_seq = tm // seq_rows, (POOL_CARRY + seq_rows) // SUBLANES
        u_seq = u_pool.reshape(n_seq, seq_rows, POOL_DIM)
        ext = jnp.concatenate([hist_ref[...], u_seq], axis=1)
        ext = ext.reshape(n_seq * per_seq, SUBLANES, POOL_DIM)
        pos = pos0 + row_i % seq_rows
        plast_ref[...] = u_seq[:, seq_rows - POOL_KEEP:, :]

    def tile_rows(win, dropped):
        if seq_rows is None:
            return win[lead - dropped:].reshape(tm, POOL_GDIM)
        pad = jnp.zeros((dropped,) + win.shape[1:], win.dtype)
        full = jnp.concatenate([pad, win], axis=0).reshape(n_seq, per_seq, SUBLANES, POOL_GDIM)
        return full[:, lead:].reshape(tm, POOL_GDIM)

    for g, w in enumerate(POOL_WINDOWS):
        cols = slice(g * POOL_GDIM, (g + 1) * POOL_GDIM)
        win, span, dropped = ext[:, :, cols], 1, 0
        while span < w:
            prev = win[:-1] if span == SUBLANES else _shift_rows(win, span)
            win, span, dropped = win[1:] + prev, 2 * span, dropped + 1
        win = tile_rows(win, dropped)
        u_g = u_pool[:, cols]
        cnt = jnp.minimum(pos + 1, w).astype(F32)
        pooled = win / cnt - u_g
        mixed = _dot(pooled.astype(BF16), wpool_ref[g]) * pscale_ref[:, cols]
        pool_ref[:, cols] = mixed.astype(pool_ref.dtype)


def _rope_table(pos0, s):
    pos = (pos0 + jnp.arange(s)).astype(F32)
    inv = ROPE_THETA ** (-jnp.arange(ROPE_HALF, dtype=F32) / ROPE_HALF)
    ang = pos[:, None] * inv[None, :]
    return jnp.tile(jnp.concatenate([jnp.cos(ang), jnp.sin(ang)], axis=1), (1, LANES // QK_ROPE))


def _prep_ab_weights(w_in, w_uq, w_uk, w_uv):
    kpe_w = w_in[:, OFF_KPE:]
    swapped = jnp.concatenate([kpe_w[:, ROPE_HALF:], kpe_w[:, :ROPE_HALF]], axis=1)
    pad = jnp.zeros((D_MODEL, IN_AB_PAD - IN_AB - QK_ROPE), w_in.dtype)
    win_p = jnp.concatenate([w_in, swapped, pad], axis=1).astype(BF16)

    tail = HEAD_PAD - QK_NOPE - QK_ROPE
    wq = w_uq.reshape(Q_LORA, MLA_HEADS, QK_NOPE + QK_ROPE)
    wq = jnp.pad(wq, ((0, 0), (0, 0), (0, tail))).reshape(Q_LORA, MLA_HEADS * HEAD_PAD).astype(BF16)
    wk = jnp.pad(w_uk, ((0, 0), (0, 0), (0, HEAD_PAD - QK_NOPE))).reshape(KV_LORA, MLA_HEADS * HEAD_PAD)
    wk = wk.astype(BF16)
    wv = w_uv.reshape(KV_LORA, MLA_OUT).astype(BF16)
    place = np.zeros((LANES, MLA_HEADS * HEAD_PAD), np.float32)
    for h in range(MLA_HEADS):
        for i in range(QK_ROPE):
            place[i, h * HEAD_PAD + QK_NOPE + i] = 1.0
    return win_p, wq, wk, wv, jnp.asarray(place, BF16)


def _ab_in(x, pos0, pool_hist, g_mix, win_p, q_norm, wq, kv_norm, w_pool, pool_scale, wk, wv, place,
           *, tm, emit_kv, seq_rows=None):
    b, s, _ = x.shape
    n_t = s // tm
    n_seq = pool_hist.shape[0]
    tab = _rope_table(pos0, s) if seq_rows is None else jnp.tile(_rope_table(pos0, seq_rows), (n_seq, 1))
    hist = jnp.pad(pool_hist.astype(F32), ((0, 0), (POOL_CARRY - POOL_HIST, 0), (0, 0)))
    row = lambda shape: pl.BlockSpec((None,) + shape, lambda bi, ti: (bi, ti, 0))
    if seq_rows is None:
        per_seq = lambda rows: pl.BlockSpec((None, rows, POOL_DIM), lambda bi, ti: (bi, 0, 0))
    else:
        assert b == 1 and n_t == 1 and tm == n_seq * seq_rows and seq_rows >= POOL_KEEP
        per_seq = lambda rows: pl.BlockSpec((n_seq, rows, POOL_DIM), lambda bi, ti: (0, 0, 0))

    in_specs = [
        row((tm, D_MODEL)),
        _const_spec((1, D_MODEL)),
        _const_spec((D_MODEL, IN_AB_PAD)),
        _const_spec((1, Q_LORA)),
        _const_spec((Q_LORA, MLA_HEADS * HEAD_PAD)),
        _const_spec((1, KV_LORA)),
        _const_spec((POOL_GROUPS, POOL_GDIM, POOL_GDIM)),
        _const_spec((1, POOL_DIM)),
        pl.BlockSpec((tm, LANES), lambda bi, ti: (ti, 0)),
        per_seq(POOL_CARRY),
    ]
    args = [x, g_mix.reshape(1, -1), win_p, (q_norm * Q_SCALE).reshape(1, -1), wq, kv_norm.reshape(1, -1),
            w_pool.astype(BF16), pool_scale.reshape(1, -1), tab, hist]
    if emit_kv:
        in_specs += [_const_spec((KV_LORA, MLA_HEADS * HEAD_PAD)), _const_spec((KV_LORA, MLA_OUT)),
                     _const_spec((LANES, MLA_HEADS * HEAD_PAD))]
        args += [wk, wv, place]

    head_spec = pl.BlockSpec((None, MLA_HEADS, tm, HEAD_PAD), lambda bi, ti: (bi, 0, ti, 0))
    out_shape = [
        jax.ShapeDtypeStruct((b, MLA_HEADS, s, HEAD_PAD), BF16),
        jax.ShapeDtypeStruct((b, s, KV_LORA), F32),
        jax.ShapeDtypeStruct((b, QK_ROPE, s) if emit_kv else (b, s, QK_ROPE), F32),
        jax.ShapeDtypeStruct((b, s, POOL_DIM), BF16),
        jax.ShapeDtypeStruct((n_seq, POOL_KEEP, POOL_DIM), F32),
    ]
    out_specs = [
        head_spec,
        row((tm, KV_LORA)),
        pl.BlockSpec((None, QK_ROPE, tm), lambda bi, ti: (bi, 0, ti)) if emit_kv else row((tm, QK_ROPE)),
        row((tm, POOL_DIM)),
        per_seq(POOL_KEEP),
    ]
    if emit_kv:
        out_shape += [jax.ShapeDtypeStruct((b, MLA_HEADS, s, HEAD_PAD), BF16),
                      jax.ShapeDtypeStruct((b, MLA_HEADS, s, LANES), BF16)]
        out_specs += [head_spec, head_spec]

    return pl.pallas_call(
        functools.partial(_ab_in_kernel, tm=tm, pos0=pos0, emit_kv=emit_kv, seq_rows=seq_rows),
        out_shape=out_shape,
        grid=(b, n_t),
        in_specs=in_specs,
        out_specs=out_specs,
        scratch_shapes=[pltpu.VMEM((POOL_CARRY, POOL_DIM), F32)],
        compiler_params=pltpu.CompilerParams(dimension_semantics=("arbitrary", "arbitrary"),
                                             vmem_limit_bytes=VMEM_LIMIT),
        name="ab_in",
    )(*args)


def _flash_kernel(q_ref, k_ref, v_ref, o_ref, m_scr, acc_scr, *, tq, tk):
    t = pl.program_id(2)
    heads = (0, 1)

    def group(blocks):
        s = {}
        for n, (blk, d) in enumerate(blocks):
            off = pl.multiple_of(blk * tk, tk)
            r0 = 0 if d is None else d * tk
            for hh in heads:
                sh = _dot_nt(q_ref[hh, r0:, :], k_ref[hh, pl.ds(off, tk), :])
                if d is not None:
                    qc = (r0 + lax.broadcasted_iota(jnp.int32, (tq - r0, tk), 0)) // CHUNK
                    kc = (d * tk + lax.broadcasted_iota(jnp.int32, (tq - r0, tk), 1)) // CHUNK
                    sh = jnp.where(kc <= qc, sh, NEG)
                s[n, hh] = sh
        for n, (blk, d) in enumerate(blocks):
            off = pl.multiple_of(blk * tk, tk)
            r0 = 0 if d is None else d * tk
            for hh in heads:
                sh = s[n, hh]
                m_prev = m_scr[hh, r0:, :]
                m_new = jnp.maximum(m_prev, jnp.max(sh, axis=-1, keepdims=True))
                alpha = jnp.exp2(m_prev - m_new)
                p = jnp.concatenate([jnp.exp2(sh[:, c * LANES:(c + 1) * LANES] - m_new)
                                     for c in range(tk // LANES)], axis=1).astype(BF16)
                acc_scr[hh, r0:, :] = alpha * acc_scr[hh, r0:, :] + _dot(p, v_ref[hh, pl.ds(off, tk), :])
                m_scr[hh, r0:, :] = m_new

    m_scr[...] = jnp.full(m_scr.shape, NEG, F32)
    acc_scr[...] = jnp.zeros(acc_scr.shape, F32)

    def body(i, carry):
        group([(2 * i, None), (2 * i + 1, None)])
        return carry

    lax.fori_loop(0, t, body, 0)
    group([(2 * t, 0), (2 * t + 1, 1)])

    acc0, acc1 = acc_scr[0], acc_scr[1]
    out0 = acc0 / acc0[:, ROWSUM_LANE_EVEN:ROWSUM_LANE_EVEN + 1]
    out1 = acc1 / acc1[:, ROWSUM_LANE_ODD:ROWSUM_LANE_ODD + 1]
    lane = lax.broadcasted_iota(jnp.int32, (tq, LANES), 1)
    o_ref[...] = jnp.where(lane < V_HEAD, out0, out1).astype(o_ref.dtype)


def _flash_attention(q, k, v, *, tq):
    b, h, s, _ = q.shape
    tk = tq // 2
    assert tk % CHUNK == 0 and s % tq == 0
    pair = lambda n: pl.BlockSpec((None, 2, n, LANES), lambda bi, hp, ti: (bi, hp, 0, 0))
    return pl.pallas_call(
        functools.partial(_flash_kernel, tq=tq, tk=tk),
        out_shape=jax.ShapeDtypeStruct((b, s, MLA_OUT), BF16),
        grid=(b, h // 2, s // tq),
        in_specs=[
            pl.BlockSpec((None, 2, tq, HEAD_PAD), lambda bi, hp, ti: (bi, hp, ti, 0)),
            pair(s),
            pair(s),
        ],
        out_specs=pl.BlockSpec((None, tq, 2 * V_HEAD), lambda bi, hp, ti: (bi, ti, hp)),
        scratch_shapes=[pltpu.VMEM((2, tq, LANES), F32), pltpu.VMEM((2, tq, LANES), F32)],
        compiler_params=pltpu.CompilerParams(dimension_semantics=("arbitrary", "arbitrary", "arbitrary"),
                                             vmem_limit_bytes=VMEM_LIMIT),
        name="flash_attention",
    )(q, k, v)


def _latent_attn_kernel(q_ref, wabs_ref, cckv_ref, ckpe_ref, nckv_ref, nkpe_ref, wuv_ref, o_ref,
                        *, sq, past, pos0, n_seq):
    rows = MLA_HEADS * sq
    q_chunk = (pos0 + np.arange(rows)[:, None] % sq) // CHUNK
    vis_past = (pos0 - past + np.arange(past)[None, :]) // CHUNK <= q_chunk
    vis_new = (pos0 + np.arange(sq)[None, :]) // CHUNK <= q_chunk

    def masked(s, vis, n_keys, k_pos0):
        if vis.all():
            return s
        qc = (pos0 + lax.broadcasted_iota(jnp.int32, (rows, 1), 0) % sq) // CHUNK
        kc = (k_pos0 + lax.broadcasted_iota(jnp.int32, (1, n_keys), 1)) // CHUNK
        return jnp.where(kc <= qc, s, NEG)

    for i in range(n_seq):
        qcat = jnp.concatenate([_dot(q_ref[h, i * sq:(i + 1) * sq, :], wabs_ref[h]) for h in range(MLA_HEADS)],
                               axis=0)
        qa = qcat[:, :KV_LORA].astype(BF16)
        qp = qcat[:, KV_LORA:].astype(BF16)
        ck = cckv_ref[i].astype(BF16)
        kp_t = ckpe_ref[i].astype(BF16)
        nk = nckv_ref[i].astype(BF16)
        np_ = nkpe_ref[i].astype(BF16)
        s_past = masked(_dot_nt(qa, ck) + _dot(qp, kp_t), vis_past, past, pos0 - past)
        s_new = masked(_dot_nt(qa, nk) + _dot_nt(qp, np_), vis_new, sq, pos0)

        m = jnp.maximum(jnp.max(s_past, axis=-1, keepdims=True), jnp.max(s_new, axis=-1, keepdims=True))
        p_past = jnp.exp2(s_past - m)
        p_new = jnp.exp2(s_new - m)
        l = jnp.sum(p_past, axis=-1, keepdims=True) + jnp.sum(p_new, axis=-1, keepdims=True)
        o_lat = (_dot(p_past.astype(BF16), ck) + _dot(p_new.astype(BF16), nk)) / l
        o_lat = o_lat.astype(BF16)
        out = _dot(o_lat[0:sq], wuv_ref[0])
        for h in range(1, MLA_HEADS):
            out = out + _dot(o_lat[h * sq:(h + 1) * sq], wuv_ref[h])
        o_ref[i] = out.astype(o_ref.dtype)


def _prep_latent_weights(w_uk, w_uv):
    wabs = np.zeros((MLA_HEADS, HEAD_PAD, KV_LORA + QK_ROPE), np.float32)
    for i in range(QK_ROPE):
        wabs[:, QK_NOPE + i, KV_LORA + i] = 1.0
    wabs = jnp.asarray(wabs).at[:, :QK_NOPE, :KV_LORA].set(jnp.transpose(w_uk, (1, 2, 0)))
    wuv_place = jnp.zeros((MLA_HEADS, KV_LORA, MLA_OUT), F32)
    for h in range(MLA_HEADS):
        wuv_place = wuv_place.at[h, :, h * V_HEAD:(h + 1) * V_HEAD].set(w_uv[:, h, :])
    return wabs.astype(BF16), wuv_place.astype(BF16)


def _latent_attention(q, wabs, cache_ckv, cache_kpe, ckv_new, kpe_new, wuv_place, *, pos0):
    h = q.shape[0]
    b, sq, _ = ckv_new.shape
    past = cache_ckv.shape[1]
    n_seq = LATENT_SEQS_PER_STEP if b % LATENT_SEQS_PER_STEP == 0 else 1
    seq = lambda n, d: pl.BlockSpec((n_seq, n, d), lambda bi: (bi, 0, 0))
    return pl.pallas_call(
        functools.partial(_latent_attn_kernel, sq=sq, past=past, pos0=pos0, n_seq=n_seq),
        out_shape=jax.ShapeDtypeStruct((b, sq, MLA_OUT), BF16),
        grid=(b // n_seq,),
        in_specs=[
            pl.BlockSpec((h, n_seq * sq, HEAD_PAD), lambda bi: (0, bi, 0)),
            _const_spec((h, HEAD_PAD, KV_LORA + QK_ROPE)),
            seq(past, KV_LORA), seq(QK_ROPE, past), seq(sq, KV_LORA), seq(sq, QK_ROPE),
            _const_spec((h, KV_LORA, MLA_OUT)),
        ],
        out_specs=seq(sq, MLA_OUT),
        compiler_params=pltpu.CompilerParams(dimension_semantics=("arbitrary",),
                                             vmem_limit_bytes=VMEM_LIMIT),
        name="latent_attention",
    )(q, wabs, cache_ckv, cache_kpe, ckv_new, kpe_new, wuv_place)


def _out_ffn_kernel(*refs, n_act, final_norm):
    x_ref = refs[0]
    act_refs = refs[1:1 + n_act]
    wout_refs = refs[1 + n_act:1 + 2 * n_act]
    gffn_ref, wg_ref, wu_ref, wd_ref = refs[1 + 2 * n_act:5 + 2 * n_act]
    rest = refs[5 + 2 * n_act:]
    if final_norm:
        gfin_ref, o_ref = rest
    else:
        (o_ref,) = rest

    y = _dot(act_refs[0][...], wout_refs[0][...])
    for a_ref, w_ref in zip(act_refs[1:], wout_refs[1:]):
        y = y + _dot(a_ref[...], w_ref[...])
    x1 = x_ref[...] + y
    h2 = _rms(x1, gffn_ref[...]).astype(BF16)
    gate = _dot(h2, wg_ref[...])
    up = _dot(h2, wu_ref[...])
    act = (gate * jax.nn.sigmoid(gate) * up).astype(BF16)
    x2 = x1 + _dot(act, wd_ref[...])
    if final_norm:
        x2 = _rms(x2, gfin_ref[...])
    o_ref[...] = x2


def _out_ffn(x, acts, wouts, g_ffn, ffn_w, layer, g_final, *, tm):
    t, _ = x.shape
    n_act = len(acts)
    row = lambda d: pl.BlockSpec((tm, d), lambda i: (i, 0))
    slab = lambda w: pl.BlockSpec((None,) + w.shape[1:], lambda i: (layer, 0, 0), pipeline_mode=pl.Buffered(1))
    in_specs = ([row(D_MODEL)] + [row(a.shape[1]) for a in acts] + [_const_spec(w.shape) for w in wouts]
                + [_const_spec((1, D_MODEL))] + [slab(w) for w in ffn_w])
    args = [x, *acts, *wouts, g_ffn.reshape(1, -1), *ffn_w]
    if g_final is not None:
        in_specs.append(_const_spec((1, D_MODEL)))
        args.append(g_final.reshape(1, -1))
    return pl.pallas_call(
        functools.partial(_out_ffn_kernel, n_act=n_act, final_norm=g_final is not None),
        out_shape=jax.ShapeDtypeStruct((t, D_MODEL), F32),
        grid=(t // tm,),
        in_specs=in_specs,
        out_specs=row(D_MODEL),
        compiler_params=pltpu.CompilerParams(dimension_semantics=("arbitrary",),
                                             vmem_limit_bytes=VMEM_LIMIT),
        name="out_ffn",
    )(*args)


def _softplus(x):
    return jnp.maximum(x, 0.0) + jnp.log1p(jnp.exp(-jnp.abs(x)))


def _shift_rows(v, k):
    sub = lax.broadcasted_iota(jnp.int32, (1,) + v.shape[1:], 1)
    mixed = jnp.where(sub >= SUBLANES - k, v[:-1], v[1:])
    return pltpu.roll(mixed, k, axis=1)


def _ssd_in_kernel(x_ref, g_ref, win_ref, wdtt_ref, cw_ref, cb_ref, dtb_ref, dtbt_ref,
                   hist_ref, z_ref, xs_ref, b_ref, c_ref, dt_ref, dtt_ref, clast_ref, carry_scr,
                   *, tm, seq_rows):
    t = pl.program_id(1)
    hb = _rms(x_ref[...], g_ref[...]).astype(BF16)
    dt_ref[...] = _softplus(_dot(hb, win_ref[:, D_INNER + CONV_DIM:]) + dtb_ref[...])
    dtt_ref[...] = _softplus(_dot_nt(wdtt_ref[...].astype(BF16), hb) + dtbt_ref[...])

    lead = CONV_CARRY // SUBLANES
    if seq_rows is None:
        @pl.when(t == 0)
        def _():
            carry_scr[...] = hist_ref[...]
    else:
        n_seq, per_seq = tm // seq_rows, (CONV_CARRY + seq_rows) // SUBLANES

    gn = SSM_GROUPS * D_STATE
    outs = ((xs_ref, 0, D_INNER), (b_ref, D_INNER, gn), (c_ref, D_INNER + gn, gn))
    n_grp = tm // SUBLANES
    starts = list(range(0, CONV_DIM, CONV_CHUNK))
    xbc_w = lambda c0: win_ref[:, D_INNER + c0:D_INNER + c0 + CONV_CHUNK]
    xc_next = _dot(hb, xbc_w(0))
    for i, c0 in enumerate(starts):
        cols = slice(c0, c0 + CONV_CHUNK)
        xc = xc_next
        if i + 1 < len(starts):
            xc_next = _dot(hb, xbc_w(starts[i + 1]))
        if c0 < D_INNER:
            z_ref[:, cols] = _dot(hb, win_ref[:, cols]).astype(z_ref.dtype)
        if seq_rows is None:
            ext = jnp.concatenate([carry_scr[:, cols], xc], axis=0)
            ext = ext.reshape(n_grp + lead, SUBLANES, CONV_CHUNK)
            carry_scr[:, cols] = xc[tm - CONV_CARRY:, :]
            clast_ref[:, cols] = xc[tm - CONV_CARRY:, :]
        else:
            x_seq = xc.reshape(n_seq, seq_rows, CONV_CHUNK)
            ext = jnp.concatenate([hist_ref[:, :, cols], x_seq], axis=1)
            ext = ext.reshape(n_seq * per_seq, SUBLANES, CONV_CHUNK)
            clast_ref[:, :, cols] = x_seq[:, seq_rows - CONV_CARRY:, :]
        w = [cw_ref[k:k + 1, cols] for k in range(D_CONV)]
        ref, start = next((r, s0) for r, s0, width in outs if s0 <= c0 < s0 + width)
        x1 = _shift_rows(ext, 1)
        u = ext[2:] * w[3] + x1[1:] * w[2]
        v = ext[1:] * w[1] + x1 * w[0]
        half = u + _shift_rows(v, 2) + cb_ref[:, cols]
        if seq_rows is None:
            half = half.reshape(tm, CONV_CHUNK)
        else:
            pad = jnp.zeros((lead,) + half.shape[1:], half.dtype)
            half = jnp.concatenate([pad, half], axis=0).reshape(n_seq, per_seq, SUBLANES, CONV_CHUNK)
            half = half[:, lead:].reshape(tm, CONV_CHUNK)
        act = half + half * jnp.tanh(half)
        ref[:, c0 - start:c0 - start + CONV_CHUNK] = act.astype(ref.dtype)


def _ssd_in(x, conv_hist, g_mix, w_in, conv_w, conv_b, dt_bias, *, tm, seq_rows=None):
    b, s, _ = x.shape
    n_t = s // tm
    n_seq = conv_hist.shape[0]
    gn = SSM_GROUPS * D_STATE
    win_b = w_in.astype(BF16)
    wdt_t = w_in[:, D_INNER + CONV_DIM:].T
    hist = jnp.pad(conv_hist.astype(F32), ((0, 0), (CONV_CARRY - (D_CONV - 1), 0), (0, 0)))
    row = lambda d: pl.BlockSpec((None, tm, d), lambda bi, ti: (bi, ti, 0))
    if seq_rows is None:
        per_seq = pl.BlockSpec((None, CONV_CARRY, CONV_DIM), lambda bi, ti: (bi, 0, 0))
    else:
        assert b == 1 and n_t == 1 and tm == n_seq * seq_rows and seq_rows >= CONV_CARRY
        per_seq = pl.BlockSpec((n_seq, CONV_CARRY, CONV_DIM), lambda bi, ti: (0, 0, 0))
    return pl.pallas_call(
        functools.partial(_ssd_in_kernel, tm=tm, seq_rows=seq_rows),
        out_shape=[
            jax.ShapeDtypeStruct((b, s, D_INNER), BF16),
            jax.ShapeDtypeStruct((b, s, D_INNER), BF16),
            jax.ShapeDtypeStruct((b, s, gn), BF16),
            jax.ShapeDtypeStruct((b, s, gn), BF16),
            jax.ShapeDtypeStruct((b, s, SSM_HEADS), F32),
            jax.ShapeDtypeStruct((b, SSM_HEADS, s), F32),
            jax.ShapeDtypeStruct((n_seq, CONV_CARRY, CONV_DIM), F32),
        ],
        grid=(b, n_t),
        in_specs=[
            row(D_MODEL),
            _const_spec((1, D_MODEL)),
            _const_spec((D_MODEL, IN_C)),
            _const_spec((SSM_HEADS, D_MODEL)),
            _const_spec((D_CONV, CONV_DIM)),
            _const_spec((1, CONV_DIM)),
            _const_spec((1, SSM_HEADS)),
            _const_spec((SSM_HEADS, 1)),
            per_seq,
        ],
        out_specs=[
            row(D_INNER), row(D_INNER), row(gn), row(gn), row(SSM_HEADS),
            pl.BlockSpec((None, SSM_HEADS, tm), lambda bi, ti: (bi, 0, ti)),
            per_seq,
        ],
        scratch_shapes=[pltpu.VMEM((CONV_CARRY, CONV_DIM), F32)],
        compiler_params=pltpu.CompilerParams(dimension_semantics=("arbitrary", "arbitrary"),
                                             vmem_limit_bytes=VMEM_LIMIT),
        name="ssd_in",
    )(x, g_mix.reshape(1, -1), win_b, wdt_t, 0.5 * conv_w, 0.5 * conv_b.reshape(1, -1),
      dt_bias.reshape(1, -1), dt_bias.reshape(-1, 1), hist)


def _ssd_scan_kernel(xs_ref, z_ref, b_ref, c_ref, dt_ref, dtt_ref, alog_ref, alogt_ref, dexp_ref, norm_ref,
                     expand_ref, h0_ref, y_ref, hfin_ref, h_scr, *, ln, n_sub, n_steps):
    c = pl.program_id(1)

    @pl.when(c == 0)
    def _():
        for g in range(SSM_GROUPS):
            h_scr[g] = h0_ref[g * GROUP_COLS:(g + 1) * GROUP_COLS, :].T

    row_i = lax.broadcasted_iota(jnp.int32, (ln, ln), 0)
    col_i = lax.broadcasted_iota(jnp.int32, (ln, ln), 1)
    causal = row_i >= col_i
    tril = jnp.where(causal, 1.0, 0.0).astype(BF16)
    triu = jnp.where(row_i <= col_i, 1.0, 0.0).astype(BF16)
    lane = lax.broadcasted_iota(jnp.int32, (ln, LANES), 1)

    for sc in range(n_sub):
        rows = slice(sc * ln, (sc + 1) * ln)
        dt = dt_ref[rows, :]
        dtt = dtt_ref[:, rows]
        acum = _dot_sel_l(tril, dt * -jnp.exp(alog_ref[...]))
        acum_t = _dot_sel_r(dtt * -jnp.exp(alogt_ref[...]), triu)
        acum2 = acum * LOG2E
        key2_t = (acum_t - jnp.log(dtt)) * LOG2E
        last = acum[ln - 1:ln, :]
        ea_hi, ea_mid, _ = _split3(jnp.exp(acum))
        wj_b = (jnp.exp(last - acum) * dt).astype(BF16)
        pair = 2 * SSM_HEADDIM

        for g in range(SSM_GROUPS):
            gcols = slice(g * GROUP_COLS, (g + 1) * GROUP_COLS)
            expand = expand_ref[:, gcols]
            ea_e = _dot(ea_hi, expand) + _dot(ea_mid, expand)
            wj_e = _dot(wj_b, expand).astype(BF16)
            bg = b_ref[rows, g * D_STATE:(g + 1) * D_STATE]
            cg = c_ref[rows, g * D_STATE:(g + 1) * D_STATE]
            cb = _dot_nt(cg, bg)
            y_state = _dot(cg, h_scr[g].astype(BF16))
            gated = []
            for p in range(SSM_HPG // 2):
                h0 = g * SSM_HPG + 2 * p
                cols = slice(h0 * SSM_HEADDIM, (h0 + 2) * SSM_HEADDIM)
                pcols = slice(p * pair, (p + 1) * pair)
                scores = []
                for h in (h0, h0 + 1):
                    seg = jnp.where(causal, acum2[:, h:h + 1] - key2_t[h:h + 1, :], NEG)
                    scores.append((cb * jnp.exp2(seg)).astype(BF16))
                xp = xs_ref[rows, cols]
                zero = jnp.zeros_like(xp)
                rhs = jnp.concatenate([jnp.where(lane < SSM_HEADDIM, xp, zero),
                                       jnp.where(lane >= SSM_HEADDIM, xp, zero)], axis=0)
                y = _dot(jnp.concatenate(scores, axis=1), rhs)
                y = y + y_state[:, pcols] * ea_e[:, pcols]
                y = y + xp.astype(F32) * dexp_ref[:, cols]
                zc = z_ref[rows, cols].astype(F32)
                gated.append(y * (zc * jax.nn.sigmoid(zc)))
            yg = jnp.concatenate(gated, axis=1)
            yg = yg * lax.rsqrt(jnp.mean(yg * yg, axis=-1, keepdims=True) + EPS)
            y_ref[rows, gcols] = (yg * norm_ref[:, gcols]).astype(y_ref.dtype)
            xw = xs_ref[rows, gcols] * wj_e
            h_scr[g] = h_scr[g] * ea_e[ln - 1:ln, :] + _dot_tn(bg, xw)

    @pl.when(c == n_steps - 1)
    def _():
        for g in range(SSM_GROUPS):
            hfin_ref[g * GROUP_COLS:(g + 1) * GROUP_COLS, :] = h_scr[g].T


def _ssd_scan(xs, z, bm, cm, dt, dtt, a_log, d_skip, ssm_norm, h0, *, ln, n_sub=1):
    b, s, _ = xs.shape
    step = ln * n_sub
    n_steps = s // step
    assert s % step == 0
    gn = SSM_GROUPS * D_STATE
    expand = jnp.asarray(np.kron(np.eye(SSM_HEADS, dtype=np.float32), np.ones((1, SSM_HEADDIM), np.float32)),
                         BF16)
    d_exp = jnp.repeat(d_skip.astype(F32), SSM_HEADDIM).reshape(1, D_INNER)
    row = lambda d: pl.BlockSpec((None, step, d), lambda bi, ci: (bi, ci, 0))
    state_spec = pl.BlockSpec((None, D_INNER, D_STATE), lambda bi, ci: (bi, 0, 0))
    y, h_fin = pl.pallas_call(
        functools.partial(_ssd_scan_kernel, ln=ln, n_sub=n_sub, n_steps=n_steps),
        out_shape=[jax.ShapeDtypeStruct((b, s, D_INNER), BF16),
                   jax.ShapeDtypeStruct((b, D_INNER, D_STATE), F32)],
        grid=(b, n_steps),
        in_specs=[
            row(D_INNER), row(D_INNER), row(gn), row(gn), row(SSM_HEADS),
            pl.BlockSpec((None, SSM_HEADS, step), lambda bi, ci: (bi, 0, ci)),
            _const_spec((1, SSM_HEADS)), _const_spec((SSM_HEADS, 1)),
            _const_spec((1, D_INNER)), _const_spec((1, D_INNER)),
            _const_spec((SSM_HEADS, D_INNER)),
            state_spec,
        ],
        out_specs=[row(D_INNER), state_spec],
        scratch_shapes=[pltpu.VMEM((SSM_GROUPS, D_STATE, GROUP_COLS), F32)],
        compiler_params=pltpu.CompilerParams(dimension_semantics=("arbitrary", "arbitrary"),
                                             vmem_limit_bytes=VMEM_LIMIT),
        name="ssd_scan",
    )(xs, z, bm, cm, dt, dtt, a_log.reshape(1, -1), a_log.reshape(-1, 1), d_exp, ssm_norm.reshape(1, -1),
      expand, h0.astype(F32).reshape(b, D_INNER, D_STATE))
    return y, h_fin.reshape(b, SSM_HEADS, SSM_HEADDIM, D_STATE)


def _trunk(x, pos0, cache, pool_hist, conv_hist, ssm_hist, w, *, tm_proj, tm_ffn, tq, ln, n_sub=1):
    b, s, _ = x.shape
    prompt = cache is None
    stacked = dict(tm=b * s, seq_rows=s) if not prompt else dict(tm=tm_proj)
    as_rows = (lambda a: a.reshape(1, b * s, a.shape[-1])) if not prompt else (lambda a: a)
    win_p, wq, wk, wv, place = _prep_ab_weights(w["w_in_ab"][0], w["w_uq"][0], w["w_uk"][0], w["w_uv"][0])
    outs = _ab_in(as_rows(x), pos0, pool_hist, w["norm_mix"][0], win_p, w["q_norm"][0], wq, w["kv_norm"][0],
                  w["w_pool"][0], w["pool_scale"][0], wk, wv, place, emit_kv=prompt, **stacked)
    q, ckv, kpe, pool_out, pool_last = outs[:5]
    if prompt:
        k, v = outs[5:]
        attn = _flash_attention(q, k, v, tq=tq)
        kpe = jnp.swapaxes(kpe, 1, 2)
    else:
        ckv, kpe = ckv.reshape(b, s, KV_LORA), kpe.reshape(b, s, QK_ROPE)
        wabs, wuv_place = _prep_latent_weights(w["w_uk"][0], w["w_uv"][0])
        attn = _latent_attention(q[0], wabs, cache[0], jnp.swapaxes(cache[1], 1, 2), ckv, kpe, wuv_place,
                                 pos0=pos0)
    ffn_w = (w["w_gate"].astype(BF16), w["w_up"].astype(BF16), w["w_down"].astype(BF16))
    w_out = w["w_out_ab"][0].astype(BF16)
    x1 = _out_ffn(x.reshape(b * s, D_MODEL),
                  [pool_out.reshape(b * s, POOL_DIM), attn.reshape(b * s, MLA_OUT)],
                  [w_out[:POOL_DIM], w_out[POOL_DIM:]],
                  w["norm_ffn"][0], ffn_w, 0, None, tm=tm_ffn)

    z, xs, bm, cm, dt, dtt, conv_last = _ssd_in(as_rows(x1.reshape(b, s, D_MODEL)), conv_hist, w["norm_mix"][1],
                                                w["w_in_c"][0], w["conv_w"][0], w["conv_b"][0],
                                                w["dt_bias"][0], **stacked)
    if not prompt:
        z, xs, bm, cm, dt = (a.reshape(b, s, a.shape[-1]) for a in (z, xs, bm, cm, dt))
        dtt = jnp.transpose(dtt.reshape(SSM_HEADS, b, s), (1, 0, 2))
    y, h_fin = _ssd_scan(xs, z, bm, cm, dt, dtt, w["a_log"][0], w["d_skip"][0], w["ssm_norm"][0], ssm_hist,
                         ln=ln, n_sub=n_sub)
    x2 = _out_ffn(x1, [y.reshape(b * s, D_INNER)], [w["w_out_c"][0].astype(BF16)],
                  w["norm_ffn"][1], ffn_w, 1, w["norm_final"], tm=tm_ffn)
    return (x2.reshape(b, s, D_MODEL), ckv[None], kpe[None], pool_last[None, :, POOL_KEEP - POOL_HIST:],
            conv_last[None, :, CONV_CARRY - (D_CONV - 1):], h_fin[None])


def kernel(x_prompt, x_sample, cache_ckv, cache_kpe, state_pool, state_conv, state_ssm, norm_mix, norm_ffn, norm_final, w_in_ab, w_pool, pool_scale, q_norm, w_uq, kv_norm, w_uk, w_uv, w_out_ab, w_in_c, conv_w, conv_b, dt_bias, a_log, d_skip, ssm_norm, w_out_c, w_gate, w_up, w_down):
    assert norm_mix.shape[0] == 2 and w_in_ab.shape[0] == 1 and w_in_c.shape[0] == 1
    w = dict(norm_mix=norm_mix, norm_ffn=norm_ffn, norm_final=norm_final, w_in_ab=w_in_ab, w_pool=w_pool,
             pool_scale=pool_scale, q_norm=q_norm, w_uq=w_uq, kv_norm=kv_norm, w_uk=w_uk, w_uv=w_uv,
             w_out_ab=w_out_ab, w_in_c=w_in_c, conv_w=conv_w, conv_b=conv_b, dt_bias=dt_bias, a_log=a_log,
             d_skip=d_skip, ssm_norm=ssm_norm, w_out_c=w_out_c, w_gate=w_gate, w_up=w_up, w_down=w_down)
    bp = x_prompt.shape[0]
    bs, ss, _ = x_sample.shape
    past = cache_ckv.shape[2]

    prompt = _trunk(x_prompt, 0, None,
                    jnp.zeros((bp, POOL_HIST, POOL_DIM), F32), jnp.zeros((bp, D_CONV - 1, CONV_DIM), F32),
                    jnp.zeros((bp, SSM_HEADS, SSM_HEADDIM, D_STATE), F32), w,
                    tm_proj=PROMPT_PROJ_ROWS, tm_ffn=PROMPT_FFN_ROWS, tq=PROMPT_Q_ROWS,
                    ln=SSD_CHUNK, n_sub=SSD_CHUNKS_PER_STEP)
    sample = _trunk(x_sample, past, (cache_ckv[0], cache_kpe[0]), state_pool[0], state_conv[0], state_ssm[0], w,
                    tm_proj=None, tm_ffn=bs * ss, tq=None, ln=ss)
    return (prompt[0], sample[0]) + prompt[1:] + sample[1:]
```

```python
import functools
import math

import numpy as np
import jax
import jax.numpy as jnp
from jax import lax
from jax.experimental import pallas as pl
from jax.experimental.pallas import tpu as pltpu

F32 = jnp.float32
BF16 = jnp.bfloat16

D_MODEL = 1024
CHUNK = 64
EPS = 1e-6
NEG = -1e30

POOL_WINDOWS = (2, 4, 8, 16)
POOL_GROUPS = 4
POOL_DIM = D_MODEL // 2
POOL_GDIM = POOL_DIM // POOL_GROUPS
POOL_HIST = max(POOL_WINDOWS) - 1

MLA_HEADS = 8
QK_NOPE = 64
QK_ROPE = 32
ROPE_HALF = QK_ROPE // 2
V_HEAD = 64
Q_LORA = 384
KV_LORA = 256
MLA_OUT = MLA_HEADS * V_HEAD
ROPE_THETA = 10000.0
SM_SCALE = (QK_NOPE + QK_ROPE) ** -0.5
LOG2E = math.log2(math.e)
Q_SCALE = SM_SCALE * LOG2E
IN_AB = POOL_DIM + Q_LORA + KV_LORA + QK_ROPE

D_INNER = 2 * D_MODEL
SSM_HEADDIM = 64
SSM_HEADS = D_INNER // SSM_HEADDIM
SSM_GROUPS = 4
SSM_HPG = SSM_HEADS // SSM_GROUPS
D_STATE = 128
D_CONV = 4
CONV_DIM = D_INNER + 2 * SSM_GROUPS * D_STATE
IN_C = D_INNER + CONV_DIM + SSM_HEADS
GROUP_COLS = D_INNER // SSM_GROUPS

LANES = 128
SUBLANES = 8
HEAD_PAD = LANES
ROWSUM_LANE_EVEN = V_HEAD
ROWSUM_LANE_ODD = 0
POOL_CARRY = 4 * SUBLANES
POOL_KEEP = 2 * SUBLANES
LATENT_SEQS_PER_STEP = 2
CONV_CARRY = 2 * SUBLANES
CONV_CHUNK = 256
IN_AB_PAD = 10 * LANES
VMEM_LIMIT = 56 * 1024 * 1024

PROMPT_PROJ_ROWS = 1024
PROMPT_FFN_ROWS = 512
PROMPT_Q_ROWS = 1024
FFN_STREAM_COLS = 256
SSD_CHUNK = 128
SSD_CHUNKS_PER_STEP = 4

OFF_Q = POOL_DIM
OFF_KV = OFF_Q + Q_LORA
OFF_KPE = OFF_KV + KV_LORA


def _const_spec(shape):
    nd = len(shape)
    return pl.BlockSpec(shape, lambda *_: (0,) * nd, pipeline_mode=pl.Buffered(1))


def _rms(x, g):
    return x * lax.rsqrt(jnp.mean(x * x, axis=-1, keepdims=True) + EPS) * g


def _dot(a, b):
    return jnp.dot(a, b, preferred_element_type=F32)


def _dot_nt(a, b):
    return lax.dot_general(a, b, (((1,), (1,)), ((), ())), preferred_element_type=F32)


def _dot_tn(a, b):
    return lax.dot_general(a, b, (((0,), (0,)), ((), ())), preferred_element_type=F32)


def _split3(x):
    hi = x.astype(BF16)
    r1 = x - hi.astype(F32)
    mid = r1.astype(BF16)
    lo = (r1 - mid.astype(F32)).astype(BF16)
    return hi, mid, lo


def _dot_sel_r(x, sel):
    hi, mid, lo = _split3(x)
    return _dot(hi, sel) + _dot(mid, sel) + _dot(lo, sel)


def _dot_sel_l(sel, x):
    hi, mid, lo = _split3(x)
    return _dot(sel, hi) + _dot(sel, mid) + _dot(sel, lo)


def _ab_in_kernel(*refs, tm, pos0, emit_kv, seq_rows):
    (x_ref, g_ref, win_ref, qn_ref, wuq_ref, kvn_ref, wpool_ref, pscale_ref, tab_ref, hist_ref) = refs[:10]
    refs = refs[10:]
    if emit_kv:
        wuk_ref, wuv_ref, place_ref = refs[:3]
        refs = refs[3:]
    q_ref, ckv_ref, kpe_ref, pool_ref, plast_ref = refs[:5]
    refs = refs[5:]
    if emit_kv:
        k_ref, v_ref = refs[:2]
        refs = refs[2:]
    (cat_scr,) = refs

    t = pl.program_id(1)
    x = x_ref[...]
    hb = _rms(x, g_ref[...]).astype(BF16)
    proj = _dot(hb, win_ref[...])
    u_pool = proj[:, :POOL_DIM]
    q_lat = proj[:, OFF_Q:OFF_KV]
    kv_lat = proj[:, OFF_KV:OFF_KPE]
    kblk = proj[:, OFF_KPE:OFF_KPE + LANES]

    tab = tab_ref[...]
    up = pltpu.roll(tab, ROPE_HALF, 1)
    down = pltpu.roll(tab, LANES - ROPE_HALF, 1)
    ln_i = lax.broadcasted_iota(jnp.int32, (tm, LANES), 1)
    x1_q, x2_q = QK_NOPE, QK_NOPE + ROPE_HALF
    tq_c = jnp.where(ln_i < x1_q, 1.0, jnp.where(ln_i < x2_q, tab, jnp.where(ln_i < x2_q + ROPE_HALF, up, 0.0)))
    tq_a = jnp.where((ln_i >= x2_q) & (ln_i < x2_q + ROPE_HALF), tab, 0.0)
    tq_b = jnp.where((ln_i >= x1_q) & (ln_i < x2_q), -down, 0.0)
    tk_c = jnp.where(ln_i < ROPE_HALF, tab, jnp.where(ln_i < QK_ROPE, up, 0.0))
    tk_s = jnp.where(ln_i < ROPE_HALF, -down, jnp.where(ln_i < QK_ROPE, tab, 0.0))

    qn = _rms(q_lat, qn_ref[...]).astype(BF16)
    q = _dot(qn, wuq_ref[...])
    for h in range(MLA_HEADS):
        qh = q[:, h * HEAD_PAD:(h + 1) * HEAD_PAD]
        qr = (qh * tq_c + pltpu.roll(qh, ROPE_HALF, 1) * tq_a
              + pltpu.roll(qh, HEAD_PAD - ROPE_HALF, 1) * tq_b)
        q_ref[h] = qr.astype(q_ref.dtype)

    ckv = _rms(kv_lat, kvn_ref[...])
    ckv_ref[...] = ckv
    kpe_blk = kblk * tk_c + pltpu.roll(kblk, LANES - QK_ROPE, 1) * tk_s
    if emit_kv:
        kpe_ref[...] = kpe_blk.T[:QK_ROPE, :]
    else:
        kpe_ref[...] = kpe_blk[:, :QK_ROPE]

    if emit_kv:
        ckv_b = ckv.astype(BF16)
        k_all = _dot(ckv_b, wuk_ref[...]) + _dot(kpe_blk.astype(BF16), place_ref[...])
        for h in range(MLA_HEADS):
            k_ref[h] = k_all[:, h * HEAD_PAD:(h + 1) * HEAD_PAD].astype(k_ref.dtype)
        v_all = _dot(ckv_b, wuv_ref[...])
        lane = lax.broadcasted_iota(jnp.int32, (tm, LANES), 1)
        for hp in range(MLA_HEADS // 2):
            vp = v_all[:, hp * LANES:(hp + 1) * LANES]
            even = jnp.where(lane < V_HEAD, vp, jnp.where(lane == ROWSUM_LANE_EVEN, 1.0, 0.0))
            odd = jnp.where(lane >= V_HEAD, vp, jnp.where(lane == ROWSUM_LANE_ODD, 1.0, 0.0))
            v_ref[2 * hp] = even.astype(v_ref.dtype)
            v_ref[2 * hp + 1] = odd.astype(v_ref.dtype)

    lead = POOL_CARRY // SUBLANES
    row_i = lax.broadcasted_iota(jnp.int32, (tm, 1), 0)
    if seq_rows is None:
        @pl.when(t == 0)
        def _():
            cat_scr[...] = hist_ref[...]

        ext2d = jnp.concatenate([cat_scr[...], u_pool], axis=0)
        ext = ext2d.reshape((POOL_CARRY + tm) // SUBLANES, SUBLANES, POOL_DIM)
        pos = pos0 + t * tm + row_i
        cat_scr[...] = ext2d[tm:, :]
        plast_ref[...] = ext2d[POOL_CARRY + tm - POOL_KEEP:, :]
    else:
        n_seq, per_seq = tm // seq_rows, (POOL_CARRY + seq_rows) // SUBLANES
        u_seq = u_pool.reshape(n_seq, seq_rows, POOL_DIM)
        ext = jnp.concatenate([hist_ref[...], u_seq], axis=1)
        ext = ext.reshape(n_seq * per_seq, SUBLANES, POOL_DIM)
        pos = pos0 + row_i % seq_rows
        plast_ref[...] = u_seq[:, seq_rows - POOL_KEEP:, :]

    def tile_rows(win, dropped):
        if seq_rows is None:
            return win[lead - dropped:].reshape(tm, POOL_GDIM)
        pad = jnp.zeros((dropped,) + win.shape[1:], win.dtype)
        full = jnp.concatenate([pad, win], axis=0).reshape(n_seq, per_seq, SUBLANES, POOL_GDIM)
        return full[:, lead:].reshape(tm, POOL_GDIM)

    for g, w in enumerate(POOL_WINDOWS):
        cols = slice(g * POOL_GDIM, (g + 1) * POOL_GDIM)
        win, span, dropped = ext[:, :, cols], 1, 0
        while span < w:
            prev = win[:-1] if span == SUBLANES else _shift_rows(win, span)
            win, span, dropped = win[1:] + prev, 2 * span, dropped + 1
        win = tile_rows(win, dropped)
        u_g = u_pool[:, cols]
        cnt = jnp.minimum(pos + 1, w).astype(F32)
        pooled = win / cnt - u_g
        mixed = _dot(pooled.astype(BF16), wpool_ref[g]) * pscale_ref[:, cols]
        pool_ref[:, cols] = mixed.astype(pool_ref.dtype)


def _rope_table(pos0, s):
    pos = (pos0 + jnp.arange(s)).astype(F32)
    inv = ROPE_THETA ** (-jnp.arange(ROPE_HALF, dtype=F32) / ROPE_HALF)
    ang = pos[:, None] * inv[None, :]
    return jnp.tile(jnp.concatenate([jnp.cos(ang), jnp.sin(ang)], axis=1), (1, LANES // QK_ROPE))


def _prep_ab_weights(w_in, w_uq, w_uk, w_uv):
    kpe_w = w_in[:, OFF_KPE:]
    swapped = jnp.concatenate([kpe_w[:, ROPE_HALF:], kpe_w[:, :ROPE_HALF]], axis=1)
    pad = jnp.zeros((D_MODEL, IN_AB_PAD - IN_AB - QK_ROPE), w_in.dtype)
    win_p = jnp.concatenate([w_in, swapped, pad], axis=1).astype(BF16)

    tail = HEAD_PAD - QK_NOPE - QK_ROPE
    wq = w_uq.reshape(Q_LORA, MLA_HEADS, QK_NOPE + QK_ROPE)
    wq = jnp.pad(wq, ((0, 0), (0, 0), (0, tail))).reshape(Q_LORA, MLA_HEADS * HEAD_PAD).astype(BF16)
    wk = jnp.pad(w_uk, ((0, 0), (0, 0), (0, HEAD_PAD - QK_NOPE))).reshape(KV_LORA, MLA_HEADS * HEAD_PAD)
    wk = wk.astype(BF16)
    wv = w_uv.reshape(KV_LORA, MLA_OUT).astype(BF16)
    place = np.zeros((LANES, MLA_HEADS * HEAD_PAD), np.float32)
    for h in range(MLA_HEADS):
        for i in range(QK_ROPE):
            place[i, h * HEAD_PAD + QK_NOPE + i] = 1.0
    return win_p, wq, wk, wv, jnp.asarray(place, BF16)


def _ab_in(x, pos0, pool_hist, g_mix, win_p, q_norm, wq, kv_norm, w_pool, pool_scale, wk, wv, place,
           *, tm, emit_kv, seq_rows=None):
    b, s, _ = x.shape
    n_t = s // tm
    n_seq = pool_hist.shape[0]
    tab = _rope_table(pos0, s) if seq_rows is None else jnp.tile(_rope_table(pos0, seq_rows), (n_seq, 1))
    hist = jnp.pad(pool_hist.astype(F32), ((0, 0), (POOL_CARRY - POOL_HIST, 0), (0, 0)))
    row = lambda shape: pl.BlockSpec((None,) + shape, lambda bi, ti: (bi, ti, 0))
    if seq_rows is None:
        per_seq = lambda rows: pl.BlockSpec((None, rows, POOL_DIM), lambda bi, ti: (bi, 0, 0))
    else:
        assert b == 1 and n_t == 1 and tm == n_seq * seq_rows and seq_rows >= POOL_KEEP
        per_seq = lambda rows: pl.BlockSpec((n_seq, rows, POOL_DIM), lambda bi, ti: (0, 0, 0))

    in_specs = [
        row((tm, D_MODEL)),
        _const_spec((1, D_MODEL)),
        _const_spec((D_MODEL, IN_AB_PAD)),
        _const_spec((1, Q_LORA)),
        _const_spec((Q_LORA, MLA_HEADS * HEAD_PAD)),
        _const_spec((1, KV_LORA)),
        _const_spec((POOL_GROUPS, POOL_GDIM, POOL_GDIM)),
        _const_spec((1, POOL_DIM)),
        pl.BlockSpec((tm, LANES), lambda bi, ti: (ti, 0)),
        per_seq(POOL_CARRY),
    ]
    args = [x, g_mix.reshape(1, -1), win_p, (q_norm * Q_SCALE).reshape(1, -1), wq, kv_norm.reshape(1, -1),
            w_pool.astype(BF16), pool_scale.reshape(1, -1), tab, hist]
    if emit_kv:
        in_specs += [_const_spec((KV_LORA, MLA_HEADS * HEAD_PAD)), _const_spec((KV_LORA, MLA_OUT)),
                     _const_spec((LANES, MLA_HEADS * HEAD_PAD))]
        args += [wk, wv, place]

    head_spec = pl.BlockSpec((None, MLA_HEADS, tm, HEAD_PAD), lambda bi, ti: (bi, 0, ti, 0))
    out_shape = [
        jax.ShapeDtypeStruct((b, MLA_HEADS, s, HEAD_PAD), BF16),
        jax.ShapeDtypeStruct((b, s, KV_LORA), F32),
        jax.ShapeDtypeStruct((b, QK_ROPE, s) if emit_kv else (b, s, QK_ROPE), F32),
        jax.ShapeDtypeStruct((b, s, POOL_DIM), BF16),
        jax.ShapeDtypeStruct((n_seq, POOL_KEEP, POOL_DIM), F32),
    ]
    out_specs = [
        head_spec,
        row((tm, KV_LORA)),
        pl.BlockSpec((None, QK_ROPE, tm), lambda bi, ti: (bi, 0, ti)) if emit_kv else row((tm, QK_ROPE)),
        row((tm, POOL_DIM)),
        per_seq(POOL_KEEP),
    ]
    if emit_kv:
        out_shape += [jax.ShapeDtypeStruct((b, MLA_HEADS, s, HEAD_PAD), BF16),
                      jax.ShapeDtypeStruct((b, MLA_HEADS, s, LANES), BF16)]
        out_specs += [head_spec, head_spec]

    return pl.pallas_call(
        functools.partial(_ab_in_kernel, tm=tm, pos0=pos0, emit_kv=emit_kv, seq_rows=seq_rows),
        out_shape=out_shape,
        grid=(b, n_t),
        in_specs=in_specs,
        out_specs=out_specs,
        scratch_shapes=[pltpu.VMEM((POOL_CARRY, POOL_DIM), F32)],
        compiler_params=pltpu.CompilerParams(dimension_semantics=("arbitrary", "arbitrary"),
                                             vmem_limit_bytes=VMEM_LIMIT),
        name="ab_in",
    )(*args)


def _flash_kernel(q_ref, k_ref, v_ref, o_ref, m_scr, acc_scr, *, tq, tk):
    t = pl.program_id(2)
    heads = (0, 1)

    def group(blocks):
        s = {}
        for n, (blk, d) in enumerate(blocks):
            off = pl.multiple_of(blk * tk, tk)
            r0 = 0 if d is None else d * tk
            for hh in heads:
                sh = _dot_nt(q_ref[hh, r0:, :], k_ref[hh, pl.ds(off, tk), :])
                if d is not None:
                    qc = (r0 + lax.broadcasted_iota(jnp.int32, (tq - r0, tk), 0)) // CHUNK
                    kc = (d * tk + lax.broadcasted_iota(jnp.int32, (tq - r0, tk), 1)) // CHUNK
                    sh = jnp.where(kc <= qc, sh, NEG)
                s[n, hh] = sh
        for n, (blk, d) in enumerate(blocks):
            off = pl.multiple_of(blk * tk, tk)
            r0 = 0 if d is None else d * tk
            for hh in heads:
                sh = s[n, hh]
                m_prev = m_scr[hh, r0:, :]
                m_new = jnp.maximum(m_prev, jnp.max(sh, axis=-1, keepdims=True))
                alpha = jnp.exp2(m_prev - m_new)
                p = jnp.concatenate([jnp.exp2(sh[:, c * LANES:(c + 1) * LANES] - m_new)
                                     for c in range(tk // LANES)], axis=1).astype(BF16)
                acc_scr[hh, r0:, :] = alpha * acc_scr[hh, r0:, :] + _dot(p, v_ref[hh, pl.ds(off, tk), :])
                m_scr[hh, r0:, :] = m_new

    m_scr[...] = jnp.full(m_scr.shape, NEG, F32)
    acc_scr[...] = jnp.zeros(acc_scr.shape, F32)

    def body(i, carry):
        group([(2 * i, None), (2 * i + 1, None)])
        return carry

    lax.fori_loop(0, t, body, 0)
    group([(2 * t, 0), (2 * t + 1, 1)])

    acc0, acc1 = acc_scr[0], acc_scr[1]
    out0 = acc0 / acc0[:, ROWSUM_LANE_EVEN:ROWSUM_LANE_EVEN + 1]
    out1 = acc1 / acc1[:, ROWSUM_LANE_ODD:ROWSUM_LANE_ODD + 1]
    lane = lax.broadcasted_iota(jnp.int32, (tq, LANES), 1)
    o_ref[...] = jnp.where(lane < V_HEAD, out0, out1).astype(o_ref.dtype)


def _flash_attention(q, k, v, *, tq):
    b, h, s, _ = q.shape
    tk = tq // 2
    assert tk % CHUNK == 0 and s % tq == 0
    pair = lambda n: pl.BlockSpec((None, 2, n, LANES), lambda bi, hp, ti: (bi, hp, 0, 0))
    return pl.pallas_call(
        functools.partial(_flash_kernel, tq=tq, tk=tk),
        out_shape=jax.ShapeDtypeStruct((b, s, MLA_OUT), BF16),
        grid=(b, h // 2, s // tq),
        in_specs=[
            pl.BlockSpec((None, 2, tq, HEAD_PAD), lambda bi, hp, ti: (bi, hp, ti, 0)),
            pair(s),
            pair(s),
        ],
        out_specs=pl.BlockSpec((None, tq, 2 * V_HEAD), lambda bi, hp, ti: (bi, ti, hp)),
        scratch_shapes=[pltpu.VMEM((2, tq, LANES), F32), pltpu.VMEM((2, tq, LANES), F32)],
        compiler_params=pltpu.CompilerParams(dimension_semantics=("arbitrary", "arbitrary", "arbitrary"),
                                             vmem_limit_bytes=VMEM_LIMIT),
        name="flash_attention",
    )(q, k, v)


def _latent_attn_kernel(q_ref, wabs_ref, cckv_ref, ckpe_ref, nckv_ref, nkpe_ref, wuv_ref, o_ref,
                        *, sq, past, pos0, n_seq):
    rows = MLA_HEADS * sq
    q_chunk = (pos0 + np.arange(rows)[:, None] % sq) // CHUNK
    vis_past = (pos0 - past + np.arange(past)[None, :]) // CHUNK <= q_chunk
    vis_new = (pos0 + np.arange(sq)[None, :]) // CHUNK <= q_chunk

    def masked(s, vis, n_keys, k_pos0):
        if vis.all():
            return s
        qc = (pos0 + lax.broadcasted_iota(jnp.int32, (rows, 1), 0) % sq) // CHUNK
        kc = (k_pos0 + lax.broadcasted_iota(jnp.int32, (1, n_keys), 1)) // CHUNK
        return jnp.where(kc <= qc, s, NEG)

    for i in range(n_seq):
        qcat = jnp.concatenate([_dot(q_ref[h, i * sq:(i + 1) * sq, :], wabs_ref[h]) for h in range(MLA_HEADS)],
                               axis=0)
        qa = qcat[:, :KV_LORA].astype(BF16)
        qp = qcat[:, KV_LORA:].astype(BF16)
        ck = cckv_ref[i].astype(BF16)
        kp_t = ckpe_ref[i].astype(BF16)
        nk = nckv_ref[i].astype(BF16)
        np_ = nkpe_ref[i].astype(BF16)
        s_past = masked(_dot_nt(qa, ck) + _dot(qp, kp_t), vis_past, past, pos0 - past)
        s_new = masked(_dot_nt(qa, nk) + _dot_nt(qp, np_), vis_new, sq, pos0)

        m = jnp.maximum(jnp.max(s_past, axis=-1, keepdims=True), jnp.max(s_new, axis=-1, keepdims=True))
        p_past = jnp.exp2(s_past - m)
        p_new = jnp.exp2(s_new - m)
        l = jnp.sum(p_past, axis=-1, keepdims=True) + jnp.sum(p_new, axis=-1, keepdims=True)
        o_lat = (_dot(p_past.astype(BF16), ck) + _dot(p_new.astype(BF16), nk)) / l
        o_lat = o_lat.astype(BF16)
        out = _dot(o_lat[0:sq], wuv_ref[0])
        for h in range(1, MLA_HEADS):
            out = out + _dot(o_lat[h * sq:(h + 1) * sq], wuv_ref[h])
        o_ref[i] = out.astype(o_ref.dtype)


def _prep_latent_weights(w_uk, w_uv):
    wabs = np.zeros((MLA_HEADS, HEAD_PAD, KV_LORA + QK_ROPE), np.float32)
    for i in range(QK_ROPE):
        wabs[:, QK_NOPE + i, KV_LORA + i] = 1.0
    wabs = jnp.asarray(wabs).at[:, :QK_NOPE, :KV_LORA].set(jnp.transpose(w_uk, (1, 2, 0)))
    wuv_place = jnp.zeros((MLA_HEADS, KV_LORA, MLA_OUT), F32)
    for h in range(MLA_HEADS):
        wuv_place = wuv_place.at[h, :, h * V_HEAD:(h + 1) * V_HEAD].set(w_uv[:, h, :])
    return wabs.astype(BF16), wuv_place.astype(BF16)


def _latent_attention(q, wabs, cache_ckv, cache_kpe, ckv_new, kpe_new, wuv_place, *, pos0):
    h = q.shape[0]
    b, sq, _ = ckv_new.shape
    past = cache_ckv.shape[1]
    n_seq = LATENT_SEQS_PER_STEP if b % LATENT_SEQS_PER_STEP == 0 else 1
    seq = lambda n, d: pl.BlockSpec((n_seq, n, d), lambda bi: (bi, 0, 0))
    return pl.pallas_call(
        functools.partial(_latent_attn_kernel, sq=sq, past=past, pos0=pos0, n_seq=n_seq),
        out_shape=jax.ShapeDtypeStruct((b, sq, MLA_OUT), BF16),
        grid=(b // n_seq,),
        in_specs=[
            pl.BlockSpec((h, n_seq * sq, HEAD_PAD), lambda bi: (0, bi, 0)),
            _const_spec((h, HEAD_PAD, KV_LORA + QK_ROPE)),
            seq(past, KV_LORA), seq(QK_ROPE, past), seq(sq, KV_LORA), seq(sq, QK_ROPE),
            _const_spec((h, KV_LORA, MLA_OUT)),
        ],
        out_specs=seq(sq, MLA_OUT),
        compiler_params=pltpu.CompilerParams(dimension_semantics=("arbitrary",),
                                             vmem_limit_bytes=VMEM_LIMIT),
        name="latent_attention",
    )(q, wabs, cache_ckv, cache_kpe, ckv_new, kpe_new, wuv_place)


def _out_ffn_kernel(*refs, n_act, final_norm):
    x_ref = refs[0]
    act_refs = refs[1:1 + n_act]
    wout_refs = refs[1 + n_act:1 + 2 * n_act]
    gffn_ref, wg_ref, wu_ref, wd_ref = refs[1 + 2 * n_act:5 + 2 * n_act]
    rest = refs[5 + 2 * n_act:]
    if final_norm:
        gfin_ref, o_ref = rest
    else:
        (o_ref,) = rest

    y = _dot(act_refs[0][...], wout_refs[0][...])
    for a_ref, w_ref in zip(act_refs[1:], wout_refs[1:]):
        y = y + _dot(a_ref[...], w_ref[...])
    x1 = x_ref[...] + y
    h2 = _rms(x1, gffn_ref[...]).astype(BF16)
    gate = _dot(h2, wg_ref[...])
    up = _dot(h2, wu_ref[...])
    act = (gate * jax.nn.sigmoid(gate) * up).astype(BF16)
    x2 = x1 + _dot(act, wd_ref[...])
    if final_norm:
        x2 = _rms(x2, gfin_ref[...])
    o_ref[...] = x2


def _out_ffn_stream_kernel(*refs, n_act, final_norm, n_steps):
    x_ref = refs[0]
    act_refs = refs[1:1 + n_act]
    wout_refs = refs[1 + n_act:1 + 2 * n_act]
    gffn_ref, wg_ref, wu_ref, wd_ref = refs[1 + 2 * n_act:5 + 2 * n_act]
    rest = refs[5 + 2 * n_act:]
    if final_norm:
        gfin_ref, o_ref, x1_scr, h2_scr, acc_scr = rest
    else:
        o_ref, x1_scr, h2_scr, acc_scr = rest
    j = pl.program_id(0)

    @pl.when(j == 0)
    def _():
        y = _dot(act_refs[0][...], wout_refs[0][...])
        for a_ref, w_ref in zip(act_refs[1:], wout_refs[1:]):
            y = y + _dot(a_ref[...], w_ref[...])
        x1 = x_ref[...] + y
        x1_scr[...] = x1
        h2_scr[...] = _rms(x1, gffn_ref[...]).astype(BF16)
        acc_scr[...] = jnp.zeros(acc_scr.shape, F32)

    h2 = h2_scr[...]
    gate = _dot(h2, wg_ref[...])
    up = _dot(h2, wu_ref[...])
    act = (gate * jax.nn.sigmoid(gate) * up).astype(BF16)
    acc_scr[...] += _dot(act, wd_ref[...])

    @pl.when(j == n_steps - 1)
    def _():
        x2 = x1_scr[...] + acc_scr[...]
        if final_norm:
            x2 = _rms(x2, gfin_ref[...])
        o_ref[...] = x2


def _out_ffn_stream(x, acts, wouts, g_ffn, ffn_w, layer, g_final):
    t, _ = x.shape
    n_act = len(acts)
    d_ff = ffn_w[0].shape[2]
    n_steps = d_ff // FFN_STREAM_COLS
    assert d_ff % FFN_STREAM_COLS == 0
    whole = lambda a: pl.BlockSpec(a.shape, lambda j: (0,) * a.ndim)
    cols = pl.BlockSpec((None, D_MODEL, FFN_STREAM_COLS), lambda j: (layer, 0, j))
    rows = pl.BlockSpec((None, FFN_STREAM_COLS, D_MODEL), lambda j: (layer, j, 0))
    g_row = g_ffn.reshape(1, -1)
    in_specs = [whole(x)] + [whole(a) for a in acts] + [whole(w) for w in wouts] + [whole(g_row), cols, cols, rows]
    args = [x, *acts, *wouts, g_row, *ffn_w]
    if g_final is not None:
        in_specs.append(whole(g_final.reshape(1, -1)))
        args.append(g_final.reshape(1, -1))
    return pl.pallas_call(
        functools.partial(_out_ffn_stream_kernel, n_act=n_act, final_norm=g_final is not None, n_steps=n_steps),
        out_shape=jax.ShapeDtypeStruct((t, D_MODEL), F32),
        grid=(n_steps,),
        in_specs=in_specs,
        out_specs=pl.BlockSpec((t, D_MODEL), lambda j: (0, 0)),
        scratch_shapes=[pltpu.VMEM((t, D_MODEL), F32), pltpu.VMEM((t, D_MODEL), BF16),
                        pltpu.VMEM((t, D_MODEL), F32)],
        compiler_params=pltpu.CompilerParams(dimension_semantics=("arbitrary",),
                                             vmem_limit_bytes=VMEM_LIMIT),
        name="out_ffn_stream",
    )(*args)


def _out_ffn(x, acts, wouts, g_ffn, ffn_w, layer, g_final, *, tm):
    t, _ = x.shape
    if t == tm:
        return _out_ffn_stream(x, acts, wouts, g_ffn, ffn_w, layer, g_final)
    n_act = len(acts)
    row = lambda d: pl.BlockSpec((tm, d), lambda i: (i, 0))
    slab = lambda w: pl.BlockSpec((None,) + w.shape[1:], lambda i: (layer, 0, 0), pipeline_mode=pl.Buffered(1))
    in_specs = ([row(D_MODEL)] + [row(a.shape[1]) for a in acts] + [_const_spec(w.shape) for w in wouts]
                + [_const_spec((1, D_MODEL))] + [slab(w) for w in ffn_w])
    args = [x, *acts, *wouts, g_ffn.reshape(1, -1), *ffn_w]
    if g_final is not None:
        in_specs.append(_const_spec((1, D_MODEL)))
        args.append(g_final.reshape(1, -1))
    return pl.pallas_call(
        functools.partial(_out_ffn_kernel, n_act=n_act, final_norm=g_final is not None),
        out_shape=jax.ShapeDtypeStruct((t, D_MODEL), F32),
        grid=(t // tm,),
        in_specs=in_specs,
        out_specs=row(D_MODEL),
        compiler_params=pltpu.CompilerParams(dimension_semantics=("arbitrary",),
                                             vmem_limit_bytes=VMEM_LIMIT),
        name="out_ffn",
    )(*args)


def _softplus(x):
    return jnp.maximum(x, 0.0) + jnp.log1p(jnp.exp(-jnp.abs(x)))


def _shift_rows(v, k):
    sub = lax.broadcasted_iota(jnp.int32, (1,) + v.shape[1:], 1)
    mixed = jnp.where(sub >= SUBLANES - k, v[:-1], v[1:])
    return pltpu.roll(mixed, k, axis=1)


def _ssd_in_kernel(x_ref, g_ref, win_ref, wdtt_ref, cw_ref, cb_ref, dtb_ref, dtbt_ref,
                   hist_ref, z_ref, xs_ref, b_ref, c_ref, dt_ref, dtt_ref, clast_ref, carry_scr,
                   *, tm, seq_rows):
    t = pl.program_id(1)
    hb = _rms(x_ref[...], g_ref[...]).astype(BF16)
    dt_ref[...] = _softplus(_dot(hb, win_ref[:, D_INNER + CONV_DIM:]) + dtb_ref[...])
    dtt_ref[...] = _softplus(_dot_nt(wdtt_ref[...].astype(BF16), hb) + dtbt_ref[...])

    lead = CONV_CARRY // SUBLANES
    if seq_rows is None:
        @pl.when(t == 0)
        def _():
            carry_scr[...] = hist_ref[...]
    else:
        n_seq, per_seq = tm // seq_rows, (CONV_CARRY + seq_rows) // SUBLANES

    gn = SSM_GROUPS * D_STATE
    outs = ((xs_ref, 0, D_INNER), (b_ref, D_INNER, gn), (c_ref, D_INNER + gn, gn))
    n_grp = tm // SUBLANES
    starts = list(range(0, CONV_DIM, CONV_CHUNK))
    xbc_w = lambda c0: win_ref[:, D_INNER + c0:D_INNER + c0 + CONV_CHUNK]
    xc_next = _dot(hb, xbc_w(0))
    for i, c0 in enumerate(starts):
        cols = slice(c0, c0 + CONV_CHUNK)
        xc = xc_next
        if i + 1 < len(starts):
            xc_next = _dot(hb, xbc_w(starts[i + 1]))
        if c0 < D_INNER:
            z_ref[:, cols] = _dot(hb, win_ref[:, cols]).astype(z_ref.dtype)
        if seq_rows is None:
            ext = jnp.concatenate([carry_scr[:, cols], xc], axis=0)
            ext = ext.reshape(n_grp + lead, SUBLANES, CONV_CHUNK)
            carry_scr[:, cols] = xc[tm - CONV_CARRY:, :]
            clast_ref[:, cols] = xc[tm - CONV_CARRY:, :]
        else:
            x_seq = xc.reshape(n_seq, seq_rows, CONV_CHUNK)
            ext = jnp.concatenate([hist_ref[:, :, cols], x_seq], axis=1)
            ext = ext.reshape(n_seq * per_seq, SUBLANES, CONV_CHUNK)
            clast_ref[:, :, cols] = x_seq[:, seq_rows - CONV_CARRY:, :]
        w = [cw_ref[k:k + 1, cols] for k in range(D_CONV)]
        ref, start = next((r, s0) for r, s0, width in outs if s0 <= c0 < s0 + width)
        x1 = _shift_rows(ext, 1)
        u = ext[2:] * w[3] + x1[1:] * w[2]
        v = ext[1:] * w[1] + x1 * w[0]
        half = u + _shift_rows(v, 2) + cb_ref[:, cols]
        if seq_rows is None:
            half = half.reshape(tm, CONV_CHUNK)
        else:
            pad = jnp.zeros((lead,) + half.shape[1:], half.dtype)
            half = jnp.concatenate([pad, half], axis=0).reshape(n_seq, per_seq, SUBLANES, CONV_CHUNK)
            half = half[:, lead:].reshape(tm, CONV_CHUNK)
        act = half + half * jnp.tanh(half)
        ref[:, c0 - start:c0 - start + CONV_CHUNK] = act.astype(ref.dtype)


def _ssd_in(x, conv_hist, g_mix, w_in, conv_w, conv_b, dt_bias, *, tm, seq_rows=None):
    b, s, _ = x.shape
    n_t = s // tm
    n_seq = conv_hist.shape[0]
    gn = SSM_GROUPS * D_STATE
    win_b = w_in.astype(BF16)
    wdt_t = w_in[:, D_INNER + CONV_DIM:].T
    hist = jnp.pad(conv_hist.astype(F32), ((0, 0), (CONV_CARRY - (D_CONV - 1), 0), (0, 0)))
    row = lambda d: pl.BlockSpec((None, tm, d), lambda bi, ti: (bi, ti, 0))
    if seq_rows is None:
        per_seq = pl.BlockSpec((None, CONV_CARRY, CONV_DIM), lambda bi, ti: (bi, 0, 0))
    else:
        assert b == 1 and n_t == 1 and tm == n_seq * seq_rows and seq_rows >= CONV_CARRY
        per_seq = pl.BlockSpec((n_seq, CONV_CARRY, CONV_DIM), lambda bi, ti: (0, 0, 0))
    return pl.pallas_call(
        functools.partial(_ssd_in_kernel, tm=tm, seq_rows=seq_rows),
        out_shape=[
            jax.ShapeDtypeStruct((b, s, D_INNER), BF16),
            jax.ShapeDtypeStruct((b, s, D_INNER), BF16),
            jax.ShapeDtypeStruct((b, s, gn), BF16),
            jax.ShapeDtypeStruct((b, s, gn), BF16),
            jax.ShapeDtypeStruct((b, s, SSM_HEADS), F32),
            jax.ShapeDtypeStruct((b, SSM_HEADS, s), F32),
            jax.ShapeDtypeStruct((n_seq, CONV_CARRY, CONV_DIM), F32),
        ],
        grid=(b, n_t),
        in_specs=[
            row(D_MODEL),
            _const_spec((1, D_MODEL)),
            _const_spec((D_MODEL, IN_C)),
            _const_spec((SSM_HEADS, D_MODEL)),
            _const_spec((D_CONV, CONV_DIM)),
            _const_spec((1, CONV_DIM)),
            _const_spec((1, SSM_HEADS)),
            _const_spec((SSM_HEADS, 1)),
            per_seq,
        ],
        out_specs=[
            row(D_INNER), row(D_INNER), row(gn), row(gn), row(SSM_HEADS),
            pl.BlockSpec((None, SSM_HEADS, tm), lambda bi, ti: (bi, 0, ti)),
            per_seq,
        ],
        scratch_shapes=[pltpu.VMEM((CONV_CARRY, CONV_DIM), F32)],
        compiler_params=pltpu.CompilerParams(dimension_semantics=("arbitrary", "arbitrary"),
                                             vmem_limit_bytes=VMEM_LIMIT),
        name="ssd_in",
    )(x, g_mix.reshape(1, -1), win_b, wdt_t, 0.5 * conv_w, 0.5 * conv_b.reshape(1, -1),
      dt_bias.reshape(1, -1), dt_bias.reshape(-1, 1), hist)


def _ssd_scan_kernel(xs_ref, z_ref, b_ref, c_ref, dt_ref, dtt_ref, alog_ref, alogt_ref, dexp_ref, norm_ref,
                     expand_ref, h0_ref, y_ref, hfin_ref, h_scr, *, ln, n_sub, n_steps):
    c = pl.program_id(1)

    @pl.when(c == 0)
    def _():
        for g in range(SSM_GROUPS):
            h_scr[g] = h0_ref[g * GROUP_COLS:(g + 1) * GROUP_COLS, :].T

    row_i = lax.broadcasted_iota(jnp.int32, (ln, ln), 0)
    col_i = lax.broadcasted_iota(jnp.int32, (ln, ln), 1)
    causal = row_i >= col_i
    tril = jnp.where(causal, 1.0, 0.0).astype(BF16)
    triu = jnp.where(row_i <= col_i, 1.0, 0.0).astype(BF16)
    lane = lax.broadcasted_iota(jnp.int32, (ln, LANES), 1)

    for sc in range(n_sub):
        rows = slice(sc * ln, (sc + 1) * ln)
        dt = dt_ref[rows, :]
        dtt = dtt_ref[:, rows]
        acum = _dot_sel_l(tril, dt * -jnp.exp(alog_ref[...]))
        acum_t = _dot_sel_r(dtt * -jnp.exp(alogt_ref[...]), triu)
        acum2 = acum * LOG2E
        key2_t = (acum_t - jnp.log(dtt)) * LOG2E
        last = acum[ln - 1:ln, :]
        ea_hi, ea_mid, _ = _split3(jnp.exp(acum))
        wj_b = (jnp.exp(last - acum) * dt).astype(BF16)
        pair = 2 * SSM_HEADDIM

        for g in range(SSM_GROUPS):
            gcols = slice(g * GROUP_COLS, (g + 1) * GROUP_COLS)
            expand = expand_ref[:, gcols]
            ea_e = _dot(ea_hi, expand) + _dot(ea_mid, expand)
            wj_e = _dot(wj_b, expand).astype(BF16)
            bg = b_ref[rows, g * D_STATE:(g + 1) * D_STATE]
            cg = c_ref[rows, g * D_STATE:(g + 1) * D_STATE]
            cb = _dot_nt(cg, bg)
            y_state = _dot(cg, h_scr[g].astype(BF16))
            gated = []
            for p in range(SSM_HPG // 2):
                h0 = g * SSM_HPG + 2 * p
                cols = slice(h0 * SSM_HEADDIM, (h0 + 2) * SSM_HEADDIM)
                pcols = slice(p * pair, (p + 1) * pair)
                scores = []
                for h in (h0, h0 + 1):
                    seg = jnp.where(causal, acum2[:, h:h + 1] - key2_t[h:h + 1, :], NEG)
                    scores.append((cb * jnp.exp2(seg)).astype(BF16))
                xp = xs_ref[rows, cols]
                zero = jnp.zeros_like(xp)
                rhs = jnp.concatenate([jnp.where(lane < SSM_HEADDIM, xp, zero),
                                       jnp.where(lane >= SSM_HEADDIM, xp, zero)], axis=0)
                y = _dot(jnp.concatenate(scores, axis=1), rhs)
                y = y + y_state[:, pcols] * ea_e[:, pcols]
                y = y + xp.astype(F32) * dexp_ref[:, cols]
                zc = z_ref[rows, cols].astype(F32)
                gated.append(y * (zc * jax.nn.sigmoid(zc)))
            yg = jnp.concatenate(gated, axis=1)
            yg = yg * lax.rsqrt(jnp.mean(yg * yg, axis=-1, keepdims=True) + EPS)
            y_ref[rows, gcols] = (yg * norm_ref[:, gcols]).astype(y_ref.dtype)
            xw = xs_ref[rows, gcols] * wj_e
            h_scr[g] = h_scr[g] * ea_e[ln - 1:ln, :] + _dot_tn(bg, xw)

    @pl.when(c == n_steps - 1)
    def _():
        for g in range(SSM_GROUPS):
            hfin_ref[g * GROUP_COLS:(g + 1) * GROUP_COLS, :] = h_scr[g].T


def _ssd_scan(xs, z, bm, cm, dt, dtt, a_log, d_skip, ssm_norm, h0, *, ln, n_sub=1):
    b, s, _ = xs.shape
    step = ln * n_sub
    n_steps = s // step
    assert s % step == 0
    gn = SSM_GROUPS * D_STATE
    expand = jnp.asarray(np.kron(np.eye(SSM_HEADS, dtype=np.float32), np.ones((1, SSM_HEADDIM), np.float32)),
                         BF16)
    d_exp = jnp.repeat(d_skip.astype(F32), SSM_HEADDIM).reshape(1, D_INNER)
    row = lambda d: pl.BlockSpec((None, step, d), lambda bi, ci: (bi, ci, 0))
    state_spec = pl.BlockSpec((None, D_INNER, D_STATE), lambda bi, ci: (bi, 0, 0))
    y, h_fin = pl.pallas_call(
        functools.partial(_ssd_scan_kernel, ln=ln, n_sub=n_sub, n_steps=n_steps),
        out_shape=[jax.ShapeDtypeStruct((b, s, D_INNER), BF16),
                   jax.ShapeDtypeStruct((b, D_INNER, D_STATE), F32)],
        grid=(b, n_steps),
        in_specs=[
            row(D_INNER), row(D_INNER), row(gn), row(gn), row(SSM_HEADS),
            pl.BlockSpec((None, SSM_HEADS, step), lambda bi, ci: (bi, 0, ci)),
            _const_spec((1, SSM_HEADS)), _const_spec((SSM_HEADS, 1)),
            _const_spec((1, D_INNER)), _const_spec((1, D_INNER)),
            _const_spec((SSM_HEADS, D_INNER)),
            state_spec,
        ],
        out_specs=[row(D_INNER), state_spec],
        scratch_shapes=[pltpu.VMEM((SSM_GROUPS, D_STATE, GROUP_COLS), F32)],
        compiler_params=pltpu.CompilerParams(dimension_semantics=("arbitrary", "arbitrary"),
                                             vmem_limit_bytes=VMEM_LIMIT),
        name="ssd_scan",
    )(xs, z, bm, cm, dt, dtt, a_log.reshape(1, -1), a_log.reshape(-1, 1), d_exp, ssm_norm.reshape(1, -1),
      expand, h0.astype(F32).reshape(b, D_INNER, D_STATE))
    return y, h_fin.reshape(b, SSM_HEADS, SSM_HEADDIM, D_STATE)


def _trunk(x, pos0, cache, pool_hist, conv_hist, ssm_hist, w, *, tm_proj, tm_ffn, tq, ln, n_sub=1):
    b, s, _ = x.shape
    prompt = cache is None
    stacked = dict(tm=b * s, seq_rows=s) if not prompt else dict(tm=tm_proj)
    as_rows = (lambda a: a.reshape(1, b * s, a.shape[-1])) if not prompt else (lambda a: a)
    win_p, wq, wk, wv, place = _prep_ab_weights(w["w_in_ab"][0], w["w_uq"][0], w["w_uk"][0], w["w_uv"][0])
    outs = _ab_in(as_rows(x), pos0, pool_hist, w["norm_mix"][0], win_p, w["q_norm"][0], wq, w["kv_norm"][0],
                  w["w_pool"][0], w["pool_scale"][0], wk, wv, place, emit_kv=prompt, **stacked)
    q, ckv, kpe, pool_out, pool_last = outs[:5]
    if prompt:
        k, v = outs[5:]
        attn = _flash_attention(q, k, v, tq=tq)
        kpe = jnp.swapaxes(kpe, 1, 2)
    else:
        ckv, kpe = ckv.reshape(b, s, KV_LORA), kpe.reshape(b, s, QK_ROPE)
        wabs, wuv_place = _prep_latent_weights(w["w_uk"][0], w["w_uv"][0])
        attn = _latent_attention(q[0], wabs, cache[0], jnp.swapaxes(cache[1], 1, 2), ckv, kpe, wuv_place,
                                 pos0=pos0)
    ffn_w = (w["w_gate"].astype(BF16), w["w_up"].astype(BF16), w["w_down"].astype(BF16))
    w_out = w["w_out_ab"][0].astype(BF16)
    x1 = _out_ffn(x.reshape(b * s, D_MODEL),
                  [pool_out.reshape(b * s, POOL_DIM), attn.reshape(b * s, MLA_OUT)],
                  [w_out[:POOL_DIM], w_out[POOL_DIM:]],
                  w["norm_ffn"][0], ffn_w, 0, None, tm=tm_ffn)

    z, xs, bm, cm, dt, dtt, conv_last = _ssd_in(as_rows(x1.reshape(b, s, D_MODEL)), conv_hist, w["norm_mix"][1],
                                                w["w_in_c"][0], w["conv_w"][0], w["conv_b"][0],
                                                w["dt_bias"][0], **stacked)
    if not prompt:
        z, xs, bm, cm, dt = (a.reshape(b, s, a.shape[-1]) for a in (z, xs, bm, cm, dt))
        dtt = jnp.transpose(dtt.reshape(SSM_HEADS, b, s), (1, 0, 2))
    y, h_fin = _ssd_scan(xs, z, bm, cm, dt, dtt, w["a_log"][0], w["d_skip"][0], w["ssm_norm"][0], ssm_hist,
                         ln=ln, n_sub=n_sub)
    x2 = _out_ffn(x1, [y.reshape(b * s, D_INNER)], [w["w_out_c"][0].astype(BF16)],
                  w["norm_ffn"][1], ffn_w, 1, w["norm_final"], tm=tm_ffn)
    return (x2.reshape(b, s, D_MODEL), ckv[None], kpe[None], pool_last[None, :, POOL_KEEP - POOL_HIST:],
            conv_last[None, :, CONV_CARRY - (D_CONV - 1):], h_fin[None])


def kernel(x_prompt, x_sample, cache_ckv, cache_kpe, state_pool, state_conv, state_ssm, norm_mix, norm_ffn, norm_final, w_in_ab, w_pool, pool_scale, q_norm, w_uq, kv_norm, w_uk, w_uv, w_out_ab, w_in_c, conv_w, conv_b, dt_bias, a_log, d_skip, ssm_norm, w_out_c, w_gate, w_up, w_down):
    assert norm_mix.shape[0] == 2 and w_in_ab.shape[0] == 1 and w_in_c.shape[0] == 1
    w = dict(norm_mix=norm_mix, norm_ffn=norm_ffn, norm_final=norm_final, w_in_ab=w_in_ab, w_pool=w_pool,
             pool_scale=pool_scale, q_norm=q_norm, w_uq=w_uq, kv_norm=kv_norm, w_uk=w_uk, w_uv=w_uv,
             w_out_ab=w_out_ab, w_in_c=w_in_c, conv_w=conv_w, conv_b=conv_b, dt_bias=dt_bias, a_log=a_log,
             d_skip=d_skip, ssm_norm=ssm_norm, w_out_c=w_out_c, w_gate=w_gate, w_up=w_up, w_down=w_down)
    bp = x_prompt.shape[0]
    bs, ss, _ = x_sample.shape
    past = cache_ckv.shape[2]

    prompt = _trunk(x_prompt, 0, None,
                    jnp.zeros((bp, POOL_HIST, POOL_DIM), F32), jnp.zeros((bp, D_CONV - 1, CONV_DIM), F32),
                    jnp.zeros((bp, SSM_HEADS, SSM_HEADDIM, D_STATE), F32), w,
                    tm_proj=PROMPT_PROJ_ROWS, tm_ffn=PROMPT_FFN_ROWS, tq=PROMPT_Q_ROWS,
                    ln=SSD_CHUNK, n_sub=SSD_CHUNKS_PER_STEP)
    sample = _trunk(x_sample, past, (cache_ckv[0], cache_kpe[0]), state_pool[0], state_conv[0], state_ssm[0], w,
                    tm_proj=None, tm_ffn=bs * ss, tq=None, ln=ss)
    return (prompt[0], sample[0]) + prompt[1:] + sample[1:]
```
